```python
import math
import jax
import jax.numpy as jnp
from jax import lax
import numpy as np

D_MODEL = 2048
BATCH = 4
SEQ = 2048
DEPTH = 4
DEC_BATCH = 8
DEC_SEQ = 4
PAST_LEN = 16384
PAGE_SIZE = 128

GLA_HEADS = 4
GLA_DK = D_MODEL // (2 * GLA_HEADS)
GLA_DV = D_MODEL // GLA_HEADS
GLA_GATE_RANK = 16
GLA_TAU = 16.0
GLA_CHUNK = 64

NSA_HEADS = 16
NSA_KV_HEADS = 2
NSA_DH = D_MODEL // NSA_HEADS
NSA_GROUP = NSA_HEADS // NSA_KV_HEADS
CMP_LEN = 32
CMP_STRIDE = 16
CMP_HIDDEN = 2 * NSA_DH
SEL_LEN = 64
SEL_TOPN = 16
SEL_QBLK = 64
WINDOW = 512
WIN_QBLK = 128

ROPE_THETA = 500000.0
ROPE_DIM = NSA_DH // 4
MLP_HIDDEN = 4 * D_MODEL
NORM_EPS = 1e-6
NEG_INF = -1e30
FORCED_SCORE = 1e6

IN_SPLITS = (GLA_HEADS * GLA_DK, GLA_HEADS * GLA_DK, GLA_HEADS * GLA_DV, GLA_HEADS * GLA_DV, GLA_GATE_RANK,
             NSA_HEADS * NSA_DH, 6 * NSA_KV_HEADS * NSA_DH, 3 * NSA_HEADS, 2 * D_MODEL)
N_IN = sum(IN_SPLITS)

kernel_name = 'gla_nsa_parallel_hybrid_decode_step'

F32 = jnp.float32


def rms_norm(x, g):
    xf = x.astype(F32)
    y = xf * lax.rsqrt(jnp.mean(xf * xf, axis=-1, keepdims=True) + NORM_EPS)
    return (y * g.astype(F32)).astype(x.dtype)


def partial_rope(x, pos):
    half = ROPE_DIM // 2
    inv = 1.0 / (ROPE_THETA ** (jnp.arange(half, dtype=F32) / half))
    ang = pos.astype(F32)[:, None] * inv[None, :]
    cos = jnp.cos(ang)[None, :, None, :]
    sin = jnp.sin(ang)[None, :, None, :]
    xf = x.astype(F32)
    x1 = xf[..., :half]
    x2 = xf[..., half:ROPE_DIM]
    out = jnp.concatenate([x1 * cos - x2 * sin, x1 * sin + x2 * cos, xf[..., ROPE_DIM:]], axis=-1)
    return out.astype(x.dtype)


def masked_softmax(s, valid):
    s = jnp.where(valid, s, NEG_INF)
    m = jnp.max(s, axis=-1, keepdims=True)
    p = jnp.where(valid, jnp.exp(s - m), 0.0)
    return p / jnp.maximum(jnp.sum(p, axis=-1, keepdims=True), 1e-30)


def gla_chunked(q, k, v, log_a, s0):
    B, S, H, DK = q.shape
    DV = v.shape[-1]
    C = math.gcd(S, GLA_CHUNK)
    n = S // C

    def to_chunks(t):
        return t.astype(F32).reshape(B, n, C, H, t.shape[-1]).transpose(1, 0, 3, 2, 4)

    qc, kc, vc, ac = to_chunks(q.astype(F32) * (DK ** -0.5)), to_chunks(k), to_chunks(v), to_chunks(log_a)
    causal = jnp.tril(jnp.ones((C, C), dtype=bool))

    def step(s, inp):
        qi, ki, vi, ai = inp
        b = jnp.cumsum(ai, axis=2)
        q_dec = qi * jnp.exp(b)
        k_dec = ki * jnp.exp(-b)
        o_inter = jnp.einsum('bhck,bhkv->bhcv', q_dec, s)
        att = jnp.where(causal, jnp.einsum('bhck,bhsk->bhcs', q_dec, k_dec), 0.0)
        o = o_inter + jnp.einsum('bhcs,bhsv->bhcv', att, vi)
        b_last = b[:, :, -1:, :]
        k_end = ki * jnp.exp(b_last - b)
        s_new = jnp.exp(b_last[:, :, 0, :])[..., None] * s + jnp.einsum('bhck,bhcv->bhkv', k_end, vi)
        return s_new, o

    s_fin, o = lax.scan(step, s0.astype(F32), (qc, kc, vc, ac))
    o = o.transpose(1, 0, 3, 2, 4).reshape(B, S, H, DV)
    return o, s_fin


def compress_rows(rows, pe, mix, w1, w2):
    B, T, H, Dh = rows.shape
    r = CMP_LEN // CMP_STRIDE
    nch = T // CMP_STRIDE
    ch = rows[:, :nch * CMP_STRIDE].reshape(B, nch, CMP_STRIDE, H, Dh)
    n_cmp = nch - r + 1
    blocks = jnp.concatenate([ch[:, i:i + n_cmp] for i in range(r)], axis=2)
    h = jnp.einsum('bnlhd,ld->bnhd', blocks + pe[:, None, :], mix)
    return jax.nn.gelu(h @ w1) @ w2


def cmp_attention(q, q_pos, kc, vc):
    B, Sq = q.shape[:2]
    n = kc.shape[1]
    qg = q.astype(F32).reshape(B, Sq, NSA_KV_HEADS, NSA_GROUP, NSA_DH)
    s = jnp.einsum('bqhgd,bnhd->bqhgn', qg, kc.astype(F32)) * (NSA_DH ** -0.5)
    end = jnp.arange(n) * CMP_STRIDE + (CMP_LEN - 1)
    valid = (end[None, :] <= q_pos[:, None])[None, :, None, None, :]
    p = masked_softmax(s, valid)
    o = jnp.einsum('bqhgn,bnhd->bqhgd', p, vc.astype(F32)).reshape(B, Sq, NSA_HEADS, NSA_DH)
    return o, p


def select_blocks(p, q_pos, n_slc):
    p_grp = p.sum(axis=3)
    n_cmp = p.shape[-1]
    start = jnp.arange(n_cmp) * CMP_STRIDE
    j = jnp.arange(n_slc)
    overlap = ((start[:, None] < (j[None, :] + 1) * SEL_LEN)
               & (start[:, None] + CMP_LEN > j[None, :] * SEL_LEN)).astype(F32)
    imp = jnp.einsum('bqhn,nj->bqhj', p_grp, overlap)
    cur = (q_pos // SEL_LEN)[:, None]
    forced = (j[None, :] == 0) | (j[None, :] == cur) | (j[None, :] == cur - 1)
    future = j[None, :] > cur
    imp = jnp.where(forced[None, :, None, :], FORCED_SCORE, imp)
    imp = jnp.where(future[None, :, None, :], -1.0, imp)
    _, idx = lax.top_k(imp, min(SEL_TOPN, n_slc))
    return idx


def sel_attention(q, q_pos, k_rows, v_rows, idx):
    B, Sq = q.shape[:2]
    T = k_rows.shape[1]
    n_slc = -(-T // SEL_LEN)
    pad = n_slc * SEL_LEN - T

    def to_blocks(r):
        r = jnp.pad(r, ((0, 0), (0, pad), (0, 0), (0, 0)))
        return r.reshape(B, n_slc, SEL_LEN, NSA_KV_HEADS, NSA_DH).transpose(0, 3, 1, 2, 4)

    kb, vb = to_blocks(k_rows), to_blocks(v_rows)
    qb = math.gcd(Sq, SEL_QBLK)
    nq = Sq // qb
    n_top = idx.shape[-1]
    qg = q.reshape(B, nq, qb, NSA_KV_HEADS, NSA_GROUP, NSA_DH).transpose(1, 0, 3, 2, 4, 5)
    ig = idx.reshape(B, nq, qb, NSA_KV_HEADS, n_top).transpose(1, 0, 3, 2, 4)
    pg = q_pos.reshape(nq, qb)
    take = jax.vmap(jax.vmap(lambda blk, ii: blk[ii]))
    offs = jnp.arange(SEL_LEN)

    def one(args):
        qi, ii, pi = args
        kg = take(kb, ii).astype(F32)
        vg = take(vb, ii).astype(F32)
        s = jnp.einsum('bhqgd,bhqnld->bhqgnl', qi.astype(F32), kg) * (NSA_DH ** -0.5)
        kpos = ii[..., None] * SEL_LEN + offs
        valid = (kpos <= pi[None, None, :, None, None])[:, :, :, None]
        p = masked_softmax(s.reshape(s.shape[:4] + (n_top * SEL_LEN,)),
                           valid.reshape(valid.shape[:4] + (n_top * SEL_LEN,)))
        return jnp.einsum('bhqgm,bhqmd->bhqgd', p, vg.reshape(vg.shape[:3] + (n_top * SEL_LEN, NSA_DH)))

    o = lax.map(one, (qg, ig, pg))
    return o.transpose(1, 0, 3, 2, 4, 5).reshape(B, Sq, NSA_HEADS, NSA_DH)


def win_attention_banded(q, k, v):
    B, S = q.shape[:2]
    nb = S // WIN_QBLK
    nw = -(-WINDOW // WIN_QBLK)

    def band(r):
        r = jnp.pad(r, ((0, 0), (nw * WIN_QBLK, 0), (0, 0), (0, 0)))
        r = r.reshape(B, nb + nw, WIN_QBLK, NSA_KV_HEADS, NSA_DH)
        return jnp.concatenate([r[:, i:i + nb] for i in range(nw + 1)], axis=2).astype(F32)

    kb, vb = band(k), band(v)
    qg = q.astype(F32).reshape(B, nb, WIN_QBLK, NSA_KV_HEADS, NSA_GROUP, NSA_DH)
    s = jnp.einsum('bnqhgd,bnkhd->bnhgqk', qg, kb) * (NSA_DH ** -0.5)
    qpos = jnp.arange(nb)[:, None] * WIN_QBLK + jnp.arange(WIN_QBLK)[None, :]
    kpos = (jnp.arange(nb)[:, None] - nw) * WIN_QBLK + jnp.arange((nw + 1) * WIN_QBLK)[None, :]
    kp = kpos[:, None, :]
    qp = qpos[:, :, None]
    valid = (kp <= qp) & (kp > qp - WINDOW) & (kp >= 0)
    p = masked_softmax(s, valid[None, :, None, None])
    o = jnp.einsum('bnhgqk,bnkhd->bnqhgd', p, vb)
    return o.reshape(B, S, NSA_HEADS, NSA_DH)


def win_attention_direct(q, q_pos, k, v, k_pos):
    B, Sq = q.shape[:2]
    qg = q.astype(F32).reshape(B, Sq, NSA_KV_HEADS, NSA_GROUP, NSA_DH)
    s = jnp.einsum('bqhgd,bkhd->bqhgk', qg, k.astype(F32)) * (NSA_DH ** -0.5)
    valid = (k_pos[None, :] <= q_pos[:, None]) & (k_pos[None, :] > q_pos[:, None] - WINDOW)
    p = masked_softmax(s, valid[None, :, None, None, :])
    o = jnp.einsum('bqhgk,bkhd->bqhgd', p, v.astype(F32))
    return o.reshape(B, Sq, NSA_HEADS, NSA_DH)


def token_mixers(h, pos, lp, past):
    B, S, _ = h.shape
    proj = h @ lp['w_in']
    cuts = [int(c) for c in np.cumsum(IN_SPLITS)[:-1]]
    gq, gk, gv, gg, ga, nq, nkv, ngate, mgate = jnp.split(proj, cuts, axis=-1)

    q_a = gq.reshape(B, S, GLA_HEADS, GLA_DK)
    k_a = gk.reshape(B, S, GLA_HEADS, GLA_DK)
    v_a = gv.reshape(B, S, GLA_HEADS, GLA_DV)
    log_a = jax.nn.log_sigmoid((ga @ lp['gla_wa2'] + lp['gla_ba']).astype(F32)) / GLA_TAU
    log_a = log_a.reshape(B, S, GLA_HEADS, GLA_DK)
    s0 = jnp.zeros((B, GLA_HEADS, GLA_DK, GLA_DV), F32) if past is None else past['gla']
    o_a, gla_state = gla_chunked(q_a, k_a, v_a, log_a, s0)
    o_a = rms_norm(o_a, lp['gla_norm']) * jax.nn.silu(gg.reshape(B, S, GLA_HEADS, GLA_DV).astype(F32))
    o_a = o_a.reshape(B, S, D_MODEL)

    q_b = partial_rope(nq.reshape(B, S, NSA_HEADS, NSA_DH), pos)
    kv = nkv.reshape(B, S, 6, NSA_KV_HEADS, NSA_DH)
    new_cmp = jnp.stack([partial_rope(kv[:, :, 0], pos), kv[:, :, 1]], axis=2)
    new_sel = jnp.stack([partial_rope(kv[:, :, 2], pos), kv[:, :, 3]], axis=2)
    new_win = jnp.stack([partial_rope(kv[:, :, 4], pos), kv[:, :, 5]], axis=2)
    if past is None:
        all_cmp, all_sel = new_cmp, new_sel
        o_win = win_attention_banded(q_b, new_win[:, :, 0], new_win[:, :, 1])
        win_state = new_win[:, S - min(WINDOW, S):]
    else:
        def gather_pages(pool):
            return pool[past['page_table']].reshape(B, -1, 2, NSA_KV_HEADS, NSA_DH)
        all_cmp = jnp.concatenate([gather_pages(past['cmp']), new_cmp], axis=1)
        all_sel = jnp.concatenate([gather_pages(past['sel']), new_sel], axis=1)
        win_rows = jnp.concatenate([past['win'], new_win], axis=1)
        wb = past['win'].shape[1]
        k_pos = pos[0] - wb + jnp.arange(wb + S)
        o_win = win_attention_direct(q_b, pos, win_rows[:, :, 0], win_rows[:, :, 1], k_pos)
        win_state = win_rows[:, S:]

    kc = compress_rows(all_cmp[:, :, 0], lp['cmp_pe'][0], lp['cmp_mix'][0], lp['cmp_w1'][0], lp['cmp_w2'][0])
    vc = compress_rows(all_cmp[:, :, 1], lp['cmp_pe'][1], lp['cmp_mix'][1], lp['cmp_w1'][1], lp['cmp_w2'][1])
    o_cmp, p_cmp = cmp_attention(q_b, pos, kc, vc)
    n_slc = -(-all_sel.shape[1] // SEL_LEN)
    sel_idx = select_blocks(p_cmp, pos, n_slc)
    o_sel = sel_attention(q_b, pos, all_sel[:, :, 0], all_sel[:, :, 1], sel_idx)
    g = jax.nn.sigmoid(ngate.astype(F32)).reshape(B, S, NSA_HEADS, 3)
    o_b = (g[..., 0:1] * o_cmp + g[..., 1:2] * o_sel + g[..., 2:3] * o_win).reshape(B, S, D_MODEL)

    mg = jax.nn.sigmoid(mgate.astype(F32)).reshape(B, S, 2, D_MODEL)
    merged = (mg[:, :, 0] * o_a + mg[:, :, 1] * o_b).astype(h.dtype)
    return merged @ lp['w_o'], (new_cmp, new_sel, win_state, gla_state)


def decoder_layer(x, pos, lp, past):
    h = rms_norm(x, lp['norms'][0])
    mix, new_state = token_mixers(h, pos, lp, past)
    x = x + rms_norm(mix, lp['norms'][1])
    h = rms_norm(x, lp['norms'][2])
    u = jnp.square(jax.nn.relu(h @ lp['w_up']))
    x = x + rms_norm(u @ lp['w_down'], lp['norms'][3])
    return x, new_state


def setup_inputs(seed: int = 0) -> dict:
    key = jax.random.key(seed)
    ks = jax.random.split(key, 20)
    n_pages = PAST_LEN // PAGE_SIZE
    n_used = DEC_BATCH * n_pages
    n_pool = n_used + n_used // 4
    wb = min(WINDOW, PAST_LEN)

    def nrm(k, shape, scale=1.0):
        return jax.random.normal(k, shape, F32) * scale

    return {
        'x_prompt': nrm(ks[0], (BATCH, SEQ, D_MODEL)),
        'x_sample': nrm(ks[1], (DEC_BATCH, DEC_SEQ, D_MODEL)),
        'cache_cmp_kv': nrm(ks[2], (DEPTH, n_pool, PAGE_SIZE, 2, NSA_KV_HEADS, NSA_DH)),
        'cache_sel_kv': nrm(ks[3], (DEPTH, n_pool, PAGE_SIZE, 2, NSA_KV_HEADS, NSA_DH)),
        'cache_win_kv': nrm(ks[4], (DEPTH, DEC_BATCH, wb, 2, NSA_KV_HEADS, NSA_DH)),
        'state_gla': nrm(ks[5], (DEPTH, DEC_BATCH, GLA_HEADS, GLA_DK, GLA_DV)),
        'page_table': jax.random.permutation(ks[6], n_pool)[:n_used].reshape(DEC_BATCH, n_pages).astype(jnp.int32),
        'w_in': nrm(ks[7], (DEPTH, D_MODEL, N_IN), D_MODEL ** -0.5),
        'gla_wa2': nrm(ks[8], (DEPTH, GLA_GATE_RANK, GLA_HEADS * GLA_DK), GLA_GATE_RANK ** -0.5),
        'gla_ba': nrm(ks[9], (DEPTH, GLA_HEADS * GLA_DK), 0.1),
        'gla_norm': 1.0 + nrm(ks[10], (DEPTH, GLA_DV), 0.02),
        'cmp_pe': nrm(ks[11], (DEPTH, 2, CMP_LEN, NSA_DH), 0.1),
        'cmp_mix': nrm(ks[12], (DEPTH, 2, CMP_LEN, NSA_DH), CMP_LEN ** -0.5),
        'cmp_w1': nrm(ks[13], (DEPTH, 2, NSA_DH, CMP_HIDDEN), NSA_DH ** -0.5),
        'cmp_w2': nrm(ks[14], (DEPTH, 2, CMP_HIDDEN, NSA_DH), CMP_HIDDEN ** -0.5),
        'w_o': nrm(ks[15], (DEPTH, D_MODEL, D_MODEL), D_MODEL ** -0.5),
        'norms': 1.0 + nrm(ks[16], (DEPTH, 4, D_MODEL), 0.02),
        'w_up': nrm(ks[17], (DEPTH, D_MODEL, MLP_HIDDEN), D_MODEL ** -0.5),
        'w_down': nrm(ks[18], (DEPTH, MLP_HIDDEN, D_MODEL), MLP_HIDDEN ** -0.5),
    }


def reference(x_prompt, x_sample, cache_cmp_kv, cache_sel_kv, cache_win_kv, state_gla, page_table,
              w_in, gla_wa2, gla_ba, gla_norm, cmp_pe, cmp_mix, cmp_w1, cmp_w2, w_o, norms, w_up, w_down):
    params = {'w_in': w_in, 'gla_wa2': gla_wa2, 'gla_ba': gla_ba, 'gla_norm': gla_norm,
              'cmp_pe': cmp_pe, 'cmp_mix': cmp_mix, 'cmp_w1': cmp_w1, 'cmp_w2': cmp_w2,
              'w_o': w_o, 'norms': norms, 'w_up': w_up, 'w_down': w_down}
    pos_p = jnp.arange(x_prompt.shape[1], dtype=jnp.int32)
    pos_s = PAST_LEN + jnp.arange(x_sample.shape[1], dtype=jnp.int32)
    y_p, y_s = x_prompt, x_sample
    st_p, st_s = [], []
    for layer in range(DEPTH):
        lp = {name: arr[layer] for name, arr in params.items()}
        y_p, sp = decoder_layer(y_p, pos_p, lp, None)
        past = {'cmp': cache_cmp_kv[layer], 'sel': cache_sel_kv[layer], 'win': cache_win_kv[layer],
                'gla': state_gla[layer], 'page_table': page_table}
        y_s, ss = decoder_layer(y_s, pos_s, lp, past)
        st_p.append(sp)
        st_s.append(ss)

    def stack(states, i):
        return jnp.stack([s[i] for s in states])

    return (y_p, y_s, stack(st_p, 0), stack(st_s, 0), stack(st_p, 1), stack(st_s, 1),
            stack(st_p, 2), stack(st_s, 2), stack(st_p, 3), stack(st_s, 3))
```

```python
import functools

import jax
import jax.numpy as jnp
import numpy as np
from jax import lax
from jax.experimental import pallas as pl
from jax.experimental.pallas import tpu as pltpu

F32 = jnp.float32
BF16 = jnp.bfloat16
HI = lax.Precision.HIGHEST

D_MODEL = 2048
DEPTH = 4
PAST_LEN = 16384
PAGE_SIZE = 128

GLA_HEADS = 4
GLA_DK = 256
GLA_DV = 512
GLA_GATE_RANK = 16
GLA_TAU = 16.0
GLA_CHUNK = 64

NSA_HEADS = 16
NSA_KV_HEADS = 2
NSA_DH = 128
NSA_GROUP = NSA_HEADS // NSA_KV_HEADS
CMP_LEN = 32
CMP_STRIDE = 16
CMP_HIDDEN = 256
SEL_LEN = 64
SEL_SHIFT = 6
SEL_TOPN = 16
WINDOW = 512
WIN_QBLK = 128

ROPE_THETA = 500000.0
ROPE_DIM = 32
ROPE_HALF = 16
MLP_HIDDEN = 4 * D_MODEL
NORM_EPS = 1e-6
NEG_INF = -1e30
FORCED_SCORE = 1e6
KV_ROW = 2 * NSA_KV_HEADS * NSA_DH

VMEM_LIMIT_BYTES = 56 * 1024 * 1024
LANES = 128

COL_MG = 0
COL_GV = 4096
COL_GG = 6144
COL_NQ = 8192
COL_GQ = 10240
COL_GK = 11264
COL_NKV = 12288
COL_SM = 13824
N_PROJ = 14336
SM_GATE0 = GLA_GATE_RANK

_SRC_SEGMENTS = ((9792, 13888), (2048, 4096), (4096, 6144), (6160, 8208), (0, 1024), (1024, 2048),
                 (8208, 9744), (6144, 6160), (9744, 9792))

SAMPLE_ROWS = 8


def _cparams(sem):
    return pltpu.CompilerParams(dimension_semantics=sem, vmem_limit_bytes=VMEM_LIMIT_BYTES)


def _rms(x, g):
    return x * lax.rsqrt(jnp.mean(x * x, axis=-1, keepdims=True) + NORM_EPS) * g


def _dot(a, b, precision=None):
    return jnp.dot(a, b, preferred_element_type=F32, precision=precision)


def _dot_nt(a, b, precision=None):
    return lax.dot_general(a, b, (((1,), (1,)), ((), ())), preferred_element_type=F32, precision=precision)


def _dot_tn(a, b, precision=None):
    return lax.dot_general(a, b, (((0,), (0,)), ((), ())), preferred_element_type=F32, precision=precision)


def _iota(shape, dim):
    return lax.broadcasted_iota(jnp.int32, shape, dim)


def _masked_softmax(s, valid):
    s = jnp.where(valid, s, NEG_INF)
    m = jnp.max(s, axis=-1, keepdims=True)
    p = jnp.where(valid, jnp.exp(s - m), 0.0)
    return p / jnp.maximum(jnp.sum(p, axis=-1, keepdims=True), 1e-30)


def _norm_matmul_kernel(x_ref, g_ref, w_ref, o_ref, h_ref):
    @pl.when(pl.program_id(1) == 0)
    def _():
        h_ref[...] = _rms(x_ref[...], g_ref[...]).astype(BF16)

    o_ref[...] = _dot(h_ref[...], w_ref[...])


def norm_matmul(x, g, w, tm, tn):
    m, k = x.shape
    n = w.shape[1]
    return pl.pallas_call(
        _norm_matmul_kernel,
        grid=(m // tm, n // tn),
        in_specs=[pl.BlockSpec((tm, k), lambda i, j: (i, 0)),
                  pl.BlockSpec((1, k), lambda i, j: (0, 0)),
                  pl.BlockSpec((k, tn), lambda i, j: (0, j))],
        out_specs=pl.BlockSpec((tm, tn), lambda i, j: (i, j)),
        out_shape=jax.ShapeDtypeStruct((m, n), F32),
        scratch_shapes=[pltpu.VMEM((tm, k), BF16)],
        compiler_params=_cparams(("parallel", "arbitrary")),
        name="norm_matmul",
    )(x, g, w)


def _merge_wo_kernel(x_ref, mg0_ref, mg1_ref, oa_ref, ob_ref, g_ref, wo_ref, o_ref):
    ob = jnp.concatenate([ob_ref[h] for h in range(NSA_HEADS)], axis=1)
    a = jax.nn.sigmoid(mg0_ref[...]) * oa_ref[...] + jax.nn.sigmoid(mg1_ref[...]) * ob
    m = _dot(a.astype(BF16), wo_ref[...])
    o_ref[...] = x_ref[...] + _rms(m, g_ref[...])


def merge_wo(x, proj, o_a, o_b_hm, g, wo, tm):
    m, d = x.shape
    return pl.pallas_call(
        _merge_wo_kernel,
        grid=(m // tm,),
        in_specs=[pl.BlockSpec((tm, d), lambda i: (i, 0)),
                  pl.BlockSpec((tm, d), lambda i: (i, COL_MG // D_MODEL)),
                  pl.BlockSpec((tm, d), lambda i: (i, COL_MG // D_MODEL + 1)),
                  pl.BlockSpec((tm, d), lambda i: (i, 0)),
                  pl.BlockSpec((NSA_HEADS, tm, NSA_DH), lambda i: (0, i, 0)),
                  pl.BlockSpec((1, d), lambda i: (0, 0)),
                  pl.BlockSpec((d, d), lambda i: (0, 0))],
        out_specs=pl.BlockSpec((tm, d), lambda i: (i, 0)),
        out_shape=jax.ShapeDtypeStruct((m, d), F32),
        compiler_params=_cparams(("parallel",)),
        name="merge_wo",
    )(x, proj, proj, o_a, o_b_hm, g, wo)


def _mlp_kernel(x_ref, g2_ref, g3_ref, wu_ref, wd_ref, o_ref, h_ref, acc_ref):
    j = pl.program_id(1)

    @pl.when(j == 0)
    def _():
        h_ref[...] = _rms(x_ref[...], g2_ref[...]).astype(BF16)
        acc_ref[...] = jnp.zeros_like(acc_ref)

    u = _dot(h_ref[...], wu_ref[...])
    u = jnp.square(jnp.maximum(u, 0.0)).astype(BF16)
    acc_ref[...] += _dot(u, wd_ref[...])

    @pl.when(j == pl.num_programs(1) - 1)
    def _():
        o_ref[...] = x_ref[...] + _rms(acc_ref[...], g3_ref[...])


def mlp(x, g2, g3, wu, wd, tm, th):
    m, d = x.shape
    hid = wu.shape[1]
    return pl.pallas_call(
        _mlp_kernel,
        grid=(m // tm, hid // th),
        in_specs=[pl.BlockSpec((tm, d), lambda i, j: (i, 0)),
                  pl.BlockSpec((1, d), lambda i, j: (0, 0)),
                  pl.BlockSpec((1, d), lambda i, j: (0, 0)),
                  pl.BlockSpec((d, th), lambda i, j: (0, j)),
                  pl.BlockSpec((th, d), lambda i, j: (j, 0))],
        out_specs=pl.BlockSpec((tm, d), lambda i, j: (i, 0)),
        out_shape=jax.ShapeDtypeStruct((m, d), F32),
        scratch_shapes=[pltpu.VMEM((tm, d), BF16), pltpu.VMEM((tm, d), F32)],
        compiler_params=_cparams(("parallel", "arbitrary")),
        name="mlp",
    )(x, g2, g3, wu, wd)


def _gla_kernel(q_ref, k_ref, v_ref, gg_ref, sm_ref, wa2_ref, ba_ref, gn_ref, s0_ref,
                o_ref, sout_ref, st_ref, *, rows_in, n_inner, n_valid, has_s0):
    c = pl.program_id(2)
    C = GLA_CHUNK

    @pl.when(c == 0)
    def _():
        if has_s0:
            st_ref[...] = s0_ref[...].T
        else:
            st_ref[...] = jnp.zeros_like(st_ref)

    row = _iota((C, 1), 0)
    col = _iota((1, C), 1)
    causal = row >= col
    tril = causal.astype(F32)
    live = row < n_valid

    def pad(x):
        if rows_in == C:
            return x
        return jnp.concatenate([x, jnp.zeros((C - rows_in, x.shape[1]), x.dtype)], axis=0)

    for ci in range(n_inner):
        rs = pl.ds(ci * rows_in, rows_in)
        q = pad(q_ref[rs, :]) * (GLA_DK ** -0.5)
        k = pad(k_ref[rs, :])
        v = pad(v_ref[rs, :])
        ga = pad(sm_ref[rs, :])[:, :GLA_GATE_RANK]
        z = _dot(ga, wa2_ref[...], HI) + ba_ref[...]
        log_a = (jnp.minimum(z, 0.0) - jnp.log1p(jnp.exp(-jnp.abs(z)))) / GLA_TAU
        if n_valid < C:
            log_a = jnp.where(live, log_a, 0.0)
            k = jnp.where(live, k, 0.0)
        b = _dot(tril, log_a, HI)
        q_dec = q * jnp.exp(b)
        k_dec = k * jnp.exp(-b)
        st = st_ref[...]
        o = _dot_nt(q_dec.astype(BF16), st.astype(BF16))
        att = jnp.where(causal, _dot_nt(q_dec.astype(BF16), k_dec.astype(BF16)), 0.0)
        vb = v.astype(BF16)
        o = o + _dot(att.astype(BF16), vb)
        b_last = b[C - 1:C, :]
        k_end = k * jnp.exp(b_last - b)
        st_ref[...] = st * jnp.exp(b_last) + _dot_tn(vb, k_end.astype(BF16))
        o_n = _rms(o, gn_ref[...])
        gg = gg_ref[rs, :]
        o_ref[rs, :] = o_n[:rows_in] * (gg * jax.nn.sigmoid(gg))

    @pl.when(c == pl.num_programs(2) - 1)
    def _():
        sout_ref[...] = st_ref[...].T


def gla(proj, wa2, ba, gn, s0, batch, rows_per_batch, rows_in, n_inner, n_valid):
    m = proj.shape[0]
    r = rows_in * n_inner
    n_steps = rows_per_batch // r
    has_s0 = s0 is not None
    if s0 is None:
        s0 = jnp.zeros((1, 1, GLA_DK, GLA_DV), F32)
        s0_map = lambda b, h, c: (0, 0, 0, 0)
    else:
        s0_map = lambda b, h, c: (b, h, 0, 0)
    rowmap = lambda off: (lambda b, h, c: (b * n_steps + c, off + h))
    kern = functools.partial(_gla_kernel, rows_in=rows_in, n_inner=n_inner, n_valid=n_valid, has_s0=has_s0)
    return pl.pallas_call(
        kern,
        grid=(batch, GLA_HEADS, n_steps),
        in_specs=[pl.BlockSpec((r, GLA_DK), rowmap(COL_GQ // GLA_DK)),
                  pl.BlockSpec((r, GLA_DK), rowmap(COL_GK // GLA_DK)),
                  pl.BlockSpec((r, GLA_DV), rowmap(COL_GV // GLA_DV)),
                  pl.BlockSpec((r, GLA_DV), rowmap(COL_GG // GLA_DV)),
                  pl.BlockSpec((r, LANES), lambda b, h, c: (b * n_steps + c, COL_SM // LANES)),
                  pl.BlockSpec((GLA_GATE_RANK, GLA_DK), lambda b, h, c: (0, h)),
                  pl.BlockSpec((1, GLA_DK), lambda b, h, c: (0, h)),
                  pl.BlockSpec((1, GLA_DV), lambda b, h, c: (0, 0)),
                  pl.BlockSpec((None, None, GLA_DK, GLA_DV), s0_map)],
        out_specs=[pl.BlockSpec((r, GLA_DV), lambda b, h, c: (b * n_steps + c, h)),
                   pl.BlockSpec((None, None, GLA_DK, GLA_DV), lambda b, h, c: (b, h, 0, 0))],
        out_shape=[jax.ShapeDtypeStruct((m, D_MODEL), F32),
                   jax.ShapeDtypeStruct((batch, GLA_HEADS, GLA_DK, GLA_DV), F32)],
        scratch_shapes=[pltpu.VMEM((GLA_DV, GLA_DK), F32)],
        compiler_params=_cparams(("parallel", "parallel", "arbitrary")),
        name="gla",
    )(proj, proj, proj, proj, proj, wa2, ba, gn, s0)


def _rope(x, c, sa, sb):
    return x * c + pltpu.roll(x, LANES - ROPE_HALF, 1) * sa + pltpu.roll(x, ROPE_HALF, 1) * sb


def _nsa_prep_kernel(nq_ref, nkv_ref, c_ref, sa_ref, sb_ref, q_ref, cmp_ref, sel_ref, win_ref):
    c, sa, sb = c_ref[...], sa_ref[...], sb_ref[...]
    for h in range(NSA_HEADS):
        q_ref[h] = _rope(nq_ref[:, h * NSA_DH:(h + 1) * NSA_DH], c, sa, sb)
    for s, out in enumerate((cmp_ref, sel_ref, win_ref)):
        base = s * KV_ROW
        for hh in range(NSA_KV_HEADS):
            lo = hh * NSA_DH
            out[:, lo:lo + NSA_DH] = _rope(nkv_ref[:, base + lo:base + lo + NSA_DH], c, sa, sb)
        half = NSA_KV_HEADS * NSA_DH
        out[:, half:] = nkv_ref[:, base + half:base + KV_ROW]


def nsa_prep(proj, tabs, tr):
    m = proj.shape[0]
    n_tab = tabs[0].shape[0] // tr
    tab_spec = pl.BlockSpec((tr, LANES), lambda i: (i % n_tab, 0))
    nkv_w = 3 * KV_ROW
    return pl.pallas_call(
        _nsa_prep_kernel,
        grid=(m // tr,),
        in_specs=[pl.BlockSpec((tr, D_MODEL), lambda i: (i, COL_NQ // D_MODEL)),
                  pl.BlockSpec((tr, nkv_w), lambda i: (i, COL_NKV // nkv_w)),
                  tab_spec, tab_spec, tab_spec],
        out_specs=[pl.BlockSpec((NSA_HEADS, tr, NSA_DH), lambda i: (0, i, 0)),
                   pl.BlockSpec((tr, KV_ROW), lambda i: (i, 0)),
                   pl.BlockSpec((tr, KV_ROW), lambda i: (i, 0)),
                   pl.BlockSpec((tr, KV_ROW), lambda i: (i, 0))],
        out_shape=[jax.ShapeDtypeStruct((NSA_HEADS, m, NSA_DH), F32),
                   jax.ShapeDtypeStruct((m, KV_ROW), F32),
                   jax.ShapeDtypeStruct((m, KV_ROW), F32),
                   jax.ShapeDtypeStruct((m, KV_ROW), F32)],
        compiler_params=_cparams(("parallel",)),
        name="nsa_prep",
    )(proj, proj, *tabs)


def rope_tables(pos):
    inv = 1.0 / (ROPE_THETA ** (jnp.arange(ROPE_HALF, dtype=F32) / ROPE_HALF))
    ang = pos.astype(F32)[:, None] * inv[None, :]
    cos, sin = jnp.cos(ang), jnp.sin(ang)
    n = pos.shape[0]
    rest = LANES - ROPE_DIM
    c = jnp.concatenate([cos, cos, jnp.ones((n, rest), F32)], axis=1)
    sa = jnp.concatenate([-sin, jnp.zeros((n, LANES - ROPE_HALF), F32)], axis=1)
    sb = jnp.concatenate([jnp.zeros((n, ROPE_HALF), F32), sin, jnp.zeros((n, rest), F32)], axis=1)
    return c, sa, sb


def _chunk_sums_body(x, pe_ref, mix_ref, a_ref, b_ref):
    r = x.shape[0]
    nch = r // CMP_STRIDE
    for kv in range(2):
        pe, mix = pe_ref[kv], mix_ref[kv]
        for hh in range(NSA_KV_HEADS):
            lo = (kv * NSA_KV_HEADS + hh) * NSA_DH
            xs = x[:, lo:lo + NSA_DH].reshape(nch, CMP_STRIDE, NSA_DH)
            a_ref[:, lo:lo + NSA_DH] = jnp.sum((xs + pe[:CMP_STRIDE]) * mix[:CMP_STRIDE], axis=1)
            b_ref[:, lo:lo + NSA_DH] = jnp.sum((xs + pe[CMP_STRIDE:]) * mix[CMP_STRIDE:], axis=1)


def _chunk_sums_kernel(x_ref, pe_ref, mix_ref, a_ref, b_ref):
    _chunk_sums_body(x_ref[...], pe_ref, mix_ref, a_ref, b_ref)


def _chunk_sums_paged_kernel(pt_ref, x_ref, pe_ref, mix_ref, a_ref, b_ref):
    _chunk_sums_body(x_ref[...], pe_ref, mix_ref, a_ref, b_ref)


def chunk_sums(rows, pe, mix, tr):
    m = rows.shape[0]
    nch = tr // CMP_STRIDE
    full = lambda i: (0, 0, 0)
    return pl.pallas_call(
        _chunk_sums_kernel,
        grid=(m // tr,),
        in_specs=[pl.BlockSpec((tr, KV_ROW), lambda i: (i, 0)),
                  pl.BlockSpec((2, CMP_LEN, NSA_DH), full),
                  pl.BlockSpec((2, CMP_LEN, NSA_DH), full)],
        out_specs=[pl.BlockSpec((nch, KV_ROW), lambda i: (i, 0))] * 2,
        out_shape=[jax.ShapeDtypeStruct((m // CMP_STRIDE, KV_ROW), F32)] * 2,
        compiler_params=_cparams(("parallel",)),
        name="chunk_sums",
    )(rows, pe, mix)


def chunk_sums_paged(cache, layer, page_table, pe, mix):
    batch, n_pages = page_table.shape
    nch = PAGE_SIZE // CMP_STRIDE
    full = lambda b, p, pt: (0, 0, 0)
    gs = pltpu.PrefetchScalarGridSpec(
        num_scalar_prefetch=1,
        grid=(batch, n_pages),
        in_specs=[pl.BlockSpec((None, None, PAGE_SIZE, KV_ROW), lambda b, p, pt: (layer, pt[b, p], 0, 0)),
                  pl.BlockSpec((2, CMP_LEN, NSA_DH), full),
                  pl.BlockSpec((2, CMP_LEN, NSA_DH), full)],
        out_specs=[pl.BlockSpec((nch, KV_ROW), lambda b, p, pt: (b * n_pages + p, 0))] * 2,
    )
    return pl.pallas_call(
        _chunk_sums_paged_kernel,
        grid_spec=gs,
        out_shape=[jax.ShapeDtypeStruct((batch * n_pages * nch, KV_ROW), F32)] * 2,
        compiler_params=_cparams(("parallel", "arbitrary")),
        name="chunk_sums_paged",
    )(page_table, cache, pe, mix)


def _cmp_mlp_kernel(a_ref, b_ref, w1_ref, w2_ref, o_ref):
    h = a_ref[...] + b_ref[...]
    for kv in range(2):
        for hh in range(NSA_KV_HEADS):
            lo = (kv * NSA_KV_HEADS + hh) * NSA_DH
            y = jax.nn.gelu(_dot(h[:, lo:lo + NSA_DH], w1_ref[kv], HI))
            o_ref[:, lo:lo + NSA_DH] = _dot(y, w2_ref[kv], HI)


def cmp_mlp(a, b_shift, w1, w2, tr):
    m = a.shape[0]
    return pl.pallas_call(
        _cmp_mlp_kernel,
        grid=(m // tr,),
        in_specs=[pl.BlockSpec((tr, KV_ROW), lambda i: (i, 0)),
                  pl.BlockSpec((tr, KV_ROW), lambda i: (i, 0)),
                  pl.BlockSpec((2, NSA_DH, CMP_HIDDEN), lambda i: (0, 0, 0)),
                  pl.BlockSpec((2, CMP_HIDDEN, NSA_DH), lambda i: (0, 0, 0))],
        out_specs=pl.BlockSpec((tr, KV_ROW), lambda i: (i, 0)),
        out_shape=jax.ShapeDtypeStruct((m, KV_ROW), F32),
        compiler_params=_cparams(("parallel",)),
        name="cmp_mlp",
    )(a, b_shift, w1, w2)


def compressed_kv(a, b, batch, w1, w2, tr):
    nch = a.shape[0] // batch
    b3 = b.reshape(batch, nch, KV_ROW)
    b_shift = jnp.concatenate([b3[:, 1:], jnp.zeros((batch, 1, KV_ROW), F32)], axis=1).reshape(batch * nch, KV_ROW)
    return cmp_mlp(a, b_shift, w1, w2, tr)


def _select_topn(imp, n_top):
    j = _iota(imp.shape, 1)
    big = jnp.int32(imp.shape[1])
    sel = jnp.zeros(imp.shape, F32)
    for _ in range(n_top):
        m = jnp.max(imp, axis=-1, keepdims=True)
        idx = jnp.min(jnp.where(imp == m, j, big), axis=-1, keepdims=True)
        hit = j == idx
        sel = jnp.where(hit, 1.0, sel)
        imp = jnp.where(hit, NEG_INF, imp)
    return sel


def _importance(p_grp, qpos, n_cmp_pad, n_slc, width):
    n_r = _iota((n_cmp_pad, 1), 0) * CMP_STRIDE
    j_c = _iota((1, width), 1)
    overlap = ((n_r < (j_c + 1) * SEL_LEN) & (n_r + CMP_LEN > j_c * SEL_LEN)).astype(F32)
    imp = _dot(p_grp, overlap, HI)
    cur = qpos >> SEL_SHIFT
    forced = (j_c == 0) | (j_c == cur) | (j_c == cur - 1)
    imp = jnp.where(forced, FORCED_SCORE, imp)
    imp = jnp.where(j_c > cur, -1.0, imp)
    return jnp.where(j_c >= n_slc, -2.0, imp)


def _nsa_prompt_kernel(q_ref, kc_ref, vc_ref, ks_ref, vs_ref, kw_ref, vw_ref, sm_ref, o_ref,
                       valid_ref, osel_ref, *, Q, T):
    G = NSA_GROUP
    hk = pl.program_id(1)
    qi = pl.program_id(2)
    q0 = qi * Q
    scale = NSA_DH ** -0.5
    n_cmp_pad = kc_ref.shape[0]
    n_slc = T // SEL_LEN

    qs = q_ref[...].reshape(G * Q, NSA_DH)
    qpos_s = q0 + (_iota((G * Q, 1), 0) & (Q - 1))
    qpos = q0 + _iota((Q, 1), 0)

    s = _dot_nt(qs, kc_ref[...], HI) * scale
    end = _iota((1, n_cmp_pad), 1) * CMP_STRIDE + (CMP_LEN - 1)
    p = _masked_softmax(s, end <= qpos_s)
    o_cmp = _dot(p.astype(BF16), vc_ref[...].astype(BF16))
    p_grp = jnp.sum(p.reshape(G, Q, n_cmp_pad), axis=0)

    imp = _importance(p_grp, qpos, n_cmp_pad, n_slc, LANES)
    sel = _select_topn(imp, min(SEL_TOPN, n_slc))
    kpos = _iota((1, T), 1)
    expand = (_iota((LANES, 1), 0) == (kpos >> SEL_SHIFT)).astype(BF16)
    keymask = _dot(sel.astype(BF16), expand)
    valid_ref[...] = jnp.where((keymask > 0.5) & (kpos <= qpos), 1.0, 0.0)

    ks0 = pl.multiple_of(jnp.maximum(qi - WINDOW // WIN_QBLK, 0) * WIN_QBLK, WIN_QBLK)
    wlen = WINDOW + WIN_QBLK
    kw = kw_ref[pl.ds(ks0, wlen), :].astype(BF16)
    vw = vw_ref[pl.ds(ks0, wlen), :].astype(BF16)
    wpos = ks0 + _iota((1, wlen), 1)
    sw = _dot_nt(qs.astype(BF16), kw) * scale
    pw = _masked_softmax(sw, (wpos <= qpos_s) & (wpos > qpos_s - WINDOW))
    o_win = _dot(pw.astype(BF16), vw)

    ksb = ks_ref[...].astype(BF16)
    vsb = vs_ref[...].astype(BF16)

    def head(g, carry):
        sg = _dot_nt(q_ref[g].astype(BF16), ksb) * scale
        pg = _masked_softmax(sg, valid_ref[...] > 0.5)
        osel_ref[g] = _dot(pg.astype(BF16), vsb)
        return carry

    lax.fori_loop(0, G, head, 0)

    ng = jax.nn.sigmoid(sm_ref[...])
    for g in range(G):
        gate = []
        for c3 in range(3):
            lane0 = SM_GATE0 + 3 * g + c3
            lane1 = lane0 + 3 * G
            gate.append(jnp.where(hk == 0, ng[:, lane0:lane0 + 1], ng[:, lane1:lane1 + 1]))
        rows = slice(g * Q, (g + 1) * Q)
        o_ref[g] = gate[0] * o_cmp[rows] + gate[1] * osel_ref[g] + gate[2] * o_win[rows]


def nsa_prompt(q_hm, kvc, sel_rows, win_rows, proj, batch, T, Q):
    m = batch * T
    nq = T // Q
    n_cmp_pad = kvc.shape[0] // batch
    G = NSA_GROUP
    kern = functools.partial(_nsa_prompt_kernel, Q=Q, T=T)
    kcol = lambda off: (lambda b, hk, qi: (b, off + hk))
    return pl.pallas_call(
        kern,
        grid=(batch, NSA_KV_HEADS, nq),
        in_specs=[pl.BlockSpec((G, Q, NSA_DH), lambda b, hk, qi: (hk, b * nq + qi, 0)),
                  pl.BlockSpec((n_cmp_pad, NSA_DH), kcol(0)),
                  pl.BlockSpec((n_cmp_pad, NSA_DH), kcol(NSA_KV_HEADS)),
                  pl.BlockSpec((T, NSA_DH), kcol(0)),
                  pl.BlockSpec((T, NSA_DH), kcol(NSA_KV_HEADS)),
                  pl.BlockSpec((T, NSA_DH), kcol(0)),
                  pl.BlockSpec((T, NSA_DH), kcol(NSA_KV_HEADS)),
                  pl.BlockSpec((Q, LANES), lambda b, hk, qi: (b * nq + qi, COL_SM // LANES))],
        out_specs=pl.BlockSpec((G, Q, NSA_DH), lambda b, hk, qi: (hk, b * nq + qi, 0)),
        out_shape=jax.ShapeDtypeStruct((NSA_HEADS, m, NSA_DH), F32),
        scratch_shapes=[pltpu.VMEM((Q, T), F32), pltpu.VMEM((G, Q, NSA_DH), F32)],
        compiler_params=_cparams(("parallel", "parallel", "arbitrary")),
        name="nsa_prompt",
    )(q_hm, kvc, kvc, sel_rows, sel_rows, win_rows, win_rows, proj)


SEL_W = 384


def _sample_cmp_select_kernel(q_ref, kc_ref, vc_ref, ocmp_ref, selm_ref, *, n_slc):
    G, R = NSA_GROUP, SAMPLE_ROWS
    scale = NSA_DH ** -0.5
    n_cmp_pad = kc_ref.shape[0]
    qs = q_ref[...].reshape(G * R, NSA_DH)
    qpos_s = PAST_LEN + (_iota((G * R, 1), 0) & (R - 1))
    qpos = PAST_LEN + _iota((R, 1), 0)
    s = _dot_nt(qs, kc_ref[...], HI) * scale
    end = _iota((1, n_cmp_pad), 1) * CMP_STRIDE + (CMP_LEN - 1)
    p = _masked_softmax(s, end <= qpos_s)
    o_cmp = _dot(p.astype(BF16), vc_ref[...].astype(BF16))
    for g in range(G):
        ocmp_ref[g] = o_cmp[g * R:(g + 1) * R]
    p_grp = jnp.sum(p.reshape(G, R, n_cmp_pad), axis=0)
    imp = _importance(p_grp, qpos, n_cmp_pad, n_slc, SEL_W)
    selm_ref[...] = _select_topn(imp, min(SEL_TOPN, n_slc))


def sample_cmp_select(q_hm, kvc, batch, n_slc):
    G, R = NSA_GROUP, SAMPLE_ROWS
    n_cmp_pad = kvc.shape[0] // batch
    kern = functools.partial(_sample_cmp_select_kernel, n_slc=n_slc)
    return pl.pallas_call(
        kern,
        grid=(batch, NSA_KV_HEADS),
        in_specs=[pl.BlockSpec((G, R, NSA_DH), lambda b, hk: (hk, b, 0)),
                  pl.BlockSpec((n_cmp_pad, NSA_DH), lambda b, hk: (b, hk)),
                  pl.BlockSpec((n_cmp_pad, NSA_DH), lambda b, hk: (b, NSA_KV_HEADS + hk))],
        out_specs=[pl.BlockSpec((G, R, NSA_DH), lambda b, hk: (hk, b, 0)),
                   pl.BlockSpec((None, None, R, SEL_W), lambda b, hk: (b, hk, 0, 0))],
        out_shape=[jax.ShapeDtypeStruct((NSA_HEADS, batch * R, NSA_DH), F32),
                   jax.ShapeDtypeStruct((batch, NSA_KV_HEADS, R, SEL_W), F32)],
        compiler_params=_cparams(("parallel", "parallel")),
        name="sample_cmp_select",
    )(q_hm, kvc, kvc)


def _sample_attn_kernel(pt_ref, q_ref, selm_ref, page_ref, nsel_ref, cwin_ref, nwin_ref, sm_ref, ocmp_ref,
                        o_ref, m_ref, l_ref, acc_ref, *, n_pages):
    G, R = NSA_GROUP, SAMPLE_ROWS
    p = pl.program_id(1)
    scale = NSA_DH ** -0.5
    t_s = _iota((G * R, 1), 0) & (R - 1)

    @pl.when(p == 0)
    def _():
        m_ref[...] = jnp.full_like(m_ref, NEG_INF)
        l_ref[...] = jnp.zeros_like(l_ref)
        acc_ref[...] = jnp.zeros_like(acc_ref)

    def update(hk, s, valid, v):
        s = jnp.where(valid, s, NEG_INF)
        m_old = m_ref[hk]
        m_new = jnp.maximum(m_old, jnp.max(s, axis=-1, keepdims=True))
        alpha = jnp.exp(m_old - m_new)
        pr = jnp.where(valid, jnp.exp(s - m_new), 0.0)
        l_ref[hk] = alpha * l_ref[hk] + jnp.sum(pr, axis=-1, keepdims=True)
        acc_ref[hk] = alpha * acc_ref[hk] + _dot(pr.astype(BF16), v.astype(BF16))
        m_ref[hk] = m_new

    key = _iota((1, PAGE_SIZE), 1)
    blk_of_key = (PAGE_SIZE // SEL_LEN) * p + (key >> SEL_SHIFT)
    expand = (_iota((SEL_W, 1), 0) == blk_of_key).astype(F32)
    for hk in range(NSA_KV_HEADS):
        qs = q_ref[hk * G:(hk + 1) * G].reshape(G * R, NSA_DH).astype(BF16)
        kpg = page_ref[:, hk * NSA_DH:(hk + 1) * NSA_DH]
        vpg = page_ref[:, (NSA_KV_HEADS + hk) * NSA_DH:(NSA_KV_HEADS + hk + 1) * NSA_DH]
        s = _dot_nt(qs, kpg.astype(BF16)) * scale
        keymask = _dot(selm_ref[hk], expand)
        valid = jnp.concatenate([keymask] * G, axis=0) > 0.5
        update(hk, s, valid, vpg)

    @pl.when(p == n_pages - 1)
    def _():
        ng = jax.nn.sigmoid(sm_ref[...])
        new_blk = PAST_LEN // SEL_LEN
        zpad = jnp.zeros((LANES - R, NSA_DH), F32)
        jn = _iota((1, LANES), 1)
        for hk in range(NSA_KV_HEADS):
            qs = q_ref[hk * G:(hk + 1) * G].reshape(G * R, NSA_DH).astype(BF16)
            klo, vlo = hk * NSA_DH, (NSA_KV_HEADS + hk) * NSA_DH
            kn = jnp.concatenate([nsel_ref[:, klo:klo + NSA_DH], zpad], axis=0)
            vn = jnp.concatenate([nsel_ref[:, vlo:vlo + NSA_DH], zpad], axis=0)
            s = _dot_nt(qs, kn.astype(BF16)) * scale
            picked = jnp.concatenate([selm_ref[hk][:, new_blk:new_blk + 1]] * G, axis=0) > 0.5
            update(hk, s, picked & (jn <= t_s) & (jn < R), vn)
            o_sel = acc_ref[hk] / jnp.maximum(l_ref[hk], 1e-30)
            kw = jnp.concatenate([cwin_ref[:, klo:klo + NSA_DH], nwin_ref[:, klo:klo + NSA_DH], zpad], axis=0)
            vw = jnp.concatenate([cwin_ref[:, vlo:vlo + NSA_DH], nwin_ref[:, vlo:vlo + NSA_DH], zpad], axis=0)
            iw = _iota((1, WINDOW + LANES), 1)
            wvalid = ((iw < WINDOW) & (iw > t_s)) | ((iw >= WINDOW) & (iw - WINDOW <= t_s) & (iw - WINDOW < R))
            pw = _masked_softmax(_dot_nt(qs, kw.astype(BF16)) * scale, wvalid)
            o_win = _dot(pw.astype(BF16), vw.astype(BF16))
            for g in range(G):
                h = hk * G + g
                lane = SM_GATE0 + 3 * h
                rows = slice(g * R, (g + 1) * R)
                o_ref[h] = (ng[:, lane:lane + 1] * ocmp_ref[h] + ng[:, lane + 1:lane + 2] * o_sel[rows]
                            + ng[:, lane + 2:lane + 3] * o_win[rows])


def sample_attn(q_hm, selm, cache_sel, layer, page_table, new_sel, cache_win, new_win, proj, o_cmp_hm):
    batch, n_pages = page_table.shape
    G, R = NSA_GROUP, SAMPLE_ROWS
    hm_spec = pl.BlockSpec((NSA_HEADS, R, NSA_DH), lambda b, p, pt: (0, b, 0))
    row_spec = pl.BlockSpec((R, KV_ROW), lambda b, p, pt: (b, 0))
    gs = pltpu.PrefetchScalarGridSpec(
        num_scalar_prefetch=1,
        grid=(batch, n_pages),
        in_specs=[hm_spec,
                  pl.BlockSpec((None, NSA_KV_HEADS, R, SEL_W), lambda b, p, pt: (b, 0, 0, 0)),
                  pl.BlockSpec((None, None, PAGE_SIZE, KV_ROW), lambda b, p, pt: (layer, pt[b, p], 0, 0)),
                  row_spec,
                  pl.BlockSpec((None, None, WINDOW, KV_ROW), lambda b, p, pt: (layer, b, 0, 0)),
                  row_spec,
                  pl.BlockSpec((R, LANES), lambda b, p, pt: (b, COL_SM // LANES)),
                  hm_spec],
        out_specs=hm_spec,
        scratch_shapes=[pltpu.VMEM((NSA_KV_HEADS, G * R, 1), F32),
                        pltpu.VMEM((NSA_KV_HEADS, G * R, 1), F32),
                        pltpu.VMEM((NSA_KV_HEADS, G * R, NSA_DH), F32)],
    )
    kern = functools.partial(_sample_attn_kernel, n_pages=n_pages)
    return pl.pallas_call(
        kern,
        grid_spec=gs,
        out_shape=jax.ShapeDtypeStruct((NSA_HEADS, batch * R, NSA_DH), F32),
        compiler_params=_cparams(("parallel", "arbitrary")),
        name="sample_attn",
    )(page_table, q_hm, selm, cache_sel, new_sel, cache_win, new_win, proj, o_cmp_hm)


def _pack_w_in(w_in):
    segs = [w_in[..., lo:hi] for lo, hi in _SRC_SEGMENTS]
    used = sum(hi - lo for lo, hi in _SRC_SEGMENTS)
    segs.append(jnp.zeros(w_in.shape[:-1] + (N_PROJ - used,), w_in.dtype))
    return jnp.concatenate(segs, axis=-1).astype(BF16)


def _layer_prompt(x, lw, tabs, batch, T):
    proj = norm_matmul(x, lw['n0'], lw['w_in'], 512, 1024)
    o_a, gla_state = gla(proj, lw['wa2'], lw['ba'], lw['gn'], None, batch, T, GLA_CHUNK, 4, GLA_CHUNK)
    q_hm, cmp_rows, sel_rows, win_rows = nsa_prep(proj, tabs, 512)
    a, b = chunk_sums(cmp_rows, lw['pe'], lw['mix'], 512)
    kvc = compressed_kv(a, b, batch, lw['w1'], lw['w2'], 128)
    o_b = nsa_prompt(q_hm, kvc, sel_rows, win_rows, proj, batch, T, WIN_QBLK)
    x = merge_wo(x, proj, o_a, o_b, lw['n1'], lw['w_o'], 256)
    x = mlp(x, lw['n2'], lw['n3'], lw['w_up'], lw['w_down'], 512, 1024)
    return x, (cmp_rows, sel_rows, win_rows, gla_state)


def _layer_sample(x, lw, tabs, layer, batch, cache_cmp, cache_sel, cache_win, s0, page_table):
    R = SAMPLE_ROWS
    m = batch * R
    proj = norm_matmul(x, lw['n0'], lw['w_in'], m, 1024)
    o_a, gla_state = gla(proj, lw['wa2'], lw['ba'], lw['gn'], s0, batch, R, R, 1, 4)
    q_hm, cmp_rows, sel_rows, win_rows = nsa_prep(proj, tabs, m)
    a, b = chunk_sums_paged(cache_cmp, layer, page_table, lw['pe'], lw['mix'])
    kvc = compressed_kv(a, b, batch, lw['w1'], lw['w2'], 1024)
    n_slc = -(-(PAST_LEN + 4) // SEL_LEN)
    o_cmp, selm = sample_cmp_select(q_hm, kvc, batch, n_slc)
    o_b = sample_attn(q_hm, selm, cache_sel, layer, page_table, sel_rows, cache_win, win_rows, proj, o_cmp)
    x = merge_wo(x, proj, o_a, o_b, lw['n1'], lw['w_o'], m)
    x = mlp(x, lw['n2'], lw['n3'], lw['w_up'], lw['w_down'], m, 1024)
    return x, (cmp_rows, sel_rows, win_rows, gla_state)


def kernel(x_prompt, x_sample, cache_cmp_kv, cache_sel_kv, cache_win_kv, state_gla, page_table,
           w_in, gla_wa2, gla_ba, gla_norm, cmp_pe, cmp_mix, cmp_w1, cmp_w2, w_o, norms, w_up, w_down):
    bp, T, d = x_prompt.shape
    bs, ss, _ = x_sample.shape
    R = SAMPLE_ROWS
    depth = w_in.shape[0]
    n_pool = cache_cmp_kv.shape[1]
    wb = cache_win_kv.shape[2]
    assert wb == WINDOW and T % 512 == 0 and ss <= R

    w_in_p = _pack_w_in(w_in)
    w_o_b, w_up_b, w_down_b = w_o.astype(BF16), w_up.astype(BF16), w_down.astype(BF16)
    cache_cmp = cache_cmp_kv.reshape(depth, n_pool, PAGE_SIZE, KV_ROW)
    cache_sel = cache_sel_kv.reshape(depth, n_pool, PAGE_SIZE, KV_ROW)
    cache_win = cache_win_kv.reshape(depth, bs, wb, KV_ROW)

    tabs_p = rope_tables(jnp.arange(T, dtype=jnp.int32))
    tabs_s = tuple(jnp.tile(t, (bs, 1)) for t in rope_tables(PAST_LEN + jnp.arange(R, dtype=jnp.int32)))

    y_p = x_prompt.reshape(bp * T, d)
    y_s = jnp.pad(x_sample, ((0, 0), (0, R - ss), (0, 0))).reshape(bs * R, d)
    st_p, st_s = [], []
    for layer in range(depth):
        lw = {'w_in': w_in_p[layer], 'wa2': gla_wa2[layer], 'ba': gla_ba[layer][None, :],
              'gn': gla_norm[layer][None, :], 'pe': cmp_pe[layer], 'mix': cmp_mix[layer],
              'w1': cmp_w1[layer], 'w2': cmp_w2[layer], 'w_o': w_o_b[layer],
              'n0': norms[layer, 0][None, :], 'n1': norms[layer, 1][None, :],
              'n2': norms[layer, 2][None, :], 'n3': norms[layer, 3][None, :],
              'w_up': w_up_b[layer], 'w_down': w_down_b[layer]}
        y_p, sp = _layer_prompt(y_p, lw, tabs_p, bp, T)
        y_s, s_s = _layer_sample(y_s, lw, tabs_s, layer, bs, cache_cmp, cache_sel, cache_win,
                                 state_gla[layer], page_table)
        st_p.append(sp)
        st_s.append(s_s)

    kv_shape = (2, NSA_KV_HEADS, NSA_DH)

    def rows_p(i):
        return jnp.stack([s[i].reshape(bp, T, *kv_shape) for s in st_p])

    def rows_s(i):
        return jnp.stack([s[i].reshape(bs, R, *kv_shape)[:, :ss] for s in st_s])

    win_p = rows_p(2)[:, :, T - min(WINDOW, T):]
    win_s = jnp.concatenate([cache_win_kv[:, :, ss:], rows_s(2)], axis=2)
    return (y_p.reshape(bp, T, d), y_s.reshape(bs, R, d)[:, :ss],
            rows_p(0), rows_s(0), rows_p(1), rows_s(1), win_p, win_s,
            jnp.stack([s[3] for s in st_p]), jnp.stack([s[3] for s in st_s]))
```

```python
import functools

import jax
import jax.numpy as jnp
import numpy as np
from jax import lax
from jax.experimental import pallas as pl
from jax.experimental.pallas import tpu as pltpu

F32 = jnp.float32
BF16 = jnp.bfloat16
HI = lax.Precision.HIGHEST

D_MODEL = 2048
DEPTH = 4
PAST_LEN = 16384
PAGE_SIZE = 128

GLA_HEADS = 4
GLA_DK = 256
GLA_DV = 512
GLA_GATE_RANK = 16
GLA_TAU = 16.0
GLA_CHUNK = 64

NSA_HEADS = 16
NSA_KV_HEADS = 2
NSA_DH = 128
NSA_GROUP = NSA_HEADS // NSA_KV_HEADS
CMP_LEN = 32
CMP_STRIDE = 16
CMP_HIDDEN = 256
SEL_LEN = 64
SEL_SHIFT = 6
SEL_TOPN = 16
WINDOW = 512
WIN_QBLK = 128

ROPE_THETA = 500000.0
ROPE_DIM = 32
ROPE_HALF = 16
MLP_HIDDEN = 4 * D_MODEL
NORM_EPS = 1e-6
NEG_INF = -1e30
FORCED_SCORE = 1e6
KV_PARTS = 2 * NSA_KV_HEADS
KV_ROW = KV_PARTS * NSA_DH
PAGES_PER_STEP = 8

VMEM_LIMIT_BYTES = 56 * 1024 * 1024
LANES = 128

COL_MG = 0
COL_GV = 4096
COL_GG = 6144
COL_NQ = 8192
COL_GQ = 10240
COL_GK = 11264
COL_NKV = 12288
COL_SM = 13824
N_PROJ = 14336
SM_GATE0 = GLA_GATE_RANK

_SRC_SEGMENTS = ((9792, 13888), (2048, 4096), (4096, 6144), (6160, 8208), (0, 1024), (1024, 2048),
                 (8208, 9744), (6144, 6160), (9744, 9792))

SAMPLE_ROWS = 8


def _cparams(sem):
    return pltpu.CompilerParams(dimension_semantics=sem, vmem_limit_bytes=VMEM_LIMIT_BYTES)


def _rms(x, g):
    return x * lax.rsqrt(jnp.mean(x * x, axis=-1, keepdims=True) + NORM_EPS) * g


def _dot(a, b, precision=None):
    return jnp.dot(a, b, preferred_element_type=F32, precision=precision)


def _dot_nt(a, b, precision=None):
    return lax.dot_general(a, b, (((1,), (1,)), ((), ())), preferred_element_type=F32, precision=precision)


def _dot_tn(a, b, precision=None):
    return lax.dot_general(a, b, (((0,), (0,)), ((), ())), preferred_element_type=F32, precision=precision)


def _split_bf16(x):
    hi = x.astype(BF16)
    return hi, (x - hi.astype(F32)).astype(BF16)


def _dot_nt_3x(a, b):
    ah, al = _split_bf16(a)
    bh, bl = _split_bf16(b)
    return (_dot_nt(jnp.concatenate([ah, al], axis=1), jnp.concatenate([bh, bh], axis=1))
            + _dot_nt(ah, bl))


def _iota(shape, dim):
    return lax.broadcasted_iota(jnp.int32, shape, dim)


def _masked_softmax(s, valid):
    s = jnp.where(valid, s, NEG_INF)
    m = jnp.max(s, axis=-1, keepdims=True)
    p = jnp.where(valid, jnp.exp(s - m), 0.0)
    return p / jnp.maximum(jnp.sum(p, axis=-1, keepdims=True), 1e-30)


def _norm_matmul_kernel(x_ref, g_ref, w_ref, o_ref, h_ref):
    @pl.when(pl.program_id(1) == 0)
    def _():
        h_ref[...] = _rms(x_ref[...], g_ref[...]).astype(BF16)

    o_ref[...] = _dot(h_ref[...], w_ref[...])


def norm_matmul(x, g, w, tm, tn):
    m, k = x.shape
    n = w.shape[1]
    return pl.pallas_call(
        _norm_matmul_kernel,
        grid=(m // tm, n // tn),
        in_specs=[pl.BlockSpec((tm, k), lambda i, j: (i, 0)),
                  pl.BlockSpec((1, k), lambda i, j: (0, 0)),
                  pl.BlockSpec((k, tn), lambda i, j: (0, j))],
        out_specs=pl.BlockSpec((tm, tn), lambda i, j: (i, j)),
        out_shape=jax.ShapeDtypeStruct((m, n), F32),
        scratch_shapes=[pltpu.VMEM((tm, k), BF16)],
        compiler_params=_cparams(("parallel", "arbitrary")),
        name="norm_matmul",
    )(x, g, w)


def _merge_wo_kernel(x_ref, mg0_ref, mg1_ref, oa_ref, ob_ref, g_ref, wo_ref, o_ref):
    ob = jnp.concatenate([ob_ref[h] for h in range(NSA_HEADS)], axis=1)
    a = jax.nn.sigmoid(mg0_ref[...]) * oa_ref[...] + jax.nn.sigmoid(mg1_ref[...]) * ob
    m = _dot(a.astype(BF16), wo_ref[...])
    o_ref[...] = x_ref[...] + _rms(m, g_ref[...])


def merge_wo(x, proj, o_a, o_b_hm, g, wo, tm):
    m, d = x.shape
    return pl.pallas_call(
        _merge_wo_kernel,
        grid=(m // tm,),
        in_specs=[pl.BlockSpec((tm, d), lambda i: (i, 0)),
                  pl.BlockSpec((tm, d), lambda i: (i, COL_MG // D_MODEL)),
                  pl.BlockSpec((tm, d), lambda i: (i, COL_MG // D_MODEL + 1)),
                  pl.BlockSpec((tm, d), lambda i: (i, 0)),
                  pl.BlockSpec((NSA_HEADS, tm, NSA_DH), lambda i: (0, i, 0)),
                  pl.BlockSpec((1, d), lambda i: (0, 0)),
                  pl.BlockSpec((d, d), lambda i: (0, 0))],
        out_specs=pl.BlockSpec((tm, d), lambda i: (i, 0)),
        out_shape=jax.ShapeDtypeStruct((m, d), F32),
        compiler_params=_cparams(("parallel",)),
        name="merge_wo",
    )(x, proj, proj, o_a, o_b_hm, g, wo)


def _mlp_kernel(x_ref, g2_ref, g3_ref, wu_ref, wd_ref, o_ref, h_ref, acc_ref):
    j = pl.program_id(1)

    @pl.when(j == 0)
    def _():
        h_ref[...] = _rms(x_ref[...], g2_ref[...]).astype(BF16)
        acc_ref[...] = jnp.zeros_like(acc_ref)

    u = _dot(h_ref[...], wu_ref[...])
    u = jnp.square(jnp.maximum(u, 0.0)).astype(BF16)
    acc_ref[...] += _dot(u, wd_ref[...])

    @pl.when(j == pl.num_programs(1) - 1)
    def _():
        o_ref[...] = x_ref[...] + _rms(acc_ref[...], g3_ref[...])


def mlp(x, g2, g3, wu, wd, tm, th):
    m, d = x.shape
    hid = wu.shape[1]
    return pl.pallas_call(
        _mlp_kernel,
        grid=(m // tm, hid // th),
        in_specs=[pl.BlockSpec((tm, d), lambda i, j: (i, 0)),
                  pl.BlockSpec((1, d), lambda i, j: (0, 0)),
                  pl.BlockSpec((1, d), lambda i, j: (0, 0)),
                  pl.BlockSpec((d, th), lambda i, j: (0, j)),
                  pl.BlockSpec((th, d), lambda i, j: (j, 0))],
        out_specs=pl.BlockSpec((tm, d), lambda i, j: (i, 0)),
        out_shape=jax.ShapeDtypeStruct((m, d), F32),
        scratch_shapes=[pltpu.VMEM((tm, d), BF16), pltpu.VMEM((tm, d), F32)],
        compiler_params=_cparams(("parallel", "arbitrary")),
        name="mlp",
    )(x, g2, g3, wu, wd)


def _gla_kernel(q_ref, k_ref, v_ref, gg_ref, sm_ref, wa2_ref, ba_ref, gn_ref, s0_ref,
                o_ref, sout_ref, st_ref, *, rows_in, n_inner, n_valid, has_s0):
    c = pl.program_id(1)
    C = GLA_CHUNK

    @pl.when(c == 0)
    def _():
        for h in range(GLA_HEADS):
            if has_s0:
                st_ref[h] = s0_ref[h].T
            else:
                st_ref[h] = jnp.zeros((GLA_DV, GLA_DK), F32)

    row = _iota((C, 1), 0)
    col = _iota((1, C), 1)
    causal = row >= col
    tril = causal.astype(BF16)
    live = row < n_valid

    def pad(x):
        if rows_in == C:
            return x
        return jnp.concatenate([x, jnp.zeros((C - rows_in, x.shape[1]), x.dtype)], axis=0)

    for ci in range(n_inner):
        rs = pl.ds(ci * rows_in, rows_in)
        ga_h, ga_l = _split_bf16(pad(sm_ref[rs, :])[:, :GLA_GATE_RANK])
        for h in range(GLA_HEADS):
            kq = slice(h * GLA_DK, (h + 1) * GLA_DK)
            kv = slice(h * GLA_DV, (h + 1) * GLA_DV)
            q = pad(q_ref[rs, kq]) * (GLA_DK ** -0.5)
            k = pad(k_ref[rs, kq])
            v = pad(v_ref[rs, kv])
            w_h, w_l = _split_bf16(wa2_ref[:, kq])
            z = _dot(ga_h, w_h) + _dot(ga_l, w_h) + _dot(ga_h, w_l) + ba_ref[:, kq]
            log_a = (jnp.minimum(z, 0.0) - jnp.log1p(jnp.exp(-jnp.abs(z)))) / GLA_TAU
            if n_valid < C:
                log_a = jnp.where(live, log_a, 0.0)
                k = jnp.where(live, k, 0.0)
            la_h, la_l = _split_bf16(log_a)
            b = _dot(tril, la_h) + _dot(tril, la_l)
            q_dec = q * jnp.exp(b)
            k_dec = k * jnp.exp(-b)
            st = st_ref[h]
            o = _dot_nt(q_dec.astype(BF16), st.astype(BF16))
            att = jnp.where(causal, _dot_nt(q_dec.astype(BF16), k_dec.astype(BF16)), 0.0)
            vb = v.astype(BF16)
            o = o + _dot(att.astype(BF16), vb)
            b_last = b[C - 1:C, :]
            k_end = k * jnp.exp(b_last - b)
            st_ref[h] = st * jnp.exp(b_last) + _dot_tn(vb, k_end.astype(BF16))
            o_n = _rms(o, gn_ref[...])
            gg = gg_ref[rs, kv]
            o_ref[rs, kv] = o_n[:rows_in] * (gg * jax.nn.sigmoid(gg))

    @pl.when(c == pl.num_programs(1) - 1)
    def _():
        for h in range(GLA_HEADS):
            sout_ref[h] = st_ref[h].T


def gla(proj, wa2, ba, gn, s0, batch, rows_per_batch, rows_in, n_inner, n_valid):
    m = proj.shape[0]
    r = rows_in * n_inner
    n_steps = rows_per_batch // r
    has_s0 = s0 is not None
    if s0 is None:
        s0 = jnp.zeros((1, GLA_HEADS, GLA_DK, GLA_DV), F32)
        s0_map = lambda b, c: (0, 0, 0, 0)
    else:
        s0_map = lambda b, c: (b, 0, 0, 0)
    wk, wv = GLA_HEADS * GLA_DK, GLA_HEADS * GLA_DV
    rowmap = lambda off: (lambda b, c: (b * n_steps + c, off))
    state_blk = (None, GLA_HEADS, GLA_DK, GLA_DV)
    kern = functools.partial(_gla_kernel, rows_in=rows_in, n_inner=n_inner, n_valid=n_valid, has_s0=has_s0)
    return pl.pallas_call(
        kern,
        grid=(batch, n_steps),
        in_specs=[pl.BlockSpec((r, wk), rowmap(COL_GQ // wk)),
                  pl.BlockSpec((r, wk), rowmap(COL_GK // wk)),
                  pl.BlockSpec((r, wv), rowmap(COL_GV // wv)),
                  pl.BlockSpec((r, wv), rowmap(COL_GG // wv)),
                  pl.BlockSpec((r, LANES), rowmap(COL_SM // LANES)),
                  pl.BlockSpec((GLA_GATE_RANK, wk), lambda b, c: (0, 0)),
                  pl.BlockSpec((1, wk), lambda b, c: (0, 0)),
                  pl.BlockSpec((1, GLA_DV), lambda b, c: (0, 0)),
                  pl.BlockSpec(state_blk, s0_map)],
        out_specs=[pl.BlockSpec((r, wv), rowmap(0)),
                   pl.BlockSpec(state_blk, lambda b, c: (b, 0, 0, 0))],
        out_shape=[jax.ShapeDtypeStruct((m, D_MODEL), F32),
                   jax.ShapeDtypeStruct((batch, GLA_HEADS, GLA_DK, GLA_DV), F32)],
        scratch_shapes=[pltpu.VMEM((GLA_HEADS, GLA_DV, GLA_DK), F32)],
        compiler_params=_cparams(("parallel", "arbitrary")),
        name="gla",
    )(proj, proj, proj, proj, proj, wa2, ba, gn, s0)


def _rope(x, c, sa, sb):
    return x * c + pltpu.roll(x, LANES - ROPE_HALF, 1) * sa + pltpu.roll(x, ROPE_HALF, 1) * sb


def _nsa_prep_kernel(nq_ref, nkv_ref, c_ref, sa_ref, sb_ref, q_ref, cmp_ref, sel_ref, win_ref):
    c, sa, sb = c_ref[...], sa_ref[...], sb_ref[...]
    for h in range(NSA_HEADS):
        q_ref[h] = _rope(nq_ref[:, h * NSA_DH:(h + 1) * NSA_DH], c, sa, sb)
    for s, out in enumerate((cmp_ref, sel_ref, win_ref)):
        base = s * KV_ROW
        for hh in range(NSA_KV_HEADS):
            lo = hh * NSA_DH
            out[:, lo:lo + NSA_DH] = _rope(nkv_ref[:, base + lo:base + lo + NSA_DH], c, sa, sb)
        half = NSA_KV_HEADS * NSA_DH
        out[:, half:] = nkv_ref[:, base + half:base + KV_ROW]


def nsa_prep(proj, tabs, tr):
    m = proj.shape[0]
    n_tab = tabs[0].shape[0] // tr
    tab_spec = pl.BlockSpec((tr, LANES), lambda i: (i % n_tab, 0))
    nkv_w = 3 * KV_ROW
    return pl.pallas_call(
        _nsa_prep_kernel,
        grid=(m // tr,),
        in_specs=[pl.BlockSpec((tr, D_MODEL), lambda i: (i, COL_NQ // D_MODEL)),
                  pl.BlockSpec((tr, nkv_w), lambda i: (i, COL_NKV // nkv_w)),
                  tab_spec, tab_spec, tab_spec],
        out_specs=[pl.BlockSpec((NSA_HEADS, tr, NSA_DH), lambda i: (0, i, 0)),
                   pl.BlockSpec((tr, KV_ROW), lambda i: (i, 0)),
                   pl.BlockSpec((tr, KV_ROW), lambda i: (i, 0)),
                   pl.BlockSpec((tr, KV_ROW), lambda i: (i, 0))],
        out_shape=[jax.ShapeDtypeStruct((NSA_HEADS, m, NSA_DH), F32),
                   jax.ShapeDtypeStruct((m, KV_ROW), F32),
                   jax.ShapeDtypeStruct((m, KV_ROW), F32),
                   jax.ShapeDtypeStruct((m, KV_ROW), F32)],
        compiler_params=_cparams(("parallel",)),
        name="nsa_prep",
    )(proj, proj, *tabs)


def rope_tables(pos):
    inv = 1.0 / (ROPE_THETA ** (jnp.arange(ROPE_HALF, dtype=F32) / ROPE_HALF))
    ang = pos.astype(F32)[:, None] * inv[None, :]
    cos, sin = jnp.cos(ang), jnp.sin(ang)
    n = pos.shape[0]
    rest = LANES - ROPE_DIM
    c = jnp.concatenate([cos, cos, jnp.ones((n, rest), F32)], axis=1)
    sa = jnp.concatenate([-sin, jnp.zeros((n, LANES - ROPE_HALF), F32)], axis=1)
    sb = jnp.concatenate([jnp.zeros((n, ROPE_HALF), F32), sin, jnp.zeros((n, rest), F32)], axis=1)
    return c, sa, sb


def _chunk_sums_body(x, pe_ref, mix_ref, a_ref, b_ref):
    r = x.shape[0]
    nch = r // CMP_STRIDE
    for kv in range(2):
        pe, mix = pe_ref[kv], mix_ref[kv]
        for hh in range(NSA_KV_HEADS):
            lo = (kv * NSA_KV_HEADS + hh) * NSA_DH
            xs = x[:, lo:lo + NSA_DH].reshape(nch, CMP_STRIDE, NSA_DH)
            a_ref[:, lo:lo + NSA_DH] = jnp.sum((xs + pe[:CMP_STRIDE]) * mix[:CMP_STRIDE], axis=1)
            b_ref[:, lo:lo + NSA_DH] = jnp.sum((xs + pe[CMP_STRIDE:]) * mix[CMP_STRIDE:], axis=1)


def _chunk_sums_kernel(x_ref, pe_ref, mix_ref, a_ref, b_ref):
    _chunk_sums_body(x_ref[...], pe_ref, mix_ref, a_ref, b_ref)


def _page_specs(layer):
    def spec(i):
        return pl.BlockSpec((None, None, PAGE_SIZE * KV_PARTS, NSA_DH),
                            lambda b, s, pt: (layer, pt[b, s * PAGES_PER_STEP + i], 0, 0))
    return [spec(i) for i in range(PAGES_PER_STEP)]


def _page_part(page_ref, part):
    return page_ref[pl.ds(part, PAGE_SIZE, stride=KV_PARTS), :]


def _chunk_sums_paged_kernel(pt_ref, *refs):
    page_refs, (pe_ref, mix_ref, a_ref, b_ref) = refs[:PAGES_PER_STEP], refs[PAGES_PER_STEP:]
    nch = PAGE_SIZE // CMP_STRIDE
    for i, page_ref in enumerate(page_refs):
        rows = slice(i * nch, (i + 1) * nch)
        for kv in range(2):
            pe, mix = pe_ref[kv], mix_ref[kv]
            for hh in range(NSA_KV_HEADS):
                part = kv * NSA_KV_HEADS + hh
                lo = part * NSA_DH
                xs = _page_part(page_ref, part).reshape(nch, CMP_STRIDE, NSA_DH)
                a_ref[rows, lo:lo + NSA_DH] = jnp.sum((xs + pe[:CMP_STRIDE]) * mix[:CMP_STRIDE], axis=1)
                b_ref[rows, lo:lo + NSA_DH] = jnp.sum((xs + pe[CMP_STRIDE:]) * mix[CMP_STRIDE:], axis=1)


def chunk_sums(rows, pe, mix, tr):
    m = rows.shape[0]
    nch = tr // CMP_STRIDE
    full = lambda i: (0, 0, 0)
    return pl.pallas_call(
        _chunk_sums_kernel,
        grid=(m // tr,),
        in_specs=[pl.BlockSpec((tr, KV_ROW), lambda i: (i, 0)),
                  pl.BlockSpec((2, CMP_LEN, NSA_DH), full),
                  pl.BlockSpec((2, CMP_LEN, NSA_DH), full)],
        out_specs=[pl.BlockSpec((nch, KV_ROW), lambda i: (i, 0))] * 2,
        out_shape=[jax.ShapeDtypeStruct((m // CMP_STRIDE, KV_ROW), F32)] * 2,
        compiler_params=_cparams(("parallel",)),
        name="chunk_sums",
    )(rows, pe, mix)


def chunk_sums_paged(cache, layer, page_table, pe, mix):
    batch, n_pages = page_table.shape
    nch = PAGES_PER_STEP * PAGE_SIZE // CMP_STRIDE
    n_steps = n_pages // PAGES_PER_STEP
    full = lambda b, s, pt: (0, 0, 0)
    gs = pltpu.PrefetchScalarGridSpec(
        num_scalar_prefetch=1,
        grid=(batch, n_steps),
        in_specs=_page_specs(layer) + [pl.BlockSpec((2, CMP_LEN, NSA_DH), full),
                                       pl.BlockSpec((2, CMP_LEN, NSA_DH), full)],
        out_specs=[pl.BlockSpec((nch, KV_ROW), lambda b, s, pt: (b * n_steps + s, 0))] * 2,
    )
    return pl.pallas_call(
        _chunk_sums_paged_kernel,
        grid_spec=gs,
        out_shape=[jax.ShapeDtypeStruct((batch * n_steps * nch, KV_ROW), F32)] * 2,
        compiler_params=_cparams(("parallel", "arbitrary")),
        name="chunk_sums_paged",
    )(page_table, *([cache] * PAGES_PER_STEP), pe, mix)


def _cmp_mlp_kernel(a_ref, b_ref, w1_ref, w2_ref, o_ref):
    h = a_ref[...] + b_ref[...]
    for kv in range(2):
        for hh in range(NSA_KV_HEADS):
            lo = (kv * NSA_KV_HEADS + hh) * NSA_DH
            y = jax.nn.gelu(_dot(h[:, lo:lo + NSA_DH], w1_ref[kv], HI))
            o_ref[:, lo:lo + NSA_DH] = _dot(y, w2_ref[kv], HI)


def cmp_mlp(a, b_shift, w1, w2, tr):
    m = a.shape[0]
    return pl.pallas_call(
        _cmp_mlp_kernel,
        grid=(m // tr,),
        in_specs=[pl.BlockSpec((tr, KV_ROW), lambda i: (i, 0)),
                  pl.BlockSpec((tr, KV_ROW), lambda i: (i, 0)),
                  pl.BlockSpec((2, NSA_DH, CMP_HIDDEN), lambda i: (0, 0, 0)),
                  pl.BlockSpec((2, CMP_HIDDEN, NSA_DH), lambda i: (0, 0, 0))],
        out_specs=pl.BlockSpec((tr, KV_ROW), lambda i: (i, 0)),
        out_shape=jax.ShapeDtypeStruct((m, KV_ROW), F32),
        compiler_params=_cparams(("parallel",)),
        name="cmp_mlp",
    )(a, b_shift, w1, w2)


def compressed_kv(a, b, batch, w1, w2, tr):
    nch = a.shape[0] // batch
    b3 = b.reshape(batch, nch, KV_ROW)
    b_shift = jnp.concatenate([b3[:, 1:], jnp.zeros((batch, 1, KV_ROW), F32)], axis=1).reshape(batch * nch, KV_ROW)
    return cmp_mlp(a, b_shift, w1, w2, tr)


def _select_topn(imp, n_top):
    j = _iota(imp.shape, 1)
    big = jnp.int32(imp.shape[1])
    sel = jnp.zeros(imp.shape, F32)
    for _ in range(n_top):
        m = jnp.max(imp, axis=-1, keepdims=True)
        idx = jnp.min(jnp.where(imp == m, j, big), axis=-1, keepdims=True)
        hit = j == idx
        sel = jnp.where(hit, 1.0, sel)
        imp = jnp.where(hit, NEG_INF, imp)
    return sel


def _importance(p_grp, qpos, n_cmp_pad, n_slc, width):
    n_r = _iota((n_cmp_pad, 1), 0) * CMP_STRIDE
    j_c = _iota((1, width), 1)
    overlap = ((n_r < (j_c + 1) * SEL_LEN) & (n_r + CMP_LEN > j_c * SEL_LEN)).astype(F32)
    imp = _dot(p_grp, overlap, HI)
    cur = qpos >> SEL_SHIFT
    forced = (j_c == 0) | (j_c == cur) | (j_c == cur - 1)
    imp = jnp.where(forced, FORCED_SCORE, imp)
    imp = jnp.where(j_c > cur, -1.0, imp)
    return jnp.where(j_c >= n_slc, -2.0, imp)


LOG2E = 1.4426950408889634


def _nsa_prompt_kernel(q_ref, kc_ref, vc_ref, ks_ref, vs_ref, kw_ref, vw_ref, sm_ref, o_ref,
                       qb_ref, s_ref, p_ref, m_ref, l_ref, a_ref, acc_ref, *, Q, T, TK):
    G = NSA_GROUP
    hk = pl.program_id(1)
    qi = pl.program_id(2)
    q0 = qi * Q
    scale = NSA_DH ** -0.5
    n_cmp_pad = kc_ref.shape[0]
    n_slc = T // SEL_LEN
    qpos = q0 + _iota((Q, 1), 0)
    qs = q_ref[...].reshape(G * Q, NSA_DH)
    qb_ref[...] = (qs * (scale * LOG2E)).astype(BF16)

    ng = jax.nn.sigmoid(sm_ref[...])

    def gate(g, c3):
        lane0 = SM_GATE0 + 3 * g + c3
        lane1 = lane0 + 3 * G
        return jnp.where(hk == 0, ng[:, lane0:lane0 + 1], ng[:, lane1:lane1 + 1])

    def softmax_tile(kb, vb, bias, width):
        s_ref[:, :width] = _dot_nt(qb_ref[...], kb)
        for g in range(G):
            rows = pl.ds(g * Q, Q)
            s = s_ref[rows, :width] + bias
            m_old = m_ref[rows, :]
            m_new = jnp.maximum(m_old, jnp.max(s, axis=-1, keepdims=True))
            alpha = jnp.exp2(m_old - m_new)
            p = jnp.exp2(s - jnp.tile(m_new, (1, width // LANES)))
            l_ref[rows, :] = alpha * l_ref[rows, :] + jnp.sum(p, axis=-1, keepdims=True)
            a_ref[rows, :] = alpha
            m_ref[rows, :] = m_new
            p_ref[rows, :width] = p.astype(BF16)
        return _dot(p_ref[:, :width], vb)

    def reset():
        m_ref[...] = jnp.full_like(m_ref, NEG_INF)
        l_ref[...] = jnp.zeros_like(l_ref)

    cmp_valid = _iota((1, n_cmp_pad), 1) * CMP_STRIDE + (CMP_LEN - 1) <= qpos
    p_grp = jnp.zeros((Q, n_cmp_pad), F32)
    s_cmp = _dot_nt_3x(qs, kc_ref[...]) * scale
    vcb = vc_ref[...].astype(BF16)
    for g in range(G):
        p = _masked_softmax(s_cmp[g * Q:(g + 1) * Q], cmp_valid)
        p_grp = p_grp + p
        o_ref[g] = gate(g, 0) * _dot(p.astype(BF16), vcb)

    ks0 = pl.multiple_of(jnp.maximum(qi - WINDOW // WIN_QBLK, 0) * WIN_QBLK, WIN_QBLK)
    wlen = WINDOW + WIN_QBLK
    wpos = ks0 + _iota((1, wlen), 1)
    win_bias = jnp.where((wpos <= qpos) & (wpos > qpos - WINDOW), 0.0, NEG_INF)
    reset()
    pv = softmax_tile(kw_ref[pl.ds(ks0, wlen), :].astype(BF16), vw_ref[pl.ds(ks0, wlen), :].astype(BF16),
                      win_bias, wlen)
    for g in range(G):
        rows = pl.ds(g * Q, Q)
        o_ref[g] = o_ref[g] + gate(g, 2) * (pv[g * Q:(g + 1) * Q] / jnp.maximum(l_ref[rows, :], 1e-30))

    n_top = min(SEL_TOPN, n_slc)
    j_r = _iota((n_slc, 1), 0)
    n_c = _iota((1, n_cmp_pad), 1) * CMP_STRIDE
    overlap_t = ((n_c < (j_r + 1) * SEL_LEN) & (n_c + CMP_LEN > j_r * SEL_LEN)).astype(F32)
    imp = _dot_nt(overlap_t, p_grp, HI)
    cur = (q0 + _iota((1, Q), 1)) >> SEL_SHIFT
    imp = jnp.where((j_r == 0) | (j_r == cur) | (j_r == cur - 1), FORCED_SCORE, imp)
    imp = jnp.where(j_r > cur, -1.0, imp)
    rank = jnp.zeros((n_slc, Q), F32)
    for jp in range(n_slc):
        other = imp[jp:jp + 1, :]
        beats = (other > imp) | ((other == imp) & (j_r > jp))
        rank = rank + jnp.where(beats, 1.0, 0.0)
    sel_t = jnp.where(rank < n_top, 1.0, 0.0).astype(BF16)

    reset()
    acc_ref[...] = jnp.zeros_like(acc_ref)

    def key_tile(kt, carry):
        k0 = pl.multiple_of(kt * TK, TK)
        kpos = k0 + _iota((1, TK), 1)
        expand = (j_r == (kpos >> SEL_SHIFT)).astype(BF16)
        keymask = _dot_tn(sel_t, expand)
        bias = jnp.where((keymask > 0.5) & (kpos <= qpos), 0.0, NEG_INF)
        pv = softmax_tile(ks_ref[pl.ds(k0, TK), :].astype(BF16), vs_ref[pl.ds(k0, TK), :].astype(BF16), bias, TK)
        acc_ref[...] = a_ref[...] * acc_ref[...] + pv
        return carry

    lax.fori_loop(0, (q0 + Q + TK - 1) // TK, key_tile, 0)

    for g in range(G):
        rows = pl.ds(g * Q, Q)
        o_ref[g] = o_ref[g] + gate(g, 1) * (acc_ref[rows, :] / jnp.maximum(l_ref[rows, :], 1e-30))


def nsa_prompt(q_hm, kvc, sel_rows, win_rows, proj, batch, T, Q):
    m = batch * T
    nq = T // Q
    n_cmp_pad = kvc.shape[0] // batch
    G = NSA_GROUP
    kern = functools.partial(_nsa_prompt_kernel, Q=Q, T=T, TK=512)
    kcol = lambda off: (lambda b, hk, qi: (b, off + hk))
    return pl.pallas_call(
        kern,
        grid=(batch, NSA_KV_HEADS, nq),
        in_specs=[pl.BlockSpec((G, Q, NSA_DH), lambda b, hk, qi: (hk, b * nq + qi, 0)),
                  pl.BlockSpec((n_cmp_pad, NSA_DH), kcol(0)),
                  pl.BlockSpec((n_cmp_pad, NSA_DH), kcol(NSA_KV_HEADS)),
                  pl.BlockSpec((T, NSA_DH), kcol(0)),
                  pl.BlockSpec((T, NSA_DH), kcol(NSA_KV_HEADS)),
                  pl.BlockSpec((T, NSA_DH), kcol(0)),
                  pl.BlockSpec((T, NSA_DH), kcol(NSA_KV_HEADS)),
                  pl.BlockSpec((Q, LANES), lambda b, hk, qi: (b * nq + qi, COL_SM // LANES))],
        out_specs=pl.BlockSpec((G, Q, NSA_DH), lambda b, hk, qi: (hk, b * nq + qi, 0)),
        out_shape=jax.ShapeDtypeStruct((NSA_HEADS, m, NSA_DH), F32),
        scratch_shapes=[pltpu.VMEM((G * Q, NSA_DH), BF16),
                        pltpu.VMEM((G * Q, WINDOW + WIN_QBLK), F32), pltpu.VMEM((G * Q, WINDOW + WIN_QBLK), BF16),
                        pltpu.VMEM((G * Q, LANES), F32), pltpu.VMEM((G * Q, LANES), F32),
                        pltpu.VMEM((G * Q, LANES), F32), pltpu.VMEM((G * Q, NSA_DH), F32)],
        compiler_params=_cparams(("parallel", "parallel", "arbitrary")),
        name="nsa_prompt",
    )(q_hm, kvc, kvc, sel_rows, sel_rows, win_rows, win_rows, proj)


SEL_W = 384


def _sample_cmp_select_kernel(q_ref, kc_ref, vc_ref, ocmp_ref, selm_ref, *, n_slc):
    G, R = NSA_GROUP, SAMPLE_ROWS
    scale = NSA_DH ** -0.5
    n_cmp_pad = kc_ref.shape[0]
    qs = q_ref[...].reshape(G * R, NSA_DH)
    qpos_s = PAST_LEN + (_iota((G * R, 1), 0) & (R - 1))
    qpos = PAST_LEN + _iota((R, 1), 0)
    s = _dot_nt(qs, kc_ref[...], HI) * scale
    end = _iota((1, n_cmp_pad), 1) * CMP_STRIDE + (CMP_LEN - 1)
    p = _masked_softmax(s, end <= qpos_s)
    o_cmp = _dot(p.astype(BF16), vc_ref[...].astype(BF16))
    for g in range(G):
        ocmp_ref[g] = o_cmp[g * R:(g + 1) * R]
    p_grp = jnp.sum(p.reshape(G, R, n_cmp_pad), axis=0)
    imp = _importance(p_grp, qpos, n_cmp_pad, n_slc, SEL_W)
    selm_ref[...] = _select_topn(imp, min(SEL_TOPN, n_slc))


def sample_cmp_select(q_hm, kvc, batch, n_slc):
    G, R = NSA_GROUP, SAMPLE_ROWS
    n_cmp_pad = kvc.shape[0] // batch
    kern = functools.partial(_sample_cmp_select_kernel, n_slc=n_slc)
    return pl.pallas_call(
        kern,
        grid=(batch, NSA_KV_HEADS),
        in_specs=[pl.BlockSpec((G, R, NSA_DH), lambda b, hk: (hk, b, 0)),
                  pl.BlockSpec((n_cmp_pad, NSA_DH), lambda b, hk: (b, hk)),
                  pl.BlockSpec((n_cmp_pad, NSA_DH), lambda b, hk: (b, NSA_KV_HEADS + hk))],
        out_specs=[pl.BlockSpec((G, R, NSA_DH), lambda b, hk: (hk, b, 0)),
                   pl.BlockSpec((None, None, R, SEL_W), lambda b, hk: (b, hk, 0, 0))],
        out_shape=[jax.ShapeDtypeStruct((NSA_HEADS, batch * R, NSA_DH), F32),
                   jax.ShapeDtypeStruct((batch, NSA_KV_HEADS, R, SEL_W), F32)],
        compiler_params=_cparams(("parallel", "parallel")),
        name="sample_cmp_select",
    )(q_hm, kvc, kvc)


def _sample_attn_kernel(pt_ref, q_ref, selm_ref, *refs, n_steps):
    page_refs = refs[:PAGES_PER_STEP]
    nsel_ref, cwin_ref, nwin_ref, sm_ref, ocmp_ref, o_ref, m_ref, l_ref, acc_ref = refs[PAGES_PER_STEP:]
    G, R = NSA_GROUP, SAMPLE_ROWS
    p = pl.program_id(1)
    scale = NSA_DH ** -0.5
    t_s = _iota((G * R, 1), 0) & (R - 1)
    n_keys = PAGES_PER_STEP * PAGE_SIZE

    @pl.when(p == 0)
    def _():
        m_ref[...] = jnp.full_like(m_ref, NEG_INF)
        l_ref[...] = jnp.zeros_like(l_ref)
        acc_ref[...] = jnp.zeros_like(acc_ref)

    def update(hk, s, valid, v):
        s = jnp.where(valid, s, NEG_INF)
        m_old = m_ref[hk]
        m_new = jnp.maximum(m_old, jnp.max(s, axis=-1, keepdims=True))
        alpha = jnp.exp(m_old - m_new)
        pr = jnp.where(valid, jnp.exp(s - m_new), 0.0)
        l_ref[hk] = alpha * l_ref[hk] + jnp.sum(pr, axis=-1, keepdims=True)
        acc_ref[hk] = alpha * acc_ref[hk] + _dot(pr.astype(BF16), v.astype(BF16))
        m_ref[hk] = m_new

    key = _iota((1, n_keys), 1)
    blk_of_key = (n_keys // SEL_LEN) * p + (key >> SEL_SHIFT)
    expand = (_iota((SEL_W, 1), 0) == blk_of_key).astype(BF16)
    for hk in range(NSA_KV_HEADS):
        qs = q_ref[hk * G:(hk + 1) * G].reshape(G * R, NSA_DH).astype(BF16)
        kpg = jnp.concatenate([_page_part(r, hk).astype(BF16) for r in page_refs], axis=0)
        vpg = jnp.concatenate([_page_part(r, NSA_KV_HEADS + hk).astype(BF16) for r in page_refs], axis=0)
        s = _dot_nt(qs, kpg) * scale
        keymask = _dot(selm_ref[hk].astype(BF16), expand)
        valid = jnp.concatenate([keymask] * G, axis=0) > 0.5
        update(hk, s, valid, vpg)

    def win_part(ref, part):
        return ref[pl.ds(part, WINDOW, stride=KV_PARTS), :]

    @pl.when(p == n_steps - 1)
    def _():
        ng = jax.nn.sigmoid(sm_ref[...])
        new_blk = PAST_LEN // SEL_LEN
        zpad = jnp.zeros((LANES - R, NSA_DH), F32)
        jn = _iota((1, LANES), 1)
        for hk in range(NSA_KV_HEADS):
            qs = q_ref[hk * G:(hk + 1) * G].reshape(G * R, NSA_DH).astype(BF16)
            klo, vlo = hk * NSA_DH, (NSA_KV_HEADS + hk) * NSA_DH
            kn = jnp.concatenate([nsel_ref[:, klo:klo + NSA_DH], zpad], axis=0)
            vn = jnp.concatenate([nsel_ref[:, vlo:vlo + NSA_DH], zpad], axis=0)
            s = _dot_nt(qs, kn.astype(BF16)) * scale
            picked = jnp.concatenate([selm_ref[hk][:, new_blk:new_blk + 1]] * G, axis=0) > 0.5
            update(hk, s, picked & (jn <= t_s) & (jn < R), vn)
            o_sel = acc_ref[hk] / jnp.maximum(l_ref[hk], 1e-30)
            kw = jnp.concatenate([win_part(cwin_ref, hk), nwin_ref[:, klo:klo + NSA_DH], zpad], axis=0)
            vw = jnp.concatenate([win_part(cwin_ref, NSA_KV_HEADS + hk), nwin_ref[:, vlo:vlo + NSA_DH], zpad], axis=0)
            iw = _iota((1, WINDOW + LANES), 1)
            wvalid = ((iw < WINDOW) & (iw > t_s)) | ((iw >= WINDOW) & (iw - WINDOW <= t_s) & (iw - WINDOW < R))
            pw = _masked_softmax(_dot_nt(qs, kw.astype(BF16)) * scale, wvalid)
            o_win = _dot(pw.astype(BF16), vw.astype(BF16))
            for g in range(G):
                h = hk * G + g
                lane = SM_GATE0 + 3 * h
                rows = slice(g * R, (g + 1) * R)
                o_ref[h] = (ng[:, lane:lane + 1] * ocmp_ref[h] + ng[:, lane + 1:lane + 2] * o_sel[rows]
                            + ng[:, lane + 2:lane + 3] * o_win[rows])


def sample_attn(q_hm, selm, cache_sel, layer, page_table, new_sel, cache_win, new_win, proj, o_cmp_hm):
    batch, n_pages = page_table.shape
    n_steps = n_pages // PAGES_PER_STEP
    G, R = NSA_GROUP, SAMPLE_ROWS
    hm_spec = pl.BlockSpec((NSA_HEADS, R, NSA_DH), lambda b, p, pt: (0, b, 0))
    row_spec = pl.BlockSpec((R, KV_ROW), lambda b, p, pt: (b, 0))
    gs = pltpu.PrefetchScalarGridSpec(
        num_scalar_prefetch=1,
        grid=(batch, n_steps),
        in_specs=[hm_spec,
                  pl.BlockSpec((None, NSA_KV_HEADS, R, SEL_W), lambda b, p, pt: (b, 0, 0, 0))]
                 + _page_specs(layer)
                 + [row_spec,
                    pl.BlockSpec((None, None, WINDOW * KV_PARTS, NSA_DH), lambda b, p, pt: (layer, b, 0, 0)),
                    row_spec,
                    pl.BlockSpec((R, LANES), lambda b, p, pt: (b, COL_SM // LANES)),
                    hm_spec],
        out_specs=hm_spec,
        scratch_shapes=[pltpu.VMEM((NSA_KV_HEADS, G * R, 1), F32),
                        pltpu.VMEM((NSA_KV_HEADS, G * R, 1), F32),
                        pltpu.VMEM((NSA_KV_HEADS, G * R, NSA_DH), F32)],
    )
    kern = functools.partial(_sample_attn_kernel, n_steps=n_steps)
    return pl.pallas_call(
        kern,
        grid_spec=gs,
        out_shape=jax.ShapeDtypeStruct((NSA_HEADS, batch * R, NSA_DH), F32),
        compiler_params=_cparams(("parallel", "arbitrary")),
        name="sample_attn",
    )(page_table, q_hm, selm, *([cache_sel] * PAGES_PER_STEP), new_sel, cache_win, new_win, proj, o_cmp_hm)


def _pack_w_in(w_in):
    segs = [w_in[..., lo:hi] for lo, hi in _SRC_SEGMENTS]
    used = sum(hi - lo for lo, hi in _SRC_SEGMENTS)
    segs.append(jnp.zeros(w_in.shape[:-1] + (N_PROJ - used,), w_in.dtype))
    return jnp.concatenate(segs, axis=-1).astype(BF16)


def _layer_prompt(x, lw, tabs, batch, T):
    proj = norm_matmul(x, lw['n0'], lw['w_in'], 512, 2048)
    o_a, gla_state = gla(proj, lw['wa2'], lw['ba'], lw['gn'], None, batch, T, GLA_CHUNK, 4, GLA_CHUNK)
    q_hm, cmp_rows, sel_rows, win_rows = nsa_prep(proj, tabs, 512)
    a, b = chunk_sums(cmp_rows, lw['pe'], lw['mix'], 512)
    kvc = compressed_kv(a, b, batch, lw['w1'], lw['w2'], 128)
    o_b = nsa_prompt(q_hm, kvc, sel_rows, win_rows, proj, batch, T, WIN_QBLK)
    x = merge_wo(x, proj, o_a, o_b, lw['n1'], lw['w_o'], 256)
    x = mlp(x, lw['n2'], lw['n3'], lw['w_up'], lw['w_down'], 512, 1024)
    return x, (cmp_rows, sel_rows, win_rows, gla_state)


def _layer_sample(x, lw, tabs, layer, batch, cache_cmp, cache_sel, cache_win, s0, page_table):
    R = SAMPLE_ROWS
    m = batch * R
    proj = norm_matmul(x, lw['n0'], lw['w_in'], m, 1024)
    o_a, gla_state = gla(proj, lw['wa2'], lw['ba'], lw['gn'], s0, batch, R, R, 1, 4)
    q_hm, cmp_rows, sel_rows, win_rows = nsa_prep(proj, tabs, m)
    a, b = chunk_sums_paged(cache_cmp, layer, page_table, lw['pe'], lw['mix'])
    kvc = compressed_kv(a, b, batch, lw['w1'], lw['w2'], 1024)
    n_slc = -(-(PAST_LEN + 4) // SEL_LEN)
    o_cmp, selm = sample_cmp_select(q_hm, kvc, batch, n_slc)
    o_b = sample_attn(q_hm, selm, cache_sel, layer, page_table, sel_rows, cache_win, win_rows, proj, o_cmp)
    x = merge_wo(x, proj, o_a, o_b, lw['n1'], lw['w_o'], m)
    x = mlp(x, lw['n2'], lw['n3'], lw['w_up'], lw['w_down'], m, 1024)
    return x, (cmp_rows, sel_rows, win_rows, gla_state)


def kernel(x_prompt, x_sample, cache_cmp_kv, cache_sel_kv, cache_win_kv, state_gla, page_table,
           w_in, gla_wa2, gla_ba, gla_norm, cmp_pe, cmp_mix, cmp_w1, cmp_w2, w_o, norms, w_up, w_down):
    bp, T, d = x_prompt.shape
    bs, ss, _ = x_sample.shape
    R = SAMPLE_ROWS
    depth = w_in.shape[0]
    n_pool = cache_cmp_kv.shape[1]
    wb = cache_win_kv.shape[2]
    assert wb == WINDOW and T % 512 == 0 and ss <= R

    w_in_p = _pack_w_in(w_in)
    w_o_b, w_up_b, w_down_b = w_o.astype(BF16), w_up.astype(BF16), w_down.astype(BF16)
    cache_cmp = cache_cmp_kv.reshape(depth, n_pool, PAGE_SIZE * KV_PARTS, NSA_DH)
    cache_sel = cache_sel_kv.reshape(depth, n_pool, PAGE_SIZE * KV_PARTS, NSA_DH)
    cache_win = cache_win_kv.reshape(depth, bs, wb * KV_PARTS, NSA_DH)

    tabs_p = rope_tables(jnp.arange(T, dtype=jnp.int32))
    tabs_s = tuple(jnp.tile(t, (bs, 1)) for t in rope_tables(PAST_LEN + jnp.arange(R, dtype=jnp.int32)))

    y_p = x_prompt.reshape(bp * T, d)
    y_s = jnp.pad(x_sample, ((0, 0), (0, R - ss), (0, 0))).reshape(bs * R, d)
    st_p, st_s = [], []
    for layer in range(depth):
        lw = {'w_in': w_in_p[layer], 'wa2': gla_wa2[layer], 'ba': gla_ba[layer][None, :],
              'gn': gla_norm[layer][None, :], 'pe': cmp_pe[layer], 'mix': cmp_mix[layer],
              'w1': cmp_w1[layer], 'w2': cmp_w2[layer], 'w_o': w_o_b[layer],
              'n0': norms[layer, 0][None, :], 'n1': norms[layer, 1][None, :],
              'n2': norms[layer, 2][None, :], 'n3': norms[layer, 3][None, :],
              'w_up': w_up_b[layer], 'w_down': w_down_b[layer]}
        y_p, sp = _layer_prompt(y_p, lw, tabs_p, bp, T)
        y_s, s_s = _layer_sample(y_s, lw, tabs_s, layer, bs, cache_cmp, cache_sel, cache_win,
                                 state_gla[layer], page_table)
        st_p.append(sp)
        st_s.append(s_s)

    kv_shape = (2, NSA_KV_HEADS, NSA_DH)

    def rows_p(i):
        return jnp.stack([s[i].reshape(bp, T, *kv_shape) for s in st_p])

    def rows_s(i):
        return jnp.stack([s[i].reshape(bs, R, *kv_shape)[:, :ss] for s in st_s])

    win_p = rows_p(2)[:, :, T - min(WINDOW, T):]
    win_s = jnp.concatenate([cache_win_kv[:, :, ss:], rows_s(2)], axis=2)
    return (y_p.reshape(bp, T, d), y_s.reshape(bs, R, d)[:, :ss],
            rows_p(0), rows_s(0), rows_p(1), rows_s(1), win_p, win_s,
            jnp.stack([s[3] for s in st_p]), jnp.stack([s[3] for s in st_s]))
```

```python
import functools

import jax
import jax.numpy as jnp
import numpy as np
from jax import lax
from jax.experimental import pallas as pl
from jax.experimental.pallas import tpu as pltpu

F32 = jnp.float32
BF16 = jnp.bfloat16
HI = lax.Precision.HIGHEST

D_MODEL = 2048
DEPTH = 4
PAST_LEN = 16384
PAGE_SIZE = 128

GLA_HEADS = 4
GLA_DK = 256
GLA_DV = 512
GLA_GATE_RANK = 16
GLA_TAU = 16.0
GLA_CHUNK = 64

NSA_HEADS = 16
NSA_KV_HEADS = 2
NSA_DH = 128
NSA_GROUP = NSA_HEADS // NSA_KV_HEADS
CMP_LEN = 32
CMP_STRIDE = 16
CMP_HIDDEN = 256
SEL_LEN = 64
SEL_SHIFT = 6
SEL_TOPN = 16
WINDOW = 512
WIN_QBLK = 128

ROPE_THETA = 500000.0
ROPE_DIM = 32
ROPE_HALF = 16
MLP_HIDDEN = 4 * D_MODEL
NORM_EPS = 1e-6
NEG_INF = -1e30
FORCED_SCORE = 1e6
KV_PARTS = 2 * NSA_KV_HEADS
KV_ROW = KV_PARTS * NSA_DH
PAGES_PER_STEP = 16

VMEM_LIMIT_BYTES = 56 * 1024 * 1024
LANES = 128

COL_MG = 0
COL_GV = 4096
COL_GG = 6144
COL_NQ = 8192
COL_GQ = 10240
COL_GK = 11264
COL_NKV = 12288
COL_SM = 13824
N_PROJ = 14336
SM_GATE0 = GLA_GATE_RANK

_SRC_SEGMENTS = ((9792, 13888), (2048, 4096), (4096, 6144), (6160, 8208), (0, 1024), (1024, 2048),
                 (8208, 9744), (6144, 6160), (9744, 9792))

SAMPLE_ROWS = 8


def _cparams(sem):
    return pltpu.CompilerParams(dimension_semantics=sem, vmem_limit_bytes=VMEM_LIMIT_BYTES)


def _rms(x, g):
    return x * lax.rsqrt(jnp.mean(x * x, axis=-1, keepdims=True) + NORM_EPS) * g


def _dot(a, b, precision=None):
    return jnp.dot(a, b, preferred_element_type=F32, precision=precision)


def _dot_nt(a, b, precision=None):
    return lax.dot_general(a, b, (((1,), (1,)), ((), ())), preferred_element_type=F32, precision=precision)


def _dot_tn(a, b, precision=None):
    return lax.dot_general(a, b, (((0,), (0,)), ((), ())), preferred_element_type=F32, precision=precision)


def _split_bf16(x):
    hi = x.astype(BF16)
    return hi, (x - hi.astype(F32)).astype(BF16)


def _dot_nt_3x(a, b):
    ah, al = _split_bf16(a)
    bh, bl = _split_bf16(b)
    return (_dot_nt(jnp.concatenate([ah, al], axis=1), jnp.concatenate([bh, bh], axis=1))
            + _dot_nt(ah, bl))


def _iota(shape, dim):
    return lax.broadcasted_iota(jnp.int32, shape, dim)


def _masked_softmax(s, valid):
    s = jnp.where(valid, s, NEG_INF)
    m = jnp.max(s, axis=-1, keepdims=True)
    p = jnp.where(valid, jnp.exp(s - m), 0.0)
    return p / jnp.maximum(jnp.sum(p, axis=-1, keepdims=True), 1e-30)


def _norm_matmul_kernel(x_ref, g_ref, w_ref, o_ref, h_ref):
    @pl.when(pl.program_id(1) == 0)
    def _():
        h_ref[...] = _rms(x_ref[...], g_ref[...]).astype(BF16)

    o_ref[...] = _dot(h_ref[...], w_ref[...])


def norm_matmul(x, g, w, tm, tn):
    m, k = x.shape
    n = w.shape[1]
    return pl.pallas_call(
        _norm_matmul_kernel,
        grid=(m // tm, n // tn),
        in_specs=[pl.BlockSpec((tm, k), lambda i, j: (i, 0)),
                  pl.BlockSpec((1, k), lambda i, j: (0, 0)),
                  pl.BlockSpec((k, tn), lambda i, j: (0, j))],
        out_specs=pl.BlockSpec((tm, tn), lambda i, j: (i, j)),
        out_shape=jax.ShapeDtypeStruct((m, n), F32),
        scratch_shapes=[pltpu.VMEM((tm, k), BF16)],
        compiler_params=_cparams(("parallel", "arbitrary")),
        name="norm_matmul",
    )(x, g, w)


def _merge_wo_kernel(x_ref, mg0_ref, mg1_ref, oa_ref, ob_ref, g_ref, wo_ref, o_ref):
    ob = jnp.concatenate([ob_ref[h] for h in range(NSA_HEADS)], axis=1)
    a = jax.nn.sigmoid(mg0_ref[...]) * oa_ref[...] + jax.nn.sigmoid(mg1_ref[...]) * ob
    m = _dot(a.astype(BF16), wo_ref[...])
    o_ref[...] = x_ref[...] + _rms(m, g_ref[...])


def merge_wo(x, proj, o_a, o_b_hm, g, wo, tm):
    m, d = x.shape
    return pl.pallas_call(
        _merge_wo_kernel,
        grid=(m // tm,),
        in_specs=[pl.BlockSpec((tm, d), lambda i: (i, 0)),
                  pl.BlockSpec((tm, d), lambda i: (i, COL_MG // D_MODEL)),
                  pl.BlockSpec((tm, d), lambda i: (i, COL_MG // D_MODEL + 1)),
                  pl.BlockSpec((tm, d), lambda i: (i, 0)),
                  pl.BlockSpec((NSA_HEADS, tm, NSA_DH), lambda i: (0, i, 0)),
                  pl.BlockSpec((1, d), lambda i: (0, 0)),
                  pl.BlockSpec((d, d), lambda i: (0, 0))],
        out_specs=pl.BlockSpec((tm, d), lambda i: (i, 0)),
        out_shape=jax.ShapeDtypeStruct((m, d), F32),
        compiler_params=_cparams(("parallel",)),
        name="merge_wo",
    )(x, proj, proj, o_a, o_b_hm, g, wo)


def _mlp_kernel(x_ref, g2_ref, g3_ref, wu_ref, wd_ref, o_ref, h_ref, acc_ref):
    j = pl.program_id(1)

    @pl.when(j == 0)
    def _():
        h_ref[...] = _rms(x_ref[...], g2_ref[...]).astype(BF16)
        acc_ref[...] = jnp.zeros_like(acc_ref)

    u = _dot(h_ref[...], wu_ref[...])
    u = jnp.square(jnp.maximum(u, 0.0)).astype(BF16)
    acc_ref[...] += _dot(u, wd_ref[...])

    @pl.when(j == pl.num_programs(1) - 1)
    def _():
        o_ref[...] = x_ref[...] + _rms(acc_ref[...], g3_ref[...])


def mlp(x, g2, g3, wu, wd, tm, th):
    m, d = x.shape
    hid = wu.shape[1]
    return pl.pallas_call(
        _mlp_kernel,
        grid=(m // tm, hid // th),
        in_specs=[pl.BlockSpec((tm, d), lambda i, j: (i, 0)),
                  pl.BlockSpec((1, d), lambda i, j: (0, 0)),
                  pl.BlockSpec((1, d), lambda i, j: (0, 0)),
                  pl.BlockSpec((d, th), lambda i, j: (0, j)),
                  pl.BlockSpec((th, d), lambda i, j: (j, 0))],
        out_specs=pl.BlockSpec((tm, d), lambda i, j: (i, 0)),
        out_shape=jax.ShapeDtypeStruct((m, d), F32),
        scratch_shapes=[pltpu.VMEM((tm, d), BF16), pltpu.VMEM((tm, d), F32)],
        compiler_params=_cparams(("parallel", "arbitrary")),
        name="mlp",
    )(x, g2, g3, wu, wd)


def _gla_kernel(q_ref, k_ref, v_ref, gg_ref, sm_ref, wa2_ref, ba_ref, gn_ref, s0_ref,
                o_ref, sout_ref, st_ref, *, rows_in, n_inner, n_valid, has_s0):
    c = pl.program_id(1)
    C = GLA_CHUNK

    @pl.when(c == 0)
    def _():
        for h in range(GLA_HEADS):
            if has_s0:
                st_ref[h] = s0_ref[h].T
            else:
                st_ref[h] = jnp.zeros((GLA_DV, GLA_DK), F32)

    row = _iota((C, 1), 0)
    col = _iota((1, C), 1)
    causal = row >= col
    tril = causal.astype(BF16)
    live = row < n_valid

    def pad(x):
        if rows_in == C:
            return x
        return jnp.concatenate([x, jnp.zeros((C - rows_in, x.shape[1]), x.dtype)], axis=0)

    for ci in range(n_inner):
        rs = pl.ds(ci * rows_in, rows_in)
        ga_h, ga_l = _split_bf16(pad(sm_ref[rs, :])[:, :GLA_GATE_RANK])
        for h in range(GLA_HEADS):
            kq = slice(h * GLA_DK, (h + 1) * GLA_DK)
            kv = slice(h * GLA_DV, (h + 1) * GLA_DV)
            q = pad(q_ref[rs, kq]) * (GLA_DK ** -0.5)
            k = pad(k_ref[rs, kq])
            v = pad(v_ref[rs, kv])
            w_h, w_l = _split_bf16(wa2_ref[:, kq])
            z = _dot(ga_h, w_h) + _dot(ga_l, w_h) + _dot(ga_h, w_l) + ba_ref[:, kq]
            log_a = (jnp.minimum(z, 0.0) - jnp.log1p(jnp.exp(-jnp.abs(z)))) / GLA_TAU
            if n_valid < C:
                log_a = jnp.where(live, log_a, 0.0)
                k = jnp.where(live, k, 0.0)
            la_h, la_l = _split_bf16(log_a)
            b = _dot(tril, la_h) + _dot(tril, la_l)
            q_dec = q * jnp.exp(b)
            k_dec = k * jnp.exp(-b)
            st = st_ref[h]
            o = _dot_nt(q_dec.astype(BF16), st.astype(BF16))
            att = jnp.where(causal, _dot_nt(q_dec.astype(BF16), k_dec.astype(BF16)), 0.0)
            vb = v.astype(BF16)
            o = o + _dot(att.astype(BF16), vb)
            b_last = b[C - 1:C, :]
            k_end = k * jnp.exp(b_last - b)
            st_ref[h] = st * jnp.exp(b_last) + _dot_tn(vb, k_end.astype(BF16))
            o_n = _rms(o, gn_ref[...])
            gg = gg_ref[rs, kv]
            o_ref[rs, kv] = o_n[:rows_in] * (gg * jax.nn.sigmoid(gg))

    @pl.when(c == pl.num_programs(1) - 1)
    def _():
        for h in range(GLA_HEADS):
            sout_ref[h] = st_ref[h].T


def gla(proj, wa2, ba, gn, s0, batch, rows_per_batch, rows_in, n_inner, n_valid):
    m = proj.shape[0]
    r = rows_in * n_inner
    n_steps = rows_per_batch // r
    has_s0 = s0 is not None
    if s0 is None:
        s0 = jnp.zeros((1, GLA_HEADS, GLA_DK, GLA_DV), F32)
        s0_map = lambda b, c: (0, 0, 0, 0)
    else:
        s0_map = lambda b, c: (b, 0, 0, 0)
    wk, wv = GLA_HEADS * GLA_DK, GLA_HEADS * GLA_DV
    rowmap = lambda off: (lambda b, c: (b * n_steps + c, off))
    state_blk = (None, GLA_HEADS, GLA_DK, GLA_DV)
    kern = functools.partial(_gla_kernel, rows_in=rows_in, n_inner=n_inner, n_valid=n_valid, has_s0=has_s0)
    return pl.pallas_call(
        kern,
        grid=(batch, n_steps),
        in_specs=[pl.BlockSpec((r, wk), rowmap(COL_GQ // wk)),
                  pl.BlockSpec((r, wk), rowmap(COL_GK // wk)),
                  pl.BlockSpec((r, wv), rowmap(COL_GV // wv)),
                  pl.BlockSpec((r, wv), rowmap(COL_GG // wv)),
                  pl.BlockSpec((r, LANES), rowmap(COL_SM // LANES)),
                  pl.BlockSpec((GLA_GATE_RANK, wk), lambda b, c: (0, 0)),
                  pl.BlockSpec((1, wk), lambda b, c: (0, 0)),
                  pl.BlockSpec((1, GLA_DV), lambda b, c: (0, 0)),
                  pl.BlockSpec(state_blk, s0_map)],
        out_specs=[pl.BlockSpec((r, wv), rowmap(0)),
                   pl.BlockSpec(state_blk, lambda b, c: (b, 0, 0, 0))],
        out_shape=[jax.ShapeDtypeStruct((m, D_MODEL), F32),
                   jax.ShapeDtypeStruct((batch, GLA_HEADS, GLA_DK, GLA_DV), F32)],
        scratch_shapes=[pltpu.VMEM((GLA_HEADS, GLA_DV, GLA_DK), F32)],
        compiler_params=_cparams(("parallel", "arbitrary")),
        name="gla",
    )(proj, proj, proj, proj, proj, wa2, ba, gn, s0)


def _rope(x, c, sa, sb):
    return x * c + pltpu.roll(x, LANES - ROPE_HALF, 1) * sa + pltpu.roll(x, ROPE_HALF, 1) * sb


def _nsa_prep_kernel(nq_ref, nkv_ref, c_ref, sa_ref, sb_ref, q_ref, cmp_ref, sel_ref, win_ref):
    c, sa, sb = c_ref[...], sa_ref[...], sb_ref[...]
    for h in range(NSA_HEADS):
        q_ref[h] = _rope(nq_ref[:, h * NSA_DH:(h + 1) * NSA_DH], c, sa, sb)
    for s, out in enumerate((cmp_ref, sel_ref, win_ref)):
        base = s * KV_ROW
        for hh in range(NSA_KV_HEADS):
            lo = hh * NSA_DH
            out[:, lo:lo + NSA_DH] = _rope(nkv_ref[:, base + lo:base + lo + NSA_DH], c, sa, sb)
        half = NSA_KV_HEADS * NSA_DH
        out[:, half:] = nkv_ref[:, base + half:base + KV_ROW]


def nsa_prep(proj, tabs, tr):
    m = proj.shape[0]
    n_tab = tabs[0].shape[0] // tr
    tab_spec = pl.BlockSpec((tr, LANES), lambda i: (i % n_tab, 0))
    nkv_w = 3 * KV_ROW
    return pl.pallas_call(
        _nsa_prep_kernel,
        grid=(m // tr,),
        in_specs=[pl.BlockSpec((tr, D_MODEL), lambda i: (i, COL_NQ // D_MODEL)),
                  pl.BlockSpec((tr, nkv_w), lambda i: (i, COL_NKV // nkv_w)),
                  tab_spec, tab_spec, tab_spec],
        out_specs=[pl.BlockSpec((NSA_HEADS, tr, NSA_DH), lambda i: (0, i, 0)),
                   pl.BlockSpec((tr, KV_ROW), lambda i: (i, 0)),
                   pl.BlockSpec((tr, KV_ROW), lambda i: (i, 0)),
                   pl.BlockSpec((tr, KV_ROW), lambda i: (i, 0))],
        out_shape=[jax.ShapeDtypeStruct((NSA_HEADS, m, NSA_DH), F32),
                   jax.ShapeDtypeStruct((m, KV_ROW), F32),
                   jax.ShapeDtypeStruct((m, KV_ROW), F32),
                   jax.ShapeDtypeStruct((m, KV_ROW), F32)],
        compiler_params=_cparams(("parallel",)),
        name="nsa_prep",
    )(proj, proj, *tabs)


def rope_tables(pos):
    inv = 1.0 / (ROPE_THETA ** (jnp.arange(ROPE_HALF, dtype=F32) / ROPE_HALF))
    ang = pos.astype(F32)[:, None] * inv[None, :]
    cos, sin = jnp.cos(ang), jnp.sin(ang)
    n = pos.shape[0]
    rest = LANES - ROPE_DIM
    c = jnp.concatenate([cos, cos, jnp.ones((n, rest), F32)], axis=1)
    sa = jnp.concatenate([-sin, jnp.zeros((n, LANES - ROPE_HALF), F32)], axis=1)
    sb = jnp.concatenate([jnp.zeros((n, ROPE_HALF), F32), sin, jnp.zeros((n, rest), F32)], axis=1)
    return c, sa, sb


def _chunk_sums_body(x, pe_ref, mix_ref, a_ref, b_ref):
    r = x.shape[0]
    nch = r // CMP_STRIDE
    for kv in range(2):
        pe, mix = pe_ref[kv], mix_ref[kv]
        for hh in range(NSA_KV_HEADS):
            lo = (kv * NSA_KV_HEADS + hh) * NSA_DH
            xs = x[:, lo:lo + NSA_DH].reshape(nch, CMP_STRIDE, NSA_DH)
            a_ref[:, lo:lo + NSA_DH] = jnp.sum((xs + pe[:CMP_STRIDE]) * mix[:CMP_STRIDE], axis=1)
            b_ref[:, lo:lo + NSA_DH] = jnp.sum((xs + pe[CMP_STRIDE:]) * mix[CMP_STRIDE:], axis=1)


def _chunk_sums_kernel(x_ref, pe_ref, mix_ref, a_ref, b_ref):
    _chunk_sums_body(x_ref[...], pe_ref, mix_ref, a_ref, b_ref)


def _page_specs(layer):
    def spec(i):
        return pl.BlockSpec((None, None, PAGE_SIZE * KV_PARTS, NSA_DH),
                            lambda b, s, pt: (layer, pt[b, s * PAGES_PER_STEP + i], 0, 0))
    return [spec(i) for i in range(PAGES_PER_STEP)]


def _page_part(page_ref, part):
    return page_ref[pl.ds(part, PAGE_SIZE, stride=KV_PARTS), :]


def _chunk_sums_paged_kernel(pt_ref, *refs):
    page_refs, (pe_ref, mix_ref, a_ref, b_ref) = refs[:PAGES_PER_STEP], refs[PAGES_PER_STEP:]
    nch = PAGE_SIZE // CMP_STRIDE
    for i, page_ref in enumerate(page_refs):
        rows = slice(i * nch, (i + 1) * nch)
        for kv in range(2):
            pe, mix = pe_ref[kv], mix_ref[kv]
            for hh in range(NSA_KV_HEADS):
                part = kv * NSA_KV_HEADS + hh
                lo = part * NSA_DH
                xs = _page_part(page_ref, part).reshape(nch, CMP_STRIDE, NSA_DH)
                a_ref[rows, lo:lo + NSA_DH] = jnp.sum((xs + pe[:CMP_STRIDE]) * mix[:CMP_STRIDE], axis=1)
                b_ref[rows, lo:lo + NSA_DH] = jnp.sum((xs + pe[CMP_STRIDE:]) * mix[CMP_STRIDE:], axis=1)


def chunk_sums(rows, pe, mix, tr):
    m = rows.shape[0]
    nch = tr // CMP_STRIDE
    full = lambda i: (0, 0, 0)
    return pl.pallas_call(
        _chunk_sums_kernel,
        grid=(m // tr,),
        in_specs=[pl.BlockSpec((tr, KV_ROW), lambda i: (i, 0)),
                  pl.BlockSpec((2, CMP_LEN, NSA_DH), full),
                  pl.BlockSpec((2, CMP_LEN, NSA_DH), full)],
        out_specs=[pl.BlockSpec((nch, KV_ROW), lambda i: (i, 0))] * 2,
        out_shape=[jax.ShapeDtypeStruct((m // CMP_STRIDE, KV_ROW), F32)] * 2,
        compiler_params=_cparams(("parallel",)),
        name="chunk_sums",
    )(rows, pe, mix)


def chunk_sums_paged(cache, layer, page_table, pe, mix):
    batch, n_pages = page_table.shape
    nch = PAGES_PER_STEP * PAGE_SIZE // CMP_STRIDE
    n_steps = n_pages // PAGES_PER_STEP
    full = lambda b, s, pt: (0, 0, 0)
    gs = pltpu.PrefetchScalarGridSpec(
        num_scalar_prefetch=1,
        grid=(batch, n_steps),
        in_specs=_page_specs(layer) + [pl.BlockSpec((2, CMP_LEN, NSA_DH), full),
                                       pl.BlockSpec((2, CMP_LEN, NSA_DH), full)],
        out_specs=[pl.BlockSpec((nch, KV_ROW), lambda b, s, pt: (b * n_steps + s, 0))] * 2,
    )
    return pl.pallas_call(
        _chunk_sums_paged_kernel,
        grid_spec=gs,
        out_shape=[jax.ShapeDtypeStruct((batch * n_steps * nch, KV_ROW), F32)] * 2,
        compiler_params=_cparams(("parallel", "arbitrary")),
        name="chunk_sums_paged",
    )(page_table, *([cache] * PAGES_PER_STEP), pe, mix)


def _cmp_mlp_kernel(a_ref, b_ref, w1_ref, w2_ref, o_ref):
    h = a_ref[...] + b_ref[...]
    for kv in range(2):
        for hh in range(NSA_KV_HEADS):
            lo = (kv * NSA_KV_HEADS + hh) * NSA_DH
            y = jax.nn.gelu(_dot(h[:, lo:lo + NSA_DH], w1_ref[kv], HI))
            o_ref[:, lo:lo + NSA_DH] = _dot(y, w2_ref[kv], HI)


def cmp_mlp(a, b_shift, w1, w2, tr):
    m = a.shape[0]
    return pl.pallas_call(
        _cmp_mlp_kernel,
        grid=(m // tr,),
        in_specs=[pl.BlockSpec((tr, KV_ROW), lambda i: (i, 0)),
                  pl.BlockSpec((tr, KV_ROW), lambda i: (i, 0)),
                  pl.BlockSpec((2, NSA_DH, CMP_HIDDEN), lambda i: (0, 0, 0)),
                  pl.BlockSpec((2, CMP_HIDDEN, NSA_DH), lambda i: (0, 0, 0))],
        out_specs=pl.BlockSpec((tr, KV_ROW), lambda i: (i, 0)),
        out_shape=jax.ShapeDtypeStruct((m, KV_ROW), F32),
        compiler_params=_cparams(("parallel",)),
        name="cmp_mlp",
    )(a, b_shift, w1, w2)


def compressed_kv(a, b, batch, w1, w2, tr):
    nch = a.shape[0] // batch
    b3 = b.reshape(batch, nch, KV_ROW)
    b_shift = jnp.concatenate([b3[:, 1:], jnp.zeros((batch, 1, KV_ROW), F32)], axis=1).reshape(batch * nch, KV_ROW)
    return cmp_mlp(a, b_shift, w1, w2, tr)


def _select_topn(imp, n_top):
    j = _iota(imp.shape, 1)
    big = jnp.int32(imp.shape[1])
    sel = jnp.zeros(imp.shape, F32)
    for _ in range(n_top):
        m = jnp.max(imp, axis=-1, keepdims=True)
        idx = jnp.min(jnp.where(imp == m, j, big), axis=-1, keepdims=True)
        hit = j == idx
        sel = jnp.where(hit, 1.0, sel)
        imp = jnp.where(hit, NEG_INF, imp)
    return sel


def _importance(p_grp, qpos, n_cmp_pad, n_slc, width):
    n_r = _iota((n_cmp_pad, 1), 0) * CMP_STRIDE
    j_c = _iota((1, width), 1)
    overlap = ((n_r < (j_c + 1) * SEL_LEN) & (n_r + CMP_LEN > j_c * SEL_LEN)).astype(F32)
    imp = _dot(p_grp, overlap, HI)
    cur = qpos >> SEL_SHIFT
    forced = (j_c == 0) | (j_c == cur) | (j_c == cur - 1)
    imp = jnp.where(forced, FORCED_SCORE, imp)
    imp = jnp.where(j_c > cur, -1.0, imp)
    return jnp.where(j_c >= n_slc, -2.0, imp)


LOG2E = 1.4426950408889634


def _nsa_prompt_kernel(q_ref, kc_ref, vc_ref, ks_ref, vs_ref, kw_ref, vw_ref, sm_ref, o_ref,
                       qb_ref, s_ref, p_ref, m_ref, l_ref, a_ref, acc_ref, *, Q, T, TK):
    G = NSA_GROUP
    hk = pl.program_id(1)
    qi = pl.program_id(2)
    q0 = qi * Q
    scale = NSA_DH ** -0.5
    n_cmp_pad = kc_ref.shape[0]
    n_slc = T // SEL_LEN
    qpos = q0 + _iota((Q, 1), 0)
    qs = q_ref[...].reshape(G * Q, NSA_DH)
    qb_ref[...] = (qs * (scale * LOG2E)).astype(BF16)

    ng = jax.nn.sigmoid(sm_ref[...])

    def gate(g, c3):
        lane0 = SM_GATE0 + 3 * g + c3
        lane1 = lane0 + 3 * G
        return jnp.where(hk == 0, ng[:, lane0:lane0 + 1], ng[:, lane1:lane1 + 1])

    def softmax_tile(kb, vb, bias, width):
        s_ref[:, :width] = _dot_nt(qb_ref[...], kb)
        for g in range(G):
            rows = pl.ds(g * Q, Q)
            s = s_ref[rows, :width] + bias
            m_old = m_ref[rows, :]
            m_new = jnp.maximum(m_old, jnp.max(s, axis=-1, keepdims=True))
            alpha = jnp.exp2(m_old - m_new)
            p = jnp.exp2(s - jnp.tile(m_new, (1, width // LANES)))
            l_ref[rows, :] = alpha * l_ref[rows, :] + jnp.sum(p, axis=-1, keepdims=True)
            a_ref[rows, :] = alpha
            m_ref[rows, :] = m_new
            p_ref[rows, :width] = p.astype(BF16)
        return _dot(p_ref[:, :width], vb)

    def reset():
        m_ref[...] = jnp.full_like(m_ref, NEG_INF)
        l_ref[...] = jnp.zeros_like(l_ref)

    cmp_valid = _iota((1, n_cmp_pad), 1) * CMP_STRIDE + (CMP_LEN - 1) <= qpos
    p_grp = jnp.zeros((Q, n_cmp_pad), F32)
    s_cmp = _dot_nt_3x(qs, kc_ref[...]) * scale
    vcb = vc_ref[...].astype(BF16)
    for g in range(G):
        p = _masked_softmax(s_cmp[g * Q:(g + 1) * Q], cmp_valid)
        p_grp = p_grp + p
        o_ref[g] = gate(g, 0) * _dot(p.astype(BF16), vcb)

    ks0 = pl.multiple_of(jnp.maximum(qi - WINDOW // WIN_QBLK, 0) * WIN_QBLK, WIN_QBLK)
    wlen = WINDOW + WIN_QBLK
    wpos = ks0 + _iota((1, wlen), 1)
    win_bias = jnp.where((wpos <= qpos) & (wpos > qpos - WINDOW), 0.0, NEG_INF)
    reset()
    pv = softmax_tile(kw_ref[pl.ds(ks0, wlen), :].astype(BF16), vw_ref[pl.ds(ks0, wlen), :].astype(BF16),
                      win_bias, wlen)
    for g in range(G):
        rows = pl.ds(g * Q, Q)
        o_ref[g] = o_ref[g] + gate(g, 2) * (pv[g * Q:(g + 1) * Q] / jnp.maximum(l_ref[rows, :], 1e-30))

    n_top = min(SEL_TOPN, n_slc)
    j_r = _iota((n_slc, 1), 0)
    n_c = _iota((1, n_cmp_pad), 1) * CMP_STRIDE
    overlap_t = ((n_c < (j_r + 1) * SEL_LEN) & (n_c + CMP_LEN > j_r * SEL_LEN)).astype(F32)
    imp = _dot_nt(overlap_t, p_grp, HI)
    cur = (q0 + _iota((1, Q), 1)) >> SEL_SHIFT
    imp = jnp.where((j_r == 0) | (j_r == cur) | (j_r == cur - 1), FORCED_SCORE, imp)
    imp = jnp.where(j_r > cur, -1.0, imp)
    rank = jnp.zeros((n_slc, Q), F32)
    for jp in range(n_slc):
        other = imp[jp:jp + 1, :]
        beats = (other > imp) | ((other == imp) & (j_r > jp))
        rank = rank + jnp.where(beats, 1.0, 0.0)
    sel_t = jnp.where(rank < n_top, 1.0, 0.0).astype(BF16)

    reset()
    acc_ref[...] = jnp.zeros_like(acc_ref)

    def key_tile(kt, carry):
        k0 = pl.multiple_of(kt * TK, TK)
        kpos = k0 + _iota((1, TK), 1)
        expand = (j_r == (kpos >> SEL_SHIFT)).astype(BF16)
        keymask = _dot_tn(sel_t, expand)
        bias = jnp.where((keymask > 0.5) & (kpos <= qpos), 0.0, NEG_INF)
        pv = softmax_tile(ks_ref[pl.ds(k0, TK), :].astype(BF16), vs_ref[pl.ds(k0, TK), :].astype(BF16), bias, TK)
        acc_ref[...] = a_ref[...] * acc_ref[...] + pv
        return carry

    lax.fori_loop(0, (q0 + Q + TK - 1) // TK, key_tile, 0)

    for g in range(G):
        rows = pl.ds(g * Q, Q)
        o_ref[g] = o_ref[g] + gate(g, 1) * (acc_ref[rows, :] / jnp.maximum(l_ref[rows, :], 1e-30))


def nsa_prompt(q_hm, kvc, sel_rows, win_rows, proj, batch, T, Q):
    m = batch * T
    nq = T // Q
    n_cmp_pad = kvc.shape[0] // batch
    G = NSA_GROUP
    kern = functools.partial(_nsa_prompt_kernel, Q=Q, T=T, TK=512)
    kcol = lambda off: (lambda b, hk, qi: (b, off + hk))
    return pl.pallas_call(
        kern,
        grid=(batch, NSA_KV_HEADS, nq),
        in_specs=[pl.BlockSpec((G, Q, NSA_DH), lambda b, hk, qi: (hk, b * nq + qi, 0)),
                  pl.BlockSpec((n_cmp_pad, NSA_DH), kcol(0)),
                  pl.BlockSpec((n_cmp_pad, NSA_DH), kcol(NSA_KV_HEADS)),
                  pl.BlockSpec((T, NSA_DH), kcol(0)),
                  pl.BlockSpec((T, NSA_DH), kcol(NSA_KV_HEADS)),
                  pl.BlockSpec((T, NSA_DH), kcol(0)),
                  pl.BlockSpec((T, NSA_DH), kcol(NSA_KV_HEADS)),
                  pl.BlockSpec((Q, LANES), lambda b, hk, qi: (b * nq + qi, COL_SM // LANES))],
        out_specs=pl.BlockSpec((G, Q, NSA_DH), lambda b, hk, qi: (hk, b * nq + qi, 0)),
        out_shape=jax.ShapeDtypeStruct((NSA_HEADS, m, NSA_DH), F32),
        scratch_shapes=[pltpu.VMEM((G * Q, NSA_DH), BF16),
                        pltpu.VMEM((G * Q, WINDOW + WIN_QBLK), F32), pltpu.VMEM((G * Q, WINDOW + WIN_QBLK), BF16),
                        pltpu.VMEM((G * Q, LANES), F32), pltpu.VMEM((G * Q, LANES), F32),
                        pltpu.VMEM((G * Q, LANES), F32), pltpu.VMEM((G * Q, NSA_DH), F32)],
        compiler_params=_cparams(("parallel", "parallel", "arbitrary")),
        name="nsa_prompt",
    )(q_hm, kvc, kvc, sel_rows, sel_rows, win_rows, win_rows, proj)


SEL_W = 384


def _sample_cmp_select_kernel(q_ref, kc_ref, vc_ref, ocmp_ref, selm_ref, *, n_slc):
    G, R = NSA_GROUP, SAMPLE_ROWS
    scale = NSA_DH ** -0.5
    n_cmp_pad = kc_ref.shape[0]
    qs = q_ref[...].reshape(G * R, NSA_DH)
    qpos_s = PAST_LEN + (_iota((G * R, 1), 0) & (R - 1))
    qpos = PAST_LEN + _iota((R, 1), 0)
    s = _dot_nt(qs, kc_ref[...], HI) * scale
    end = _iota((1, n_cmp_pad), 1) * CMP_STRIDE + (CMP_LEN - 1)
    p = _masked_softmax(s, end <= qpos_s)
    o_cmp = _dot(p.astype(BF16), vc_ref[...].astype(BF16))
    for g in range(G):
        ocmp_ref[g] = o_cmp[g * R:(g + 1) * R]
    p_grp = jnp.sum(p.reshape(G, R, n_cmp_pad), axis=0)
    imp = _importance(p_grp, qpos, n_cmp_pad, n_slc, SEL_W)
    selm_ref[...] = _select_topn(imp, min(SEL_TOPN, n_slc))


def sample_cmp_select(q_hm, kvc, batch, n_slc):
    G, R = NSA_GROUP, SAMPLE_ROWS
    n_cmp_pad = kvc.shape[0] // batch
    kern = functools.partial(_sample_cmp_select_kernel, n_slc=n_slc)
    return pl.pallas_call(
        kern,
        grid=(batch, NSA_KV_HEADS),
        in_specs=[pl.BlockSpec((G, R, NSA_DH), lambda b, hk: (hk, b, 0)),
                  pl.BlockSpec((n_cmp_pad, NSA_DH), lambda b, hk: (b, hk)),
                  pl.BlockSpec((n_cmp_pad, NSA_DH), lambda b, hk: (b, NSA_KV_HEADS + hk))],
        out_specs=[pl.BlockSpec((G, R, NSA_DH), lambda b, hk: (hk, b, 0)),
                   pl.BlockSpec((None, None, R, SEL_W), lambda b, hk: (b, hk, 0, 0))],
        out_shape=[jax.ShapeDtypeStruct((NSA_HEADS, batch * R, NSA_DH), F32),
                   jax.ShapeDtypeStruct((batch, NSA_KV_HEADS, R, SEL_W), F32)],
        compiler_params=_cparams(("parallel", "parallel")),
        name="sample_cmp_select",
    )(q_hm, kvc, kvc)


def _sample_attn_kernel(pt_ref, q_ref, selm_ref, *refs, n_steps):
    page_refs = refs[:PAGES_PER_STEP]
    nsel_ref, cwin_ref, nwin_ref, sm_ref, ocmp_ref, o_ref, m_ref, l_ref, acc_ref = refs[PAGES_PER_STEP:]
    G, R = NSA_GROUP, SAMPLE_ROWS
    p = pl.program_id(1)
    scale = NSA_DH ** -0.5
    t_s = _iota((G * R, 1), 0) & (R - 1)
    n_keys = PAGES_PER_STEP * PAGE_SIZE

    @pl.when(p == 0)
    def _():
        m_ref[...] = jnp.full_like(m_ref, NEG_INF)
        l_ref[...] = jnp.zeros_like(l_ref)
        acc_ref[...] = jnp.zeros_like(acc_ref)

    def update(hk, s, valid, v):
        s = jnp.where(valid, s, NEG_INF)
        m_old = m_ref[hk]
        m_new = jnp.maximum(m_old, jnp.max(s, axis=-1, keepdims=True))
        alpha = jnp.exp(m_old - m_new)
        pr = jnp.where(valid, jnp.exp(s - m_new), 0.0)
        l_ref[hk] = alpha * l_ref[hk] + jnp.sum(pr, axis=-1, keepdims=True)
        acc_ref[hk] = alpha * acc_ref[hk] + _dot(pr.astype(BF16), v.astype(BF16))
        m_ref[hk] = m_new

    key = _iota((1, n_keys), 1)
    blk_of_key = (n_keys // SEL_LEN) * p + (key >> SEL_SHIFT)
    expand = (_iota((SEL_W, 1), 0) == blk_of_key).astype(BF16)
    for hk in range(NSA_KV_HEADS):
        qs = q_ref[hk * G:(hk + 1) * G].reshape(G * R, NSA_DH).astype(BF16)
        kpg = jnp.concatenate([_page_part(r, hk).astype(BF16) for r in page_refs], axis=0)
        vpg = jnp.concatenate([_page_part(r, NSA_KV_HEADS + hk).astype(BF16) for r in page_refs], axis=0)
        s = _dot_nt(qs, kpg) * scale
        keymask = _dot(selm_ref[hk].astype(BF16), expand)
        valid = jnp.concatenate([keymask] * G, axis=0) > 0.5
        update(hk, s, valid, vpg)

    def win_part(ref, part):
        return ref[pl.ds(part, WINDOW, stride=KV_PARTS), :]

    @pl.when(p == n_steps - 1)
    def _():
        ng = jax.nn.sigmoid(sm_ref[...])
        new_blk = PAST_LEN // SEL_LEN
        zpad = jnp.zeros((LANES - R, NSA_DH), F32)
        jn = _iota((1, LANES), 1)
        for hk in range(NSA_KV_HEADS):
            qs = q_ref[hk * G:(hk + 1) * G].reshape(G * R, NSA_DH).astype(BF16)
            klo, vlo = hk * NSA_DH, (NSA_KV_HEADS + hk) * NSA_DH
            kn = jnp.concatenate([nsel_ref[:, klo:klo + NSA_DH], zpad], axis=0)
            vn = jnp.concatenate([nsel_ref[:, vlo:vlo + NSA_DH], zpad], axis=0)
            s = _dot_nt(qs, kn.astype(BF16)) * scale
            picked = jnp.concatenate([selm_ref[hk][:, new_blk:new_blk + 1]] * G, axis=0) > 0.5
            update(hk, s, picked & (jn <= t_s) & (jn < R), vn)
            o_sel = acc_ref[hk] / jnp.maximum(l_ref[hk], 1e-30)
            kw = jnp.concatenate([win_part(cwin_ref, hk), nwin_ref[:, klo:klo + NSA_DH], zpad], axis=0)
            vw = jnp.concatenate([win_part(cwin_ref, NSA_KV_HEADS + hk), nwin_ref[:, vlo:vlo + NSA_DH], zpad], axis=0)
            iw = _iota((1, WINDOW + LANES), 1)
            wvalid = ((iw < WINDOW) & (iw > t_s)) | ((iw >= WINDOW) & (iw - WINDOW <= t_s) & (iw - WINDOW < R))
            pw = _masked_softmax(_dot_nt(qs, kw.astype(BF16)) * scale, wvalid)
            o_win = _dot(pw.astype(BF16), vw.astype(BF16))
            for g in range(G):
                h = hk * G + g
                lane = SM_GATE0 + 3 * h
                rows = slice(g * R, (g + 1) * R)
                o_ref[h] = (ng[:, lane:lane + 1] * ocmp_ref[h] + ng[:, lane + 1:lane + 2] * o_sel[rows]
                            + ng[:, lane + 2:lane + 3] * o_win[rows])


def sample_attn(q_hm, selm, cache_sel, layer, page_table, new_sel, cache_win, new_win, proj, o_cmp_hm):
    batch, n_pages = page_table.shape
    n_steps = n_pages // PAGES_PER_STEP
    G, R = NSA_GROUP, SAMPLE_ROWS
    hm_spec = pl.BlockSpec((NSA_HEADS, R, NSA_DH), lambda b, p, pt: (0, b, 0))
    row_spec = pl.BlockSpec((R, KV_ROW), lambda b, p, pt: (b, 0))
    gs = pltpu.PrefetchScalarGridSpec(
        num_scalar_prefetch=1,
        grid=(batch, n_steps),
        in_specs=[hm_spec,
                  pl.BlockSpec((None, NSA_KV_HEADS, R, SEL_W), lambda b, p, pt: (b, 0, 0, 0))]
                 + _page_specs(layer)
                 + [row_spec,
                    pl.BlockSpec((None, None, WINDOW * KV_PARTS, NSA_DH), lambda b, p, pt: (layer, b, 0, 0)),
                    row_spec,
                    pl.BlockSpec((R, LANES), lambda b, p, pt: (b, COL_SM // LANES)),
                    hm_spec],
        out_specs=hm_spec,
        scratch_shapes=[pltpu.VMEM((NSA_KV_HEADS, G * R, 1), F32),
                        pltpu.VMEM((NSA_KV_HEADS, G * R, 1), F32),
                        pltpu.VMEM((NSA_KV_HEADS, G * R, NSA_DH), F32)],
    )
    kern = functools.partial(_sample_attn_kernel, n_steps=n_steps)
    return pl.pallas_call(
        kern,
        grid_spec=gs,
        out_shape=jax.ShapeDtypeStruct((NSA_HEADS, batch * R, NSA_DH), F32),
        compiler_params=_cparams(("parallel", "arbitrary")),
        name="sample_attn",
    )(page_table, q_hm, selm, *([cache_sel] * PAGES_PER_STEP), new_sel, cache_win, new_win, proj, o_cmp_hm)


PACK_W = 512
PACK_SRC = PACK_W // LANES + 1
_MAIN_SEGMENTS = _SRC_SEGMENTS[:7]


def _pack_table():
    first, shift = [], []
    for lo, hi in _MAIN_SEGMENTS:
        assert (hi - lo) % PACK_W == 0
        for c in range(lo, hi, PACK_W):
            first.append(c // LANES)
            shift.append(c % LANES)
    assert len(first) * PACK_W == COL_SM
    n = N_PROJ // PACK_W
    first += [0] * (n - len(first))
    shift += [0] * (n - len(shift))
    return np.array([first, shift], np.int32)


def _pack_kernel(tab_ref, *refs):
    srcs, (sm_ref, o_ref) = refs[:PACK_SRC], refs[PACK_SRC:]
    j = pl.program_id(1)
    n_main = COL_SM // PACK_W
    shift = tab_ref[1, j]
    lane = _iota((1, LANES), 1)

    @pl.when(j >= n_main)
    def _():
        o_ref[...] = sm_ref[...]

    for sv in sorted({lo % LANES for lo, _ in _MAIN_SEGMENTS}):
        @pl.when((j < n_main) & (shift == sv))
        def _():
            for i in range(PACK_W // LANES):
                cols = slice(i * LANES, (i + 1) * LANES)
                if sv == 0:
                    o_ref[:, cols] = srcs[i][...].astype(BF16)
                else:
                    a = pltpu.roll(srcs[i][...], LANES - sv, 1)
                    b = pltpu.roll(srcs[i + 1][...], LANES - sv, 1)
                    o_ref[:, cols] = jnp.where(lane < LANES - sv, a, b).astype(BF16)


def _pack_w_in(w_in):
    depth, d, n_in = w_in.shape
    (ga_lo, ga_hi), (ng_lo, ng_hi) = _SRC_SEGMENTS[7:]
    small = jnp.concatenate([w_in[..., ga_lo:ga_hi], w_in[..., ng_lo:ng_hi]], axis=-1).astype(BF16)
    small = jnp.pad(small, ((0, 0), (0, 0), (0, PACK_W - small.shape[-1])))
    last_blk = (n_in - 1) // LANES

    def src_spec(i):
        return pl.BlockSpec((None, d, LANES), lambda l, j, tab: (l, 0, jnp.minimum(tab[0, j] + i, last_blk)))

    gs = pltpu.PrefetchScalarGridSpec(
        num_scalar_prefetch=1,
        grid=(depth, N_PROJ // PACK_W),
        in_specs=[src_spec(i) for i in range(PACK_SRC)] + [pl.BlockSpec((None, d, PACK_W), lambda l, j, tab: (l, 0, 0))],
        out_specs=pl.BlockSpec((None, d, PACK_W), lambda l, j, tab: (l, 0, j)),
    )
    return pl.pallas_call(
        _pack_kernel,
        grid_spec=gs,
        out_shape=jax.ShapeDtypeStruct((depth, d, N_PROJ), BF16),
        compiler_params=_cparams(("parallel", "arbitrary")),
        name="pack_w_in",
    )(jnp.asarray(_pack_table()), *([w_in] * PACK_SRC), small)


def _layer_prompt(x, lw, tabs, batch, T):
    proj = norm_matmul(x, lw['n0'], lw['w_in'], 1024, 1024)
    o_a, gla_state = gla(proj, lw['wa2'], lw['ba'], lw['gn'], None, batch, T, GLA_CHUNK, 4, GLA_CHUNK)
    q_hm, cmp_rows, sel_rows, win_rows = nsa_prep(proj, tabs, 512)
    a, b = chunk_sums(cmp_rows, lw['pe'], lw['mix'], 512)
    kvc = compressed_kv(a, b, batch, lw['w1'], lw['w2'], 128)
    o_b = nsa_prompt(q_hm, kvc, sel_rows, win_rows, proj, batch, T, WIN_QBLK)
    x = merge_wo(x, proj, o_a, o_b, lw['n1'], lw['w_o'], 256)
    x = mlp(x, lw['n2'], lw['n3'], lw['w_up'], lw['w_down'], 512, 1024)
    return x, (cmp_rows, sel_rows, win_rows, gla_state)


def _layer_sample(x, lw, tabs, layer, batch, cache_cmp, cache_sel, cache_win, s0, page_table):
    R = SAMPLE_ROWS
    m = batch * R
    proj = norm_matmul(x, lw['n0'], lw['w_in'], m, 1024)
    o_a, gla_state = gla(proj, lw['wa2'], lw['ba'], lw['gn'], s0, batch, R, R, 1, 4)
    q_hm, cmp_rows, sel_rows, win_rows = nsa_prep(proj, tabs, m)
    a, b = chunk_sums_paged(cache_cmp, layer, page_table, lw['pe'], lw['mix'])
    kvc = compressed_kv(a, b, batch, lw['w1'], lw['w2'], 1024)
    n_slc = -(-(PAST_LEN + 4) // SEL_LEN)
    o_cmp, selm = sample_cmp_select(q_hm, kvc, batch, n_slc)
    o_b = sample_attn(q_hm, selm, cache_sel, layer, page_table, sel_rows, cache_win, win_rows, proj, o_cmp)
    x = merge_wo(x, proj, o_a, o_b, lw['n1'], lw['w_o'], m)
    x = mlp(x, lw['n2'], lw['n3'], lw['w_up'], lw['w_down'], m, 1024)
    return x, (cmp_rows, sel_rows, win_rows, gla_state)


def kernel(x_prompt, x_sample, cache_cmp_kv, cache_sel_kv, cache_win_kv, state_gla, page_table,
           w_in, gla_wa2, gla_ba, gla_norm, cmp_pe, cmp_mix, cmp_w1, cmp_w2, w_o, norms, w_up, w_down):
    bp, T, d = x_prompt.shape
    bs, ss, _ = x_sample.shape
    R = SAMPLE_ROWS
    depth = w_in.shape[0]
    n_pool = cache_cmp_kv.shape[1]
    wb = cache_win_kv.shape[2]
    assert wb == WINDOW and T % 512 == 0 and ss <= R

    w_in_p = _pack_w_in(w_in)
    w_o_b, w_up_b, w_down_b = w_o.astype(BF16), w_up.astype(BF16), w_down.astype(BF16)
    cache_cmp = cache_cmp_kv.reshape(depth, n_pool, PAGE_SIZE * KV_PARTS, NSA_DH)
    cache_sel = cache_sel_kv.reshape(depth, n_pool, PAGE_SIZE * KV_PARTS, NSA_DH)
    cache_win = cache_win_kv.reshape(depth, bs, wb * KV_PARTS, NSA_DH)

    tabs_p = rope_tables(jnp.arange(T, dtype=jnp.int32))
    tabs_s = tuple(jnp.tile(t, (bs, 1)) for t in rope_tables(PAST_LEN + jnp.arange(R, dtype=jnp.int32)))

    y_p = x_prompt.reshape(bp * T, d)
    y_s = jnp.pad(x_sample, ((0, 0), (0, R - ss), (0, 0))).reshape(bs * R, d)
    st_p, st_s = [], []
    for layer in range(depth):
        lw = {'w_in': w_in_p[layer], 'wa2': gla_wa2[layer], 'ba': gla_ba[layer][None, :],
              'gn': gla_norm[layer][None, :], 'pe': cmp_pe[layer], 'mix': cmp_mix[layer],
              'w1': cmp_w1[layer], 'w2': cmp_w2[layer], 'w_o': w_o_b[layer],
              'n0': norms[layer, 0][None, :], 'n1': norms[layer, 1][None, :],
              'n2': norms[layer, 2][None, :], 'n3': norms[layer, 3][None, :],
              'w_up': w_up_b[layer], 'w_down': w_down_b[layer]}
        y_p, sp = _layer_prompt(y_p, lw, tabs_p, bp, T)
        y_s, s_s = _layer_sample(y_s, lw, tabs_s, layer, bs, cache_cmp, cache_sel, cache_win,
                                 state_gla[layer], page_table)
        st_p.append(sp)
        st_s.append(s_s)

    kv_shape = (2, NSA_KV_HEADS, NSA_DH)

    def rows_p(i):
        return jnp.stack([s[i].reshape(bp, T, *kv_shape) for s in st_p])

    def rows_s(i):
        return jnp.stack([s[i].reshape(bs, R, *kv_shape)[:, :ss] for s in st_s])

    win_p = rows_p(2)[:, :, T - min(WINDOW, T):]
    win_s = jnp.concatenate([cache_win_kv[:, :, ss:], rows_s(2)], axis=2)
    return (y_p.reshape(bp, T, d), y_s.reshape(bs, R, d)[:, :ss],
            rows_p(0), rows_s(0), rows_p(1), rows_s(1), win_p, win_s,
            jnp.stack([s[3] for s in st_p]), jnp.stack([s[3] for s in st_s]))
```

```python
import functools

import jax
import jax.numpy as jnp
import numpy as np
from jax import lax
from jax.experimental import pallas as pl
from jax.experimental.pallas import tpu as pltpu

F32 = jnp.float32
BF16 = jnp.bfloat16
HI = lax.Precision.HIGHEST

D_MODEL = 2048
DEPTH = 4
PAST_LEN = 16384
PAGE_SIZE = 128

GLA_HEADS = 4
GLA_DK = 256
GLA_DV = 512
GLA_GATE_RANK = 16
GLA_TAU = 16.0
GLA_CHUNK = 64
GLA_CHUNK_SHIFT = 6

NSA_HEADS = 16
NSA_KV_HEADS = 2
NSA_DH = 128
NSA_GROUP = NSA_HEADS // NSA_KV_HEADS
CMP_LEN = 32
CMP_STRIDE = 16
CMP_HIDDEN = 256
SEL_LEN = 64
SEL_SHIFT = 6
SEL_TOPN = 16
WINDOW = 512
WIN_QBLK = 128

ROPE_THETA = 500000.0
ROPE_DIM = 32
ROPE_HALF = 16
MLP_HIDDEN = 4 * D_MODEL
NORM_EPS = 1e-6
NEG_INF = -1e30
FORCED_SCORE = 1e6
KV_PARTS = 2 * NSA_KV_HEADS
KV_ROW = KV_PARTS * NSA_DH
PAGES_PER_STEP = 16

VMEM_LIMIT_BYTES = 56 * 1024 * 1024
LANES = 128

COL_MG = 0
COL_GV = 4096
COL_GG = 6144
COL_NQ = 8192
COL_GQ = 10240
COL_GK = 11264
COL_NKV = 12288
COL_SM = 13824
N_PROJ = 14336
SM_GATE0 = GLA_GATE_RANK
SM_USED = GLA_GATE_RANK + 3 * NSA_HEADS

_SRC_SEGMENTS = ((9792, 13888), (2048, 4096), (4096, 6144), (6160, 8208), (0, 1024), (1024, 2048),
                 (8208, 9744), (6144, 6160), (9744, 9792))

SAMPLE_ROWS = 8


def _cparams(sem):
    return pltpu.CompilerParams(dimension_semantics=sem, vmem_limit_bytes=VMEM_LIMIT_BYTES)


def _rms(x, g):
    return x * lax.rsqrt(jnp.mean(x * x, axis=-1, keepdims=True) + NORM_EPS) * g


def _dot(a, b, precision=None):
    return jnp.dot(a, b, preferred_element_type=F32, precision=precision)


def _dot_nt(a, b, precision=None):
    return lax.dot_general(a, b, (((1,), (1,)), ((), ())), preferred_element_type=F32, precision=precision)


def _dot_tn(a, b, precision=None):
    return lax.dot_general(a, b, (((0,), (0,)), ((), ())), preferred_element_type=F32, precision=precision)


def _split_bf16(x):
    hi = x.astype(BF16)
    return hi, (x - hi.astype(F32)).astype(BF16)


def _dot_nt_3x(a, b):
    ah, al = _split_bf16(a)
    bh, bl = _split_bf16(b)
    return (_dot_nt(jnp.concatenate([ah, al], axis=1), jnp.concatenate([bh, bh], axis=1))
            + _dot_nt(ah, bl))


def _iota(shape, dim):
    return lax.broadcasted_iota(jnp.int32, shape, dim)


def _masked_softmax(s, valid):
    s = jnp.where(valid, s, NEG_INF)
    m = jnp.max(s, axis=-1, keepdims=True)
    p = jnp.where(valid, jnp.exp(s - m), 0.0)
    return p / jnp.maximum(jnp.sum(p, axis=-1, keepdims=True), 1e-30)


def _norm_matmul_kernel(x_ref, g_ref, w_ref, o_ref, h_ref):
    @pl.when(pl.program_id(1) == 0)
    def _():
        h_ref[...] = _rms(x_ref[...], g_ref[...]).astype(BF16)

    o_ref[...] = _dot(h_ref[...], w_ref[...])


def norm_matmul(x, g, w, layer, tm, tn):
    m, k = x.shape
    n = w.shape[2]
    return pl.pallas_call(
        _norm_matmul_kernel,
        grid=(m // tm, n // tn),
        in_specs=[pl.BlockSpec((tm, k), lambda i, j: (i, 0)),
                  pl.BlockSpec((1, k), lambda i, j: (0, 0)),
                  pl.BlockSpec((None, k, tn), lambda i, j: (layer, 0, j))],
        out_specs=pl.BlockSpec((tm, tn), lambda i, j: (i, j)),
        out_shape=jax.ShapeDtypeStruct((m, n), F32),
        scratch_shapes=[pltpu.VMEM((tm, k), BF16)],
        compiler_params=_cparams(("parallel", "arbitrary")),
        name="norm_matmul",
    )(x, g, w)


def _merge_wo_kernel(x_ref, mg0_ref, mg1_ref, oa_ref, ob_ref, g_ref, wo_ref, o_ref):
    ob = jnp.concatenate([ob_ref[h] for h in range(NSA_HEADS)], axis=1)
    a = jax.nn.sigmoid(mg0_ref[...]) * oa_ref[...] + jax.nn.sigmoid(mg1_ref[...]) * ob
    m = _dot(a.astype(BF16), wo_ref[...])
    o_ref[...] = x_ref[...] + _rms(m, g_ref[...])


def merge_wo(x, proj, o_a, o_b_hm, g, wo, layer, tm):
    m, d = x.shape
    return pl.pallas_call(
        _merge_wo_kernel,
        grid=(m // tm,),
        in_specs=[pl.BlockSpec((tm, d), lambda i: (i, 0)),
                  pl.BlockSpec((tm, d), lambda i: (i, COL_MG // D_MODEL)),
                  pl.BlockSpec((tm, d), lambda i: (i, COL_MG // D_MODEL + 1)),
                  pl.BlockSpec((tm, d), lambda i: (i, 0)),
                  pl.BlockSpec((NSA_HEADS, tm, NSA_DH), lambda i: (0, i, 0)),
                  pl.BlockSpec((1, d), lambda i: (0, 0)),
                  pl.BlockSpec((None, d, d), lambda i: (layer, 0, 0))],
        out_specs=pl.BlockSpec((tm, d), lambda i: (i, 0)),
        out_shape=jax.ShapeDtypeStruct((m, d), F32),
        compiler_params=_cparams(("parallel",)),
        name="merge_wo",
    )(x, proj, proj, o_a, o_b_hm, g, wo)


def _mlp_kernel(x_ref, g2_ref, g3_ref, wu_ref, wd_ref, o_ref, h_ref, acc_ref):
    j = pl.program_id(1)

    @pl.when(j == 0)
    def _():
        h_ref[...] = _rms(x_ref[...], g2_ref[...]).astype(BF16)
        acc_ref[...] = jnp.zeros_like(acc_ref)

    u = _dot(h_ref[...], wu_ref[...])
    u = jnp.square(jnp.maximum(u, 0.0)).astype(BF16)
    acc_ref[...] += _dot(u, wd_ref[...])

    @pl.when(j == pl.num_programs(1) - 1)
    def _():
        o_ref[...] = x_ref[...] + _rms(acc_ref[...], g3_ref[...])


def mlp(x, g2, g3, wu, wd, layer, tm, th):
    m, d = x.shape
    hid = wu.shape[2]
    return pl.pallas_call(
        _mlp_kernel,
        grid=(m // tm, hid // th),
        in_specs=[pl.BlockSpec((tm, d), lambda i, j: (i, 0)),
                  pl.BlockSpec((1, d), lambda i, j: (0, 0)),
                  pl.BlockSpec((1, d), lambda i, j: (0, 0)),
                  pl.BlockSpec((None, d, th), lambda i, j: (layer, 0, j)),
                  pl.BlockSpec((None, th, d), lambda i, j: (layer, j, 0))],
        out_specs=pl.BlockSpec((tm, d), lambda i, j: (i, 0)),
        out_shape=jax.ShapeDtypeStruct((m, d), F32),
        scratch_shapes=[pltpu.VMEM((tm, d), BF16), pltpu.VMEM((tm, d), F32)],
        compiler_params=_cparams(("parallel", "arbitrary")),
        name="mlp",
    )(x, g2, g3, wu, wd)


def _gla_kernel(q_ref, k_ref, v_ref, gg_ref, sm_ref, wa2_ref, ba_ref, gn_ref, s0_ref,
                o_ref, sout_ref, st_ref, *, rows_in, n_inner, n_valid, has_s0):
    c = pl.program_id(1)
    C = GLA_CHUNK

    @pl.when(c == 0)
    def _():
        for h in range(GLA_HEADS):
            if has_s0:
                st_ref[h] = s0_ref[h].T
            else:
                st_ref[h] = jnp.zeros((GLA_DV, GLA_DK), F32)

    rp = n_inner * C

    def load(ref):
        x = ref[...]
        if rows_in == C:
            return x
        assert n_inner == 1
        return jnp.concatenate([x, jnp.zeros((C - rows_in, x.shape[1]), x.dtype)], axis=0)

    row = _iota((rp, 1), 0)
    col = _iota((1, rp), 1)
    same_chunk = (row >> GLA_CHUNK_SHIFT) == (col >> GLA_CHUNK_SHIFT)
    causal = same_chunk & (row >= col)
    tril = causal.astype(BF16)
    ones_blk = same_chunk.astype(BF16)
    live = (row & (C - 1)) < n_valid

    ga_h, ga_l = _split_bf16(load(sm_ref)[:, :GLA_GATE_RANK])
    w_h, w_l = _split_bf16(wa2_ref[...])
    z = _dot(ga_h, w_h) + _dot(ga_l, w_h) + _dot(ga_h, w_l) + ba_ref[...]
    log_a = (jnp.minimum(z, 0.0) - jnp.log1p(jnp.exp(-jnp.abs(z)))) / GLA_TAU
    k = load(k_ref)
    if n_valid < C:
        log_a = jnp.where(live, log_a, 0.0)
        k = jnp.where(live, k, 0.0)
    la_h, la_l = _split_bf16(log_a)
    b = _dot(tril, la_h) + _dot(tril, la_l)
    b_last = _dot(ones_blk, la_h) + _dot(ones_blk, la_l)
    q_dec = (load(q_ref) * (GLA_DK ** -0.5) * jnp.exp(b)).astype(BF16)
    k_dec = (k * jnp.exp(-b)).astype(BF16)
    k_end = (k * jnp.exp(b_last - b)).astype(BF16)
    decay = jnp.exp(b_last)
    vb = load(v_ref).astype(BF16)

    for h in range(GLA_HEADS):
        kq = slice(h * GLA_DK, (h + 1) * GLA_DK)
        kv = slice(h * GLA_DV, (h + 1) * GLA_DV)
        att = jnp.where(causal, _dot_nt(q_dec[:, kq], k_dec[:, kq]), 0.0)
        intra = _dot(att.astype(BF16), vb[:, kv])
        for ci in range(n_inner):
            cr = slice(ci * C, (ci + 1) * C)
            st = st_ref[h]
            o = intra[cr] + _dot_nt(q_dec[cr, kq], st.astype(BF16))
            st_ref[h] = st * decay[ci * C:ci * C + 1, kq] + _dot_tn(vb[cr, kv], k_end[cr, kq])
            o_n = _rms(o, gn_ref[...])
            rs = pl.ds(ci * rows_in, rows_in)
            gg = gg_ref[rs, kv]
            o_ref[rs, kv] = o_n[:rows_in] * (gg * jax.nn.sigmoid(gg))

    @pl.when(c == pl.num_programs(1) - 1)
    def _():
        for h in range(GLA_HEADS):
            sout_ref[h] = st_ref[h].T


def gla(proj, wa2, ba, gn, s0, layer, batch, rows_per_batch, rows_in, n_inner, n_valid):
    m = proj.shape[0]
    r = rows_in * n_inner
    n_steps = rows_per_batch // r
    has_s0 = s0 is not None
    if s0 is None:
        s0 = jnp.zeros((1, 1, GLA_HEADS, GLA_DK, GLA_DV), F32)
        s0_map = lambda b, c: (0, 0, 0, 0, 0)
    else:
        s0_map = lambda b, c: (layer, b, 0, 0, 0)
    wk, wv = GLA_HEADS * GLA_DK, GLA_HEADS * GLA_DV
    rowmap = lambda off: (lambda b, c: (b * n_steps + c, off))
    state_blk = (None, GLA_HEADS, GLA_DK, GLA_DV)
    kern = functools.partial(_gla_kernel, rows_in=rows_in, n_inner=n_inner, n_valid=n_valid, has_s0=has_s0)
    return pl.pallas_call(
        kern,
        grid=(batch, n_steps),
        in_specs=[pl.BlockSpec((r, wk), rowmap(COL_GQ // wk)),
                  pl.BlockSpec((r, wk), rowmap(COL_GK // wk)),
                  pl.BlockSpec((r, wv), rowmap(COL_GV // wv)),
                  pl.BlockSpec((r, wv), rowmap(COL_GG // wv)),
                  pl.BlockSpec((r, LANES), rowmap(COL_SM // LANES)),
                  pl.BlockSpec((GLA_GATE_RANK, wk), lambda b, c: (0, 0)),
                  pl.BlockSpec((1, wk), lambda b, c: (0, 0)),
                  pl.BlockSpec((1, GLA_DV), lambda b, c: (0, 0)),
                  pl.BlockSpec((None,) + state_blk, s0_map)],
        out_specs=[pl.BlockSpec((r, wv), rowmap(0)),
                   pl.BlockSpec(state_blk, lambda b, c: (b, 0, 0, 0))],
        out_shape=[jax.ShapeDtypeStruct((m, D_MODEL), F32),
                   jax.ShapeDtypeStruct((batch, GLA_HEADS, GLA_DK, GLA_DV), F32)],
        scratch_shapes=[pltpu.VMEM((GLA_HEADS, GLA_DV, GLA_DK), F32)],
        compiler_params=_cparams(("parallel", "arbitrary")),
        name="gla",
    )(proj, proj, proj, proj, proj, wa2, ba, gn, s0)


def _rope(x, c, sa, sb):
    return x * c + pltpu.roll(x, LANES - ROPE_HALF, 1) * sa + pltpu.roll(x, ROPE_HALF, 1) * sb


def _nsa_prep_kernel(nq_ref, nkv_ref, c_ref, sa_ref, sb_ref, q_ref, cmp_ref, sel_ref, win_ref):
    c, sa, sb = c_ref[...], sa_ref[...], sb_ref[...]
    for h in range(NSA_HEADS):
        q_ref[h] = _rope(nq_ref[:, h * NSA_DH:(h + 1) * NSA_DH], c, sa, sb)
    for s, out in enumerate((cmp_ref, sel_ref, win_ref)):
        base = s * KV_ROW
        for hh in range(NSA_KV_HEADS):
            lo = hh * NSA_DH
            out[:, lo:lo + NSA_DH] = _rope(nkv_ref[:, base + lo:base + lo + NSA_DH], c, sa, sb)
        half = NSA_KV_HEADS * NSA_DH
        out[:, half:] = nkv_ref[:, base + half:base + KV_ROW]


def nsa_prep(proj, tabs, tr):
    m = proj.shape[0]
    n_tab = tabs[0].shape[0] // tr
    tab_spec = pl.BlockSpec((tr, LANES), lambda i: (i % n_tab, 0))
    nkv_w = 3 * KV_ROW
    return pl.pallas_call(
        _nsa_prep_kernel,
        grid=(m // tr,),
        in_specs=[pl.BlockSpec((tr, D_MODEL), lambda i: (i, COL_NQ // D_MODEL)),
                  pl.BlockSpec((tr, nkv_w), lambda i: (i, COL_NKV // nkv_w)),
                  tab_spec, tab_spec, tab_spec],
        out_specs=[pl.BlockSpec((NSA_HEADS, tr, NSA_DH), lambda i: (0, i, 0)),
                   pl.BlockSpec((tr, KV_ROW), lambda i: (i, 0)),
                   pl.BlockSpec((tr, KV_ROW), lambda i: (i, 0)),
                   pl.BlockSpec((tr, KV_ROW), lambda i: (i, 0))],
        out_shape=[jax.ShapeDtypeStruct((NSA_HEADS, m, NSA_DH), F32),
                   jax.ShapeDtypeStruct((m, KV_ROW), F32),
                   jax.ShapeDtypeStruct((m, KV_ROW), F32),
                   jax.ShapeDtypeStruct((m, KV_ROW), F32)],
        compiler_params=_cparams(("parallel",)),
        name="nsa_prep",
    )(proj, proj, *tabs)


def rope_tables(pos):
    inv = 1.0 / (ROPE_THETA ** (jnp.arange(ROPE_HALF, dtype=F32) / ROPE_HALF))
    ang = pos.astype(F32)[:, None] * inv[None, :]
    cos, sin = jnp.cos(ang), jnp.sin(ang)
    n = pos.shape[0]
    rest = LANES - ROPE_DIM
    c = jnp.concatenate([cos, cos, jnp.ones((n, rest), F32)], axis=1)
    sa = jnp.concatenate([-sin, jnp.zeros((n, LANES - ROPE_HALF), F32)], axis=1)
    sb = jnp.concatenate([jnp.zeros((n, ROPE_HALF), F32), sin, jnp.zeros((n, rest), F32)], axis=1)
    return c, sa, sb


def _chunk_sums_body(x, pe_ref, mix_ref, a_ref, b_ref):
    r = x.shape[0]
    nch = r // CMP_STRIDE
    for kv in range(2):
        pe, mix = pe_ref[kv], mix_ref[kv]
        for hh in range(NSA_KV_HEADS):
            lo = (kv * NSA_KV_HEADS + hh) * NSA_DH
            xs = x[:, lo:lo + NSA_DH].reshape(nch, CMP_STRIDE, NSA_DH)
            a_ref[:, lo:lo + NSA_DH] = jnp.sum((xs + pe[:CMP_STRIDE]) * mix[:CMP_STRIDE], axis=1)
            b_ref[:, lo:lo + NSA_DH] = jnp.sum((xs + pe[CMP_STRIDE:]) * mix[CMP_STRIDE:], axis=1)


def _chunk_sums_kernel(x_ref, pe_ref, mix_ref, a_ref, b_ref):
    _chunk_sums_body(x_ref[...], pe_ref, mix_ref, a_ref, b_ref)


def _page_specs(layer):
    def spec(i):
        return pl.BlockSpec((None, None, PAGE_SIZE * KV_PARTS, NSA_DH),
                            lambda b, s, pt: (layer, pt[b, s * PAGES_PER_STEP + i], 0, 0))
    return [spec(i) for i in range(PAGES_PER_STEP)]


def _page_part(page_ref, part):
    return page_ref[pl.ds(part, PAGE_SIZE, stride=KV_PARTS), :]


def _chunk_sums_paged_kernel(pt_ref, *refs):
    page_refs, (pe_ref, mix_ref, a_ref, b_ref) = refs[:PAGES_PER_STEP], refs[PAGES_PER_STEP:]
    nch = PAGE_SIZE // CMP_STRIDE
    for i, page_ref in enumerate(page_refs):
        rows = slice(i * nch, (i + 1) * nch)
        for kv in range(2):
            pe, mix = pe_ref[kv], mix_ref[kv]
            for hh in range(NSA_KV_HEADS):
                part = kv * NSA_KV_HEADS + hh
                lo = part * NSA_DH
                xs = _page_part(page_ref, part).reshape(nch, CMP_STRIDE, NSA_DH)
                a_ref[rows, lo:lo + NSA_DH] = jnp.sum((xs + pe[:CMP_STRIDE]) * mix[:CMP_STRIDE], axis=1)
                b_ref[rows, lo:lo + NSA_DH] = jnp.sum((xs + pe[CMP_STRIDE:]) * mix[CMP_STRIDE:], axis=1)


def chunk_sums(rows, pe, mix, tr):
    m = rows.shape[0]
    nch = tr // CMP_STRIDE
    full = lambda i: (0, 0, 0)
    return pl.pallas_call(
        _chunk_sums_kernel,
        grid=(m // tr,),
        in_specs=[pl.BlockSpec((tr, KV_ROW), lambda i: (i, 0)),
                  pl.BlockSpec((2, CMP_LEN, NSA_DH), full),
                  pl.BlockSpec((2, CMP_LEN, NSA_DH), full)],
        out_specs=[pl.BlockSpec((nch, KV_ROW), lambda i: (i, 0))] * 2,
        out_shape=[jax.ShapeDtypeStruct((m // CMP_STRIDE, KV_ROW), F32)] * 2,
        compiler_params=_cparams(("parallel",)),
        name="chunk_sums",
    )(rows, pe, mix)


def chunk_sums_paged(cache, layer, page_table, pe, mix):
    batch, n_pages = page_table.shape
    nch = PAGES_PER_STEP * PAGE_SIZE // CMP_STRIDE
    n_steps = n_pages // PAGES_PER_STEP
    full = lambda b, s, pt: (0, 0, 0)
    gs = pltpu.PrefetchScalarGridSpec(
        num_scalar_prefetch=1,
        grid=(batch, n_steps),
        in_specs=_page_specs(layer) + [pl.BlockSpec((2, CMP_LEN, NSA_DH), full),
                                       pl.BlockSpec((2, CMP_LEN, NSA_DH), full)],
        out_specs=[pl.BlockSpec((nch, KV_ROW), lambda b, s, pt: (b * n_steps + s, 0))] * 2,
    )
    return pl.pallas_call(
        _chunk_sums_paged_kernel,
        grid_spec=gs,
        out_shape=[jax.ShapeDtypeStruct((batch * n_steps * nch, KV_ROW), F32)] * 2,
        compiler_params=_cparams(("parallel", "arbitrary")),
        name="chunk_sums_paged",
    )(page_table, *([cache] * PAGES_PER_STEP), pe, mix)


def _cmp_mlp_kernel(a_ref, b_ref, w1_ref, w2_ref, o_ref):
    h = a_ref[...] + b_ref[...]
    for kv in range(2):
        for hh in range(NSA_KV_HEADS):
            lo = (kv * NSA_KV_HEADS + hh) * NSA_DH
            y = jax.nn.gelu(_dot(h[:, lo:lo + NSA_DH], w1_ref[kv], HI))
            o_ref[:, lo:lo + NSA_DH] = _dot(y, w2_ref[kv], HI)


def cmp_mlp(a, b_shift, w1, w2, tr):
    m = a.shape[0]
    return pl.pallas_call(
        _cmp_mlp_kernel,
        grid=(m // tr,),
        in_specs=[pl.BlockSpec((tr, KV_ROW), lambda i: (i, 0)),
                  pl.BlockSpec((tr, KV_ROW), lambda i: (i, 0)),
                  pl.BlockSpec((2, NSA_DH, CMP_HIDDEN), lambda i: (0, 0, 0)),
                  pl.BlockSpec((2, CMP_HIDDEN, NSA_DH), lambda i: (0, 0, 0))],
        out_specs=pl.BlockSpec((tr, KV_ROW), lambda i: (i, 0)),
        out_shape=jax.ShapeDtypeStruct((m, KV_ROW), F32),
        compiler_params=_cparams(("parallel",)),
        name="cmp_mlp",
    )(a, b_shift, w1, w2)


def compressed_kv(a, b, batch, w1, w2, tr):
    nch = a.shape[0] // batch
    b3 = b.reshape(batch, nch, KV_ROW)
    b_shift = jnp.concatenate([b3[:, 1:], jnp.zeros((batch, 1, KV_ROW), F32)], axis=1).reshape(batch * nch, KV_ROW)
    return cmp_mlp(a, b_shift, w1, w2, tr)


def _select_topn(imp, n_top):
    j = _iota(imp.shape, 1)
    big = jnp.int32(imp.shape[1])
    sel = jnp.zeros(imp.shape, F32)
    for _ in range(n_top):
        m = jnp.max(imp, axis=-1, keepdims=True)
        idx = jnp.min(jnp.where(imp == m, j, big), axis=-1, keepdims=True)
        hit = j == idx
        sel = jnp.where(hit, 1.0, sel)
        imp = jnp.where(hit, NEG_INF, imp)
    return sel


def _importance(p_grp, qpos, n_cmp_pad, n_slc, width):
    n_r = _iota((n_cmp_pad, 1), 0) * CMP_STRIDE
    j_c = _iota((1, width), 1)
    overlap = ((n_r < (j_c + 1) * SEL_LEN) & (n_r + CMP_LEN > j_c * SEL_LEN)).astype(F32)
    imp = _dot(p_grp, overlap, HI)
    cur = qpos >> SEL_SHIFT
    forced = (j_c == 0) | (j_c == cur) | (j_c == cur - 1)
    imp = jnp.where(forced, FORCED_SCORE, imp)
    imp = jnp.where(j_c > cur, -1.0, imp)
    return jnp.where(j_c >= n_slc, -2.0, imp)


LOG2E = 1.4426950408889634


def _nsa_prompt_kernel(q_ref, kc_ref, vc_ref, ks_ref, vs_ref, kw_ref, vw_ref, sm_ref, o_ref,
                       qb_ref, s_ref, p_ref, m_ref, l_ref, a_ref, acc_ref, *, Q, T, TK):
    G = NSA_GROUP
    hk = pl.program_id(1)
    qi = pl.program_id(2)
    q0 = qi * Q
    scale = NSA_DH ** -0.5
    n_cmp_pad = kc_ref.shape[0]
    n_slc = T // SEL_LEN
    qpos = q0 + _iota((Q, 1), 0)
    qs = q_ref[...].reshape(G * Q, NSA_DH)
    qb_ref[...] = (qs * (scale * LOG2E)).astype(BF16)

    ng = jax.nn.sigmoid(sm_ref[...])

    def gate(g, c3):
        lane0 = SM_GATE0 + 3 * g + c3
        lane1 = lane0 + 3 * G
        return jnp.where(hk == 0, ng[:, lane0:lane0 + 1], ng[:, lane1:lane1 + 1])

    def softmax_tile(kb, vb, bias, width):
        s_ref[:, :width] = _dot_nt(qb_ref[...], kb)
        for g in range(G):
            rows = pl.ds(g * Q, Q)
            s = s_ref[rows, :width] + bias
            m_old = m_ref[rows, :]
            m_new = jnp.maximum(m_old, jnp.max(s, axis=-1, keepdims=True))
            alpha = jnp.exp2(m_old - m_new)
            p = jnp.exp2(s - jnp.tile(m_new, (1, width // LANES)))
            l_ref[rows, :] = alpha * l_ref[rows, :] + jnp.sum(p, axis=-1, keepdims=True)
            a_ref[rows, :] = alpha
            m_ref[rows, :] = m_new
            p_ref[rows, :width] = p.astype(BF16)
        return _dot(p_ref[:, :width], vb)

    def reset():
        m_ref[...] = jnp.full_like(m_ref, NEG_INF)
        l_ref[...] = jnp.zeros_like(l_ref)

    cmp_valid = _iota((1, n_cmp_pad), 1) * CMP_STRIDE + (CMP_LEN - 1) <= qpos
    p_grp = jnp.zeros((Q, n_cmp_pad), F32)
    s_cmp = _dot_nt_3x(qs, kc_ref[...]) * scale
    vcb = vc_ref[...].astype(BF16)
    for g in range(G):
        p = _masked_softmax(s_cmp[g * Q:(g + 1) * Q], cmp_valid)
        p_grp = p_grp + p
        o_ref[g] = gate(g, 0) * _dot(p.astype(BF16), vcb)

    ks0 = pl.multiple_of(jnp.maximum(qi - WINDOW // WIN_QBLK, 0) * WIN_QBLK, WIN_QBLK)
    wlen = WINDOW + WIN_QBLK
    wpos = ks0 + _iota((1, wlen), 1)
    win_bias = jnp.where((wpos <= qpos) & (wpos > qpos - WINDOW), 0.0, NEG_INF)
    reset()
    pv = softmax_tile(kw_ref[pl.ds(ks0, wlen), :].astype(BF16), vw_ref[pl.ds(ks0, wlen), :].astype(BF16),
                      win_bias, wlen)
    for g in range(G):
        rows = pl.ds(g * Q, Q)
        o_ref[g] = o_ref[g] + gate(g, 2) * (pv[g * Q:(g + 1) * Q] / jnp.maximum(l_ref[rows, :], 1e-30))

    n_top = min(SEL_TOPN, n_slc)
    j_r = _iota((n_slc, 1), 0)
    n_c = _iota((1, n_cmp_pad), 1) * CMP_STRIDE
    overlap_t = ((n_c < (j_r + 1) * SEL_LEN) & (n_c + CMP_LEN > j_r * SEL_LEN)).astype(F32)
    imp = _dot_nt(overlap_t, p_grp, HI)
    cur = (q0 + _iota((1, Q), 1)) >> SEL_SHIFT
    imp = jnp.where((j_r == 0) | (j_r == cur) | (j_r == cur - 1), FORCED_SCORE, imp)
    imp = jnp.where(j_r > cur, -1.0, imp)
    rank = jnp.zeros((n_slc, Q), F32)
    for jp in range(n_slc):
        other = imp[jp:jp + 1, :]
        beats = (other > imp) | ((other == imp) & (j_r > jp))
        rank = rank + jnp.where(beats, 1.0, 0.0)
    sel_t = jnp.where(rank < n_top, 1.0, 0.0).astype(BF16)

    reset()
    acc_ref[...] = jnp.zeros_like(acc_ref)

    def key_tile(kt, carry):
        k0 = pl.multiple_of(kt * TK, TK)
        kpos = k0 + _iota((1, TK), 1)
        expand = (j_r == (kpos >> SEL_SHIFT)).astype(BF16)
        keymask = _dot_tn(sel_t, expand)
        bias = jnp.where((keymask > 0.5) & (kpos <= qpos), 0.0, NEG_INF)
        pv = softmax_tile(ks_ref[pl.ds(k0, TK), :].astype(BF16), vs_ref[pl.ds(k0, TK), :].astype(BF16), bias, TK)
        acc_ref[...] = a_ref[...] * acc_ref[...] + pv
        return carry

    lax.fori_loop(0, (q0 + Q + TK - 1) // TK, key_tile, 0)

    for g in range(G):
        rows = pl.ds(g * Q, Q)
        o_ref[g] = o_ref[g] + gate(g, 1) * (acc_ref[rows, :] / jnp.maximum(l_ref[rows, :], 1e-30))


def nsa_prompt(q_hm, kvc, sel_rows, win_rows, proj, batch, T, Q):
    m = batch * T
    nq = T // Q
    n_cmp_pad = kvc.shape[0] // batch
    G = NSA_GROUP
    kern = functools.partial(_nsa_prompt_kernel, Q=Q, T=T, TK=512)
    kcol = lambda off: (lambda b, hk, qi: (b, off + hk))
    return pl.pallas_call(
        kern,
        grid=(batch, NSA_KV_HEADS, nq),
        in_specs=[pl.BlockSpec((G, Q, NSA_DH), lambda b, hk, qi: (hk, b * nq + qi, 0)),
                  pl.BlockSpec((n_cmp_pad, NSA_DH), kcol(0)),
                  pl.BlockSpec((n_cmp_pad, NSA_DH), kcol(NSA_KV_HEADS)),
                  pl.BlockSpec((T, NSA_DH), kcol(0)),
                  pl.BlockSpec((T, NSA_DH), kcol(NSA_KV_HEADS)),
                  pl.BlockSpec((T, NSA_DH), kcol(0)),
                  pl.BlockSpec((T, NSA_DH), kcol(NSA_KV_HEADS)),
                  pl.BlockSpec((Q, LANES), lambda b, hk, qi: (b * nq + qi, COL_SM // LANES))],
        out_specs=pl.BlockSpec((G, Q, NSA_DH), lambda b, hk, qi: (hk, b * nq + qi, 0)),
        out_shape=jax.ShapeDtypeStruct((NSA_HEADS, m, NSA_DH), F32),
        scratch_shapes=[pltpu.VMEM((G * Q, NSA_DH), BF16),
                        pltpu.VMEM((G * Q, WINDOW + WIN_QBLK), F32), pltpu.VMEM((G * Q, WINDOW + WIN_QBLK), BF16),
                        pltpu.VMEM((G * Q, LANES), F32), pltpu.VMEM((G * Q, LANES), F32),
                        pltpu.VMEM((G * Q, LANES), F32), pltpu.VMEM((G * Q, NSA_DH), F32)],
        compiler_params=_cparams(("parallel", "parallel", "arbitrary")),
        name="nsa_prompt",
    )(q_hm, kvc, kvc, sel_rows, sel_rows, win_rows, win_rows, proj)


SEL_W = 384


def _sample_cmp_select_kernel(q_ref, kc_ref, vc_ref, ocmp_ref, selm_ref, *, n_slc):
    G, R = NSA_GROUP, SAMPLE_ROWS
    scale = NSA_DH ** -0.5
    n_cmp_pad = kc_ref.shape[0]
    qs = q_ref[...].reshape(G * R, NSA_DH)
    qpos_s = PAST_LEN + (_iota((G * R, 1), 0) & (R - 1))
    qpos = PAST_LEN + _iota((R, 1), 0)
    s = _dot_nt(qs, kc_ref[...], HI) * scale
    end = _iota((1, n_cmp_pad), 1) * CMP_STRIDE + (CMP_LEN - 1)
    p = _masked_softmax(s, end <= qpos_s)
    o_cmp = _dot(p.astype(BF16), vc_ref[...].astype(BF16))
    for g in range(G):
        ocmp_ref[g] = o_cmp[g * R:(g + 1) * R]
    p_grp = jnp.sum(p.reshape(G, R, n_cmp_pad), axis=0)
    imp = _importance(p_grp, qpos, n_cmp_pad, n_slc, SEL_W)
    selm_ref[...] = _select_topn(imp, min(SEL_TOPN, n_slc))


def sample_cmp_select(q_hm, kvc, batch, n_slc):
    G, R = NSA_GROUP, SAMPLE_ROWS
    n_cmp_pad = kvc.shape[0] // batch
    kern = functools.partial(_sample_cmp_select_kernel, n_slc=n_slc)
    return pl.pallas_call(
        kern,
        grid=(batch, NSA_KV_HEADS),
        in_specs=[pl.BlockSpec((G, R, NSA_DH), lambda b, hk: (hk, b, 0)),
                  pl.BlockSpec((n_cmp_pad, NSA_DH), lambda b, hk: (b, hk)),
                  pl.BlockSpec((n_cmp_pad, NSA_DH), lambda b, hk: (b, NSA_KV_HEADS + hk))],
        out_specs=[pl.BlockSpec((G, R, NSA_DH), lambda b, hk: (hk, b, 0)),
                   pl.BlockSpec((None, None, R, SEL_W), lambda b, hk: (b, hk, 0, 0))],
        out_shape=[jax.ShapeDtypeStruct((NSA_HEADS, batch * R, NSA_DH), F32),
                   jax.ShapeDtypeStruct((batch, NSA_KV_HEADS, R, SEL_W), F32)],
        compiler_params=_cparams(("parallel", "parallel")),
        name="sample_cmp_select",
    )(q_hm, kvc, kvc)


def _sample_attn_kernel(pt_ref, q_ref, selm_ref, *refs, n_steps):
    page_refs = refs[:PAGES_PER_STEP]
    nsel_ref, cwin_ref, nwin_ref, sm_ref, ocmp_ref, o_ref, m_ref, l_ref, acc_ref = refs[PAGES_PER_STEP:]
    G, R = NSA_GROUP, SAMPLE_ROWS
    p = pl.program_id(1)
    scale = NSA_DH ** -0.5
    t_s = _iota((G * R, 1), 0) & (R - 1)
    n_keys = PAGES_PER_STEP * PAGE_SIZE

    @pl.when(p == 0)
    def _():
        m_ref[...] = jnp.full_like(m_ref, NEG_INF)
        l_ref[...] = jnp.zeros_like(l_ref)
        acc_ref[...] = jnp.zeros_like(acc_ref)

    def update(hk, s, valid, v):
        s = jnp.where(valid, s, NEG_INF)
        m_old = m_ref[hk]
        m_new = jnp.maximum(m_old, jnp.max(s, axis=-1, keepdims=True))
        alpha = jnp.exp(m_old - m_new)
        pr = jnp.where(valid, jnp.exp(s - m_new), 0.0)
        l_ref[hk] = alpha * l_ref[hk] + jnp.sum(pr, axis=-1, keepdims=True)
        acc_ref[hk] = alpha * acc_ref[hk] + _dot(pr.astype(BF16), v.astype(BF16))
        m_ref[hk] = m_new

    key = _iota((1, n_keys), 1)
    blk_of_key = (n_keys // SEL_LEN) * p + (key >> SEL_SHIFT)
    expand = (_iota((SEL_W, 1), 0) == blk_of_key).astype(BF16)
    for hk in range(NSA_KV_HEADS):
        qs = q_ref[hk * G:(hk + 1) * G].reshape(G * R, NSA_DH).astype(BF16)
        kpg = jnp.concatenate([_page_part(r, hk).astype(BF16) for r in page_refs], axis=0)
        vpg = jnp.concatenate([_page_part(r, NSA_KV_HEADS + hk).astype(BF16) for r in page_refs], axis=0)
        s = _dot_nt(qs, kpg) * scale
        keymask = _dot(selm_ref[hk].astype(BF16), expand)
        valid = jnp.concatenate([keymask] * G, axis=0) > 0.5
        update(hk, s, valid, vpg)

    def win_part(ref, part):
        return ref[pl.ds(part, WINDOW, stride=KV_PARTS), :]

    @pl.when(p == n_steps - 1)
    def _():
        ng = jax.nn.sigmoid(sm_ref[...])
        new_blk = PAST_LEN // SEL_LEN
        zpad = jnp.zeros((LANES - R, NSA_DH), F32)
        jn = _iota((1, LANES), 1)
        for hk in range(NSA_KV_HEADS):
            qs = q_ref[hk * G:(hk + 1) * G].reshape(G * R, NSA_DH).astype(BF16)
            klo, vlo = hk * NSA_DH, (NSA_KV_HEADS + hk) * NSA_DH
            kn = jnp.concatenate([nsel_ref[:, klo:klo + NSA_DH], zpad], axis=0)
            vn = jnp.concatenate([nsel_ref[:, vlo:vlo + NSA_DH], zpad], axis=0)
            s = _dot_nt(qs, kn.astype(BF16)) * scale
            picked = jnp.concatenate([selm_ref[hk][:, new_blk:new_blk + 1]] * G, axis=0) > 0.5
            update(hk, s, picked & (jn <= t_s) & (jn < R), vn)
            o_sel = acc_ref[hk] / jnp.maximum(l_ref[hk], 1e-30)
            kw = jnp.concatenate([win_part(cwin_ref, hk), nwin_ref[:, klo:klo + NSA_DH], zpad], axis=0)
            vw = jnp.concatenate([win_part(cwin_ref, NSA_KV_HEADS + hk), nwin_ref[:, vlo:vlo + NSA_DH], zpad], axis=0)
            iw = _iota((1, WINDOW + LANES), 1)
            wvalid = ((iw < WINDOW) & (iw > t_s)) | ((iw >= WINDOW) & (iw - WINDOW <= t_s) & (iw - WINDOW < R))
            pw = _masked_softmax(_dot_nt(qs, kw.astype(BF16)) * scale, wvalid)
            o_win = _dot(pw.astype(BF16), vw.astype(BF16))
            for g in range(G):
                h = hk * G + g
                lane = SM_GATE0 + 3 * h
                rows = slice(g * R, (g + 1) * R)
                o_ref[h] = (ng[:, lane:lane + 1] * ocmp_ref[h] + ng[:, lane + 1:lane + 2] * o_sel[rows]
                            + ng[:, lane + 2:lane + 3] * o_win[rows])


def sample_attn(q_hm, selm, cache_sel, layer, page_table, new_sel, cache_win, new_win, proj, o_cmp_hm):
    batch, n_pages = page_table.shape
    n_steps = n_pages // PAGES_PER_STEP
    G, R = NSA_GROUP, SAMPLE_ROWS
    hm_spec = pl.BlockSpec((NSA_HEADS, R, NSA_DH), lambda b, p, pt: (0, b, 0))
    row_spec = pl.BlockSpec((R, KV_ROW), lambda b, p, pt: (b, 0))
    gs = pltpu.PrefetchScalarGridSpec(
        num_scalar_prefetch=1,
        grid=(batch, n_steps),
        in_specs=[hm_spec,
                  pl.BlockSpec((None, NSA_KV_HEADS, R, SEL_W), lambda b, p, pt: (b, 0, 0, 0))]
                 + _page_specs(layer)
                 + [row_spec,
                    pl.BlockSpec((None, None, WINDOW * KV_PARTS, NSA_DH), lambda b, p, pt: (layer, b, 0, 0)),
                    row_spec,
                    pl.BlockSpec((R, LANES), lambda b, p, pt: (b, COL_SM // LANES)),
                    hm_spec],
        out_specs=hm_spec,
        scratch_shapes=[pltpu.VMEM((NSA_KV_HEADS, G * R, 1), F32),
                        pltpu.VMEM((NSA_KV_HEADS, G * R, 1), F32),
                        pltpu.VMEM((NSA_KV_HEADS, G * R, NSA_DH), F32)],
    )
    kern = functools.partial(_sample_attn_kernel, n_steps=n_steps)
    return pl.pallas_call(
        kern,
        grid_spec=gs,
        out_shape=jax.ShapeDtypeStruct((NSA_HEADS, batch * R, NSA_DH), F32),
        compiler_params=_cparams(("parallel", "arbitrary")),
        name="sample_attn",
    )(page_table, q_hm, selm, *([cache_sel] * PAGES_PER_STEP), new_sel, cache_win, new_win, proj, o_cmp_hm)


PACK_W = 512
PACK_SRC = PACK_W // LANES + 1
_MAIN_SEGMENTS = _SRC_SEGMENTS[:7]


def _pack_table():
    first, shift = [], []
    for lo, hi in _MAIN_SEGMENTS:
        assert (hi - lo) % PACK_W == 0
        for c in range(lo, hi, PACK_W):
            first.append(c // LANES)
            shift.append(c % LANES)
    assert len(first) * PACK_W == COL_SM
    n = N_PROJ // PACK_W
    first += [0] * (n - len(first))
    shift += [0] * (n - len(shift))
    return np.array([first, shift], np.int32)


def _pack_kernel(tab_ref, *refs):
    srcs, (ga_ref, ng_ref, o_ref) = refs[:PACK_SRC], refs[PACK_SRC:]
    j = pl.program_id(1)
    n_main = COL_SM // PACK_W
    shift = tab_ref[1, j]
    lane = _iota((1, LANES), 1)

    @pl.when(j >= n_main)
    def _():
        sm = jnp.where(lane < GLA_GATE_RANK, ga_ref[...], jnp.where(lane < SM_USED, ng_ref[...], 0.0))
        o_ref[:, :LANES] = sm.astype(BF16)
        o_ref[:, LANES:] = jnp.zeros((o_ref.shape[0], PACK_W - LANES), BF16)

    for sv in sorted({lo % LANES for lo, _ in _MAIN_SEGMENTS}):
        @pl.when((j < n_main) & (shift == sv))
        def _():
            for i in range(PACK_W // LANES):
                cols = slice(i * LANES, (i + 1) * LANES)
                if sv == 0:
                    o_ref[:, cols] = srcs[i][...].astype(BF16)
                else:
                    a = pltpu.roll(srcs[i][...], LANES - sv, 1)
                    b = pltpu.roll(srcs[i + 1][...], LANES - sv, 1)
                    o_ref[:, cols] = jnp.where(lane < LANES - sv, a, b).astype(BF16)


def _pack_w_in(w_in):
    depth, d, n_in = w_in.shape
    (ga_lo, ga_hi), (ng_lo, ng_hi) = _SRC_SEGMENTS[7:]
    assert ga_lo % LANES == 0 and ga_hi - ga_lo == GLA_GATE_RANK
    assert ng_lo % LANES == GLA_GATE_RANK and ng_hi - ng_lo == SM_USED - GLA_GATE_RANK
    last_blk = (n_in - 1) // LANES

    def src_spec(i):
        return pl.BlockSpec((None, d, LANES), lambda l, j, tab: (l, 0, jnp.minimum(tab[0, j] + i, last_blk)))

    def fixed_spec(col):
        return pl.BlockSpec((None, d, LANES), lambda l, j, tab: (l, 0, col // LANES))

    gs = pltpu.PrefetchScalarGridSpec(
        num_scalar_prefetch=1,
        grid=(depth, N_PROJ // PACK_W),
        in_specs=[src_spec(i) for i in range(PACK_SRC)] + [fixed_spec(ga_lo), fixed_spec(ng_lo)],
        out_specs=pl.BlockSpec((None, d, PACK_W), lambda l, j, tab: (l, 0, j)),
    )
    return pl.pallas_call(
        _pack_kernel,
        grid_spec=gs,
        out_shape=jax.ShapeDtypeStruct((depth, d, N_PROJ), BF16),
        compiler_params=_cparams(("parallel", "arbitrary")),
        name="pack_w_in",
    )(jnp.asarray(_pack_table()), *([w_in] * (PACK_SRC + 2)))


def _layer_prompt(x, lw, tabs, layer, batch, T):
    proj = norm_matmul(x, lw['n0'], lw['w_in'], layer, 1024, 1024)
    o_a, gla_state = gla(proj, lw['wa2'], lw['ba'], lw['gn'], None, layer, batch, T, GLA_CHUNK, 4, GLA_CHUNK)
    q_hm, cmp_rows, sel_rows, win_rows = nsa_prep(proj, tabs, 512)
    a, b = chunk_sums(cmp_rows, lw['pe'], lw['mix'], 512)
    kvc = compressed_kv(a, b, batch, lw['w1'], lw['w2'], 128)
    o_b = nsa_prompt(q_hm, kvc, sel_rows, win_rows, proj, batch, T, WIN_QBLK)
    x = merge_wo(x, proj, o_a, o_b, lw['n1'], lw['w_o'], layer, 256)
    x = mlp(x, lw['n2'], lw['n3'], lw['w_up'], lw['w_down'], layer, 512, 1024)
    return x, (cmp_rows, sel_rows, win_rows, gla_state)


def _layer_sample(x, lw, tabs, layer, batch, cache_cmp, cache_sel, cache_win, s0, page_table):
    R = SAMPLE_ROWS
    m = batch * R
    proj = norm_matmul(x, lw['n0'], lw['w_in'], layer, m, 1024)
    o_a, gla_state = gla(proj, lw['wa2'], lw['ba'], lw['gn'], s0, layer, batch, R, R, 1, 4)
    q_hm, cmp_rows, sel_rows, win_rows = nsa_prep(proj, tabs, m)
    a, b = chunk_sums_paged(cache_cmp, layer, page_table, lw['pe'], lw['mix'])
    kvc = compressed_kv(a, b, batch, lw['w1'], lw['w2'], 1024)
    n_slc = -(-(PAST_LEN + 4) // SEL_LEN)
    o_cmp, selm = sample_cmp_select(q_hm, kvc, batch, n_slc)
    o_b = sample_attn(q_hm, selm, cache_sel, layer, page_table, sel_rows, cache_win, win_rows, proj, o_cmp)
    x = merge_wo(x, proj, o_a, o_b, lw['n1'], lw['w_o'], layer, m)
    x = mlp(x, lw['n2'], lw['n3'], lw['w_up'], lw['w_down'], layer, m, 1024)
    return x, (cmp_rows, sel_rows, win_rows, gla_state)


def kernel(x_prompt, x_sample, cache_cmp_kv, cache_sel_kv, cache_win_kv, state_gla, page_table,
           w_in, gla_wa2, gla_ba, gla_norm, cmp_pe, cmp_mix, cmp_w1, cmp_w2, w_o, norms, w_up, w_down):
    bp, T, d = x_prompt.shape
    bs, ss, _ = x_sample.shape
    R = SAMPLE_ROWS
    depth = w_in.shape[0]
    n_pool = cache_cmp_kv.shape[1]
    wb = cache_win_kv.shape[2]
    assert wb == WINDOW and T % 512 == 0 and ss <= R

    w_in_p = _pack_w_in(w_in)
    w_o_b, w_up_b, w_down_b = w_o.astype(BF16), w_up.astype(BF16), w_down.astype(BF16)
    cache_cmp = cache_cmp_kv.reshape(depth, n_pool, PAGE_SIZE * KV_PARTS, NSA_DH)
    cache_sel = cache_sel_kv.reshape(depth, n_pool, PAGE_SIZE * KV_PARTS, NSA_DH)
    cache_win = cache_win_kv.reshape(depth, bs, wb * KV_PARTS, NSA_DH)

    tabs_p = rope_tables(jnp.arange(T, dtype=jnp.int32))
    tabs_s = tuple(jnp.tile(t, (bs, 1)) for t in rope_tables(PAST_LEN + jnp.arange(R, dtype=jnp.int32)))

    y_p = x_prompt.reshape(bp * T, d)
    y_s = jnp.pad(x_sample, ((0, 0), (0, R - ss), (0, 0))).reshape(bs * R, d)
    st_p, st_s = [], []
    for layer in range(depth):
        lw = {'w_in': w_in_p, 'wa2': gla_wa2[layer], 'ba': gla_ba[layer][None, :],
              'gn': gla_norm[layer][None, :], 'pe': cmp_pe[layer], 'mix': cmp_mix[layer],
              'w1': cmp_w1[layer], 'w2': cmp_w2[layer], 'w_o': w_o_b,
              'n0': norms[layer, 0][None, :], 'n1': norms[layer, 1][None, :],
              'n2': norms[layer, 2][None, :], 'n3': norms[layer, 3][None, :],
              'w_up': w_up_b, 'w_down': w_down_b}
        y_p, sp = _layer_prompt(y_p, lw, tabs_p, layer, bp, T)
        y_s, s_s = _layer_sample(y_s, lw, tabs_s, layer, bs, cache_cmp, cache_sel, cache_win,
                                 state_gla, page_table)
        st_p.append(sp)
        st_s.append(s_s)

    kv_shape = (2, NSA_KV_HEADS, NSA_DH)

    def rows_p(i):
        return jnp.stack([s[i].reshape(bp, T, *kv_shape) for s in st_p])

    def rows_s(i):
        return jnp.stack([s[i].reshape(bs, R, *kv_shape)[:, :ss] for s in st_s])

    win_p = rows_p(2)[:, :, T - min(WINDOW, T):]
    win_s = jnp.concatenate([cache_win_kv[:, :, ss:], rows_s(2)], axis=2)
    return (y_p.reshape(bp, T, d), y_s.reshape(bs, R, d)[:, :ss],
            rows_p(0), rows_s(0), rows_p(1), rows_s(1), win_p, win_s,
            jnp.stack([s[3] for s in st_p]), jnp.stack([s[3] for s in st_s]))
```

```python
import functools

import jax
import jax.numpy as jnp
import numpy as np
from jax import lax
from jax.experimental import pallas as pl
from jax.experimental.pallas import tpu as pltpu

F32 = jnp.float32
BF16 = jnp.bfloat16

D_MODEL = 2048
DEPTH = 4
PAST_LEN = 16384
PAGE_SIZE = 128

GLA_HEADS = 4
GLA_DK = 256
GLA_DV = 512
GLA_GATE_RANK = 16
GLA_TAU = 16.0
GLA_CHUNK = 64
GLA_CHUNK_SHIFT = 6

NSA_HEADS = 16
NSA_KV_HEADS = 2
NSA_DH = 128
NSA_GROUP = NSA_HEADS // NSA_KV_HEADS
CMP_LEN = 32
CMP_STRIDE = 16
CMP_HIDDEN = 256
SEL_LEN = 64
SEL_SHIFT = 6
SEL_TOPN = 16
WINDOW = 512
WIN_QBLK = 128

ROPE_THETA = 500000.0
ROPE_DIM = 32
ROPE_HALF = 16
MLP_HIDDEN = 4 * D_MODEL
NORM_EPS = 1e-6
NEG_INF = -1e30
FORCED_SCORE = 1e6
KV_PARTS = 2 * NSA_KV_HEADS
KV_ROW = KV_PARTS * NSA_DH
PAGES_PER_STEP = 16

VMEM_LIMIT_BYTES = 56 * 1024 * 1024
LANES = 128

COL_MG = 0
COL_GV = 4096
COL_GG = 6144
COL_NQ = 8192
COL_GQ = 10240
COL_GK = 11264
COL_NKV = 12288
COL_SM = 13824
N_PROJ = 14336
SM_GATE0 = GLA_GATE_RANK
SM_USED = GLA_GATE_RANK + 3 * NSA_HEADS

_SRC_SEGMENTS = ((9792, 13888), (2048, 4096), (4096, 6144), (6160, 8208), (0, 1024), (1024, 2048),
                 (8208, 9744), (6144, 6160), (9744, 9792))

SAMPLE_ROWS = 8


def _cparams(sem):
    return pltpu.CompilerParams(dimension_semantics=sem, vmem_limit_bytes=VMEM_LIMIT_BYTES)


def _rms(x, g):
    return x * lax.rsqrt(jnp.mean(x * x, axis=-1, keepdims=True) + NORM_EPS) * g


def _dot(a, b, precision=None):
    return jnp.dot(a, b, preferred_element_type=F32, precision=precision)


def _dot_nt(a, b, precision=None):
    return lax.dot_general(a, b, (((1,), (1,)), ((), ())), preferred_element_type=F32, precision=precision)


def _dot_tn(a, b, precision=None):
    return lax.dot_general(a, b, (((0,), (0,)), ((), ())), preferred_element_type=F32, precision=precision)


def _split_bf16(x):
    hi = x.astype(BF16)
    return hi, (x - hi.astype(F32)).astype(BF16)


def _dot_nt_3x(a, b):
    ah, al = _split_bf16(a)
    bh, bl = _split_bf16(b)
    return (_dot_nt(jnp.concatenate([ah, al], axis=1), jnp.concatenate([bh, bh], axis=1))
            + _dot_nt(ah, bl))


def _dot_3x(a, b):
    ah, al = _split_bf16(a)
    bh, bl = _split_bf16(b)
    return (_dot(jnp.concatenate([ah, al], axis=1), jnp.concatenate([bh, bh], axis=0))
            + _dot(ah, bl))


def _iota(shape, dim):
    return lax.broadcasted_iota(jnp.int32, shape, dim)


def _masked_softmax(s, valid):
    s = jnp.where(valid, s, NEG_INF)
    m = jnp.max(s, axis=-1, keepdims=True)
    p = jnp.where(valid, jnp.exp(s - m), 0.0)
    return p / jnp.maximum(jnp.sum(p, axis=-1, keepdims=True), 1e-30)


def _norm_matmul_kernel(x_ref, g_ref, w_ref, o_ref, h_ref):
    @pl.when(pl.program_id(1) == 0)
    def _():
        h_ref[...] = _rms(x_ref[...], g_ref[...]).astype(BF16)

    o_ref[...] = _dot(h_ref[...], w_ref[...])


def norm_matmul(x, g, w, layer, tm, tn):
    m, k = x.shape
    n = w.shape[2]
    return pl.pallas_call(
        _norm_matmul_kernel,
        grid=(m // tm, n // tn),
        in_specs=[pl.BlockSpec((tm, k), lambda i, j: (i, 0)),
                  pl.BlockSpec((1, k), lambda i, j: (0, 0)),
                  pl.BlockSpec((None, k, tn), lambda i, j: (layer, 0, j))],
        out_specs=pl.BlockSpec((tm, tn), lambda i, j: (i, j)),
        out_shape=jax.ShapeDtypeStruct((m, n), F32),
        scratch_shapes=[pltpu.VMEM((tm, k), BF16)],
        compiler_params=_cparams(("parallel", "arbitrary")),
        name="norm_matmul",
    )(x, g, w)


def _merge_wo_kernel(x_ref, mg0_ref, mg1_ref, oa_ref, ob_ref, g_ref, wo_ref, o_ref):
    ob = jnp.concatenate([ob_ref[h] for h in range(NSA_HEADS)], axis=1)
    a = jax.nn.sigmoid(mg0_ref[...]) * oa_ref[...] + jax.nn.sigmoid(mg1_ref[...]) * ob
    m = _dot(a.astype(BF16), wo_ref[...])
    o_ref[...] = x_ref[...] + _rms(m, g_ref[...])


def merge_wo(x, proj, o_a, o_b_hm, g, wo, layer, tm):
    m, d = x.shape
    return pl.pallas_call(
        _merge_wo_kernel,
        grid=(m // tm,),
        in_specs=[pl.BlockSpec((tm, d), lambda i: (i, 0)),
                  pl.BlockSpec((tm, d), lambda i: (i, COL_MG // D_MODEL)),
                  pl.BlockSpec((tm, d), lambda i: (i, COL_MG // D_MODEL + 1)),
                  pl.BlockSpec((tm, d), lambda i: (i, 0)),
                  pl.BlockSpec((NSA_HEADS, tm, NSA_DH), lambda i: (0, i, 0)),
                  pl.BlockSpec((1, d), lambda i: (0, 0)),
                  pl.BlockSpec((None, d, d), lambda i: (layer, 0, 0))],
        out_specs=pl.BlockSpec((tm, d), lambda i: (i, 0)),
        out_shape=jax.ShapeDtypeStruct((m, d), F32),
        compiler_params=_cparams(("parallel",)),
        name="merge_wo",
    )(x, proj, proj, o_a, o_b_hm, g, wo)


def _mlp_kernel(x_ref, g2_ref, g3_ref, wu_ref, wd_ref, o_ref, h_ref, acc_ref):
    j = pl.program_id(1)

    @pl.when(j == 0)
    def _():
        h_ref[...] = _rms(x_ref[...], g2_ref[...]).astype(BF16)
        acc_ref[...] = jnp.zeros_like(acc_ref)

    u = _dot(h_ref[...], wu_ref[...])
    u = jnp.square(jnp.maximum(u, 0.0)).astype(BF16)
    acc_ref[...] += _dot(u, wd_ref[...])

    @pl.when(j == pl.num_programs(1) - 1)
    def _():
        o_ref[...] = x_ref[...] + _rms(acc_ref[...], g3_ref[...])


def mlp(x, g2, g3, wu, wd, layer, tm, th):
    m, d = x.shape
    hid = wu.shape[2]
    return pl.pallas_call(
        _mlp_kernel,
        grid=(m // tm, hid // th),
        in_specs=[pl.BlockSpec((tm, d), lambda i, j: (i, 0)),
                  pl.BlockSpec((1, d), lambda i, j: (0, 0)),
                  pl.BlockSpec((1, d), lambda i, j: (0, 0)),
                  pl.BlockSpec((None, d, th), lambda i, j: (layer, 0, j)),
                  pl.BlockSpec((None, th, d), lambda i, j: (layer, j, 0))],
        out_specs=pl.BlockSpec((tm, d), lambda i, j: (i, 0)),
        out_shape=jax.ShapeDtypeStruct((m, d), F32),
        scratch_shapes=[pltpu.VMEM((tm, d), BF16), pltpu.VMEM((tm, d), F32)],
        compiler_params=_cparams(("parallel", "arbitrary")),
        name="mlp",
    )(x, g2, g3, wu, wd)


def _gla_kernel(q_ref, k_ref, v_ref, gg_ref, sm_ref, wa2_ref, ba_ref, gn_ref, s0_ref,
                o_ref, sout_ref, st_ref, *, rows_in, n_inner, n_valid, has_s0):
    c = pl.program_id(1)
    C = GLA_CHUNK

    @pl.when(c == 0)
    def _():
        for h in range(GLA_HEADS):
            if has_s0:
                st_ref[h] = s0_ref[h].T
            else:
                st_ref[h] = jnp.zeros((GLA_DV, GLA_DK), F32)

    rp = n_inner * C

    def load(ref):
        x = ref[...]
        if rows_in == C:
            return x
        assert n_inner == 1
        return jnp.concatenate([x, jnp.zeros((C - rows_in, x.shape[1]), x.dtype)], axis=0)

    row = _iota((rp, 1), 0)
    col = _iota((1, rp), 1)
    same_chunk = (row >> GLA_CHUNK_SHIFT) == (col >> GLA_CHUNK_SHIFT)
    causal = same_chunk & (row >= col)
    tril = causal.astype(BF16)
    ones_blk = same_chunk.astype(BF16)
    live = (row & (C - 1)) < n_valid

    ga_h, ga_l = _split_bf16(load(sm_ref)[:, :GLA_GATE_RANK])
    w_h, w_l = _split_bf16(wa2_ref[...])
    z = _dot(ga_h, w_h) + _dot(ga_l, w_h) + _dot(ga_h, w_l) + ba_ref[...]
    log_a = (jnp.minimum(z, 0.0) - jnp.log1p(jnp.exp(-jnp.abs(z)))) / GLA_TAU
    k = load(k_ref)
    if n_valid < C:
        log_a = jnp.where(live, log_a, 0.0)
        k = jnp.where(live, k, 0.0)
    la_h, la_l = _split_bf16(log_a)
    b = _dot(tril, la_h) + _dot(tril, la_l)
    b_last = _dot(ones_blk, la_h) + _dot(ones_blk, la_l)
    q_dec = (load(q_ref) * (GLA_DK ** -0.5) * jnp.exp(b)).astype(BF16)
    k_dec = (k * jnp.exp(-b)).astype(BF16)
    k_end = (k * jnp.exp(b_last - b)).astype(BF16)
    decay = jnp.exp(b_last)
    vb = load(v_ref).astype(BF16)

    for h in range(GLA_HEADS):
        kq = slice(h * GLA_DK, (h + 1) * GLA_DK)
        kv = slice(h * GLA_DV, (h + 1) * GLA_DV)
        att = jnp.where(causal, _dot_nt(q_dec[:, kq], k_dec[:, kq]), 0.0)
        intra = _dot(att.astype(BF16), vb[:, kv])
        for ci in range(n_inner):
            cr = slice(ci * C, (ci + 1) * C)
            st = st_ref[h]
            o = intra[cr] + _dot_nt(q_dec[cr, kq], st.astype(BF16))
            st_ref[h] = st * decay[ci * C:ci * C + 1, kq] + _dot_tn(vb[cr, kv], k_end[cr, kq])
            o_n = _rms(o, gn_ref[...])
            rs = pl.ds(ci * rows_in, rows_in)
            gg = gg_ref[rs, kv]
            o_ref[rs, kv] = o_n[:rows_in] * (gg * jax.nn.sigmoid(gg))

    @pl.when(c == pl.num_programs(1) - 1)
    def _():
        for h in range(GLA_HEADS):
            sout_ref[h] = st_ref[h].T


def gla(proj, wa2, ba, gn, s0, layer, batch, rows_per_batch, rows_in, n_inner, n_valid):
    m = proj.shape[0]
    r = rows_in * n_inner
    n_steps = rows_per_batch // r
    has_s0 = s0 is not None
    if s0 is None:
        s0 = jnp.zeros((1, 1, GLA_HEADS, GLA_DK, GLA_DV), F32)
        s0_map = lambda b, c: (0, 0, 0, 0, 0)
    else:
        s0_map = lambda b, c: (layer, b, 0, 0, 0)
    wk, wv = GLA_HEADS * GLA_DK, GLA_HEADS * GLA_DV
    rowmap = lambda off: (lambda b, c: (b * n_steps + c, off))
    state_blk = (None, GLA_HEADS, GLA_DK, GLA_DV)
    kern = functools.partial(_gla_kernel, rows_in=rows_in, n_inner=n_inner, n_valid=n_valid, has_s0=has_s0)
    return pl.pallas_call(
        kern,
        grid=(batch, n_steps),
        in_specs=[pl.BlockSpec((r, wk), rowmap(COL_GQ // wk)),
                  pl.BlockSpec((r, wk), rowmap(COL_GK // wk)),
                  pl.BlockSpec((r, wv), rowmap(COL_GV // wv)),
                  pl.BlockSpec((r, wv), rowmap(COL_GG // wv)),
                  pl.BlockSpec((r, LANES), rowmap(COL_SM // LANES)),
                  pl.BlockSpec((GLA_GATE_RANK, wk), lambda b, c: (0, 0)),
                  pl.BlockSpec((1, wk), lambda b, c: (0, 0)),
                  pl.BlockSpec((1, GLA_DV), lambda b, c: (0, 0)),
                  pl.BlockSpec((None,) + state_blk, s0_map)],
        out_specs=[pl.BlockSpec((r, wv), rowmap(0)),
                   pl.BlockSpec(state_blk, lambda b, c: (b, 0, 0, 0))],
        out_shape=[jax.ShapeDtypeStruct((m, D_MODEL), F32),
                   jax.ShapeDtypeStruct((batch, GLA_HEADS, GLA_DK, GLA_DV), F32)],
        scratch_shapes=[pltpu.VMEM((GLA_HEADS, GLA_DV, GLA_DK), F32)],
        compiler_params=_cparams(("parallel", "arbitrary")),
        name="gla",
    )(proj, proj, proj, proj, proj, wa2, ba, gn, s0)


def _rope(x, c, sa, sb):
    return x * c + pltpu.roll(x, LANES - ROPE_HALF, 1) * sa + pltpu.roll(x, ROPE_HALF, 1) * sb


def _nsa_prep_kernel(*refs, with_q):
    if with_q:
        nq_ref, nkv_ref, c_ref, sa_ref, sb_ref, q_ref, cmp_ref, sel_ref, win_ref = refs
    else:
        nkv_ref, c_ref, sa_ref, sb_ref, cmp_ref, sel_ref, win_ref = refs
    c, sa, sb = c_ref[...], sa_ref[...], sb_ref[...]
    if with_q:
        for h in range(NSA_HEADS):
            q_ref[h] = _rope(nq_ref[:, h * NSA_DH:(h + 1) * NSA_DH], c, sa, sb)
    for s, out in enumerate((cmp_ref, sel_ref, win_ref)):
        base = s * KV_ROW
        for hh in range(NSA_KV_HEADS):
            lo = hh * NSA_DH
            out[:, lo:lo + NSA_DH] = _rope(nkv_ref[:, base + lo:base + lo + NSA_DH], c, sa, sb)
        half = NSA_KV_HEADS * NSA_DH
        out[:, half:] = nkv_ref[:, base + half:base + KV_ROW]


def nsa_prep(proj, tabs, tr, with_q):
    m = proj.shape[0]
    n_tab = tabs[0].shape[0] // tr
    tab_spec = pl.BlockSpec((tr, LANES), lambda i: (i % n_tab, 0))
    nkv_w = 3 * KV_ROW
    row_spec = pl.BlockSpec((tr, KV_ROW), lambda i: (i, 0))
    row_shape = jax.ShapeDtypeStruct((m, KV_ROW), F32)
    in_specs = [pl.BlockSpec((tr, nkv_w), lambda i: (i, COL_NKV // nkv_w)), tab_spec, tab_spec, tab_spec]
    out_specs, out_shape, args = [row_spec] * 3, [row_shape] * 3, (proj,) + tuple(tabs)
    if with_q:
        in_specs = [pl.BlockSpec((tr, D_MODEL), lambda i: (i, COL_NQ // D_MODEL))] + in_specs
        out_specs = [pl.BlockSpec((NSA_HEADS, tr, NSA_DH), lambda i: (0, i, 0))] + out_specs
        out_shape = [jax.ShapeDtypeStruct((NSA_HEADS, m, NSA_DH), F32)] + out_shape
        args = (proj,) + args
    return pl.pallas_call(
        functools.partial(_nsa_prep_kernel, with_q=with_q),
        grid=(m // tr,),
        in_specs=in_specs,
        out_specs=out_specs,
        out_shape=out_shape,
        compiler_params=_cparams(("parallel",)),
        name="nsa_prep",
    )(*args)


def rope_tables(pos):
    inv = 1.0 / (ROPE_THETA ** (jnp.arange(ROPE_HALF, dtype=F32) / ROPE_HALF))
    ang = pos.astype(F32)[:, None] * inv[None, :]
    cos, sin = jnp.cos(ang), jnp.sin(ang)
    n = pos.shape[0]
    rest = LANES - ROPE_DIM
    c = jnp.concatenate([cos, cos, jnp.ones((n, rest), F32)], axis=1)
    sa = jnp.concatenate([-sin, jnp.zeros((n, LANES - ROPE_HALF), F32)], axis=1)
    sb = jnp.concatenate([jnp.zeros((n, ROPE_HALF), F32), sin, jnp.zeros((n, rest), F32)], axis=1)
    return c, sa, sb


def _chunk_sums_body(x, pe_ref, mix_ref, a_ref, b_ref):
    r = x.shape[0]
    nch = r // CMP_STRIDE
    for kv in range(2):
        pe, mix = pe_ref[kv], mix_ref[kv]
        for hh in range(NSA_KV_HEADS):
            lo = (kv * NSA_KV_HEADS + hh) * NSA_DH
            xs = x[:, lo:lo + NSA_DH].reshape(nch, CMP_STRIDE, NSA_DH)
            a_ref[:, lo:lo + NSA_DH] = jnp.sum((xs + pe[:CMP_STRIDE]) * mix[:CMP_STRIDE], axis=1)
            b_ref[:, lo:lo + NSA_DH] = jnp.sum((xs + pe[CMP_STRIDE:]) * mix[CMP_STRIDE:], axis=1)


def _chunk_sums_kernel(x_ref, pe_ref, mix_ref, a_ref, b_ref):
    _chunk_sums_body(x_ref[...], pe_ref, mix_ref, a_ref, b_ref)


def _page_specs(layer):
    def spec(i):
        return pl.BlockSpec((None, None, PAGE_SIZE * KV_PARTS, NSA_DH),
                            lambda b, s, pt: (layer, pt[b, s * PAGES_PER_STEP + i], 0, 0))
    return [spec(i) for i in range(PAGES_PER_STEP)]


def _page_part(page_ref, part):
    return page_ref[pl.ds(part, PAGE_SIZE, stride=KV_PARTS), :]


def _chunk_sums_paged_kernel(pt_ref, *refs):
    page_refs, (pe_ref, mix_ref, a_ref, b_ref) = refs[:PAGES_PER_STEP], refs[PAGES_PER_STEP:]
    nch = PAGE_SIZE // CMP_STRIDE
    for i, page_ref in enumerate(page_refs):
        rows = slice(i * nch, (i + 1) * nch)
        for kv in range(2):
            pe, mix = pe_ref[kv], mix_ref[kv]
            for hh in range(NSA_KV_HEADS):
                part = kv * NSA_KV_HEADS + hh
                lo = part * NSA_DH
                xs = _page_part(page_ref, part).reshape(nch, CMP_STRIDE, NSA_DH)
                a_ref[rows, lo:lo + NSA_DH] = jnp.sum((xs + pe[:CMP_STRIDE]) * mix[:CMP_STRIDE], axis=1)
                b_ref[rows, lo:lo + NSA_DH] = jnp.sum((xs + pe[CMP_STRIDE:]) * mix[CMP_STRIDE:], axis=1)


def chunk_sums(rows, pe, mix, tr):
    m = rows.shape[0]
    nch = tr // CMP_STRIDE
    full = lambda i: (0, 0, 0)
    return pl.pallas_call(
        _chunk_sums_kernel,
        grid=(m // tr,),
        in_specs=[pl.BlockSpec((tr, KV_ROW), lambda i: (i, 0)),
                  pl.BlockSpec((2, CMP_LEN, NSA_DH), full),
                  pl.BlockSpec((2, CMP_LEN, NSA_DH), full)],
        out_specs=[pl.BlockSpec((nch, KV_ROW), lambda i: (i, 0))] * 2,
        out_shape=[jax.ShapeDtypeStruct((m // CMP_STRIDE, KV_ROW), F32)] * 2,
        compiler_params=_cparams(("parallel",)),
        name="chunk_sums",
    )(rows, pe, mix)


def chunk_sums_paged(cache, layer, page_table, pe, mix):
    batch, n_pages = page_table.shape
    nch = PAGES_PER_STEP * PAGE_SIZE // CMP_STRIDE
    n_steps = n_pages // PAGES_PER_STEP
    full = lambda b, s, pt: (0, 0, 0)
    gs = pltpu.PrefetchScalarGridSpec(
        num_scalar_prefetch=1,
        grid=(batch, n_steps),
        in_specs=_page_specs(layer) + [pl.BlockSpec((2, CMP_LEN, NSA_DH), full),
                                       pl.BlockSpec((2, CMP_LEN, NSA_DH), full)],
        out_specs=[pl.BlockSpec((nch, KV_ROW), lambda b, s, pt: (b * n_steps + s, 0))] * 2,
    )
    return pl.pallas_call(
        _chunk_sums_paged_kernel,
        grid_spec=gs,
        out_shape=[jax.ShapeDtypeStruct((batch * n_steps * nch, KV_ROW), F32)] * 2,
        compiler_params=_cparams(("parallel", "arbitrary")),
        name="chunk_sums_paged",
    )(page_table, *([cache] * PAGES_PER_STEP), pe, mix)


def _cmp_mlp_kernel(a_ref, b_ref, w1_ref, w2_ref, o_ref):
    h = a_ref[...] + b_ref[...]
    for kv in range(2):
        for hh in range(NSA_KV_HEADS):
            lo = (kv * NSA_KV_HEADS + hh) * NSA_DH
            y = jax.nn.gelu(_dot_3x(h[:, lo:lo + NSA_DH], w1_ref[kv]))
            o_ref[:, lo:lo + NSA_DH] = _dot_3x(y, w2_ref[kv])


def cmp_mlp(a, b_shift, w1, w2, tr):
    m = a.shape[0]
    return pl.pallas_call(
        _cmp_mlp_kernel,
        grid=(m // tr,),
        in_specs=[pl.BlockSpec((tr, KV_ROW), lambda i: (i, 0)),
                  pl.BlockSpec((tr, KV_ROW), lambda i: (i, 0)),
                  pl.BlockSpec((2, NSA_DH, CMP_HIDDEN), lambda i: (0, 0, 0)),
                  pl.BlockSpec((2, CMP_HIDDEN, NSA_DH), lambda i: (0, 0, 0))],
        out_specs=pl.BlockSpec((tr, KV_ROW), lambda i: (i, 0)),
        out_shape=jax.ShapeDtypeStruct((m, KV_ROW), F32),
        compiler_params=_cparams(("parallel",)),
        name="cmp_mlp",
    )(a, b_shift, w1, w2)


def compressed_kv(a, b, batch, w1, w2, tr):
    nch = a.shape[0] // batch
    b3 = b.reshape(batch, nch, KV_ROW)
    b_shift = jnp.concatenate([b3[:, 1:], jnp.zeros((batch, 1, KV_ROW), F32)], axis=1).reshape(batch * nch, KV_ROW)
    return cmp_mlp(a, b_shift, w1, w2, tr)


def _select_topn(imp, n_top):
    j = _iota(imp.shape, 1)
    big = jnp.int32(imp.shape[1])
    sel = jnp.zeros(imp.shape, F32)
    for _ in range(n_top):
        m = jnp.max(imp, axis=-1, keepdims=True)
        idx = jnp.min(jnp.where(imp == m, j, big), axis=-1, keepdims=True)
        hit = j == idx
        sel = jnp.where(hit, 1.0, sel)
        imp = jnp.where(hit, NEG_INF, imp)
    return sel


def _importance(p_grp, qpos, n_cmp_pad, n_slc, width):
    n_r = _iota((n_cmp_pad, 1), 0) * CMP_STRIDE
    j_c = _iota((1, width), 1)
    overlap = ((n_r < (j_c + 1) * SEL_LEN) & (n_r + CMP_LEN > j_c * SEL_LEN)).astype(BF16)
    p_hi, p_lo = _split_bf16(p_grp)
    imp = _dot(p_hi, overlap) + _dot(p_lo, overlap)
    cur = qpos >> SEL_SHIFT
    forced = (j_c == 0) | (j_c == cur) | (j_c == cur - 1)
    imp = jnp.where(forced, FORCED_SCORE, imp)
    imp = jnp.where(j_c > cur, -1.0, imp)
    return jnp.where(j_c >= n_slc, -2.0, imp)


LOG2E = 1.4426950408889634


def _nsa_prompt_kernel(nq_ref, c_ref, sa_ref, sb_ref, kc_ref, vc_ref, ks_ref, vs_ref, kw_ref, vw_ref, sm_ref, o_ref,
                       qf_ref, qb_ref, s_ref, p_ref, m_ref, l_ref, a_ref, acc_ref, *, Q, T, TK):
    G = NSA_GROUP
    hk = pl.program_id(1)
    qi = pl.program_id(2)
    q0 = qi * Q
    scale = NSA_DH ** -0.5
    n_cmp_pad = kc_ref.shape[0]
    n_slc = T // SEL_LEN
    qpos = q0 + _iota((Q, 1), 0)
    c, sa, sb = c_ref[...], sa_ref[...], sb_ref[...]
    for g in range(G):
        qg = _rope(nq_ref[:, g * NSA_DH:(g + 1) * NSA_DH], c, sa, sb)
        qf_ref[g * Q:(g + 1) * Q, :] = qg
        qb_ref[g * Q:(g + 1) * Q, :] = (qg * (scale * LOG2E)).astype(BF16)
    qs = qf_ref[...]

    ng = jax.nn.sigmoid(sm_ref[...])

    def gate(g, c3):
        lane0 = SM_GATE0 + 3 * g + c3
        lane1 = lane0 + 3 * G
        return jnp.where(hk == 0, ng[:, lane0:lane0 + 1], ng[:, lane1:lane1 + 1])

    def softmax_tile(kb, vb, bias, width):
        s_ref[:, :width] = _dot_nt(qb_ref[...], kb)
        for g in range(G):
            rows = pl.ds(g * Q, Q)
            s = s_ref[rows, :width] + bias
            m_old = m_ref[rows, :]
            m_new = jnp.maximum(m_old, jnp.max(s, axis=-1, keepdims=True))
            alpha = jnp.exp2(m_old - m_new)
            p = jnp.exp2(s - jnp.tile(m_new, (1, width // LANES)))
            l_ref[rows, :] = alpha * l_ref[rows, :] + jnp.sum(p, axis=-1, keepdims=True)
            a_ref[rows, :] = alpha
            m_ref[rows, :] = m_new
            p_ref[rows, :width] = p.astype(BF16)
        return _dot(p_ref[:, :width], vb)

    def reset():
        m_ref[...] = jnp.full_like(m_ref, NEG_INF)
        l_ref[...] = jnp.zeros_like(l_ref)

    cmp_valid = _iota((1, n_cmp_pad), 1) * CMP_STRIDE + (CMP_LEN - 1) <= qpos
    p_grp = jnp.zeros((Q, n_cmp_pad), F32)
    s_cmp = _dot_nt_3x(qs, kc_ref[...]) * scale
    vcb = vc_ref[...].astype(BF16)
    for g in range(G):
        p = _masked_softmax(s_cmp[g * Q:(g + 1) * Q], cmp_valid)
        p_grp = p_grp + p
        o_ref[g] = gate(g, 0) * _dot(p.astype(BF16), vcb)

    ks0 = pl.multiple_of(jnp.maximum(q0 - WINDOW, 0), LANES)
    wlen = WINDOW + Q
    wpos = ks0 + _iota((1, wlen), 1)
    win_bias = jnp.where((wpos <= qpos) & (wpos > qpos - WINDOW), 0.0, NEG_INF)
    reset()
    pv = softmax_tile(kw_ref[pl.ds(ks0, wlen), :].astype(BF16), vw_ref[pl.ds(ks0, wlen), :].astype(BF16),
                      win_bias, wlen)
    for g in range(G):
        rows = pl.ds(g * Q, Q)
        o_ref[g] = o_ref[g] + gate(g, 2) * (pv[g * Q:(g + 1) * Q] / jnp.maximum(l_ref[rows, :], 1e-30))

    n_top = min(SEL_TOPN, n_slc)
    j_r = _iota((n_slc, 1), 0)
    n_c = _iota((1, n_cmp_pad), 1) * CMP_STRIDE
    overlap_t = ((n_c < (j_r + 1) * SEL_LEN) & (n_c + CMP_LEN > j_r * SEL_LEN)).astype(BF16)
    p_hi, p_lo = _split_bf16(p_grp)
    imp = _dot_nt(overlap_t, p_hi) + _dot_nt(overlap_t, p_lo)
    cur = (q0 + _iota((1, Q), 1)) >> SEL_SHIFT
    imp = jnp.where((j_r == 0) | (j_r == cur) | (j_r == cur - 1), FORCED_SCORE, imp)
    imp = jnp.where(j_r > cur, -1.0, imp)
    rank = jnp.zeros((n_slc, Q), F32)
    for jp in range(n_slc):
        other = imp[jp:jp + 1, :]
        beats = (other > imp) | ((other == imp) & (j_r > jp))
        rank = rank + jnp.where(beats, 1.0, 0.0)
    sel_t = jnp.where(rank < n_top, 1.0, 0.0).astype(BF16)

    reset()
    acc_ref[...] = jnp.zeros_like(acc_ref)

    def key_tile(kt, carry):
        k0 = pl.multiple_of(kt * TK, TK)
        kpos = k0 + _iota((1, TK), 1)
        expand = (j_r == (kpos >> SEL_SHIFT)).astype(BF16)
        keymask = _dot_tn(sel_t, expand)
        bias = jnp.where((keymask > 0.5) & (kpos <= qpos), 0.0, NEG_INF)
        pv = softmax_tile(ks_ref[pl.ds(k0, TK), :].astype(BF16), vs_ref[pl.ds(k0, TK), :].astype(BF16), bias, TK)
        acc_ref[...] = a_ref[...] * acc_ref[...] + pv
        return carry

    lax.fori_loop(0, (q0 + Q + TK - 1) // TK, key_tile, 0)

    for g in range(G):
        rows = pl.ds(g * Q, Q)
        o_ref[g] = o_ref[g] + gate(g, 1) * (acc_ref[rows, :] / jnp.maximum(l_ref[rows, :], 1e-30))


def nsa_prompt(tabs, kvc, sel_rows, win_rows, proj, batch, T, Q):
    m = batch * T
    nq = T // Q
    n_cmp_pad = kvc.shape[0] // batch
    G = NSA_GROUP
    gw = G * NSA_DH
    kern = functools.partial(_nsa_prompt_kernel, Q=Q, T=T, TK=512)
    kcol = lambda off: (lambda b, hk, qi: (b, off + hk))
    tab_spec = pl.BlockSpec((Q, LANES), lambda b, hk, qi: (qi, 0))
    return pl.pallas_call(
        kern,
        grid=(batch, NSA_KV_HEADS, nq),
        in_specs=[pl.BlockSpec((Q, gw), lambda b, hk, qi: (b * nq + qi, COL_NQ // gw + hk)),
                  tab_spec, tab_spec, tab_spec,
                  pl.BlockSpec((n_cmp_pad, NSA_DH), kcol(0)),
                  pl.BlockSpec((n_cmp_pad, NSA_DH), kcol(NSA_KV_HEADS)),
                  pl.BlockSpec((T, NSA_DH), kcol(0)),
                  pl.BlockSpec((T, NSA_DH), kcol(NSA_KV_HEADS)),
                  pl.BlockSpec((T, NSA_DH), kcol(0)),
                  pl.BlockSpec((T, NSA_DH), kcol(NSA_KV_HEADS)),
                  pl.BlockSpec((Q, LANES), lambda b, hk, qi: (b * nq + qi, COL_SM // LANES))],
        out_specs=pl.BlockSpec((G, Q, NSA_DH), lambda b, hk, qi: (hk, b * nq + qi, 0)),
        out_shape=jax.ShapeDtypeStruct((NSA_HEADS, m, NSA_DH), F32),
        scratch_shapes=[pltpu.VMEM((G * Q, NSA_DH), F32), pltpu.VMEM((G * Q, NSA_DH), BF16),
                        pltpu.VMEM((G * Q, WINDOW + Q), F32), pltpu.VMEM((G * Q, WINDOW + Q), BF16),
                        pltpu.VMEM((G * Q, LANES), F32), pltpu.VMEM((G * Q, LANES), F32),
                        pltpu.VMEM((G * Q, LANES), F32), pltpu.VMEM((G * Q, NSA_DH), F32)],
        compiler_params=_cparams(("parallel", "parallel", "arbitrary")),
        name="nsa_prompt",
    )(proj, *tabs, kvc, kvc, sel_rows, sel_rows, win_rows, win_rows, proj)


SEL_W = 384


def _sample_cmp_select_kernel(q_ref, kvc_ref, ocmp_ref, selm_ref, *, n_slc):
    G, R = NSA_GROUP, SAMPLE_ROWS
    scale = NSA_DH ** -0.5
    n_cmp_pad = kvc_ref.shape[0]
    qpos_s = PAST_LEN + (_iota((G * R, 1), 0) & (R - 1))
    end = _iota((1, n_cmp_pad), 1) * CMP_STRIDE + (CMP_LEN - 1)
    p_grp = []
    for hk in range(NSA_KV_HEADS):
        qs = q_ref[hk * G:(hk + 1) * G].reshape(G * R, NSA_DH)
        kc = kvc_ref[:, hk * NSA_DH:(hk + 1) * NSA_DH]
        vc = kvc_ref[:, (NSA_KV_HEADS + hk) * NSA_DH:(NSA_KV_HEADS + hk + 1) * NSA_DH]
        p = _masked_softmax(_dot_nt_3x(qs, kc) * scale, end <= qpos_s)
        o_cmp = _dot(p.astype(BF16), vc.astype(BF16))
        for g in range(G):
            ocmp_ref[hk * G + g] = o_cmp[g * R:(g + 1) * R]
        p_grp.append(jnp.sum(p.reshape(G, R, n_cmp_pad), axis=0))
    qpos = PAST_LEN + (_iota((NSA_KV_HEADS * R, 1), 0) & (R - 1))
    imp = _importance(jnp.concatenate(p_grp, axis=0), qpos, n_cmp_pad, n_slc, SEL_W)
    sel = _select_topn(imp, min(SEL_TOPN, n_slc))
    for hk in range(NSA_KV_HEADS):
        selm_ref[hk] = sel[hk * R:(hk + 1) * R]


def sample_cmp_select(q_hm, kvc, batch, n_slc):
    R = SAMPLE_ROWS
    n_cmp_pad = kvc.shape[0] // batch
    kern = functools.partial(_sample_cmp_select_kernel, n_slc=n_slc)
    hm_spec = pl.BlockSpec((NSA_HEADS, R, NSA_DH), lambda b: (0, b, 0))
    return pl.pallas_call(
        kern,
        grid=(batch,),
        in_specs=[hm_spec, pl.BlockSpec((n_cmp_pad, KV_ROW), lambda b: (b, 0))],
        out_specs=[hm_spec, pl.BlockSpec((None, NSA_KV_HEADS, R, SEL_W), lambda b: (b, 0, 0, 0))],
        out_shape=[jax.ShapeDtypeStruct((NSA_HEADS, batch * R, NSA_DH), F32),
                   jax.ShapeDtypeStruct((batch, NSA_KV_HEADS, R, SEL_W), F32)],
        compiler_params=_cparams(("parallel",)),
        name="sample_cmp_select",
    )(q_hm, kvc)


def _sample_attn_kernel(pt_ref, q_ref, selm_ref, *refs, n_steps):
    page_refs = refs[:PAGES_PER_STEP]
    nsel_ref, cwin_ref, nwin_ref, sm_ref, ocmp_ref, o_ref, m_ref, l_ref, acc_ref = refs[PAGES_PER_STEP:]
    G, R = NSA_GROUP, SAMPLE_ROWS
    p = pl.program_id(1)
    scale = NSA_DH ** -0.5
    t_s = _iota((G * R, 1), 0) & (R - 1)
    n_keys = PAGES_PER_STEP * PAGE_SIZE

    @pl.when(p == 0)
    def _():
        m_ref[...] = jnp.full_like(m_ref, NEG_INF)
        l_ref[...] = jnp.zeros_like(l_ref)
        acc_ref[...] = jnp.zeros_like(acc_ref)

    def update(hk, s, valid, v):
        s = jnp.where(valid, s, NEG_INF)
        m_old = m_ref[hk]
        m_new = jnp.maximum(m_old, jnp.max(s, axis=-1, keepdims=True))
        alpha = jnp.exp(m_old - m_new)
        pr = jnp.where(valid, jnp.exp(s - m_new), 0.0)
        l_ref[hk] = alpha * l_ref[hk] + jnp.sum(pr, axis=-1, keepdims=True)
        acc_ref[hk] = alpha * acc_ref[hk] + _dot(pr.astype(BF16), v.astype(BF16))
        m_ref[hk] = m_new

    key = _iota((1, n_keys), 1)
    blk_of_key = (n_keys // SEL_LEN) * p + (key >> SEL_SHIFT)
    expand = (_iota((SEL_W, 1), 0) == blk_of_key).astype(BF16)
    for hk in range(NSA_KV_HEADS):
        qs = q_ref[hk * G:(hk + 1) * G].reshape(G * R, NSA_DH).astype(BF16)
        kpg = jnp.concatenate([_page_part(r, hk).astype(BF16) for r in page_refs], axis=0)
        vpg = jnp.concatenate([_page_part(r, NSA_KV_HEADS + hk).astype(BF16) for r in page_refs], axis=0)
        s = _dot_nt(qs, kpg) * scale
        keymask = _dot(selm_ref[hk].astype(BF16), expand)
        valid = jnp.concatenate([keymask] * G, axis=0) > 0.5
        update(hk, s, valid, vpg)

    def win_part(ref, part):
        return ref[pl.ds(part, WINDOW, stride=KV_PARTS), :]

    @pl.when(p == n_steps - 1)
    def _():
        ng = jax.nn.sigmoid(sm_ref[...])
        new_blk = PAST_LEN // SEL_LEN
        zpad = jnp.zeros((LANES - R, NSA_DH), F32)
        jn = _iota((1, LANES), 1)
        for hk in range(NSA_KV_HEADS):
            qs = q_ref[hk * G:(hk + 1) * G].reshape(G * R, NSA_DH).astype(BF16)
            klo, vlo = hk * NSA_DH, (NSA_KV_HEADS + hk) * NSA_DH
            kn = jnp.concatenate([nsel_ref[:, klo:klo + NSA_DH], zpad], axis=0)
            vn = jnp.concatenate([nsel_ref[:, vlo:vlo + NSA_DH], zpad], axis=0)
            s = _dot_nt(qs, kn.astype(BF16)) * scale
            picked = jnp.concatenate([selm_ref[hk][:, new_blk:new_blk + 1]] * G, axis=0) > 0.5
            update(hk, s, picked & (jn <= t_s) & (jn < R), vn)
            o_sel = acc_ref[hk] / jnp.maximum(l_ref[hk], 1e-30)
            kw = jnp.concatenate([win_part(cwin_ref, hk), nwin_ref[:, klo:klo + NSA_DH], zpad], axis=0)
            vw = jnp.concatenate([win_part(cwin_ref, NSA_KV_HEADS + hk), nwin_ref[:, vlo:vlo + NSA_DH], zpad], axis=0)
            iw = _iota((1, WINDOW + LANES), 1)
            wvalid = ((iw < WINDOW) & (iw > t_s)) | ((iw >= WINDOW) & (iw - WINDOW <= t_s) & (iw - WINDOW < R))
            pw = _masked_softmax(_dot_nt(qs, kw.astype(BF16)) * scale, wvalid)
            o_win = _dot(pw.astype(BF16), vw.astype(BF16))
            for g in range(G):
                h = hk * G + g
                lane = SM_GATE0 + 3 * h
                rows = slice(g * R, (g + 1) * R)
                o_ref[h] = (ng[:, lane:lane + 1] * ocmp_ref[h] + ng[:, lane + 1:lane + 2] * o_sel[rows]
                            + ng[:, lane + 2:lane + 3] * o_win[rows])


def sample_attn(q_hm, selm, cache_sel, layer, page_table, new_sel, cache_win, new_win, proj, o_cmp_hm):
    batch, n_pages = page_table.shape
    n_steps = n_pages // PAGES_PER_STEP
    G, R = NSA_GROUP, SAMPLE_ROWS
    hm_spec = pl.BlockSpec((NSA_HEADS, R, NSA_DH), lambda b, p, pt: (0, b, 0))
    row_spec = pl.BlockSpec((R, KV_ROW), lambda b, p, pt: (b, 0))
    gs = pltpu.PrefetchScalarGridSpec(
        num_scalar_prefetch=1,
        grid=(batch, n_steps),
        in_specs=[hm_spec,
                  pl.BlockSpec((None, NSA_KV_HEADS, R, SEL_W), lambda b, p, pt: (b, 0, 0, 0))]
                 + _page_specs(layer)
                 + [row_spec,
                    pl.BlockSpec((None, None, WINDOW * KV_PARTS, NSA_DH), lambda b, p, pt: (layer, b, 0, 0)),
                    row_spec,
                    pl.BlockSpec((R, LANES), lambda b, p, pt: (b, COL_SM // LANES)),
                    hm_spec],
        out_specs=hm_spec,
        scratch_shapes=[pltpu.VMEM((NSA_KV_HEADS, G * R, 1), F32),
                        pltpu.VMEM((NSA_KV_HEADS, G * R, 1), F32),
                        pltpu.VMEM((NSA_KV_HEADS, G * R, NSA_DH), F32)],
    )
    kern = functools.partial(_sample_attn_kernel, n_steps=n_steps)
    return pl.pallas_call(
        kern,
        grid_spec=gs,
        out_shape=jax.ShapeDtypeStruct((NSA_HEADS, batch * R, NSA_DH), F32),
        compiler_params=_cparams(("parallel", "arbitrary")),
        name="sample_attn",
    )(page_table, q_hm, selm, *([cache_sel] * PAGES_PER_STEP), new_sel, cache_win, new_win, proj, o_cmp_hm)


PACK_W = 512
PACK_SRC = PACK_W // LANES + 1
_MAIN_SEGMENTS = _SRC_SEGMENTS[:7]


def _pack_table():
    first, shift = [], []
    for lo, hi in _MAIN_SEGMENTS:
        assert (hi - lo) % PACK_W == 0
        for c in range(lo, hi, PACK_W):
            first.append(c // LANES)
            shift.append(c % LANES)
    assert len(first) * PACK_W == COL_SM
    n = N_PROJ // PACK_W
    first += [0] * (n - len(first))
    shift += [0] * (n - len(shift))
    return np.array([first, shift], np.int32)


def _pack_kernel(tab_ref, *refs):
    srcs, (ga_ref, ng_ref, o_ref) = refs[:PACK_SRC], refs[PACK_SRC:]
    j = pl.program_id(1)
    n_main = COL_SM // PACK_W
    shift = tab_ref[1, j]
    lane = _iota((1, LANES), 1)

    @pl.when(j >= n_main)
    def _():
        sm = jnp.where(lane < GLA_GATE_RANK, ga_ref[...], jnp.where(lane < SM_USED, ng_ref[...], 0.0))
        o_ref[:, :LANES] = sm.astype(BF16)
        o_ref[:, LANES:] = jnp.zeros((o_ref.shape[0], PACK_W - LANES), BF16)

    for sv in sorted({lo % LANES for lo, _ in _MAIN_SEGMENTS}):
        @pl.when((j < n_main) & (shift == sv))
        def _():
            for i in range(PACK_W // LANES):
                cols = slice(i * LANES, (i + 1) * LANES)
                if sv == 0:
                    o_ref[:, cols] = srcs[i][...].astype(BF16)
                else:
                    a = pltpu.roll(srcs[i][...], LANES - sv, 1)
                    b = pltpu.roll(srcs[i + 1][...], LANES - sv, 1)
                    o_ref[:, cols] = jnp.where(lane < LANES - sv, a, b).astype(BF16)


def _pack_w_in(w_in):
    depth, d, n_in = w_in.shape
    (ga_lo, ga_hi), (ng_lo, ng_hi) = _SRC_SEGMENTS[7:]
    assert ga_lo % LANES == 0 and ga_hi - ga_lo == GLA_GATE_RANK
    assert ng_lo % LANES == GLA_GATE_RANK and ng_hi - ng_lo == SM_USED - GLA_GATE_RANK
    last_blk = (n_in - 1) // LANES

    def src_spec(i):
        return pl.BlockSpec((None, d, LANES), lambda l, j, tab: (l, 0, jnp.minimum(tab[0, j] + i, last_blk)))

    def fixed_spec(col):
        return pl.BlockSpec((None, d, LANES), lambda l, j, tab: (l, 0, col // LANES))

    gs = pltpu.PrefetchScalarGridSpec(
        num_scalar_prefetch=1,
        grid=(depth, N_PROJ // PACK_W),
        in_specs=[src_spec(i) for i in range(PACK_SRC)] + [fixed_spec(ga_lo), fixed_spec(ng_lo)],
        out_specs=pl.BlockSpec((None, d, PACK_W), lambda l, j, tab: (l, 0, j)),
    )
    return pl.pallas_call(
        _pack_kernel,
        grid_spec=gs,
        out_shape=jax.ShapeDtypeStruct((depth, d, N_PROJ), BF16),
        compiler_params=_cparams(("parallel", "arbitrary")),
        name="pack_w_in",
    )(jnp.asarray(_pack_table()), *([w_in] * (PACK_SRC + 2)))


def _layer_prompt(x, lw, tabs, layer, batch, T):
    proj = norm_matmul(x, lw['n0'], lw['w_in'], layer, 1024, 1024)
    o_a, gla_state = gla(proj, lw['wa2'], lw['ba'], lw['gn'], None, layer, batch, T, GLA_CHUNK, 4, GLA_CHUNK)
    cmp_rows, sel_rows, win_rows = nsa_prep(proj, tabs, 512, with_q=False)
    a, b = chunk_sums(cmp_rows, lw['pe'], lw['mix'], 512)
    kvc = compressed_kv(a, b, batch, lw['w1'], lw['w2'], 128)
    o_b = nsa_prompt(tabs, kvc, sel_rows, win_rows, proj, batch, T, 256)
    x = merge_wo(x, proj, o_a, o_b, lw['n1'], lw['w_o'], layer, 256)
    x = mlp(x, lw['n2'], lw['n3'], lw['w_up'], lw['w_down'], layer, 512, 1024)
    return x, (cmp_rows, sel_rows, win_rows, gla_state)


def _layer_sample(x, lw, tabs, layer, batch, cache_cmp, cache_sel, cache_win, s0, page_table):
    R = SAMPLE_ROWS
    m = batch * R
    proj = norm_matmul(x, lw['n0'], lw['w_in'], layer, m, 1024)
    o_a, gla_state = gla(proj, lw['wa2'], lw['ba'], lw['gn'], s0, layer, batch, R, R, 1, 4)
    q_hm, cmp_rows, sel_rows, win_rows = nsa_prep(proj, tabs, m, with_q=True)
    a, b = chunk_sums_paged(cache_cmp, layer, page_table, lw['pe'], lw['mix'])
    kvc = compressed_kv(a, b, batch, lw['w1'], lw['w2'], 1024)
    n_slc = -(-(PAST_LEN + 4) // SEL_LEN)
    o_cmp, selm = sample_cmp_select(q_hm, kvc, batch, n_slc)
    o_b = sample_attn(q_hm, selm, cache_sel, layer, page_table, sel_rows, cache_win, win_rows, proj, o_cmp)
    x = merge_wo(x, proj, o_a, o_b, lw['n1'], lw['w_o'], layer, m)
    x = mlp(x, lw['n2'], lw['n3'], lw['w_up'], lw['w_down'], layer, m, 1024)
    return x, (cmp_rows, sel_rows, win_rows, gla_state)


def kernel(x_prompt, x_sample, cache_cmp_kv, cache_sel_kv, cache_win_kv, state_gla, page_table,
           w_in, gla_wa2, gla_ba, gla_norm, cmp_pe, cmp_mix, cmp_w1, cmp_w2, w_o, norms, w_up, w_down):
    bp, T, d = x_prompt.shape
    bs, ss, _ = x_sample.shape
    R = SAMPLE_ROWS
    depth = w_in.shape[0]
    n_pool = cache_cmp_kv.shape[1]
    wb = cache_win_kv.shape[2]
    assert wb == WINDOW and T % 512 == 0 and ss <= R

    w_in_p = _pack_w_in(w_in)
    w_o_b, w_up_b, w_down_b = w_o.astype(BF16), w_up.astype(BF16), w_down.astype(BF16)
    cache_cmp = cache_cmp_kv.reshape(depth, n_pool, PAGE_SIZE * KV_PARTS, NSA_DH)
    cache_sel = cache_sel_kv.reshape(depth, n_pool, PAGE_SIZE * KV_PARTS, NSA_DH)
    cache_win = cache_win_kv.reshape(depth, bs, wb * KV_PARTS, NSA_DH)

    tabs_p = rope_tables(jnp.arange(T, dtype=jnp.int32))
    tabs_s = tuple(jnp.tile(t, (bs, 1)) for t in rope_tables(PAST_LEN + jnp.arange(R, dtype=jnp.int32)))

    y_p = x_prompt.reshape(bp * T, d)
    y_s = jnp.pad(x_sample, ((0, 0), (0, R - ss), (0, 0))).reshape(bs * R, d)
    st_p, st_s = [], []
    for layer in range(depth):
        lw = {'w_in': w_in_p, 'wa2': gla_wa2[layer], 'ba': gla_ba[layer][None, :],
              'gn': gla_norm[layer][None, :], 'pe': cmp_pe[layer], 'mix': cmp_mix[layer],
              'w1': cmp_w1[layer], 'w2': cmp_w2[layer], 'w_o': w_o_b,
              'n0': norms[layer, 0][None, :], 'n1': norms[layer, 1][None, :],
              'n2': norms[layer, 2][None, :], 'n3': norms[layer, 3][None, :],
              'w_up': w_up_b, 'w_down': w_down_b}
        y_p, sp = _layer_prompt(y_p, lw, tabs_p, layer, bp, T)
        y_s, s_s = _layer_sample(y_s, lw, tabs_s, layer, bs, cache_cmp, cache_sel, cache_win,
                                 state_gla, page_table)
        st_p.append(sp)
        st_s.append(s_s)

    kv_shape = (2, NSA_KV_HEADS, NSA_DH)

    def rows_p(i):
        return jnp.stack([s[i].reshape(bp, T, *kv_shape) for s in st_p])

    def rows_s(i):
        return jnp.stack([s[i].reshape(bs, R, *kv_shape)[:, :ss] for s in st_s])

    win_p = rows_p(2)[:, :, T - min(WINDOW, T):]
    win_s = jnp.concatenate([cache_win_kv[:, :, ss:], rows_s(2)], axis=2)
    return (y_p.reshape(bp, T, d), y_s.reshape(bs, R, d)[:, :ss],
            rows_p(0), rows_s(0), rows_p(1), rows_s(1), win_p, win_s,
            jnp.stack([s[3] for s in st_p]), jnp.stack([s[3] for s in st_s]))
```

```python
import functools

import jax
import jax.numpy as jnp
import numpy as np
from jax import lax
from jax.experimental import pallas as pl
from jax.experimental.pallas import tpu as pltpu

F32 = jnp.float32
BF16 = jnp.bfloat16

D_MODEL = 2048
DEPTH = 4
PAST_LEN = 16384
PAGE_SIZE = 128

GLA_HEADS = 4
GLA_DK = 256
GLA_DV = 512
GLA_GATE_RANK = 16
GLA_TAU = 16.0
GLA_CHUNK = 64
GLA_CHUNK_SHIFT = 6

NSA_HEADS = 16
NSA_KV_HEADS = 2
NSA_DH = 128
NSA_GROUP = NSA_HEADS // NSA_KV_HEADS
CMP_LEN = 32
CMP_STRIDE = 16
CMP_HIDDEN = 256
SEL_LEN = 64
SEL_SHIFT = 6
SEL_TOPN = 16
WINDOW = 512
WIN_QBLK = 128

ROPE_THETA = 500000.0
ROPE_DIM = 32
ROPE_HALF = 16
MLP_HIDDEN = 4 * D_MODEL
NORM_EPS = 1e-6
NEG_INF = -1e30
FORCED_SCORE = 1e6
KV_PARTS = 2 * NSA_KV_HEADS
KV_ROW = KV_PARTS * NSA_DH
PAGES_PER_STEP = 16

VMEM_LIMIT_BYTES = 56 * 1024 * 1024
LANES = 128

COL_MG = 0
COL_GV = 4096
COL_GG = 6144
COL_NQ = 8192
COL_GQ = 10240
COL_GK = 11264
COL_NKV = 12288
COL_SM = 13824
N_PROJ = 14336
SM_GATE0 = GLA_GATE_RANK
SM_USED = GLA_GATE_RANK + 3 * NSA_HEADS

_SRC_SEGMENTS = ((9792, 13888), (2048, 4096), (4096, 6144), (6160, 8208), (0, 1024), (1024, 2048),
                 (8208, 9744), (6144, 6160), (9744, 9792))

SAMPLE_ROWS = 8


def _cparams(sem):
    return pltpu.CompilerParams(dimension_semantics=sem, vmem_limit_bytes=VMEM_LIMIT_BYTES)


def _rms(x, g):
    return x * lax.rsqrt(jnp.mean(x * x, axis=-1, keepdims=True) + NORM_EPS) * g


def _dot(a, b, precision=None):
    return jnp.dot(a, b, preferred_element_type=F32, precision=precision)


def _dot_nt(a, b, precision=None):
    return lax.dot_general(a, b, (((1,), (1,)), ((), ())), preferred_element_type=F32, precision=precision)


def _dot_tn(a, b, precision=None):
    return lax.dot_general(a, b, (((0,), (0,)), ((), ())), preferred_element_type=F32, precision=precision)


def _split_bf16(x):
    hi = x.astype(BF16)
    return hi, (x - hi.astype(F32)).astype(BF16)


def _dot_nt_3x(a, b):
    ah, al = _split_bf16(a)
    bh, bl = _split_bf16(b)
    return (_dot_nt(jnp.concatenate([ah, al], axis=1), jnp.concatenate([bh, bh], axis=1))
            + _dot_nt(ah, bl))


def _dot_3x(a, b):
    ah, al = _split_bf16(a)
    bh, bl = _split_bf16(b)
    return (_dot(jnp.concatenate([ah, al], axis=1), jnp.concatenate([bh, bh], axis=0))
            + _dot(ah, bl))


def _iota(shape, dim):
    return lax.broadcasted_iota(jnp.int32, shape, dim)


def _masked_softmax(s, valid):
    s = jnp.where(valid, s, NEG_INF)
    m = jnp.max(s, axis=-1, keepdims=True)
    p = jnp.where(valid, jnp.exp(s - m), 0.0)
    return p / jnp.maximum(jnp.sum(p, axis=-1, keepdims=True), 1e-30)


def _norm_matmul_kernel(x_ref, g_ref, w_ref, o_ref, h_ref):
    @pl.when(pl.program_id(1) == 0)
    def _():
        h_ref[...] = _rms(x_ref[...], g_ref[...]).astype(BF16)

    o_ref[...] = _dot(h_ref[...], w_ref[...]).astype(o_ref.dtype)


def norm_matmul(x, g, w, layer, tm, tn, out_dtype):
    m, k = x.shape
    n = w.shape[2]
    return pl.pallas_call(
        _norm_matmul_kernel,
        grid=(m // tm, n // tn),
        in_specs=[pl.BlockSpec((tm, k), lambda i, j: (i, 0)),
                  pl.BlockSpec((1, k), lambda i, j: (0, 0)),
                  pl.BlockSpec((None, k, tn), lambda i, j: (layer, 0, j))],
        out_specs=pl.BlockSpec((tm, tn), lambda i, j: (i, j)),
        out_shape=jax.ShapeDtypeStruct((m, n), out_dtype),
        scratch_shapes=[pltpu.VMEM((tm, k), BF16)],
        compiler_params=_cparams(("parallel", "arbitrary")),
        name="norm_matmul",
    )(x, g, w)


def _merge_wo_kernel(x_ref, mg0_ref, mg1_ref, oa_ref, ob_ref, g_ref, wo_ref, o_ref):
    ob = jnp.concatenate([ob_ref[h].astype(F32) for h in range(NSA_HEADS)], axis=1)
    a = (jax.nn.sigmoid(mg0_ref[...].astype(F32)) * oa_ref[...].astype(F32)
         + jax.nn.sigmoid(mg1_ref[...].astype(F32)) * ob)
    m = _dot(a.astype(BF16), wo_ref[...])
    o_ref[...] = x_ref[...] + _rms(m, g_ref[...])


def merge_wo(x, proj, o_a, o_b_hm, g, wo, layer, tm):
    m, d = x.shape
    return pl.pallas_call(
        _merge_wo_kernel,
        grid=(m // tm,),
        in_specs=[pl.BlockSpec((tm, d), lambda i: (i, 0)),
                  pl.BlockSpec((tm, d), lambda i: (i, COL_MG // D_MODEL)),
                  pl.BlockSpec((tm, d), lambda i: (i, COL_MG // D_MODEL + 1)),
                  pl.BlockSpec((tm, d), lambda i: (i, 0)),
                  pl.BlockSpec((NSA_HEADS, tm, NSA_DH), lambda i: (0, i, 0)),
                  pl.BlockSpec((1, d), lambda i: (0, 0)),
                  pl.BlockSpec((None, d, d), lambda i: (layer, 0, 0))],
        out_specs=pl.BlockSpec((tm, d), lambda i: (i, 0)),
        out_shape=jax.ShapeDtypeStruct((m, d), F32),
        compiler_params=_cparams(("parallel",)),
        name="merge_wo",
    )(x, proj, proj, o_a, o_b_hm, g, wo)


def _mlp_kernel(x_ref, g2_ref, g3_ref, wu_ref, wd_ref, o_ref, h_ref, acc_ref):
    j = pl.program_id(1)

    @pl.when(j == 0)
    def _():
        h_ref[...] = _rms(x_ref[...], g2_ref[...]).astype(BF16)
        acc_ref[...] = jnp.zeros_like(acc_ref)

    u = _dot(h_ref[...], wu_ref[...])
    u = jnp.square(jnp.maximum(u, 0.0)).astype(BF16)
    acc_ref[...] += _dot(u, wd_ref[...])

    @pl.when(j == pl.num_programs(1) - 1)
    def _():
        o_ref[...] = x_ref[...] + _rms(acc_ref[...], g3_ref[...])


def mlp(x, g2, g3, wu, wd, layer, tm, th):
    m, d = x.shape
    hid = wu.shape[2]
    return pl.pallas_call(
        _mlp_kernel,
        grid=(m // tm, hid // th),
        in_specs=[pl.BlockSpec((tm, d), lambda i, j: (i, 0)),
                  pl.BlockSpec((1, d), lambda i, j: (0, 0)),
                  pl.BlockSpec((1, d), lambda i, j: (0, 0)),
                  pl.BlockSpec((None, d, th), lambda i, j: (layer, 0, j)),
                  pl.BlockSpec((None, th, d), lambda i, j: (layer, j, 0))],
        out_specs=pl.BlockSpec((tm, d), lambda i, j: (i, 0)),
        out_shape=jax.ShapeDtypeStruct((m, d), F32),
        scratch_shapes=[pltpu.VMEM((tm, d), BF16), pltpu.VMEM((tm, d), F32)],
        compiler_params=_cparams(("parallel", "arbitrary")),
        name="mlp",
    )(x, g2, g3, wu, wd)


def _gla_kernel(q_ref, k_ref, v_ref, gg_ref, sm_ref, wa2_ref, ba_ref, gn_ref, s0_ref,
                o_ref, sout_ref, st_ref, *, rows_in, n_inner, n_valid, has_s0):
    c = pl.program_id(1)
    C = GLA_CHUNK

    @pl.when(c == 0)
    def _():
        for h in range(GLA_HEADS):
            if has_s0:
                st_ref[h] = s0_ref[h].T
            else:
                st_ref[h] = jnp.zeros((GLA_DV, GLA_DK), F32)

    rp = n_inner * C

    def load(ref):
        x = ref[...].astype(F32)
        if rows_in == C:
            return x
        assert n_inner == 1
        return jnp.concatenate([x, jnp.zeros((C - rows_in, x.shape[1]), x.dtype)], axis=0)

    row = _iota((rp, 1), 0)
    col = _iota((1, rp), 1)
    same_chunk = (row >> GLA_CHUNK_SHIFT) == (col >> GLA_CHUNK_SHIFT)
    causal = same_chunk & (row >= col)
    tril = causal.astype(BF16)
    ones_blk = same_chunk.astype(BF16)
    live = (row & (C - 1)) < n_valid

    ga_h, ga_l = _split_bf16(load(sm_ref)[:, :GLA_GATE_RANK])
    w_h, w_l = _split_bf16(wa2_ref[...])
    z = _dot(ga_h, w_h) + _dot(ga_l, w_h) + _dot(ga_h, w_l) + ba_ref[...]
    log_a = (jnp.minimum(z, 0.0) - jnp.log1p(jnp.exp(-jnp.abs(z)))) / GLA_TAU
    k = load(k_ref)
    if n_valid < C:
        log_a = jnp.where(live, log_a, 0.0)
        k = jnp.where(live, k, 0.0)
    la_h, la_l = _split_bf16(log_a)
    b = _dot(tril, la_h) + _dot(tril, la_l)
    b_last = _dot(ones_blk, la_h) + _dot(ones_blk, la_l)
    q_dec = (load(q_ref) * (GLA_DK ** -0.5) * jnp.exp(b)).astype(BF16)
    k_dec = (k * jnp.exp(-b)).astype(BF16)
    k_end = (k * jnp.exp(b_last - b)).astype(BF16)
    decay = jnp.exp(b_last)
    vb = load(v_ref).astype(BF16)

    for h in range(GLA_HEADS):
        kq = slice(h * GLA_DK, (h + 1) * GLA_DK)
        kv = slice(h * GLA_DV, (h + 1) * GLA_DV)
        att = jnp.where(causal, _dot_nt(q_dec[:, kq], k_dec[:, kq]), 0.0)
        intra = _dot(att.astype(BF16), vb[:, kv])
        for ci in range(n_inner):
            cr = slice(ci * C, (ci + 1) * C)
            st = st_ref[h]
            o = intra[cr] + _dot_nt(q_dec[cr, kq], st.astype(BF16))
            st_ref[h] = st * decay[ci * C:ci * C + 1, kq] + _dot_tn(vb[cr, kv], k_end[cr, kq])
            o_n = _rms(o, gn_ref[...])
            rs = pl.ds(ci * rows_in, rows_in)
            gg = gg_ref[rs, kv].astype(F32)
            o_ref[rs, kv] = (o_n[:rows_in] * (gg * jax.nn.sigmoid(gg))).astype(o_ref.dtype)

    @pl.when(c == pl.num_programs(1) - 1)
    def _():
        for h in range(GLA_HEADS):
            sout_ref[h] = st_ref[h].T


def gla(proj, wa2, ba, gn, s0, layer, batch, rows_per_batch, rows_in, n_inner, n_valid):
    m = proj.shape[0]
    r = rows_in * n_inner
    n_steps = rows_per_batch // r
    has_s0 = s0 is not None
    if s0 is None:
        s0 = jnp.zeros((1, 1, GLA_HEADS, GLA_DK, GLA_DV), F32)
        s0_map = lambda b, c: (0, 0, 0, 0, 0)
    else:
        s0_map = lambda b, c: (layer, b, 0, 0, 0)
    wk, wv = GLA_HEADS * GLA_DK, GLA_HEADS * GLA_DV
    rowmap = lambda off: (lambda b, c: (b * n_steps + c, off))
    state_blk = (None, GLA_HEADS, GLA_DK, GLA_DV)
    kern = functools.partial(_gla_kernel, rows_in=rows_in, n_inner=n_inner, n_valid=n_valid, has_s0=has_s0)
    return pl.pallas_call(
        kern,
        grid=(batch, n_steps),
        in_specs=[pl.BlockSpec((r, wk), rowmap(COL_GQ // wk)),
                  pl.BlockSpec((r, wk), rowmap(COL_GK // wk)),
                  pl.BlockSpec((r, wv), rowmap(COL_GV // wv)),
                  pl.BlockSpec((r, wv), rowmap(COL_GG // wv)),
                  pl.BlockSpec((r, LANES), rowmap(COL_SM // LANES)),
                  pl.BlockSpec((GLA_GATE_RANK, wk), lambda b, c: (0, 0)),
                  pl.BlockSpec((1, wk), lambda b, c: (0, 0)),
                  pl.BlockSpec((1, GLA_DV), lambda b, c: (0, 0)),
                  pl.BlockSpec((None,) + state_blk, s0_map)],
        out_specs=[pl.BlockSpec((r, wv), rowmap(0)),
                   pl.BlockSpec(state_blk, lambda b, c: (b, 0, 0, 0))],
        out_shape=[jax.ShapeDtypeStruct((m, D_MODEL), proj.dtype),
                   jax.ShapeDtypeStruct((batch, GLA_HEADS, GLA_DK, GLA_DV), F32)],
        scratch_shapes=[pltpu.VMEM((GLA_HEADS, GLA_DV, GLA_DK), F32)],
        compiler_params=_cparams(("parallel", "arbitrary")),
        name="gla",
    )(proj, proj, proj, proj, proj, wa2, ba, gn, s0)


def _rope(x, c, sa, sb):
    return x * c + pltpu.roll(x, LANES - ROPE_HALF, 1) * sa + pltpu.roll(x, ROPE_HALF, 1) * sb


def _nsa_prep_kernel(*refs, with_q):
    if with_q:
        nq_ref, nkv_ref, c_ref, sa_ref, sb_ref, q_ref, cmp_ref, sel_ref, win_ref = refs
    else:
        nkv_ref, c_ref, sa_ref, sb_ref, cmp_ref, sel_ref, win_ref = refs
    c, sa, sb = c_ref[...], sa_ref[...], sb_ref[...]
    if with_q:
        for h in range(NSA_HEADS):
            q_ref[h] = _rope(nq_ref[:, h * NSA_DH:(h + 1) * NSA_DH].astype(F32), c, sa, sb)
    for s, out in enumerate((cmp_ref, sel_ref, win_ref)):
        base = s * KV_ROW
        for hh in range(NSA_KV_HEADS):
            lo = hh * NSA_DH
            out[:, lo:lo + NSA_DH] = _rope(nkv_ref[:, base + lo:base + lo + NSA_DH].astype(F32), c, sa, sb)
        half = NSA_KV_HEADS * NSA_DH
        out[:, half:] = nkv_ref[:, base + half:base + KV_ROW].astype(F32)


def nsa_prep(proj, tabs, tr, with_q):
    m = proj.shape[0]
    n_tab = tabs[0].shape[0] // tr
    tab_spec = pl.BlockSpec((tr, LANES), lambda i: (i % n_tab, 0))
    nkv_w = 3 * KV_ROW
    row_spec = pl.BlockSpec((tr, KV_ROW), lambda i: (i, 0))
    row_shape = jax.ShapeDtypeStruct((m, KV_ROW), F32)
    in_specs = [pl.BlockSpec((tr, nkv_w), lambda i: (i, COL_NKV // nkv_w)), tab_spec, tab_spec, tab_spec]
    out_specs, out_shape, args = [row_spec] * 3, [row_shape] * 3, (proj,) + tuple(tabs)
    if with_q:
        in_specs = [pl.BlockSpec((tr, D_MODEL), lambda i: (i, COL_NQ // D_MODEL))] + in_specs
        out_specs = [pl.BlockSpec((NSA_HEADS, tr, NSA_DH), lambda i: (0, i, 0))] + out_specs
        out_shape = [jax.ShapeDtypeStruct((NSA_HEADS, m, NSA_DH), F32)] + out_shape
        args = (proj,) + args
    return pl.pallas_call(
        functools.partial(_nsa_prep_kernel, with_q=with_q),
        grid=(m // tr,),
        in_specs=in_specs,
        out_specs=out_specs,
        out_shape=out_shape,
        compiler_params=_cparams(("parallel",)),
        name="nsa_prep",
    )(*args)


def rope_tables(pos):
    inv = 1.0 / (ROPE_THETA ** (jnp.arange(ROPE_HALF, dtype=F32) / ROPE_HALF))
    ang = pos.astype(F32)[:, None] * inv[None, :]
    cos, sin = jnp.cos(ang), jnp.sin(ang)
    n = pos.shape[0]
    rest = LANES - ROPE_DIM
    c = jnp.concatenate([cos, cos, jnp.ones((n, rest), F32)], axis=1)
    sa = jnp.concatenate([-sin, jnp.zeros((n, LANES - ROPE_HALF), F32)], axis=1)
    sb = jnp.concatenate([jnp.zeros((n, ROPE_HALF), F32), sin, jnp.zeros((n, rest), F32)], axis=1)
    return c, sa, sb


_HALVES = (slice(0, CMP_STRIDE), slice(CMP_STRIDE, CMP_LEN))


def _pe_mix_sums(pe_ref, mix_ref):
    return [[jnp.sum(pe_ref[kv][h] * mix_ref[kv][h], axis=0, keepdims=True) for h in _HALVES] for kv in range(2)]


def _half_sums(xs, mix, consts):
    return [jnp.sum(xs * mix[h], axis=1) + c for h, c in zip(_HALVES, consts)]


def _chunk_sums_kernel(x_ref, pe_ref, mix_ref, a_ref, b_ref):
    nch = x_ref.shape[0] // CMP_STRIDE
    consts = _pe_mix_sums(pe_ref, mix_ref)
    for kv in range(2):
        for hh in range(NSA_KV_HEADS):
            lo = (kv * NSA_KV_HEADS + hh) * NSA_DH
            xs = x_ref[:, lo:lo + NSA_DH].reshape(nch, CMP_STRIDE, NSA_DH)
            a_ref[:, lo:lo + NSA_DH], b_ref[:, lo:lo + NSA_DH] = _half_sums(xs, mix_ref[kv], consts[kv])


def _page_specs(layer):
    def spec(i):
        return pl.BlockSpec((None, None, PAGE_SIZE * KV_PARTS, NSA_DH),
                            lambda b, s, pt: (layer, pt[b, s * PAGES_PER_STEP + i], 0, 0))
    return [spec(i) for i in range(PAGES_PER_STEP)]


def _page_part(page_ref, part):
    return page_ref[pl.ds(part, PAGE_SIZE, stride=KV_PARTS), :]


def _chunk_sums_paged_kernel(pt_ref, *refs):
    page_refs, (pe_ref, mix_ref, a_ref, b_ref) = refs[:PAGES_PER_STEP], refs[PAGES_PER_STEP:]
    nch = PAGE_SIZE // CMP_STRIDE
    consts = _pe_mix_sums(pe_ref, mix_ref)
    for i, page_ref in enumerate(page_refs):
        rows = slice(i * nch, (i + 1) * nch)
        for kv in range(2):
            for hh in range(NSA_KV_HEADS):
                part = kv * NSA_KV_HEADS + hh
                lo = part * NSA_DH
                xs = _page_part(page_ref, part).reshape(nch, CMP_STRIDE, NSA_DH)
                a_ref[rows, lo:lo + NSA_DH], b_ref[rows, lo:lo + NSA_DH] = _half_sums(xs, mix_ref[kv], consts[kv])


def chunk_sums(rows, pe, mix, tr):
    m = rows.shape[0]
    nch = tr // CMP_STRIDE
    full = lambda i: (0, 0, 0)
    return pl.pallas_call(
        _chunk_sums_kernel,
        grid=(m // tr,),
        in_specs=[pl.BlockSpec((tr, KV_ROW), lambda i: (i, 0)),
                  pl.BlockSpec((2, CMP_LEN, NSA_DH), full),
                  pl.BlockSpec((2, CMP_LEN, NSA_DH), full)],
        out_specs=[pl.BlockSpec((nch, KV_ROW), lambda i: (i, 0))] * 2,
        out_shape=[jax.ShapeDtypeStruct((m // CMP_STRIDE, KV_ROW), F32)] * 2,
        compiler_params=_cparams(("parallel",)),
        name="chunk_sums",
    )(rows, pe, mix)


def chunk_sums_paged(cache, layer, page_table, pe, mix):
    batch, n_pages = page_table.shape
    nch = PAGES_PER_STEP * PAGE_SIZE // CMP_STRIDE
    n_steps = n_pages // PAGES_PER_STEP
    full = lambda b, s, pt: (0, 0, 0)
    gs = pltpu.PrefetchScalarGridSpec(
        num_scalar_prefetch=1,
        grid=(batch, n_steps),
        in_specs=_page_specs(layer) + [pl.BlockSpec((2, CMP_LEN, NSA_DH), full),
                                       pl.BlockSpec((2, CMP_LEN, NSA_DH), full)],
        out_specs=[pl.BlockSpec((nch, KV_ROW), lambda b, s, pt: (b * n_steps + s, 0))] * 2,
    )
    return pl.pallas_call(
        _chunk_sums_paged_kernel,
        grid_spec=gs,
        out_shape=[jax.ShapeDtypeStruct((batch * n_steps * nch, KV_ROW), F32)] * 2,
        compiler_params=_cparams(("parallel", "arbitrary")),
        name="chunk_sums_paged",
    )(page_table, *([cache] * PAGES_PER_STEP), pe, mix)


def _cmp_mlp_kernel(a_ref, b_ref, w1_ref, w2_ref, o_ref):
    h = a_ref[...] + b_ref[...]
    for kv in range(2):
        for hh in range(NSA_KV_HEADS):
            lo = (kv * NSA_KV_HEADS + hh) * NSA_DH
            y = jax.nn.gelu(_dot_3x(h[:, lo:lo + NSA_DH], w1_ref[kv]))
            o_ref[:, lo:lo + NSA_DH] = _dot_3x(y, w2_ref[kv])


def cmp_mlp(a, b_shift, w1, w2, tr):
    m = a.shape[0]
    return pl.pallas_call(
        _cmp_mlp_kernel,
        grid=(m // tr,),
        in_specs=[pl.BlockSpec((tr, KV_ROW), lambda i: (i, 0)),
                  pl.BlockSpec((tr, KV_ROW), lambda i: (i, 0)),
                  pl.BlockSpec((2, NSA_DH, CMP_HIDDEN), lambda i: (0, 0, 0)),
                  pl.BlockSpec((2, CMP_HIDDEN, NSA_DH), lambda i: (0, 0, 0))],
        out_specs=pl.BlockSpec((tr, KV_ROW), lambda i: (i, 0)),
        out_shape=jax.ShapeDtypeStruct((m, KV_ROW), F32),
        compiler_params=_cparams(("parallel",)),
        name="cmp_mlp",
    )(a, b_shift, w1, w2)


def compressed_kv(a, b, batch, w1, w2, tr):
    nch = a.shape[0] // batch
    b3 = b.reshape(batch, nch, KV_ROW)
    b_shift = jnp.concatenate([b3[:, 1:], jnp.zeros((batch, 1, KV_ROW), F32)], axis=1).reshape(batch * nch, KV_ROW)
    return cmp_mlp(a, b_shift, w1, w2, tr)


def _select_topn(imp, n_top):
    j = _iota(imp.shape, 1)
    big = jnp.int32(imp.shape[1])
    sel = jnp.zeros(imp.shape, F32)
    for _ in range(n_top):
        m = jnp.max(imp, axis=-1, keepdims=True)
        idx = jnp.min(jnp.where(imp == m, j, big), axis=-1, keepdims=True)
        hit = j == idx
        sel = jnp.where(hit, 1.0, sel)
        imp = jnp.where(hit, NEG_INF, imp)
    return sel


def _importance(p_grp, qpos, n_cmp_pad, n_slc, width):
    n_r = _iota((n_cmp_pad, 1), 0) * CMP_STRIDE
    j_c = _iota((1, width), 1)
    overlap = ((n_r < (j_c + 1) * SEL_LEN) & (n_r + CMP_LEN > j_c * SEL_LEN)).astype(BF16)
    p_hi, p_lo = _split_bf16(p_grp)
    imp = _dot(p_hi, overlap) + _dot(p_lo, overlap)
    cur = qpos >> SEL_SHIFT
    forced = (j_c == 0) | (j_c == cur) | (j_c == cur - 1)
    imp = jnp.where(forced, FORCED_SCORE, imp)
    imp = jnp.where(j_c > cur, -1.0, imp)
    return jnp.where(j_c >= n_slc, -2.0, imp)


LOG2E = 1.4426950408889634


def _nsa_prompt_kernel(nq_ref, c_ref, sa_ref, sb_ref, kc_ref, vc_ref, ks_ref, vs_ref, kw_ref, vw_ref, sm_ref, o_ref,
                       qf_ref, qb_ref, s_ref, p_ref, m_ref, l_ref, a_ref, acc_ref, ob_ref, *, Q, T, TK):
    G = NSA_GROUP
    hk = pl.program_id(1)
    qi = pl.program_id(2)
    q0 = qi * Q
    scale = NSA_DH ** -0.5
    n_cmp_pad = kc_ref.shape[0]
    n_slc = T // SEL_LEN
    qpos = q0 + _iota((Q, 1), 0)
    c, sa, sb = c_ref[...], sa_ref[...], sb_ref[...]
    for g in range(G):
        qg = _rope(nq_ref[:, g * NSA_DH:(g + 1) * NSA_DH].astype(F32), c, sa, sb)
        qf_ref[g * Q:(g + 1) * Q, :] = qg
        qb_ref[g * Q:(g + 1) * Q, :] = (qg * (scale * LOG2E)).astype(BF16)
    qs = qf_ref[...]

    ng = jax.nn.sigmoid(sm_ref[...].astype(F32))

    def gate(g, c3):
        lane0 = SM_GATE0 + 3 * g + c3
        lane1 = lane0 + 3 * G
        return jnp.where(hk == 0, ng[:, lane0:lane0 + 1], ng[:, lane1:lane1 + 1])

    def softmax_tile(kb, vb, bias, width):
        s_ref[:, :width] = _dot_nt(qb_ref[...], kb)
        for g in range(G):
            rows = pl.ds(g * Q, Q)
            s = s_ref[rows, :width] + bias
            m_old = m_ref[rows, :]
            m_new = jnp.maximum(m_old, jnp.max(s, axis=-1, keepdims=True))
            alpha = jnp.exp2(m_old - m_new)
            p = jnp.exp2(s - jnp.tile(m_new, (1, width // LANES)))
            l_ref[rows, :] = alpha * l_ref[rows, :] + jnp.sum(p, axis=-1, keepdims=True)
            a_ref[rows, :] = alpha
            m_ref[rows, :] = m_new
            p_ref[rows, :width] = p.astype(BF16)
        return _dot(p_ref[:, :width], vb)

    def reset():
        m_ref[...] = jnp.full_like(m_ref, NEG_INF)
        l_ref[...] = jnp.zeros_like(l_ref)

    cmp_valid = _iota((1, n_cmp_pad), 1) * CMP_STRIDE + (CMP_LEN - 1) <= qpos
    p_grp = jnp.zeros((Q, n_cmp_pad), F32)
    s_cmp = _dot_nt_3x(qs, kc_ref[...]) * scale
    vcb = vc_ref[...].astype(BF16)
    for g in range(G):
        p = _masked_softmax(s_cmp[g * Q:(g + 1) * Q], cmp_valid)
        p_grp = p_grp + p
        ob_ref[g] = gate(g, 0) * _dot(p.astype(BF16), vcb)

    ks0 = pl.multiple_of(jnp.maximum(q0 - WINDOW, 0), LANES)
    wlen = WINDOW + Q
    wpos = ks0 + _iota((1, wlen), 1)
    win_bias = jnp.where((wpos <= qpos) & (wpos > qpos - WINDOW), 0.0, NEG_INF)
    reset()
    pv = softmax_tile(kw_ref[pl.ds(ks0, wlen), :].astype(BF16), vw_ref[pl.ds(ks0, wlen), :].astype(BF16),
                      win_bias, wlen)
    for g in range(G):
        rows = pl.ds(g * Q, Q)
        ob_ref[g] = ob_ref[g] + gate(g, 2) * (pv[g * Q:(g + 1) * Q] / jnp.maximum(l_ref[rows, :], 1e-30))

    n_top = min(SEL_TOPN, n_slc)
    j_r = _iota((n_slc, 1), 0)
    n_c = _iota((1, n_cmp_pad), 1) * CMP_STRIDE
    overlap_t = ((n_c < (j_r + 1) * SEL_LEN) & (n_c + CMP_LEN > j_r * SEL_LEN)).astype(BF16)
    p_hi, p_lo = _split_bf16(p_grp)
    imp = _dot_nt(overlap_t, p_hi) + _dot_nt(overlap_t, p_lo)
    cur = (q0 + _iota((1, Q), 1)) >> SEL_SHIFT
    imp = jnp.where((j_r == 0) | (j_r == cur) | (j_r == cur - 1), FORCED_SCORE, imp)
    imp = jnp.where(j_r > cur, -1.0, imp)
    rank = jnp.zeros((n_slc, Q), F32)
    for jp in range(n_slc):
        other = imp[jp:jp + 1, :]
        beats = (other > imp) | ((other == imp) & (j_r > jp))
        rank = rank + jnp.where(beats, 1.0, 0.0)
    sel_t = jnp.where(rank < n_top, 1.0, 0.0).astype(BF16)

    reset()
    acc_ref[...] = jnp.zeros_like(acc_ref)

    def key_tile(kt, carry):
        k0 = pl.multiple_of(kt * TK, TK)
        kpos = k0 + _iota((1, TK), 1)
        expand = (j_r == (kpos >> SEL_SHIFT)).astype(BF16)
        keymask = _dot_tn(sel_t, expand)
        bias = jnp.where((keymask > 0.5) & (kpos <= qpos), 0.0, NEG_INF)
        pv = softmax_tile(ks_ref[pl.ds(k0, TK), :].astype(BF16), vs_ref[pl.ds(k0, TK), :].astype(BF16), bias, TK)
        acc_ref[...] = a_ref[...] * acc_ref[...] + pv
        return carry

    lax.fori_loop(0, (q0 + Q + TK - 1) // TK, key_tile, 0)

    for g in range(G):
        rows = pl.ds(g * Q, Q)
        o_sel = acc_ref[rows, :] / jnp.maximum(l_ref[rows, :], 1e-30)
        o_ref[g] = (ob_ref[g] + gate(g, 1) * o_sel).astype(o_ref.dtype)


def nsa_prompt(tabs, kvc, sel_rows, win_rows, proj, batch, T, Q):
    m = batch * T
    nq = T // Q
    n_cmp_pad = kvc.shape[0] // batch
    G = NSA_GROUP
    gw = G * NSA_DH
    kern = functools.partial(_nsa_prompt_kernel, Q=Q, T=T, TK=512)
    kcol = lambda off: (lambda b, hk, qi: (b, off + hk))
    tab_spec = pl.BlockSpec((Q, LANES), lambda b, hk, qi: (qi, 0))
    return pl.pallas_call(
        kern,
        grid=(batch, NSA_KV_HEADS, nq),
        in_specs=[pl.BlockSpec((Q, gw), lambda b, hk, qi: (b * nq + qi, COL_NQ // gw + hk)),
                  tab_spec, tab_spec, tab_spec,
                  pl.BlockSpec((n_cmp_pad, NSA_DH), kcol(0)),
                  pl.BlockSpec((n_cmp_pad, NSA_DH), kcol(NSA_KV_HEADS)),
                  pl.BlockSpec((T, NSA_DH), kcol(0)),
                  pl.BlockSpec((T, NSA_DH), kcol(NSA_KV_HEADS)),
                  pl.BlockSpec((T, NSA_DH), kcol(0)),
                  pl.BlockSpec((T, NSA_DH), kcol(NSA_KV_HEADS)),
                  pl.BlockSpec((Q, LANES), lambda b, hk, qi: (b * nq + qi, COL_SM // LANES))],
        out_specs=pl.BlockSpec((G, Q, NSA_DH), lambda b, hk, qi: (hk, b * nq + qi, 0)),
        out_shape=jax.ShapeDtypeStruct((NSA_HEADS, m, NSA_DH), proj.dtype),
        scratch_shapes=[pltpu.VMEM((G * Q, NSA_DH), F32), pltpu.VMEM((G * Q, NSA_DH), BF16),
                        pltpu.VMEM((G * Q, WINDOW + Q), F32), pltpu.VMEM((G * Q, WINDOW + Q), BF16),
                        pltpu.VMEM((G * Q, LANES), F32), pltpu.VMEM((G * Q, LANES), F32),
                        pltpu.VMEM((G * Q, LANES), F32), pltpu.VMEM((G * Q, NSA_DH), F32),
                        pltpu.VMEM((G, Q, NSA_DH), F32)],
        compiler_params=_cparams(("parallel", "parallel", "arbitrary")),
        name="nsa_prompt",
    )(proj, *tabs, kvc, kvc, sel_rows, sel_rows, win_rows, win_rows, proj)


SEL_W = 384


def _sample_cmp_select_kernel(q_ref, kvc_ref, ocmp_ref, selm_ref, *, n_slc):
    G, R = NSA_GROUP, SAMPLE_ROWS
    scale = NSA_DH ** -0.5
    n_cmp_pad = kvc_ref.shape[0]
    qpos_s = PAST_LEN + (_iota((G * R, 1), 0) & (R - 1))
    end = _iota((1, n_cmp_pad), 1) * CMP_STRIDE + (CMP_LEN - 1)
    p_grp = []
    for hk in range(NSA_KV_HEADS):
        qs = q_ref[hk * G:(hk + 1) * G].reshape(G * R, NSA_DH)
        kc = kvc_ref[:, hk * NSA_DH:(hk + 1) * NSA_DH]
        vc = kvc_ref[:, (NSA_KV_HEADS + hk) * NSA_DH:(NSA_KV_HEADS + hk + 1) * NSA_DH]
        p = _masked_softmax(_dot_nt_3x(qs, kc) * scale, end <= qpos_s)
        o_cmp = _dot(p.astype(BF16), vc.astype(BF16))
        for g in range(G):
            ocmp_ref[hk * G + g] = o_cmp[g * R:(g + 1) * R]
        p_grp.append(jnp.sum(p.reshape(G, R, n_cmp_pad), axis=0))
    qpos = PAST_LEN + (_iota((NSA_KV_HEADS * R, 1), 0) & (R - 1))
    imp = _importance(jnp.concatenate(p_grp, axis=0), qpos, n_cmp_pad, n_slc, SEL_W)
    sel = _select_topn(imp, min(SEL_TOPN, n_slc))
    for hk in range(NSA_KV_HEADS):
        selm_ref[hk] = sel[hk * R:(hk + 1) * R]


def sample_cmp_select(q_hm, kvc, batch, n_slc):
    R = SAMPLE_ROWS
    n_cmp_pad = kvc.shape[0] // batch
    kern = functools.partial(_sample_cmp_select_kernel, n_slc=n_slc)
    hm_spec = pl.BlockSpec((NSA_HEADS, R, NSA_DH), lambda b: (0, b, 0))
    return pl.pallas_call(
        kern,
        grid=(batch,),
        in_specs=[hm_spec, pl.BlockSpec((n_cmp_pad, KV_ROW), lambda b: (b, 0))],
        out_specs=[hm_spec, pl.BlockSpec((None, NSA_KV_HEADS, R, SEL_W), lambda b: (b, 0, 0, 0))],
        out_shape=[jax.ShapeDtypeStruct((NSA_HEADS, batch * R, NSA_DH), F32),
                   jax.ShapeDtypeStruct((batch, NSA_KV_HEADS, R, SEL_W), F32)],
        compiler_params=_cparams(("parallel",)),
        name="sample_cmp_select",
    )(q_hm, kvc)


def _sample_attn_kernel(pt_ref, q_ref, selm_ref, *refs, n_steps):
    page_refs = refs[:PAGES_PER_STEP]
    nsel_ref, cwin_ref, nwin_ref, sm_ref, ocmp_ref, o_ref, m_ref, l_ref, acc_ref = refs[PAGES_PER_STEP:]
    G, R = NSA_GROUP, SAMPLE_ROWS
    p = pl.program_id(1)
    scale = NSA_DH ** -0.5
    t_s = _iota((G * R, 1), 0) & (R - 1)
    n_keys = PAGES_PER_STEP * PAGE_SIZE

    @pl.when(p == 0)
    def _():
        m_ref[...] = jnp.full_like(m_ref, NEG_INF)
        l_ref[...] = jnp.zeros_like(l_ref)
        acc_ref[...] = jnp.zeros_like(acc_ref)

    def q_stack(hk, mult):
        return (q_ref[hk * G:(hk + 1) * G].reshape(G * R, NSA_DH) * mult).astype(BF16)

    def update(hk, s, pr_of, v):
        m_old = m_ref[hk]
        m_new = jnp.maximum(m_old, jnp.max(s, axis=-1, keepdims=True))
        alpha = jnp.exp2(m_old - m_new)
        pr = pr_of(jnp.exp2(s - jnp.tile(m_new, (1, s.shape[1] // LANES))))
        l_ref[hk] = alpha * l_ref[hk] + jnp.sum(pr, axis=-1, keepdims=True)
        acc_ref[hk] = alpha * acc_ref[hk] + _dot(pr.astype(BF16), v)
        m_ref[hk] = m_new

    key = _iota((1, n_keys), 1)
    blk_of_key = (n_keys // SEL_LEN) * p + (key >> SEL_SHIFT)
    expand = (_iota((SEL_W, 1), 0) == blk_of_key).astype(BF16)
    for hk in range(NSA_KV_HEADS):
        kpg = jnp.concatenate([_page_part(r, hk).astype(BF16) for r in page_refs], axis=0)
        vpg = jnp.concatenate([_page_part(r, NSA_KV_HEADS + hk).astype(BF16) for r in page_refs], axis=0)
        keymask = _dot(selm_ref[hk].astype(BF16), expand)
        bias = jnp.where(jnp.concatenate([keymask] * G, axis=0) > 0.5, 0.0, NEG_INF)
        update(hk, _dot_nt(q_stack(hk, scale * LOG2E), kpg) + bias, lambda e: e, vpg)

    def win_part(ref, part):
        return ref[pl.ds(part, WINDOW, stride=KV_PARTS), :]

    @pl.when(p == n_steps - 1)
    def _():
        ng = jax.nn.sigmoid(sm_ref[...])
        new_blk = PAST_LEN // SEL_LEN
        zpad = jnp.zeros((LANES - R, NSA_DH), F32)
        jn = _iota((1, LANES), 1)
        for hk in range(NSA_KV_HEADS):
            klo, vlo = hk * NSA_DH, (NSA_KV_HEADS + hk) * NSA_DH
            kn = jnp.concatenate([nsel_ref[:, klo:klo + NSA_DH], zpad], axis=0)
            vn = jnp.concatenate([nsel_ref[:, vlo:vlo + NSA_DH], zpad], axis=0)
            picked = jnp.concatenate([selm_ref[hk][:, new_blk:new_blk + 1]] * G, axis=0) > 0.5
            valid = picked & (jn <= t_s) & (jn < R)
            s = jnp.where(valid, _dot_nt(q_stack(hk, scale * LOG2E), kn.astype(BF16)), NEG_INF)
            update(hk, s, lambda e: jnp.where(valid, e, 0.0), vn.astype(BF16))
            o_sel = acc_ref[hk] / jnp.maximum(l_ref[hk], 1e-30)
            qs = q_stack(hk, scale)
            kw = jnp.concatenate([win_part(cwin_ref, hk), nwin_ref[:, klo:klo + NSA_DH], zpad], axis=0)
            vw = jnp.concatenate([win_part(cwin_ref, NSA_KV_HEADS + hk), nwin_ref[:, vlo:vlo + NSA_DH], zpad], axis=0)
            iw = _iota((1, WINDOW + LANES), 1)
            wvalid = ((iw < WINDOW) & (iw > t_s)) | ((iw >= WINDOW) & (iw - WINDOW <= t_s) & (iw - WINDOW < R))
            pw = _masked_softmax(_dot_nt(qs, kw.astype(BF16)), wvalid)
            o_win = _dot(pw.astype(BF16), vw.astype(BF16))
            for g in range(G):
                h = hk * G + g
                lane = SM_GATE0 + 3 * h
                rows = slice(g * R, (g + 1) * R)
                o_ref[h] = (ng[:, lane:lane + 1] * ocmp_ref[h] + ng[:, lane + 1:lane + 2] * o_sel[rows]
                            + ng[:, lane + 2:lane + 3] * o_win[rows])


def sample_attn(q_hm, selm, cache_sel, layer, page_table, new_sel, cache_win, new_win, proj, o_cmp_hm):
    batch, n_pages = page_table.shape
    n_steps = n_pages // PAGES_PER_STEP
    G, R = NSA_GROUP, SAMPLE_ROWS
    hm_spec = pl.BlockSpec((NSA_HEADS, R, NSA_DH), lambda b, p, pt: (0, b, 0))
    row_spec = pl.BlockSpec((R, KV_ROW), lambda b, p, pt: (b, 0))
    gs = pltpu.PrefetchScalarGridSpec(
        num_scalar_prefetch=1,
        grid=(batch, n_steps),
        in_specs=[hm_spec,
                  pl.BlockSpec((None, NSA_KV_HEADS, R, SEL_W), lambda b, p, pt: (b, 0, 0, 0))]
                 + _page_specs(layer)
                 + [row_spec,
                    pl.BlockSpec((None, None, WINDOW * KV_PARTS, NSA_DH), lambda b, p, pt: (layer, b, 0, 0)),
                    row_spec,
                    pl.BlockSpec((R, LANES), lambda b, p, pt: (b, COL_SM // LANES)),
                    hm_spec],
        out_specs=hm_spec,
        scratch_shapes=[pltpu.VMEM((NSA_KV_HEADS, G * R, LANES), F32),
                        pltpu.VMEM((NSA_KV_HEADS, G * R, LANES), F32),
                        pltpu.VMEM((NSA_KV_HEADS, G * R, NSA_DH), F32)],
    )
    kern = functools.partial(_sample_attn_kernel, n_steps=n_steps)
    return pl.pallas_call(
        kern,
        grid_spec=gs,
        out_shape=jax.ShapeDtypeStruct((NSA_HEADS, batch * R, NSA_DH), F32),
        compiler_params=_cparams(("parallel", "arbitrary")),
        name="sample_attn",
    )(page_table, q_hm, selm, *([cache_sel] * PAGES_PER_STEP), new_sel, cache_win, new_win, proj, o_cmp_hm)


PACK_W = 512
PACK_SRC = PACK_W // LANES + 1
_MAIN_SEGMENTS = _SRC_SEGMENTS[:7]


def _pack_table():
    first, shift = [], []
    for lo, hi in _MAIN_SEGMENTS:
        assert (hi - lo) % PACK_W == 0
        for c in range(lo, hi, PACK_W):
            first.append(c // LANES)
            shift.append(c % LANES)
    assert len(first) * PACK_W == COL_SM
    n = N_PROJ // PACK_W
    first += [0] * (n - len(first))
    shift += [0] * (n - len(shift))
    return np.array([first, shift], np.int32)


def _pack_kernel(tab_ref, *refs):
    srcs, (ga_ref, ng_ref, o_ref) = refs[:PACK_SRC], refs[PACK_SRC:]
    j = pl.program_id(1)
    n_main = COL_SM // PACK_W
    shift = tab_ref[1, j]
    lane = _iota((1, LANES), 1)

    @pl.when(j >= n_main)
    def _():
        sm = jnp.where(lane < GLA_GATE_RANK, ga_ref[...], jnp.where(lane < SM_USED, ng_ref[...], 0.0))
        o_ref[:, :LANES] = sm.astype(BF16)
        o_ref[:, LANES:] = jnp.zeros((o_ref.shape[0], PACK_W - LANES), BF16)

    for sv in sorted({lo % LANES for lo, _ in _MAIN_SEGMENTS}):
        @pl.when((j < n_main) & (shift == sv))
        def _():
            for i in range(PACK_W // LANES):
                cols = slice(i * LANES, (i + 1) * LANES)
                if sv == 0:
                    o_ref[:, cols] = srcs[i][...].astype(BF16)
                else:
                    a = pltpu.roll(srcs[i][...], LANES - sv, 1)
                    b = pltpu.roll(srcs[i + 1][...], LANES - sv, 1)
                    o_ref[:, cols] = jnp.where(lane < LANES - sv, a, b).astype(BF16)


def _pack_w_in(w_in):
    depth, d, n_in = w_in.shape
    (ga_lo, ga_hi), (ng_lo, ng_hi) = _SRC_SEGMENTS[7:]
    assert ga_lo % LANES == 0 and ga_hi - ga_lo == GLA_GATE_RANK
    assert ng_lo % LANES == GLA_GATE_RANK and ng_hi - ng_lo == SM_USED - GLA_GATE_RANK
    last_blk = (n_in - 1) // LANES

    def src_spec(i):
        return pl.BlockSpec((None, d, LANES), lambda l, j, tab: (l, 0, jnp.minimum(tab[0, j] + i, last_blk)))

    def fixed_spec(col):
        return pl.BlockSpec((None, d, LANES), lambda l, j, tab: (l, 0, col // LANES))

    gs = pltpu.PrefetchScalarGridSpec(
        num_scalar_prefetch=1,
        grid=(depth, N_PROJ // PACK_W),
        in_specs=[src_spec(i) for i in range(PACK_SRC)] + [fixed_spec(ga_lo), fixed_spec(ng_lo)],
        out_specs=pl.BlockSpec((None, d, PACK_W), lambda l, j, tab: (l, 0, j)),
    )
    return pl.pallas_call(
        _pack_kernel,
        grid_spec=gs,
        out_shape=jax.ShapeDtypeStruct((depth, d, N_PROJ), BF16),
        compiler_params=_cparams(("parallel", "arbitrary")),
        name="pack_w_in",
    )(jnp.asarray(_pack_table()), *([w_in] * (PACK_SRC + 2)))


def _layer_prompt(x, lw, tabs, layer, batch, T):
    proj = norm_matmul(x, lw['n0'], lw['w_in'], layer, 1024, 2048, BF16)
    o_a, gla_state = gla(proj, lw['wa2'], lw['ba'], lw['gn'], None, layer, batch, T, GLA_CHUNK, 4, GLA_CHUNK)
    cmp_rows, sel_rows, win_rows = nsa_prep(proj, tabs, 512, with_q=False)
    a, b = chunk_sums(cmp_rows, lw['pe'], lw['mix'], 512)
    kvc = compressed_kv(a, b, batch, lw['w1'], lw['w2'], 128)
    o_b = nsa_prompt(tabs, kvc, sel_rows, win_rows, proj, batch, T, 256)
    x = merge_wo(x, proj, o_a, o_b, lw['n1'], lw['w_o'], layer, 256)
    x = mlp(x, lw['n2'], lw['n3'], lw['w_up'], lw['w_down'], layer, 512, 1024)
    return x, (cmp_rows, sel_rows, win_rows, gla_state)


def _layer_sample(x, lw, tabs, layer, batch, cache_cmp, cache_sel, cache_win, s0, page_table):
    R = SAMPLE_ROWS
    m = batch * R
    proj = norm_matmul(x, lw['n0'], lw['w_in'], layer, m, 1024, F32)
    o_a, gla_state = gla(proj, lw['wa2'], lw['ba'], lw['gn'], s0, layer, batch, R, R, 1, 4)
    q_hm, cmp_rows, sel_rows, win_rows = nsa_prep(proj, tabs, m, with_q=True)
    a, b = chunk_sums_paged(cache_cmp, layer, page_table, lw['pe'], lw['mix'])
    kvc = compressed_kv(a, b, batch, lw['w1'], lw['w2'], 1024)
    n_slc = -(-(PAST_LEN + 4) // SEL_LEN)
    o_cmp, selm = sample_cmp_select(q_hm, kvc, batch, n_slc)
    o_b = sample_attn(q_hm, selm, cache_sel, layer, page_table, sel_rows, cache_win, win_rows, proj, o_cmp)
    x = merge_wo(x, proj, o_a, o_b, lw['n1'], lw['w_o'], layer, m)
    x = mlp(x, lw['n2'], lw['n3'], lw['w_up'], lw['w_down'], layer, m, 1024)
    return x, (cmp_rows, sel_rows, win_rows, gla_state)


def kernel(x_prompt, x_sample, cache_cmp_kv, cache_sel_kv, cache_win_kv, state_gla, page_table,
           w_in, gla_wa2, gla_ba, gla_norm, cmp_pe, cmp_mix, cmp_w1, cmp_w2, w_o, norms, w_up, w_down):
    bp, T, d = x_prompt.shape
    bs, ss, _ = x_sample.shape
    R = SAMPLE_ROWS
    depth = w_in.shape[0]
    n_pool = cache_cmp_kv.shape[1]
    wb = cache_win_kv.shape[2]
    assert wb == WINDOW and T % 512 == 0 and ss <= R

    w_in_p = _pack_w_in(w_in)
    w_o_b, w_up_b, w_down_b = w_o.astype(BF16), w_up.astype(BF16), w_down.astype(BF16)
    cache_cmp = cache_cmp_kv.reshape(depth, n_pool, PAGE_SIZE * KV_PARTS, NSA_DH)
    cache_sel = cache_sel_kv.reshape(depth, n_pool, PAGE_SIZE * KV_PARTS, NSA_DH)
    cache_win = cache_win_kv.reshape(depth, bs, wb * KV_PARTS, NSA_DH)

    tabs_p = rope_tables(jnp.arange(T, dtype=jnp.int32))
    tabs_s = tuple(jnp.tile(t, (bs, 1)) for t in rope_tables(PAST_LEN + jnp.arange(R, dtype=jnp.int32)))

    y_p = x_prompt.reshape(bp * T, d)
    y_s = jnp.pad(x_sample, ((0, 0), (0, R - ss), (0, 0))).reshape(bs * R, d)
    st_p, st_s = [], []
    for layer in range(depth):
        lw = {'w_in': w_in_p, 'wa2': gla_wa2[layer], 'ba': gla_ba[layer][None, :],
              'gn': gla_norm[layer][None, :], 'pe': cmp_pe[layer], 'mix': cmp_mix[layer],
              'w1': cmp_w1[layer], 'w2': cmp_w2[layer], 'w_o': w_o_b,
              'n0': norms[layer, 0][None, :], 'n1': norms[layer, 1][None, :],
              'n2': norms[layer, 2][None, :], 'n3': norms[layer, 3][None, :],
              'w_up': w_up_b, 'w_down': w_down_b}
        y_p, sp = _layer_prompt(y_p, lw, tabs_p, layer, bp, T)
        y_s, s_s = _layer_sample(y_s, lw, tabs_s, layer, bs, cache_cmp, cache_sel, cache_win,
                                 state_gla, page_table)
        st_p.append(sp)
        st_s.append(s_s)

    kv_shape = (2, NSA_KV_HEADS, NSA_DH)

    def rows_p(i):
        return jnp.stack([s[i].reshape(bp, T, *kv_shape) for s in st_p])

    def rows_s(i):
        return jnp.stack([s[i].reshape(bs, R, *kv_shape)[:, :ss] for s in st_s])

    win_p = rows_p(2)[:, :, T - min(WINDOW, T):]
    win_s = jnp.concatenate([cache_win_kv[:, :, ss:], rows_s(2)], axis=2)
    return (y_p.reshape(bp, T, d), y_s.reshape(bs, R, d)[:, :ss],
            rows_p(0), rows_s(0), rows_p(1), rows_s(1), win_p, win_s,
            jnp.stack([s[3] for s in st_p]), jnp.stack([s[3] for s in st_s]))
```

```python
import functools

import jax
import jax.numpy as jnp
import numpy as np
from jax import lax
from jax.experimental import pallas as pl
from jax.experimental.pallas import tpu as pltpu

F32 = jnp.float32
BF16 = jnp.bfloat16

D_MODEL = 2048
DEPTH = 4
PAST_LEN = 16384
PAGE_SIZE = 128

GLA_HEADS = 4
GLA_DK = 256
GLA_DV = 512
GLA_GATE_RANK = 16
GLA_TAU = 16.0
GLA_CHUNK = 64
GLA_CHUNK_SHIFT = 6

NSA_HEADS = 16
NSA_KV_HEADS = 2
NSA_DH = 128
NSA_GROUP = NSA_HEADS // NSA_KV_HEADS
CMP_LEN = 32
CMP_STRIDE = 16
CMP_HIDDEN = 256
SEL_LEN = 64
SEL_SHIFT = 6
SEL_TOPN = 16
WINDOW = 512
WIN_QBLK = 128

ROPE_THETA = 500000.0
ROPE_DIM = 32
ROPE_HALF = 16
MLP_HIDDEN = 4 * D_MODEL
NORM_EPS = 1e-6
NEG_INF = -1e30
FORCED_SCORE = 1e6
KV_PARTS = 2 * NSA_KV_HEADS
KV_ROW = KV_PARTS * NSA_DH
PAGES_PER_STEP = 16

VMEM_LIMIT_BYTES = 56 * 1024 * 1024
LANES = 128

COL_MG = 0
COL_GV = 4096
COL_GG = 6144
COL_NQ = 8192
COL_GQ = 10240
COL_GK = 11264
COL_NKV = 12288
COL_SM = 13824
N_PROJ = 14336
SM_GATE0 = GLA_GATE_RANK
SM_USED = GLA_GATE_RANK + 3 * NSA_HEADS

_SRC_SEGMENTS = ((9792, 13888), (2048, 4096), (4096, 6144), (6160, 8208), (0, 1024), (1024, 2048),
                 (8208, 9744), (6144, 6160), (9744, 9792))

SAMPLE_ROWS = 8


def _cparams(sem):
    return pltpu.CompilerParams(dimension_semantics=sem, vmem_limit_bytes=VMEM_LIMIT_BYTES)


def _rms(x, g):
    return x * lax.rsqrt(jnp.mean(x * x, axis=-1, keepdims=True) + NORM_EPS) * g


def _dot(a, b, precision=None):
    return jnp.dot(a, b, preferred_element_type=F32, precision=precision)


def _dot_nt(a, b, precision=None):
    return lax.dot_general(a, b, (((1,), (1,)), ((), ())), preferred_element_type=F32, precision=precision)


def _dot_tn(a, b, precision=None):
    return lax.dot_general(a, b, (((0,), (0,)), ((), ())), preferred_element_type=F32, precision=precision)


def _split_bf16(x):
    hi = x.astype(BF16)
    return hi, (x - hi.astype(F32)).astype(BF16)


def _dot_nt_3x(a, b):
    ah, al = _split_bf16(a)
    bh, bl = _split_bf16(b)
    return (_dot_nt(jnp.concatenate([ah, al], axis=1), jnp.concatenate([bh, bh], axis=1))
            + _dot_nt(ah, bl))


def _dot_3x(a, b):
    ah, al = _split_bf16(a)
    bh, bl = _split_bf16(b)
    return (_dot(jnp.concatenate([ah, al], axis=1), jnp.concatenate([bh, bh], axis=0))
            + _dot(ah, bl))


def _iota(shape, dim):
    return lax.broadcasted_iota(jnp.int32, shape, dim)


def _masked_softmax(s, valid):
    s = jnp.where(valid, s, NEG_INF)
    m = jnp.max(s, axis=-1, keepdims=True)
    p = jnp.where(valid, jnp.exp(s - m), 0.0)
    return p / jnp.maximum(jnp.sum(p, axis=-1, keepdims=True), 1e-30)


def _norm_matmul_kernel(x_ref, g_ref, w_ref, o_ref, h_ref):
    @pl.when(pl.program_id(1) == 0)
    def _():
        h_ref[...] = _rms(x_ref[...], g_ref[...]).astype(BF16)

    o_ref[...] = _dot(h_ref[...], w_ref[...]).astype(o_ref.dtype)


def norm_matmul(x, g, w, layer, tm, tn, out_dtype):
    m, k = x.shape
    n = w.shape[2]
    return pl.pallas_call(
        _norm_matmul_kernel,
        grid=(m // tm, n // tn),
        in_specs=[pl.BlockSpec((tm, k), lambda i, j: (i, 0)),
                  pl.BlockSpec((1, k), lambda i, j: (0, 0)),
                  pl.BlockSpec((None, k, tn), lambda i, j: (layer, 0, j))],
        out_specs=pl.BlockSpec((tm, tn), lambda i, j: (i, j)),
        out_shape=jax.ShapeDtypeStruct((m, n), out_dtype),
        scratch_shapes=[pltpu.VMEM((tm, k), BF16)],
        compiler_params=_cparams(("parallel", "arbitrary")),
        name="norm_matmul",
    )(x, g, w)


def _merge_wo_kernel(x_ref, mg0_ref, mg1_ref, oa_ref, ob_ref, g_ref, wo_ref, o_ref):
    ob = jnp.concatenate([ob_ref[h].astype(F32) for h in range(NSA_HEADS)], axis=1)
    a = (jax.nn.sigmoid(mg0_ref[...].astype(F32)) * oa_ref[...].astype(F32)
         + jax.nn.sigmoid(mg1_ref[...].astype(F32)) * ob)
    m = _dot(a.astype(BF16), wo_ref[...])
    o_ref[...] = x_ref[...] + _rms(m, g_ref[...])


def merge_wo(x, proj, o_a, o_b_hm, g, wo, layer, tm):
    m, d = x.shape
    return pl.pallas_call(
        _merge_wo_kernel,
        grid=(m // tm,),
        in_specs=[pl.BlockSpec((tm, d), lambda i: (i, 0)),
                  pl.BlockSpec((tm, d), lambda i: (i, COL_MG // D_MODEL)),
                  pl.BlockSpec((tm, d), lambda i: (i, COL_MG // D_MODEL + 1)),
                  pl.BlockSpec((tm, d), lambda i: (i, 0)),
                  pl.BlockSpec((NSA_HEADS, tm, NSA_DH), lambda i: (0, i, 0)),
                  pl.BlockSpec((1, d), lambda i: (0, 0)),
                  pl.BlockSpec((None, d, d), lambda i: (layer, 0, 0))],
        out_specs=pl.BlockSpec((tm, d), lambda i: (i, 0)),
        out_shape=jax.ShapeDtypeStruct((m, d), F32),
        compiler_params=_cparams(("parallel",)),
        name="merge_wo",
    )(x, proj, proj, o_a, o_b_hm, g, wo)


def _mlp_kernel(x_ref, g2_ref, g3_ref, wu_ref, wd_ref, o_ref, h_ref, acc_ref):
    j = pl.program_id(1)

    @pl.when(j == 0)
    def _():
        h_ref[...] = _rms(x_ref[...], g2_ref[...]).astype(BF16)
        acc_ref[...] = jnp.zeros_like(acc_ref)

    u = _dot(h_ref[...], wu_ref[...])
    u = jnp.square(jnp.maximum(u, 0.0)).astype(BF16)
    acc_ref[...] += _dot(u, wd_ref[...])

    @pl.when(j == pl.num_programs(1) - 1)
    def _():
        o_ref[...] = x_ref[...] + _rms(acc_ref[...], g3_ref[...])


def mlp(x, g2, g3, wu, wd, layer, tm, th):
    m, d = x.shape
    hid = wu.shape[2]
    return pl.pallas_call(
        _mlp_kernel,
        grid=(m // tm, hid // th),
        in_specs=[pl.BlockSpec((tm, d), lambda i, j: (i, 0)),
                  pl.BlockSpec((1, d), lambda i, j: (0, 0)),
                  pl.BlockSpec((1, d), lambda i, j: (0, 0)),
                  pl.BlockSpec((None, d, th), lambda i, j: (layer, 0, j)),
                  pl.BlockSpec((None, th, d), lambda i, j: (layer, j, 0))],
        out_specs=pl.BlockSpec((tm, d), lambda i, j: (i, 0)),
        out_shape=jax.ShapeDtypeStruct((m, d), F32),
        scratch_shapes=[pltpu.VMEM((tm, d), BF16), pltpu.VMEM((tm, d), F32)],
        compiler_params=_cparams(("parallel", "arbitrary")),
        name="mlp",
    )(x, g2, g3, wu, wd)


def _gla_kernel(*refs, rows_in, n_inner, n_valid, has_s0):
    q_ref, k_ref, v_ref, gg_ref, sm_ref, wa2_ref, ba_ref, gn_ref, s0_ref = refs[:9]
    o_ref, sout_ref, st_ref = refs[-3:]
    c = pl.program_id(1)
    C = GLA_CHUNK

    @pl.when(c == 0)
    def _():
        for h in range(GLA_HEADS):
            if has_s0:
                st_ref[h] = s0_ref[h].T
            else:
                st_ref[h] = jnp.zeros((GLA_DV, GLA_DK), F32)

    rp = n_inner * C

    def load(ref):
        x = ref[...].astype(F32)
        if rows_in == C:
            return x
        assert n_inner == 1
        return jnp.concatenate([x, jnp.zeros((C - rows_in, x.shape[1]), x.dtype)], axis=0)

    row = _iota((rp, 1), 0)
    col = _iota((1, rp), 1)
    same_chunk = (row >> GLA_CHUNK_SHIFT) == (col >> GLA_CHUNK_SHIFT)
    causal = same_chunk & (row >= col)
    tril = causal.astype(BF16)
    live = (row & (C - 1)) < n_valid

    ga_h, ga_l = _split_bf16(load(sm_ref)[:, :GLA_GATE_RANK])
    w_h, w_l = _split_bf16(wa2_ref[...])
    z = _dot(ga_h, w_h) + _dot(ga_l, w_h) + _dot(ga_h, w_l) + ba_ref[...]
    log_a = (jnp.minimum(z, 0.0) - jnp.log1p(jnp.exp(-jnp.abs(z)))) / GLA_TAU
    k = load(k_ref)
    if n_valid < C:
        log_a = jnp.where(live, log_a, 0.0)
        k = jnp.where(live, k, 0.0)
    la_h, la_l = _split_bf16(log_a)
    b = _dot(tril, la_h) + _dot(tril, la_l)
    b_last = [b[ci * C + C - 1:ci * C + C, :] for ci in range(n_inner)]
    q_dec = (load(q_ref) * (GLA_DK ** -0.5) * jnp.exp(b)).astype(BF16)
    k_dec = (k * jnp.exp(-b)).astype(BF16)
    k_end = jnp.concatenate([k[ci * C:(ci + 1) * C] * jnp.exp(b_last[ci] - b[ci * C:(ci + 1) * C])
                             for ci in range(n_inner)], axis=0).astype(BF16)
    decay = [jnp.exp(r) for r in b_last]
    vb = load(v_ref).astype(BF16)

    for h in range(GLA_HEADS):
        kq = slice(h * GLA_DK, (h + 1) * GLA_DK)
        kv = slice(h * GLA_DV, (h + 1) * GLA_DV)
        att = jnp.where(causal, _dot_nt(q_dec[:, kq], k_dec[:, kq]), 0.0)
        intra = _dot(att.astype(BF16), vb[:, kv])
        for ci in range(n_inner):
            cr = slice(ci * C, (ci + 1) * C)
            st = st_ref[h]
            o = intra[cr] + _dot_nt(q_dec[cr, kq], st.astype(BF16))
            st_ref[h] = st * decay[ci][:, kq] + _dot_tn(vb[cr, kv], k_end[cr, kq])
            o_n = _rms(o, gn_ref[...])
            rs = pl.ds(ci * rows_in, rows_in)
            gg = gg_ref[rs, kv].astype(F32)
            o_ref[rs, kv] = (o_n[:rows_in] * (gg * jax.nn.sigmoid(gg))).astype(o_ref.dtype)

    @pl.when(c == pl.num_programs(1) - 1)
    def _():
        for h in range(GLA_HEADS):
            sout_ref[h] = st_ref[h].T


def gla(proj, wa2, ba, gn, s0, layer, depth, stacked, batch, rows_per_batch, rows_in, n_inner, n_valid):
    m = proj.shape[0]
    r = rows_in * n_inner
    n_steps = rows_per_batch // r
    has_s0 = s0 is not None
    if s0 is None:
        s0 = jnp.zeros((1, 1, GLA_HEADS, GLA_DK, GLA_DV), F32)
        s0_map = lambda b, c: (0, 0, 0, 0, 0)
    else:
        s0_map = lambda b, c: (layer, b, 0, 0, 0)
    wk, wv = GLA_HEADS * GLA_DK, GLA_HEADS * GLA_DV
    rowmap = lambda off: (lambda b, c: (b * n_steps + c, off))
    state_blk = (None, GLA_HEADS, GLA_DK, GLA_DV)
    kern = functools.partial(_gla_kernel, rows_in=rows_in, n_inner=n_inner, n_valid=n_valid, has_s0=has_s0)
    in_specs = [pl.BlockSpec((r, wk), rowmap(COL_GQ // wk)),
                pl.BlockSpec((r, wk), rowmap(COL_GK // wk)),
                pl.BlockSpec((r, wv), rowmap(COL_GV // wv)),
                pl.BlockSpec((r, wv), rowmap(COL_GG // wv)),
                pl.BlockSpec((r, LANES), rowmap(COL_SM // LANES)),
                pl.BlockSpec((GLA_GATE_RANK, wk), lambda b, c: (0, 0)),
                pl.BlockSpec((1, wk), lambda b, c: (0, 0)),
                pl.BlockSpec((1, GLA_DV), lambda b, c: (0, 0)),
                pl.BlockSpec((None,) + state_blk, s0_map)]
    args = (proj, proj, proj, proj, proj, wa2, ba, gn, s0)
    aliases = {}
    if stacked is not None:
        in_specs.append(pl.BlockSpec(memory_space=pl.ANY))
        aliases = {len(args): 1}
        args += (stacked,)
    return pl.pallas_call(
        kern,
        grid=(batch, n_steps),
        in_specs=in_specs,
        out_specs=[pl.BlockSpec((r, wv), rowmap(0)),
                   pl.BlockSpec((None,) + state_blk, lambda b, c: (layer, b, 0, 0, 0))],
        out_shape=[jax.ShapeDtypeStruct((m, D_MODEL), proj.dtype),
                   jax.ShapeDtypeStruct((depth, batch, GLA_HEADS, GLA_DK, GLA_DV), F32)],
        input_output_aliases=aliases,
        scratch_shapes=[pltpu.VMEM((GLA_HEADS, GLA_DV, GLA_DK), F32)],
        compiler_params=_cparams(("parallel", "arbitrary")),
        name="gla",
    )(*args)


def _rope(x, c, sa, sb):
    return x * c + pltpu.roll(x, LANES - ROPE_HALF, 1) * sa + pltpu.roll(x, ROPE_HALF, 1) * sb


def _nsa_prep_kernel(*refs, with_q):
    if with_q:
        nq_ref, nkv_ref, c_ref, sa_ref, sb_ref, q_ref, cmp_ref, sel_ref, win_ref = refs
    else:
        nkv_ref, c_ref, sa_ref, sb_ref, cmp_ref, sel_ref, win_ref = refs
    c, sa, sb = c_ref[...], sa_ref[...], sb_ref[...]
    if with_q:
        for h in range(NSA_HEADS):
            q_ref[h] = _rope(nq_ref[:, h * NSA_DH:(h + 1) * NSA_DH].astype(F32), c, sa, sb)
    for s, out in enumerate((cmp_ref, sel_ref, win_ref)):
        base = s * KV_ROW
        for hh in range(NSA_KV_HEADS):
            lo = hh * NSA_DH
            out[:, lo:lo + NSA_DH] = _rope(nkv_ref[:, base + lo:base + lo + NSA_DH].astype(F32), c, sa, sb)
        half = NSA_KV_HEADS * NSA_DH
        out[:, half:] = nkv_ref[:, base + half:base + KV_ROW].astype(F32)


def nsa_prep(proj, tabs, tr, with_q):
    m = proj.shape[0]
    n_tab = tabs[0].shape[0] // tr
    tab_spec = pl.BlockSpec((tr, LANES), lambda i: (i % n_tab, 0))
    nkv_w = 3 * KV_ROW
    row_spec = pl.BlockSpec((tr, KV_ROW), lambda i: (i, 0))
    row_shape = jax.ShapeDtypeStruct((m, KV_ROW), F32)
    in_specs = [pl.BlockSpec((tr, nkv_w), lambda i: (i, COL_NKV // nkv_w)), tab_spec, tab_spec, tab_spec]
    out_specs, out_shape, args = [row_spec] * 3, [row_shape] * 3, (proj,) + tuple(tabs)
    if with_q:
        in_specs = [pl.BlockSpec((tr, D_MODEL), lambda i: (i, COL_NQ // D_MODEL))] + in_specs
        out_specs = [pl.BlockSpec((NSA_HEADS, tr, NSA_DH), lambda i: (0, i, 0))] + out_specs
        out_shape = [jax.ShapeDtypeStruct((NSA_HEADS, m, NSA_DH), F32)] + out_shape
        args = (proj,) + args
    return pl.pallas_call(
        functools.partial(_nsa_prep_kernel, with_q=with_q),
        grid=(m // tr,),
        in_specs=in_specs,
        out_specs=out_specs,
        out_shape=out_shape,
        compiler_params=_cparams(("parallel",)),
        name="nsa_prep",
    )(*args)


def _nsa_prep_prompt_kernel(*refs):
    nkv_ref, c_ref, sa_ref, sb_ref = refs[:4]
    slabs, stacked = refs[-6:-3], refs[-3:]
    c, sa, sb = c_ref[...], sa_ref[...], sb_ref[...]
    tr = nkv_ref.shape[0]
    for s in range(3):
        for part in range(KV_PARTS):
            lo = part * NSA_DH
            v = nkv_ref[:, s * KV_ROW + lo:s * KV_ROW + lo + NSA_DH].astype(F32)
            if part < NSA_KV_HEADS:
                v = _rope(v, c, sa, sb)
            slabs[s][:, lo:lo + NSA_DH] = v
            stacked[s][pl.ds(part, tr, stride=KV_PARTS), :] = v


def nsa_prep_prompt(proj, tabs, tr, layer, depth, stacked):
    m = proj.shape[0]
    n_tab = tabs[0].shape[0] // tr
    tab_spec = pl.BlockSpec((tr, LANES), lambda i: (i % n_tab, 0))
    nkv_w = 3 * KV_ROW
    row_spec = pl.BlockSpec((tr, KV_ROW), lambda i: (i, 0))
    in_specs = [pl.BlockSpec((tr, nkv_w), lambda i: (i, COL_NKV // nkv_w)), tab_spec, tab_spec, tab_spec]
    args = (proj,) + tuple(tabs)
    aliases = {}
    if stacked is not None:
        in_specs += [pl.BlockSpec(memory_space=pl.ANY)] * 3
        aliases = {len(args) + k: 3 + k for k in range(3)}
        args += tuple(stacked)
    st_spec = pl.BlockSpec((None, tr * KV_PARTS, NSA_DH), lambda i: (layer, i, 0))
    out = pl.pallas_call(
        _nsa_prep_prompt_kernel,
        grid=(m // tr,),
        in_specs=in_specs,
        out_specs=[row_spec] * 3 + [st_spec] * 3,
        out_shape=[jax.ShapeDtypeStruct((m, KV_ROW), F32)] * 3
                  + [jax.ShapeDtypeStruct((depth, m * KV_PARTS, NSA_DH), F32)] * 3,
        input_output_aliases=aliases,
        compiler_params=_cparams(("parallel",)),
        name="nsa_prep_prompt",
    )(*args)
    return out[:3], out[3:]


def rope_tables(pos):
    inv = 1.0 / (ROPE_THETA ** (jnp.arange(ROPE_HALF, dtype=F32) / ROPE_HALF))
    ang = pos.astype(F32)[:, None] * inv[None, :]
    cos, sin = jnp.cos(ang), jnp.sin(ang)
    n = pos.shape[0]
    rest = LANES - ROPE_DIM
    c = jnp.concatenate([cos, cos, jnp.ones((n, rest), F32)], axis=1)
    sa = jnp.concatenate([-sin, jnp.zeros((n, LANES - ROPE_HALF), F32)], axis=1)
    sb = jnp.concatenate([jnp.zeros((n, ROPE_HALF), F32), sin, jnp.zeros((n, rest), F32)], axis=1)
    return c, sa, sb


_HALVES = (slice(0, CMP_STRIDE), slice(CMP_STRIDE, CMP_LEN))


def _pe_mix_sums(pe_ref, mix_ref):
    return [[jnp.sum(pe_ref[kv][h] * mix_ref[kv][h], axis=0, keepdims=True) for h in _HALVES] for kv in range(2)]


def _half_sums(xs, mix, consts):
    return [jnp.sum(xs * mix[h], axis=1) + c for h, c in zip(_HALVES, consts)]


def _chunk_sums_kernel(x_ref, pe_ref, mix_ref, a_ref, b_ref):
    nch = x_ref.shape[0] // CMP_STRIDE
    consts = _pe_mix_sums(pe_ref, mix_ref)
    for kv in range(2):
        for hh in range(NSA_KV_HEADS):
            lo = (kv * NSA_KV_HEADS + hh) * NSA_DH
            xs = x_ref[:, lo:lo + NSA_DH].reshape(nch, CMP_STRIDE, NSA_DH)
            a_ref[:, lo:lo + NSA_DH], b_ref[:, lo:lo + NSA_DH] = _half_sums(xs, mix_ref[kv], consts[kv])


def _page_specs(layer):
    def spec(i):
        return pl.BlockSpec((None, None, PAGE_SIZE * KV_PARTS, NSA_DH),
                            lambda b, s, pt: (layer, pt[b, s * PAGES_PER_STEP + i], 0, 0))
    return [spec(i) for i in range(PAGES_PER_STEP)]


def _page_part(page_ref, part):
    return page_ref[pl.ds(part, PAGE_SIZE, stride=KV_PARTS), :]


def _chunk_sums_paged_kernel(pt_ref, *refs):
    page_refs, (pe_ref, mix_ref, a_ref, b_ref) = refs[:PAGES_PER_STEP], refs[PAGES_PER_STEP:]
    nch = PAGE_SIZE // CMP_STRIDE
    consts = _pe_mix_sums(pe_ref, mix_ref)
    for i, page_ref in enumerate(page_refs):
        rows = slice(i * nch, (i + 1) * nch)
        for kv in range(2):
            for hh in range(NSA_KV_HEADS):
                part = kv * NSA_KV_HEADS + hh
                lo = part * NSA_DH
                xs = _page_part(page_ref, part).reshape(nch, CMP_STRIDE, NSA_DH)
                a_ref[rows, lo:lo + NSA_DH], b_ref[rows, lo:lo + NSA_DH] = _half_sums(xs, mix_ref[kv], consts[kv])


def chunk_sums(rows, pe, mix, tr):
    m = rows.shape[0]
    nch = tr // CMP_STRIDE
    full = lambda i: (0, 0, 0)
    return pl.pallas_call(
        _chunk_sums_kernel,
        grid=(m // tr,),
        in_specs=[pl.BlockSpec((tr, KV_ROW), lambda i: (i, 0)),
                  pl.BlockSpec((2, CMP_LEN, NSA_DH), full),
                  pl.BlockSpec((2, CMP_LEN, NSA_DH), full)],
        out_specs=[pl.BlockSpec((nch, KV_ROW), lambda i: (i, 0))] * 2,
        out_shape=[jax.ShapeDtypeStruct((m // CMP_STRIDE, KV_ROW), F32)] * 2,
        compiler_params=_cparams(("parallel",)),
        name="chunk_sums",
    )(rows, pe, mix)


def chunk_sums_paged(cache, layer, page_table, pe, mix):
    batch, n_pages = page_table.shape
    nch = PAGES_PER_STEP * PAGE_SIZE // CMP_STRIDE
    n_steps = n_pages // PAGES_PER_STEP
    full = lambda b, s, pt: (0, 0, 0)
    gs = pltpu.PrefetchScalarGridSpec(
        num_scalar_prefetch=1,
        grid=(batch, n_steps),
        in_specs=_page_specs(layer) + [pl.BlockSpec((2, CMP_LEN, NSA_DH), full),
                                       pl.BlockSpec((2, CMP_LEN, NSA_DH), full)],
        out_specs=[pl.BlockSpec((nch, KV_ROW), lambda b, s, pt: (b * n_steps + s, 0))] * 2,
    )
    return pl.pallas_call(
        _chunk_sums_paged_kernel,
        grid_spec=gs,
        out_shape=[jax.ShapeDtypeStruct((batch * n_steps * nch, KV_ROW), F32)] * 2,
        compiler_params=_cparams(("parallel", "arbitrary")),
        name="chunk_sums_paged",
    )(page_table, *([cache] * PAGES_PER_STEP), pe, mix)


def _cmp_mlp_kernel(a_ref, b_ref, w1_ref, w2_ref, o_ref):
    h = a_ref[...] + b_ref[...]
    for kv in range(2):
        for hh in range(NSA_KV_HEADS):
            lo = (kv * NSA_KV_HEADS + hh) * NSA_DH
            y = jax.nn.gelu(_dot_3x(h[:, lo:lo + NSA_DH], w1_ref[kv]))
            o_ref[:, lo:lo + NSA_DH] = _dot_3x(y, w2_ref[kv])


def cmp_mlp(a, b_shift, w1, w2, tr):
    m = a.shape[0]
    return pl.pallas_call(
        _cmp_mlp_kernel,
        grid=(m // tr,),
        in_specs=[pl.BlockSpec((tr, KV_ROW), lambda i: (i, 0)),
                  pl.BlockSpec((tr, KV_ROW), lambda i: (i, 0)),
                  pl.BlockSpec((2, NSA_DH, CMP_HIDDEN), lambda i: (0, 0, 0)),
                  pl.BlockSpec((2, CMP_HIDDEN, NSA_DH), lambda i: (0, 0, 0))],
        out_specs=pl.BlockSpec((tr, KV_ROW), lambda i: (i, 0)),
        out_shape=jax.ShapeDtypeStruct((m, KV_ROW), F32),
        compiler_params=_cparams(("parallel",)),
        name="cmp_mlp",
    )(a, b_shift, w1, w2)


def compressed_kv(a, b, batch, w1, w2, tr):
    nch = a.shape[0] // batch
    b3 = b.reshape(batch, nch, KV_ROW)
    b_shift = jnp.concatenate([b3[:, 1:], jnp.zeros((batch, 1, KV_ROW), F32)], axis=1).reshape(batch * nch, KV_ROW)
    return cmp_mlp(a, b_shift, w1, w2, tr)


def _select_topn(imp, n_top):
    j = _iota(imp.shape, 1)
    big = jnp.int32(imp.shape[1])
    sel = jnp.zeros(imp.shape, F32)
    for _ in range(n_top):
        m = jnp.max(imp, axis=-1, keepdims=True)
        idx = jnp.min(jnp.where(imp == m, j, big), axis=-1, keepdims=True)
        hit = j == idx
        sel = jnp.where(hit, 1.0, sel)
        imp = jnp.where(hit, NEG_INF, imp)
    return sel


def _importance(p_grp, qpos, n_cmp_pad, n_slc, width):
    n_r = _iota((n_cmp_pad, 1), 0) * CMP_STRIDE
    j_c = _iota((1, width), 1)
    overlap = ((n_r < (j_c + 1) * SEL_LEN) & (n_r + CMP_LEN > j_c * SEL_LEN)).astype(BF16)
    p_hi, p_lo = _split_bf16(p_grp)
    imp = _dot(p_hi, overlap) + _dot(p_lo, overlap)
    cur = qpos >> SEL_SHIFT
    forced = (j_c == 0) | (j_c == cur) | (j_c == cur - 1)
    imp = jnp.where(forced, FORCED_SCORE, imp)
    imp = jnp.where(j_c > cur, -1.0, imp)
    return jnp.where(j_c >= n_slc, -2.0, imp)


LOG2E = 1.4426950408889634


def _nsa_prompt_kernel(nq_ref, c_ref, sa_ref, sb_ref, kc_ref, vc_ref, ks_ref, vs_ref, kw_ref, vw_ref, sm_ref, o_ref,
                       qf_ref, qb_ref, s_ref, p_ref, m_ref, l_ref, a_ref, acc_ref, ob_ref, *, Q, T, TK):
    G = NSA_GROUP
    hk = pl.program_id(1)
    qi = pl.program_id(2)
    q0 = qi * Q
    scale = NSA_DH ** -0.5
    n_cmp_pad = kc_ref.shape[0]
    n_slc = T // SEL_LEN
    qpos = q0 + _iota((Q, 1), 0)
    c, sa, sb = c_ref[...], sa_ref[...], sb_ref[...]
    for g in range(G):
        qg = _rope(nq_ref[:, g * NSA_DH:(g + 1) * NSA_DH].astype(F32), c, sa, sb)
        qf_ref[g * Q:(g + 1) * Q, :] = qg
        qb_ref[g * Q:(g + 1) * Q, :] = (qg * (scale * LOG2E)).astype(BF16)
    qs = qf_ref[...]

    ng = jax.nn.sigmoid(sm_ref[...].astype(F32))

    def gate(g, c3):
        lane0 = SM_GATE0 + 3 * g + c3
        lane1 = lane0 + 3 * G
        return jnp.where(hk == 0, ng[:, lane0:lane0 + 1], ng[:, lane1:lane1 + 1])

    def softmax_tile(kb, vb, bias, width):
        s_ref[:, :width] = _dot_nt(qb_ref[...], kb)
        for g in range(G):
            rows = pl.ds(g * Q, Q)
            s = s_ref[rows, :width] + bias
            m_old = m_ref[rows, :]
            m_new = jnp.maximum(m_old, jnp.max(s, axis=-1, keepdims=True))
            alpha = jnp.exp2(m_old - m_new)
            p = jnp.exp2(s - jnp.tile(m_new, (1, width // LANES)))
            l_ref[rows, :] = alpha * l_ref[rows, :] + jnp.sum(p, axis=-1, keepdims=True)
            a_ref[rows, :] = alpha
            m_ref[rows, :] = m_new
            p_ref[rows, :width] = p.astype(BF16)
        return _dot(p_ref[:, :width], vb)

    def reset():
        m_ref[...] = jnp.full_like(m_ref, NEG_INF)
        l_ref[...] = jnp.zeros_like(l_ref)

    cmp_valid = _iota((1, n_cmp_pad), 1) * CMP_STRIDE + (CMP_LEN - 1) <= qpos
    p_grp = jnp.zeros((Q, n_cmp_pad), F32)
    s_cmp = _dot_nt_3x(qs, kc_ref[...]) * scale
    vcb = vc_ref[...].astype(BF16)
    for g in range(G):
        p = _masked_softmax(s_cmp[g * Q:(g + 1) * Q], cmp_valid)
        p_grp = p_grp + p
        ob_ref[g] = gate(g, 0) * _dot(p.astype(BF16), vcb)

    ks0 = pl.multiple_of(jnp.maximum(q0 - WINDOW, 0), LANES)
    wlen = WINDOW + Q
    wpos = ks0 + _iota((1, wlen), 1)
    win_bias = jnp.where((wpos <= qpos) & (wpos > qpos - WINDOW), 0.0, NEG_INF)
    reset()
    pv = softmax_tile(kw_ref[pl.ds(ks0, wlen), :].astype(BF16), vw_ref[pl.ds(ks0, wlen), :].astype(BF16),
                      win_bias, wlen)
    for g in range(G):
        rows = pl.ds(g * Q, Q)
        ob_ref[g] = ob_ref[g] + gate(g, 2) * (pv[g * Q:(g + 1) * Q] / jnp.maximum(l_ref[rows, :], 1e-30))

    n_top = min(SEL_TOPN, n_slc)
    j_r = _iota((n_slc, 1), 0)
    n_c = _iota((1, n_cmp_pad), 1) * CMP_STRIDE
    overlap_t = ((n_c < (j_r + 1) * SEL_LEN) & (n_c + CMP_LEN > j_r * SEL_LEN)).astype(BF16)
    p_hi, p_lo = _split_bf16(p_grp)
    imp = _dot_nt(overlap_t, p_hi) + _dot_nt(overlap_t, p_lo)
    cur = (q0 + _iota((1, Q), 1)) >> SEL_SHIFT
    imp = jnp.where((j_r == 0) | (j_r == cur) | (j_r == cur - 1), FORCED_SCORE, imp)
    imp = jnp.where(j_r > cur, -1.0, imp)
    rank = jnp.zeros((n_slc, Q), F32)
    for jp in range(n_slc):
        other = imp[jp:jp + 1, :]
        beats = (other > imp) | ((other == imp) & (j_r > jp))
        rank = rank + jnp.where(beats, 1.0, 0.0)
    sel_t = jnp.where(rank < n_top, 1.0, 0.0).astype(BF16)

    reset()
    acc_ref[...] = jnp.zeros_like(acc_ref)

    def key_tile(kt, carry):
        k0 = pl.multiple_of(kt * TK, TK)
        kpos = k0 + _iota((1, TK), 1)
        expand = (j_r == (kpos >> SEL_SHIFT)).astype(BF16)
        keymask = _dot_tn(sel_t, expand)
        bias = jnp.where((keymask > 0.5) & (kpos <= qpos), 0.0, NEG_INF)
        pv = softmax_tile(ks_ref[pl.ds(k0, TK), :].astype(BF16), vs_ref[pl.ds(k0, TK), :].astype(BF16), bias, TK)
        acc_ref[...] = a_ref[...] * acc_ref[...] + pv
        return carry

    lax.fori_loop(0, (q0 + Q + TK - 1) // TK, key_tile, 0)

    for g in range(G):
        rows = pl.ds(g * Q, Q)
        o_sel = acc_ref[rows, :] / jnp.maximum(l_ref[rows, :], 1e-30)
        o_ref[g] = (ob_ref[g] + gate(g, 1) * o_sel).astype(o_ref.dtype)


def nsa_prompt(tabs, kvc, sel_rows, win_rows, proj, batch, T, Q):
    m = batch * T
    nq = T // Q
    n_cmp_pad = kvc.shape[0] // batch
    G = NSA_GROUP
    gw = G * NSA_DH
    kern = functools.partial(_nsa_prompt_kernel, Q=Q, T=T, TK=512)
    kcol = lambda off: (lambda b, hk, qi: (b, off + hk))
    tab_spec = pl.BlockSpec((Q, LANES), lambda b, hk, qi: (qi, 0))
    return pl.pallas_call(
        kern,
        grid=(batch, NSA_KV_HEADS, nq),
        in_specs=[pl.BlockSpec((Q, gw), lambda b, hk, qi: (b * nq + qi, COL_NQ // gw + hk)),
                  tab_spec, tab_spec, tab_spec,
                  pl.BlockSpec((n_cmp_pad, NSA_DH), kcol(0)),
                  pl.BlockSpec((n_cmp_pad, NSA_DH), kcol(NSA_KV_HEADS)),
                  pl.BlockSpec((T, NSA_DH), kcol(0)),
                  pl.BlockSpec((T, NSA_DH), kcol(NSA_KV_HEADS)),
                  pl.BlockSpec((T, NSA_DH), kcol(0)),
                  pl.BlockSpec((T, NSA_DH), kcol(NSA_KV_HEADS)),
                  pl.BlockSpec((Q, LANES), lambda b, hk, qi: (b * nq + qi, COL_SM // LANES))],
        out_specs=pl.BlockSpec((G, Q, NSA_DH), lambda b, hk, qi: (hk, b * nq + qi, 0)),
        out_shape=jax.ShapeDtypeStruct((NSA_HEADS, m, NSA_DH), proj.dtype),
        scratch_shapes=[pltpu.VMEM((G * Q, NSA_DH), F32), pltpu.VMEM((G * Q, NSA_DH), BF16),
                        pltpu.VMEM((G * Q, WINDOW + Q), F32), pltpu.VMEM((G * Q, WINDOW + Q), BF16),
                        pltpu.VMEM((G * Q, LANES), F32), pltpu.VMEM((G * Q, LANES), F32),
                        pltpu.VMEM((G * Q, LANES), F32), pltpu.VMEM((G * Q, NSA_DH), F32),
                        pltpu.VMEM((G, Q, NSA_DH), F32)],
        compiler_params=_cparams(("parallel", "parallel", "arbitrary")),
        name="nsa_prompt",
    )(proj, *tabs, kvc, kvc, sel_rows, sel_rows, win_rows, win_rows, proj)


SEL_W = 384


def _sample_cmp_select_kernel(q_ref, kvc_ref, ocmp_ref, selm_ref, *, n_slc):
    G, R = NSA_GROUP, SAMPLE_ROWS
    scale = NSA_DH ** -0.5
    n_cmp_pad = kvc_ref.shape[0]
    qpos_s = PAST_LEN + (_iota((G * R, 1), 0) & (R - 1))
    end = _iota((1, n_cmp_pad), 1) * CMP_STRIDE + (CMP_LEN - 1)
    p_grp = []
    for hk in range(NSA_KV_HEADS):
        qs = q_ref[hk * G:(hk + 1) * G].reshape(G * R, NSA_DH)
        kc = kvc_ref[:, hk * NSA_DH:(hk + 1) * NSA_DH]
        vc = kvc_ref[:, (NSA_KV_HEADS + hk) * NSA_DH:(NSA_KV_HEADS + hk + 1) * NSA_DH]
        p = _masked_softmax(_dot_nt_3x(qs, kc) * scale, end <= qpos_s)
        o_cmp = _dot(p.astype(BF16), vc.astype(BF16))
        for g in range(G):
            ocmp_ref[hk * G + g] = o_cmp[g * R:(g + 1) * R]
        p_grp.append(jnp.sum(p.reshape(G, R, n_cmp_pad), axis=0))
    qpos = PAST_LEN + (_iota((NSA_KV_HEADS * R, 1), 0) & (R - 1))
    imp = _importance(jnp.concatenate(p_grp, axis=0), qpos, n_cmp_pad, n_slc, SEL_W)
    sel = _select_topn(imp, min(SEL_TOPN, n_slc))
    for hk in range(NSA_KV_HEADS):
        selm_ref[hk] = sel[hk * R:(hk + 1) * R]


def sample_cmp_select(q_hm, kvc, batch, n_slc):
    R = SAMPLE_ROWS
    n_cmp_pad = kvc.shape[0] // batch
    kern = functools.partial(_sample_cmp_select_kernel, n_slc=n_slc)
    hm_spec = pl.BlockSpec((NSA_HEADS, R, NSA_DH), lambda b: (0, b, 0))
    return pl.pallas_call(
        kern,
        grid=(batch,),
        in_specs=[hm_spec, pl.BlockSpec((n_cmp_pad, KV_ROW), lambda b: (b, 0))],
        out_specs=[hm_spec, pl.BlockSpec((None, NSA_KV_HEADS, R, SEL_W), lambda b: (b, 0, 0, 0))],
        out_shape=[jax.ShapeDtypeStruct((NSA_HEADS, batch * R, NSA_DH), F32),
                   jax.ShapeDtypeStruct((batch, NSA_KV_HEADS, R, SEL_W), F32)],
        compiler_params=_cparams(("parallel",)),
        name="sample_cmp_select",
    )(q_hm, kvc)


def _sample_attn_kernel(pt_ref, q_ref, selm_ref, *refs, n_steps):
    page_refs = refs[:PAGES_PER_STEP]
    nsel_ref, cwin_ref, nwin_ref, sm_ref, ocmp_ref, o_ref, m_ref, l_ref, acc_ref = refs[PAGES_PER_STEP:]
    G, R = NSA_GROUP, SAMPLE_ROWS
    p = pl.program_id(1)
    scale = NSA_DH ** -0.5
    t_s = _iota((G * R, 1), 0) & (R - 1)
    n_keys = PAGES_PER_STEP * PAGE_SIZE

    @pl.when(p == 0)
    def _():
        m_ref[...] = jnp.full_like(m_ref, NEG_INF)
        l_ref[...] = jnp.zeros_like(l_ref)
        acc_ref[...] = jnp.zeros_like(acc_ref)

    def q_stack(hk, mult):
        return (q_ref[hk * G:(hk + 1) * G].reshape(G * R, NSA_DH) * mult).astype(BF16)

    def update(hk, s, pr_of, v):
        m_old = m_ref[hk]
        m_new = jnp.maximum(m_old, jnp.max(s, axis=-1, keepdims=True))
        alpha = jnp.exp2(m_old - m_new)
        pr = pr_of(jnp.exp2(s - jnp.tile(m_new, (1, s.shape[1] // LANES))))
        l_ref[hk] = alpha * l_ref[hk] + jnp.sum(pr, axis=-1, keepdims=True)
        acc_ref[hk] = alpha * acc_ref[hk] + _dot(pr.astype(BF16), v)
        m_ref[hk] = m_new

    key = _iota((1, n_keys), 1)
    blk_of_key = (n_keys // SEL_LEN) * p + (key >> SEL_SHIFT)
    expand = (_iota((SEL_W, 1), 0) == blk_of_key).astype(BF16)
    for hk in range(NSA_KV_HEADS):
        kpg = jnp.concatenate([_page_part(r, hk).astype(BF16) for r in page_refs], axis=0)
        vpg = jnp.concatenate([_page_part(r, NSA_KV_HEADS + hk).astype(BF16) for r in page_refs], axis=0)
        keymask = _dot(selm_ref[hk].astype(BF16), expand)
        bias = jnp.where(jnp.concatenate([keymask] * G, axis=0) > 0.5, 0.0, NEG_INF)
        update(hk, _dot_nt(q_stack(hk, scale * LOG2E), kpg) + bias, lambda e: e, vpg)

    def win_part(ref, part):
        return ref[pl.ds(part, WINDOW, stride=KV_PARTS), :]

    @pl.when(p == n_steps - 1)
    def _():
        ng = jax.nn.sigmoid(sm_ref[...])
        new_blk = PAST_LEN // SEL_LEN
        zpad = jnp.zeros((LANES - R, NSA_DH), F32)
        jn = _iota((1, LANES), 1)
        for hk in range(NSA_KV_HEADS):
            klo, vlo = hk * NSA_DH, (NSA_KV_HEADS + hk) * NSA_DH
            kn = jnp.concatenate([nsel_ref[:, klo:klo + NSA_DH], zpad], axis=0)
            vn = jnp.concatenate([nsel_ref[:, vlo:vlo + NSA_DH], zpad], axis=0)
            picked = jnp.concatenate([selm_ref[hk][:, new_blk:new_blk + 1]] * G, axis=0) > 0.5
            valid = picked & (jn <= t_s) & (jn < R)
            s = jnp.where(valid, _dot_nt(q_stack(hk, scale * LOG2E), kn.astype(BF16)), NEG_INF)
            update(hk, s, lambda e: jnp.where(valid, e, 0.0), vn.astype(BF16))
            o_sel = acc_ref[hk] / jnp.maximum(l_ref[hk], 1e-30)
            qs = q_stack(hk, scale)
            kw = jnp.concatenate([win_part(cwin_ref, hk), nwin_ref[:, klo:klo + NSA_DH], zpad], axis=0)
            vw = jnp.concatenate([win_part(cwin_ref, NSA_KV_HEADS + hk), nwin_ref[:, vlo:vlo + NSA_DH], zpad], axis=0)
            iw = _iota((1, WINDOW + LANES), 1)
            wvalid = ((iw < WINDOW) & (iw > t_s)) | ((iw >= WINDOW) & (iw - WINDOW <= t_s) & (iw - WINDOW < R))
            pw = _masked_softmax(_dot_nt(qs, kw.astype(BF16)), wvalid)
            o_win = _dot(pw.astype(BF16), vw.astype(BF16))
            for g in range(G):
                h = hk * G + g
                lane = SM_GATE0 + 3 * h
                rows = slice(g * R, (g + 1) * R)
                o_ref[h] = (ng[:, lane:lane + 1] * ocmp_ref[h] + ng[:, lane + 1:lane + 2] * o_sel[rows]
                            + ng[:, lane + 2:lane + 3] * o_win[rows])


def sample_attn(q_hm, selm, cache_sel, layer, page_table, new_sel, cache_win, new_win, proj, o_cmp_hm):
    batch, n_pages = page_table.shape
    n_steps = n_pages // PAGES_PER_STEP
    G, R = NSA_GROUP, SAMPLE_ROWS
    hm_spec = pl.BlockSpec((NSA_HEADS, R, NSA_DH), lambda b, p, pt: (0, b, 0))
    row_spec = pl.BlockSpec((R, KV_ROW), lambda b, p, pt: (b, 0))
    gs = pltpu.PrefetchScalarGridSpec(
        num_scalar_prefetch=1,
        grid=(batch, n_steps),
        in_specs=[hm_spec,
                  pl.BlockSpec((None, NSA_KV_HEADS, R, SEL_W), lambda b, p, pt: (b, 0, 0, 0))]
                 + _page_specs(layer)
                 + [row_spec,
                    pl.BlockSpec((None, None, WINDOW * KV_PARTS, NSA_DH), lambda b, p, pt: (layer, b, 0, 0)),
                    row_spec,
                    pl.BlockSpec((R, LANES), lambda b, p, pt: (b, COL_SM // LANES)),
                    hm_spec],
        out_specs=hm_spec,
        scratch_shapes=[pltpu.VMEM((NSA_KV_HEADS, G * R, LANES), F32),
                        pltpu.VMEM((NSA_KV_HEADS, G * R, LANES), F32),
                        pltpu.VMEM((NSA_KV_HEADS, G * R, NSA_DH), F32)],
    )
    kern = functools.partial(_sample_attn_kernel, n_steps=n_steps)
    return pl.pallas_call(
        kern,
        grid_spec=gs,
        out_shape=jax.ShapeDtypeStruct((NSA_HEADS, batch * R, NSA_DH), F32),
        compiler_params=_cparams(("parallel", "arbitrary")),
        name="sample_attn",
    )(page_table, q_hm, selm, *([cache_sel] * PAGES_PER_STEP), new_sel, cache_win, new_win, proj, o_cmp_hm)


PACK_W = 512
PACK_SRC = PACK_W // LANES + 1
_MAIN_SEGMENTS = _SRC_SEGMENTS[:7]


def _pack_table():
    first, shift = [], []
    for lo, hi in _MAIN_SEGMENTS:
        assert (hi - lo) % PACK_W == 0
        for c in range(lo, hi, PACK_W):
            first.append(c // LANES)
            shift.append(c % LANES)
    assert len(first) * PACK_W == COL_SM
    n = N_PROJ // PACK_W
    first += [0] * (n - len(first))
    shift += [0] * (n - len(shift))
    return np.array([first, shift], np.int32)


def _pack_kernel(tab_ref, *refs):
    srcs, (ga_ref, ng_ref, o_ref) = refs[:PACK_SRC], refs[PACK_SRC:]
    j = pl.program_id(1)
    n_main = COL_SM // PACK_W
    shift = tab_ref[1, j]
    lane = _iota((1, LANES), 1)

    @pl.when(j >= n_main)
    def _():
        sm = jnp.where(lane < GLA_GATE_RANK, ga_ref[...], jnp.where(lane < SM_USED, ng_ref[...], 0.0))
        o_ref[:, :LANES] = sm.astype(BF16)
        o_ref[:, LANES:] = jnp.zeros((o_ref.shape[0], PACK_W - LANES), BF16)

    for sv in sorted({lo % LANES for lo, _ in _MAIN_SEGMENTS}):
        @pl.when((j < n_main) & (shift == sv))
        def _():
            for i in range(PACK_W // LANES):
                cols = slice(i * LANES, (i + 1) * LANES)
                if sv == 0:
                    o_ref[:, cols] = srcs[i][...].astype(BF16)
                else:
                    a = pltpu.roll(srcs[i][...], LANES - sv, 1)
                    b = pltpu.roll(srcs[i + 1][...], LANES - sv, 1)
                    o_ref[:, cols] = jnp.where(lane < LANES - sv, a, b).astype(BF16)


def _pack_w_in(w_in):
    depth, d, n_in = w_in.shape
    (ga_lo, ga_hi), (ng_lo, ng_hi) = _SRC_SEGMENTS[7:]
    assert ga_lo % LANES == 0 and ga_hi - ga_lo == GLA_GATE_RANK
    assert ng_lo % LANES == GLA_GATE_RANK and ng_hi - ng_lo == SM_USED - GLA_GATE_RANK
    last_blk = (n_in - 1) // LANES

    def src_spec(i):
        return pl.BlockSpec((None, d, LANES), lambda l, j, tab: (l, 0, jnp.minimum(tab[0, j] + i, last_blk)))

    def fixed_spec(col):
        return pl.BlockSpec((None, d, LANES), lambda l, j, tab: (l, 0, col // LANES))

    gs = pltpu.PrefetchScalarGridSpec(
        num_scalar_prefetch=1,
        grid=(depth, N_PROJ // PACK_W),
        in_specs=[src_spec(i) for i in range(PACK_SRC)] + [fixed_spec(ga_lo), fixed_spec(ng_lo)],
        out_specs=pl.BlockSpec((None, d, PACK_W), lambda l, j, tab: (l, 0, j)),
    )
    return pl.pallas_call(
        _pack_kernel,
        grid_spec=gs,
        out_shape=jax.ShapeDtypeStruct((depth, d, N_PROJ), BF16),
        compiler_params=_cparams(("parallel", "arbitrary")),
        name="pack_w_in",
    )(jnp.asarray(_pack_table()), *([w_in] * (PACK_SRC + 2)))


def _layer_prompt(x, lw, tabs, layer, depth, batch, T, stacked_rows, gla_states):
    proj = norm_matmul(x, lw['n0'], lw['w_in'], layer, 1024, 2048, BF16)
    o_a, gla_states = gla(proj, lw['wa2'], lw['ba'], lw['gn'], None, layer, depth, gla_states, batch, T,
                          GLA_CHUNK, 4, GLA_CHUNK)
    (cmp_rows, sel_rows, win_rows), stacked_rows = nsa_prep_prompt(proj, tabs, 512, layer, depth, stacked_rows)
    a, b = chunk_sums(cmp_rows, lw['pe'], lw['mix'], 512)
    kvc = compressed_kv(a, b, batch, lw['w1'], lw['w2'], 128)
    o_b = nsa_prompt(tabs, kvc, sel_rows, win_rows, proj, batch, T, 256)
    x = merge_wo(x, proj, o_a, o_b, lw['n1'], lw['w_o'], layer, 256)
    x = mlp(x, lw['n2'], lw['n3'], lw['w_up'], lw['w_down'], layer, 512, 1024)
    return x, gla_states, stacked_rows


def _layer_sample(x, lw, tabs, layer, depth, batch, cache_cmp, cache_sel, cache_win, s0, page_table, gla_states):
    R = SAMPLE_ROWS
    m = batch * R
    proj = norm_matmul(x, lw['n0'], lw['w_in'], layer, m, 1024, F32)
    o_a, gla_states = gla(proj, lw['wa2'], lw['ba'], lw['gn'], s0, layer, depth, gla_states, batch, R, R, 1, 4)
    q_hm, cmp_rows, sel_rows, win_rows = nsa_prep(proj, tabs, m, with_q=True)
    a, b = chunk_sums_paged(cache_cmp, layer, page_table, lw['pe'], lw['mix'])
    kvc = compressed_kv(a, b, batch, lw['w1'], lw['w2'], 1024)
    n_slc = -(-(PAST_LEN + 4) // SEL_LEN)
    o_cmp, selm = sample_cmp_select(q_hm, kvc, batch, n_slc)
    o_b = sample_attn(q_hm, selm, cache_sel, layer, page_table, sel_rows, cache_win, win_rows, proj, o_cmp)
    x = merge_wo(x, proj, o_a, o_b, lw['n1'], lw['w_o'], layer, m)
    x = mlp(x, lw['n2'], lw['n3'], lw['w_up'], lw['w_down'], layer, m, 1024)
    return x, (cmp_rows, sel_rows, win_rows), gla_states


def kernel(x_prompt, x_sample, cache_cmp_kv, cache_sel_kv, cache_win_kv, state_gla, page_table,
           w_in, gla_wa2, gla_ba, gla_norm, cmp_pe, cmp_mix, cmp_w1, cmp_w2, w_o, norms, w_up, w_down):
    bp, T, d = x_prompt.shape
    bs, ss, _ = x_sample.shape
    R = SAMPLE_ROWS
    depth = w_in.shape[0]
    n_pool = cache_cmp_kv.shape[1]
    wb = cache_win_kv.shape[2]
    assert wb == WINDOW and T % 512 == 0 and ss <= R

    w_in_p = _pack_w_in(w_in)
    w_o_b, w_up_b, w_down_b = w_o.astype(BF16), w_up.astype(BF16), w_down.astype(BF16)
    cache_cmp = cache_cmp_kv.reshape(depth, n_pool, PAGE_SIZE * KV_PARTS, NSA_DH)
    cache_sel = cache_sel_kv.reshape(depth, n_pool, PAGE_SIZE * KV_PARTS, NSA_DH)
    cache_win = cache_win_kv.reshape(depth, bs, wb * KV_PARTS, NSA_DH)

    tabs_p = rope_tables(jnp.arange(T, dtype=jnp.int32))
    tabs_s = tuple(jnp.tile(t, (bs, 1)) for t in rope_tables(PAST_LEN + jnp.arange(R, dtype=jnp.int32)))

    y_p = x_prompt.reshape(bp * T, d)
    y_s = jnp.pad(x_sample, ((0, 0), (0, R - ss), (0, 0))).reshape(bs * R, d)
    st_s, stacked_rows, gla_p, gla_s = [], None, None, None
    for layer in range(depth):
        lw = {'w_in': w_in_p, 'wa2': gla_wa2[layer], 'ba': gla_ba[layer][None, :],
              'gn': gla_norm[layer][None, :], 'pe': cmp_pe[layer], 'mix': cmp_mix[layer],
              'w1': cmp_w1[layer], 'w2': cmp_w2[layer], 'w_o': w_o_b,
              'n0': norms[layer, 0][None, :], 'n1': norms[layer, 1][None, :],
              'n2': norms[layer, 2][None, :], 'n3': norms[layer, 3][None, :],
              'w_up': w_up_b, 'w_down': w_down_b}
        y_p, gla_p, stacked_rows = _layer_prompt(y_p, lw, tabs_p, layer, depth, bp, T, stacked_rows, gla_p)
        y_s, s_s, gla_s = _layer_sample(y_s, lw, tabs_s, layer, depth, bs, cache_cmp, cache_sel, cache_win,
                                        state_gla, page_table, gla_s)
        st_s.append(s_s)

    kv_shape = (2, NSA_KV_HEADS, NSA_DH)

    def rows_p(i):
        return stacked_rows[i].reshape(depth, bp, T, *kv_shape)

    def rows_s(i):
        return jnp.stack([s[i].reshape(bs, R, *kv_shape)[:, :ss] for s in st_s])

    win_p = rows_p(2)[:, :, T - min(WINDOW, T):]
    win_s = jnp.concatenate([cache_win_kv[:, :, ss:], rows_s(2)], axis=2)
    return (y_p.reshape(bp, T, d), y_s.reshape(bs, R, d)[:, :ss],
            rows_p(0), rows_s(0), rows_p(1), rows_s(1), win_p, win_s,
            gla_p, gla_s)
```

```python
import functools

import jax
import jax.numpy as jnp
import numpy as np
from jax import lax
from jax.experimental import pallas as pl
from jax.experimental.pallas import tpu as pltpu

F32 = jnp.float32
BF16 = jnp.bfloat16

D_MODEL = 2048
DEPTH = 4
PAST_LEN = 16384
PAGE_SIZE = 128

GLA_HEADS = 4
GLA_DK = 256
GLA_DV = 512
GLA_GATE_RANK = 16
GLA_TAU = 16.0
GLA_CHUNK = 64
GLA_CHUNK_SHIFT = 6

NSA_HEADS = 16
NSA_KV_HEADS = 2
NSA_DH = 128
NSA_GROUP = NSA_HEADS // NSA_KV_HEADS
CMP_LEN = 32
CMP_STRIDE = 16
CMP_HIDDEN = 256
SEL_LEN = 64
SEL_SHIFT = 6
SEL_TOPN = 16
WINDOW = 512
WIN_QBLK = 128

ROPE_THETA = 500000.0
ROPE_DIM = 32
ROPE_HALF = 16
MLP_HIDDEN = 4 * D_MODEL
NORM_EPS = 1e-6
NEG_INF = -1e30
FORCED_SCORE = 1e6
KV_PARTS = 2 * NSA_KV_HEADS
KV_ROW = KV_PARTS * NSA_DH
PAGES_PER_STEP = 16

VMEM_LIMIT_BYTES = 56 * 1024 * 1024
LANES = 128

COL_MG = 0
COL_GV = 4096
COL_GG = 6144
COL_NQ = 8192
COL_GQ = 10240
COL_GK = 11264
COL_NKV = 12288
COL_SM = 13824
N_PROJ = 14336
SM_GATE0 = GLA_GATE_RANK
SM_USED = GLA_GATE_RANK + 3 * NSA_HEADS

_SRC_SEGMENTS = ((9792, 13888), (2048, 4096), (4096, 6144), (6160, 8208), (0, 1024), (1024, 2048),
                 (8208, 9744), (6144, 6160), (9744, 9792))

SAMPLE_ROWS = 8


def _cparams(sem):
    return pltpu.CompilerParams(dimension_semantics=sem, vmem_limit_bytes=VMEM_LIMIT_BYTES)


def _rms(x, g):
    return x * lax.rsqrt(jnp.mean(x * x, axis=-1, keepdims=True) + NORM_EPS) * g


def _dot(a, b, precision=None):
    return jnp.dot(a, b, preferred_element_type=F32, precision=precision)


def _dot_nt(a, b, precision=None):
    return lax.dot_general(a, b, (((1,), (1,)), ((), ())), preferred_element_type=F32, precision=precision)


def _dot_tn(a, b, precision=None):
    return lax.dot_general(a, b, (((0,), (0,)), ((), ())), preferred_element_type=F32, precision=precision)


def _split_bf16(x):
    hi = x.astype(BF16)
    return hi, (x - hi.astype(F32)).astype(BF16)


def _dot_nt_3x(a, b):
    ah, al = _split_bf16(a)
    bh, bl = _split_bf16(b)
    return (_dot_nt(jnp.concatenate([ah, al], axis=1), jnp.concatenate([bh, bh], axis=1))
            + _dot_nt(ah, bl))


def _dot_3x(a, b):
    ah, al = _split_bf16(a)
    bh, bl = _split_bf16(b)
    return (_dot(jnp.concatenate([ah, al], axis=1), jnp.concatenate([bh, bh], axis=0))
            + _dot(ah, bl))


def _iota(shape, dim):
    return lax.broadcasted_iota(jnp.int32, shape, dim)


def _masked_softmax(s, valid):
    s = jnp.where(valid, s, NEG_INF)
    m = jnp.max(s, axis=-1, keepdims=True)
    p = jnp.where(valid, jnp.exp(s - m), 0.0)
    return p / jnp.maximum(jnp.sum(p, axis=-1, keepdims=True), 1e-30)


def _norm_matmul_kernel(x_ref, g_ref, w_ref, o_ref, h_ref):
    @pl.when(pl.program_id(1) == 0)
    def _():
        h_ref[...] = _rms(x_ref[...], g_ref[...]).astype(BF16)

    o_ref[...] = _dot(h_ref[...], w_ref[...]).astype(o_ref.dtype)


def norm_matmul(x, g, w, layer, tm, tn, out_dtype):
    m, k = x.shape
    n = w.shape[2]
    return pl.pallas_call(
        _norm_matmul_kernel,
        grid=(m // tm, n // tn),
        in_specs=[pl.BlockSpec((tm, k), lambda i, j: (i, 0)),
                  pl.BlockSpec((1, k), lambda i, j: (0, 0)),
                  pl.BlockSpec((None, k, tn), lambda i, j: (layer, 0, j))],
        out_specs=pl.BlockSpec((tm, tn), lambda i, j: (i, j)),
        out_shape=jax.ShapeDtypeStruct((m, n), out_dtype),
        scratch_shapes=[pltpu.VMEM((tm, k), BF16)],
        compiler_params=_cparams(("parallel", "arbitrary")),
        name="norm_matmul",
    )(x, g, w)


def _merge_wo_kernel(x_ref, mg0_ref, mg1_ref, oa_ref, ob_ref, g_ref, wo_ref, o_ref):
    ob = jnp.concatenate([ob_ref[h].astype(F32) for h in range(NSA_HEADS)], axis=1)
    a = (jax.nn.sigmoid(mg0_ref[...].astype(F32)) * oa_ref[...].astype(F32)
         + jax.nn.sigmoid(mg1_ref[...].astype(F32)) * ob)
    m = _dot(a.astype(BF16), wo_ref[...])
    o_ref[...] = x_ref[...] + _rms(m, g_ref[...])


def merge_wo(x, proj, o_a, o_b_hm, g, wo, layer, tm):
    m, d = x.shape
    return pl.pallas_call(
        _merge_wo_kernel,
        grid=(m // tm,),
        in_specs=[pl.BlockSpec((tm, d), lambda i: (i, 0)),
                  pl.BlockSpec((tm, d), lambda i: (i, COL_MG // D_MODEL)),
                  pl.BlockSpec((tm, d), lambda i: (i, COL_MG // D_MODEL + 1)),
                  pl.BlockSpec((tm, d), lambda i: (i, 0)),
                  pl.BlockSpec((NSA_HEADS, tm, NSA_DH), lambda i: (0, i, 0)),
                  pl.BlockSpec((1, d), lambda i: (0, 0)),
                  pl.BlockSpec((None, d, d), lambda i: (layer, 0, 0))],
        out_specs=pl.BlockSpec((tm, d), lambda i: (i, 0)),
        out_shape=jax.ShapeDtypeStruct((m, d), F32),
        compiler_params=_cparams(("parallel",)),
        name="merge_wo",
    )(x, proj, proj, o_a, o_b_hm, g, wo)


def _mlp_kernel(x_ref, g2_ref, g3_ref, wu_ref, wd_ref, o_ref, h_ref, acc_ref):
    j = pl.program_id(1)

    @pl.when(j == 0)
    def _():
        h_ref[...] = _rms(x_ref[...], g2_ref[...]).astype(BF16)
        acc_ref[...] = jnp.zeros_like(acc_ref)

    u = _dot(h_ref[...], wu_ref[...])
    u = jnp.square(jnp.maximum(u, 0.0)).astype(BF16)
    acc_ref[...] += _dot(u, wd_ref[...])

    @pl.when(j == pl.num_programs(1) - 1)
    def _():
        o_ref[...] = x_ref[...] + _rms(acc_ref[...], g3_ref[...])


def mlp(x, g2, g3, wu, wd, layer, tm, th):
    m, d = x.shape
    hid = wu.shape[2]
    return pl.pallas_call(
        _mlp_kernel,
        grid=(m // tm, hid // th),
        in_specs=[pl.BlockSpec((tm, d), lambda i, j: (i, 0)),
                  pl.BlockSpec((1, d), lambda i, j: (0, 0)),
                  pl.BlockSpec((1, d), lambda i, j: (0, 0)),
                  pl.BlockSpec((None, d, th), lambda i, j: (layer, 0, j)),
                  pl.BlockSpec((None, th, d), lambda i, j: (layer, j, 0))],
        out_specs=pl.BlockSpec((tm, d), lambda i, j: (i, 0)),
        out_shape=jax.ShapeDtypeStruct((m, d), F32),
        scratch_shapes=[pltpu.VMEM((tm, d), BF16), pltpu.VMEM((tm, d), F32)],
        compiler_params=_cparams(("parallel", "arbitrary")),
        name="mlp",
    )(x, g2, g3, wu, wd)


def _gla_kernel(*refs, rows_in, n_inner, n_valid, has_s0):
    q_ref, k_ref, v_ref, gg_ref, sm_ref, wa2_ref, ba_ref, gn_ref, s0_ref = refs[:9]
    o_ref, sout_ref, st_ref = refs[-3:]
    c = pl.program_id(1)
    C = GLA_CHUNK

    @pl.when(c == 0)
    def _():
        for h in range(GLA_HEADS):
            if has_s0:
                st_ref[h] = s0_ref[h].T
            else:
                st_ref[h] = jnp.zeros((GLA_DV, GLA_DK), F32)

    rp = n_inner * C

    def load(ref):
        x = ref[...].astype(F32)
        if rows_in == C:
            return x
        assert n_inner == 1
        return jnp.concatenate([x, jnp.zeros((C - rows_in, x.shape[1]), x.dtype)], axis=0)

    row = _iota((rp, 1), 0)
    col = _iota((1, rp), 1)
    same_chunk = (row >> GLA_CHUNK_SHIFT) == (col >> GLA_CHUNK_SHIFT)
    causal = same_chunk & (row >= col)
    tril = causal.astype(BF16)
    live = (row & (C - 1)) < n_valid

    ga_h, ga_l = _split_bf16(load(sm_ref)[:, :GLA_GATE_RANK])
    w_h, w_l = _split_bf16(wa2_ref[...])
    z = _dot(ga_h, w_h) + _dot(ga_l, w_h) + _dot(ga_h, w_l) + ba_ref[...]
    log_a = (jnp.minimum(z, 0.0) - jnp.log1p(jnp.exp(-jnp.abs(z)))) / GLA_TAU
    k = load(k_ref)
    if n_valid < C:
        log_a = jnp.where(live, log_a, 0.0)
        k = jnp.where(live, k, 0.0)
    la_h, la_l = _split_bf16(log_a)
    b = _dot(tril, la_h) + _dot(tril, la_l)
    b_last = [b[ci * C + C - 1:ci * C + C, :] for ci in range(n_inner)]
    q_dec = (load(q_ref) * (GLA_DK ** -0.5) * jnp.exp(b)).astype(BF16)
    k_dec = (k * jnp.exp(-b)).astype(BF16)
    k_end = jnp.concatenate([k[ci * C:(ci + 1) * C] * jnp.exp(b_last[ci] - b[ci * C:(ci + 1) * C])
                             for ci in range(n_inner)], axis=0).astype(BF16)
    decay = [jnp.exp(r) for r in b_last]
    vb = load(v_ref).astype(BF16)

    heads = range(GLA_HEADS)
    kq = [slice(h * GLA_DK, (h + 1) * GLA_DK) for h in heads]
    kv = [slice(h * GLA_DV, (h + 1) * GLA_DV) for h in heads]
    att = [jnp.where(causal, _dot_nt(q_dec[:, kq[h]], k_dec[:, kq[h]]), 0.0).astype(BF16) for h in heads]
    intra = [_dot(att[h], vb[:, kv[h]]) for h in heads]
    upd = [[_dot_tn(vb[ci * C:(ci + 1) * C, kv[h]], k_end[ci * C:(ci + 1) * C, kq[h]]) for h in heads]
           for ci in range(n_inner)]
    for ci in range(n_inner):
        cr = slice(ci * C, (ci + 1) * C)
        rs = pl.ds(ci * rows_in, rows_in)
        st = [st_ref[h] for h in heads]
        o = [intra[h][cr] + _dot_nt(q_dec[cr, kq[h]], st[h].astype(BF16)) for h in heads]
        for h in heads:
            st_ref[h] = st[h] * decay[ci][:, kq[h]] + upd[ci][h]
        for h in heads:
            o_n = _rms(o[h], gn_ref[...])
            gg = gg_ref[rs, kv[h]].astype(F32)
            o_ref[rs, kv[h]] = (o_n[:rows_in] * (gg * jax.nn.sigmoid(gg))).astype(o_ref.dtype)

    @pl.when(c == pl.num_programs(1) - 1)
    def _():
        for h in range(GLA_HEADS):
            sout_ref[h] = st_ref[h].T


def gla(proj, wa2, ba, gn, s0, layer, depth, stacked, batch, rows_per_batch, rows_in, n_inner, n_valid):
    m = proj.shape[0]
    r = rows_in * n_inner
    n_steps = rows_per_batch // r
    has_s0 = s0 is not None
    if s0 is None:
        s0 = jnp.zeros((1, 1, GLA_HEADS, GLA_DK, GLA_DV), F32)
        s0_map = lambda b, c: (0, 0, 0, 0, 0)
    else:
        s0_map = lambda b, c: (layer, b, 0, 0, 0)
    wk, wv = GLA_HEADS * GLA_DK, GLA_HEADS * GLA_DV
    rowmap = lambda off: (lambda b, c: (b * n_steps + c, off))
    state_blk = (None, GLA_HEADS, GLA_DK, GLA_DV)
    kern = functools.partial(_gla_kernel, rows_in=rows_in, n_inner=n_inner, n_valid=n_valid, has_s0=has_s0)
    in_specs = [pl.BlockSpec((r, wk), rowmap(COL_GQ // wk)),
                pl.BlockSpec((r, wk), rowmap(COL_GK // wk)),
                pl.BlockSpec((r, wv), rowmap(COL_GV // wv)),
                pl.BlockSpec((r, wv), rowmap(COL_GG // wv)),
                pl.BlockSpec((r, LANES), rowmap(COL_SM // LANES)),
                pl.BlockSpec((GLA_GATE_RANK, wk), lambda b, c: (0, 0)),
                pl.BlockSpec((1, wk), lambda b, c: (0, 0)),
                pl.BlockSpec((1, GLA_DV), lambda b, c: (0, 0)),
                pl.BlockSpec((None,) + state_blk, s0_map)]
    args = (proj, proj, proj, proj, proj, wa2, ba, gn, s0)
    aliases = {}
    if stacked is not None:
        in_specs.append(pl.BlockSpec(memory_space=pl.ANY))
        aliases = {len(args): 1}
        args += (stacked,)
    return pl.pallas_call(
        kern,
        grid=(batch, n_steps),
        in_specs=in_specs,
        out_specs=[pl.BlockSpec((r, wv), rowmap(0)),
                   pl.BlockSpec((None,) + state_blk, lambda b, c: (layer, b, 0, 0, 0))],
        out_shape=[jax.ShapeDtypeStruct((m, D_MODEL), proj.dtype),
                   jax.ShapeDtypeStruct((depth, batch, GLA_HEADS, GLA_DK, GLA_DV), F32)],
        input_output_aliases=aliases,
        scratch_shapes=[pltpu.VMEM((GLA_HEADS, GLA_DV, GLA_DK), F32)],
        compiler_params=_cparams(("parallel", "arbitrary")),
        name="gla",
    )(*args)


def _rope(x, c, sa, sb):
    return x * c + pltpu.roll(x, LANES - ROPE_HALF, 1) * sa + pltpu.roll(x, ROPE_HALF, 1) * sb


def _nsa_prep_kernel(*refs, with_q):
    if with_q:
        nq_ref, nkv_ref, c_ref, sa_ref, sb_ref, q_ref, cmp_ref, sel_ref, win_ref = refs
    else:
        nkv_ref, c_ref, sa_ref, sb_ref, cmp_ref, sel_ref, win_ref = refs
    c, sa, sb = c_ref[...], sa_ref[...], sb_ref[...]
    if with_q:
        for h in range(NSA_HEADS):
            q_ref[h] = _rope(nq_ref[:, h * NSA_DH:(h + 1) * NSA_DH].astype(F32), c, sa, sb)
    for s, out in enumerate((cmp_ref, sel_ref, win_ref)):
        base = s * KV_ROW
        for hh in range(NSA_KV_HEADS):
            lo = hh * NSA_DH
            out[:, lo:lo + NSA_DH] = _rope(nkv_ref[:, base + lo:base + lo + NSA_DH].astype(F32), c, sa, sb)
        half = NSA_KV_HEADS * NSA_DH
        out[:, half:] = nkv_ref[:, base + half:base + KV_ROW].astype(F32)


def nsa_prep(proj, tabs, tr, with_q):
    m = proj.shape[0]
    n_tab = tabs[0].shape[0] // tr
    tab_spec = pl.BlockSpec((tr, LANES), lambda i: (i % n_tab, 0))
    nkv_w = 3 * KV_ROW
    row_spec = pl.BlockSpec((tr, KV_ROW), lambda i: (i, 0))
    row_shape = jax.ShapeDtypeStruct((m, KV_ROW), F32)
    in_specs = [pl.BlockSpec((tr, nkv_w), lambda i: (i, COL_NKV // nkv_w)), tab_spec, tab_spec, tab_spec]
    out_specs, out_shape, args = [row_spec] * 3, [row_shape] * 3, (proj,) + tuple(tabs)
    if with_q:
        in_specs = [pl.BlockSpec((tr, D_MODEL), lambda i: (i, COL_NQ // D_MODEL))] + in_specs
        out_specs = [pl.BlockSpec((NSA_HEADS, tr, NSA_DH), lambda i: (0, i, 0))] + out_specs
        out_shape = [jax.ShapeDtypeStruct((NSA_HEADS, m, NSA_DH), F32)] + out_shape
        args = (proj,) + args
    return pl.pallas_call(
        functools.partial(_nsa_prep_kernel, with_q=with_q),
        grid=(m // tr,),
        in_specs=in_specs,
        out_specs=out_specs,
        out_shape=out_shape,
        compiler_params=_cparams(("parallel",)),
        name="nsa_prep",
    )(*args)


def _nsa_prep_prompt_kernel(*refs):
    nkv_ref, c_ref, sa_ref, sb_ref = refs[:4]
    slabs, stacked = refs[-6:-3], refs[-3:]
    c, sa, sb = c_ref[...], sa_ref[...], sb_ref[...]
    tr = nkv_ref.shape[0]
    for s in range(3):
        for part in range(KV_PARTS):
            lo = part * NSA_DH
            v = nkv_ref[:, s * KV_ROW + lo:s * KV_ROW + lo + NSA_DH].astype(F32)
            if part < NSA_KV_HEADS:
                v = _rope(v, c, sa, sb)
            slabs[s][:, lo:lo + NSA_DH] = v
            stacked[s][pl.ds(part, tr, stride=KV_PARTS), :] = v


def nsa_prep_prompt(proj, tabs, tr, layer, depth, stacked):
    m = proj.shape[0]
    n_tab = tabs[0].shape[0] // tr
    tab_spec = pl.BlockSpec((tr, LANES), lambda i: (i % n_tab, 0))
    nkv_w = 3 * KV_ROW
    row_spec = pl.BlockSpec((tr, KV_ROW), lambda i: (i, 0))
    in_specs = [pl.BlockSpec((tr, nkv_w), lambda i: (i, COL_NKV // nkv_w)), tab_spec, tab_spec, tab_spec]
    args = (proj,) + tuple(tabs)
    aliases = {}
    if stacked is not None:
        in_specs += [pl.BlockSpec(memory_space=pl.ANY)] * 3
        aliases = {len(args) + k: 3 + k for k in range(3)}
        args += tuple(stacked)
    st_spec = pl.BlockSpec((None, tr * KV_PARTS, NSA_DH), lambda i: (layer, i, 0))
    out = pl.pallas_call(
        _nsa_prep_prompt_kernel,
        grid=(m // tr,),
        in_specs=in_specs,
        out_specs=[row_spec] * 3 + [st_spec] * 3,
        out_shape=[jax.ShapeDtypeStruct((m, KV_ROW), F32)] * 3
                  + [jax.ShapeDtypeStruct((depth, m * KV_PARTS, NSA_DH), F32)] * 3,
        input_output_aliases=aliases,
        compiler_params=_cparams(("parallel",)),
        name="nsa_prep_prompt",
    )(*args)
    return out[:3], out[3:]


def rope_tables(pos):
    inv = 1.0 / (ROPE_THETA ** (jnp.arange(ROPE_HALF, dtype=F32) / ROPE_HALF))
    ang = pos.astype(F32)[:, None] * inv[None, :]
    cos, sin = jnp.cos(ang), jnp.sin(ang)
    n = pos.shape[0]
    rest = LANES - ROPE_DIM
    c = jnp.concatenate([cos, cos, jnp.ones((n, rest), F32)], axis=1)
    sa = jnp.concatenate([-sin, jnp.zeros((n, LANES - ROPE_HALF), F32)], axis=1)
    sb = jnp.concatenate([jnp.zeros((n, ROPE_HALF), F32), sin, jnp.zeros((n, rest), F32)], axis=1)
    return c, sa, sb


_HALVES = (slice(0, CMP_STRIDE), slice(CMP_STRIDE, CMP_LEN))


def _pe_mix_sums(pe_ref, mix_ref):
    return [[jnp.sum(pe_ref[kv][h] * mix_ref[kv][h], axis=0, keepdims=True) for h in _HALVES] for kv in range(2)]


SUBLANES = 8


def _sum_rows_8(p):
    sub = _iota((SUBLANES, LANES), 0)
    n = SUBLANES
    stage = list(p)
    for shift, bit in ((4, 4), (2, 2), (1, 1)):
        nxt = []
        half = len(stage) // 2
        for c in range(half):
            lo = stage[c] + pltpu.roll(stage[c], n - shift, 0)
            hi = stage[c + half] + pltpu.roll(stage[c + half], shift, 0)
            nxt.append(jnp.where((sub & bit) == 0, lo, hi))
        stage = nxt
    return stage[0]


def _half_sums(xs, mix, consts):
    nch = xs.shape[0]
    out = []
    for h, c in zip(_HALVES, consts):
        y = xs * mix[h]
        y = y[:, :SUBLANES, :] + y[:, SUBLANES:, :]
        groups = [_sum_rows_8([y[g + i] for i in range(SUBLANES)]) for g in range(0, nch, SUBLANES)]
        out.append(jnp.concatenate(groups, axis=0) + c)
    return out


def _chunk_sums_kernel(x_ref, pe_ref, mix_ref, a_ref, b_ref):
    nch = x_ref.shape[0] // CMP_STRIDE
    consts = _pe_mix_sums(pe_ref, mix_ref)
    for kv in range(2):
        for hh in range(NSA_KV_HEADS):
            lo = (kv * NSA_KV_HEADS + hh) * NSA_DH
            xs = x_ref[:, lo:lo + NSA_DH].reshape(nch, CMP_STRIDE, NSA_DH)
            a_ref[:, lo:lo + NSA_DH], b_ref[:, lo:lo + NSA_DH] = _half_sums(xs, mix_ref[kv], consts[kv])


def _page_specs(layer):
    def spec(i):
        return pl.BlockSpec((None, None, PAGE_SIZE * KV_PARTS, NSA_DH),
                            lambda b, s, pt: (layer, pt[b, s * PAGES_PER_STEP + i], 0, 0))
    return [spec(i) for i in range(PAGES_PER_STEP)]


def _page_part(page_ref, part):
    return page_ref[pl.ds(part, PAGE_SIZE, stride=KV_PARTS), :]


def _chunk_sums_paged_kernel(pt_ref, *refs):
    page_refs, (pe_ref, mix_ref, a_ref, b_ref) = refs[:PAGES_PER_STEP], refs[PAGES_PER_STEP:]
    nch = PAGE_SIZE // CMP_STRIDE
    consts = _pe_mix_sums(pe_ref, mix_ref)
    for i, page_ref in enumerate(page_refs):
        rows = slice(i * nch, (i + 1) * nch)
        for kv in range(2):
            for hh in range(NSA_KV_HEADS):
                part = kv * NSA_KV_HEADS + hh
                lo = part * NSA_DH
                xs = _page_part(page_ref, part).reshape(nch, CMP_STRIDE, NSA_DH)
                a_ref[rows, lo:lo + NSA_DH], b_ref[rows, lo:lo + NSA_DH] = _half_sums(xs, mix_ref[kv], consts[kv])


def chunk_sums(rows, pe, mix, tr):
    m = rows.shape[0]
    nch = tr // CMP_STRIDE
    full = lambda i: (0, 0, 0)
    return pl.pallas_call(
        _chunk_sums_kernel,
        grid=(m // tr,),
        in_specs=[pl.BlockSpec((tr, KV_ROW), lambda i: (i, 0)),
                  pl.BlockSpec((2, CMP_LEN, NSA_DH), full),
                  pl.BlockSpec((2, CMP_LEN, NSA_DH), full)],
        out_specs=[pl.BlockSpec((nch, KV_ROW), lambda i: (i, 0))] * 2,
        out_shape=[jax.ShapeDtypeStruct((m // CMP_STRIDE, KV_ROW), F32)] * 2,
        compiler_params=_cparams(("parallel",)),
        name="chunk_sums",
    )(rows, pe, mix)


def chunk_sums_paged(cache, layer, page_table, pe, mix):
    batch, n_pages = page_table.shape
    nch = PAGES_PER_STEP * PAGE_SIZE // CMP_STRIDE
    n_steps = n_pages // PAGES_PER_STEP
    full = lambda b, s, pt: (0, 0, 0)
    gs = pltpu.PrefetchScalarGridSpec(
        num_scalar_prefetch=1,
        grid=(batch, n_steps),
        in_specs=_page_specs(layer) + [pl.BlockSpec((2, CMP_LEN, NSA_DH), full),
                                       pl.BlockSpec((2, CMP_LEN, NSA_DH), full)],
        out_specs=[pl.BlockSpec((nch, KV_ROW), lambda b, s, pt: (b * n_steps + s, 0))] * 2,
    )
    return pl.pallas_call(
        _chunk_sums_paged_kernel,
        grid_spec=gs,
        out_shape=[jax.ShapeDtypeStruct((batch * n_steps * nch, KV_ROW), F32)] * 2,
        compiler_params=_cparams(("parallel", "arbitrary")),
        name="chunk_sums_paged",
    )(page_table, *([cache] * PAGES_PER_STEP), pe, mix)


def _cmp_mlp_kernel(a_ref, b_ref, w1_ref, w2_ref, o_ref):
    h = a_ref[...] + b_ref[...]
    for kv in range(2):
        for hh in range(NSA_KV_HEADS):
            lo = (kv * NSA_KV_HEADS + hh) * NSA_DH
            y = jax.nn.gelu(_dot_3x(h[:, lo:lo + NSA_DH], w1_ref[kv]))
            o_ref[:, lo:lo + NSA_DH] = _dot_3x(y, w2_ref[kv])


def cmp_mlp(a, b_shift, w1, w2, tr):
    m = a.shape[0]
    return pl.pallas_call(
        _cmp_mlp_kernel,
        grid=(m // tr,),
        in_specs=[pl.BlockSpec((tr, KV_ROW), lambda i: (i, 0)),
                  pl.BlockSpec((tr, KV_ROW), lambda i: (i, 0)),
                  pl.BlockSpec((2, NSA_DH, CMP_HIDDEN), lambda i: (0, 0, 0)),
                  pl.BlockSpec((2, CMP_HIDDEN, NSA_DH), lambda i: (0, 0, 0))],
        out_specs=pl.BlockSpec((tr, KV_ROW), lambda i: (i, 0)),
        out_shape=jax.ShapeDtypeStruct((m, KV_ROW), F32),
        compiler_params=_cparams(("parallel",)),
        name="cmp_mlp",
    )(a, b_shift, w1, w2)


def compressed_kv(a, b, batch, w1, w2, tr):
    nch = a.shape[0] // batch
    b3 = b.reshape(batch, nch, KV_ROW)
    b_shift = jnp.concatenate([b3[:, 1:], jnp.zeros((batch, 1, KV_ROW), F32)], axis=1).reshape(batch * nch, KV_ROW)
    return cmp_mlp(a, b_shift, w1, w2, tr)


def _select_topn(imp, n_top):
    j = _iota(imp.shape, 1)
    big = jnp.int32(imp.shape[1])
    sel = jnp.zeros(imp.shape, F32)
    for _ in range(n_top):
        m = jnp.max(imp, axis=-1, keepdims=True)
        idx = jnp.min(jnp.where(imp == m, j, big), axis=-1, keepdims=True)
        hit = j == idx
        sel = jnp.where(hit, 1.0, sel)
        imp = jnp.where(hit, NEG_INF, imp)
    return sel


def _importance(p_grp, qpos, n_cmp_pad, n_slc, width):
    n_r = _iota((n_cmp_pad, 1), 0) * CMP_STRIDE
    j_c = _iota((1, width), 1)
    overlap = ((n_r < (j_c + 1) * SEL_LEN) & (n_r + CMP_LEN > j_c * SEL_LEN)).astype(BF16)
    p_hi, p_lo = _split_bf16(p_grp)
    imp = _dot(p_hi, overlap) + _dot(p_lo, overlap)
    cur = qpos >> SEL_SHIFT
    forced = (j_c == 0) | (j_c == cur) | (j_c == cur - 1)
    imp = jnp.where(forced, FORCED_SCORE, imp)
    imp = jnp.where(j_c > cur, -1.0, imp)
    return jnp.where(j_c >= n_slc, -2.0, imp)


LOG2E = 1.4426950408889634


def _nsa_prompt_kernel(nq_ref, c_ref, sa_ref, sb_ref, kc_ref, vc_ref, ks_ref, vs_ref, kw_ref, vw_ref, sm_ref, o_ref,
                       qf_ref, qb_ref, s_ref, p_ref, m_ref, l_ref, a_ref, acc_ref, ob_ref, *, Q, T, TK):
    G = NSA_GROUP
    hk = pl.program_id(1)
    qi = pl.program_id(2)
    q0 = qi * Q
    scale = NSA_DH ** -0.5
    n_cmp_pad = kc_ref.shape[0]
    n_slc = T // SEL_LEN
    qpos = q0 + _iota((Q, 1), 0)
    c, sa, sb = c_ref[...], sa_ref[...], sb_ref[...]
    for g in range(G):
        qg = _rope(nq_ref[:, g * NSA_DH:(g + 1) * NSA_DH].astype(F32), c, sa, sb)
        qf_ref[g * Q:(g + 1) * Q, :] = qg
        qb_ref[g * Q:(g + 1) * Q, :] = (qg * (scale * LOG2E)).astype(BF16)
    qs = qf_ref[...]

    ng = jax.nn.sigmoid(sm_ref[...].astype(F32))

    def gate(g, c3):
        lane0 = SM_GATE0 + 3 * g + c3
        lane1 = lane0 + 3 * G
        return jnp.where(hk == 0, ng[:, lane0:lane0 + 1], ng[:, lane1:lane1 + 1])

    def softmax_tile(kb, vb, bias, width):
        s_ref[:, :width] = _dot_nt(qb_ref[...], kb)
        for g in range(G):
            rows = pl.ds(g * Q, Q)
            s = s_ref[rows, :width] + bias
            m_old = m_ref[rows, :]
            m_new = jnp.maximum(m_old, jnp.max(s, axis=-1, keepdims=True))
            alpha = jnp.exp2(m_old - m_new)
            p = jnp.exp2(s - jnp.tile(m_new, (1, width // LANES)))
            l_ref[rows, :] = alpha * l_ref[rows, :] + jnp.sum(p, axis=-1, keepdims=True)
            a_ref[rows, :] = alpha
            m_ref[rows, :] = m_new
            p_ref[rows, :width] = p.astype(BF16)
        return _dot(p_ref[:, :width], vb)

    def reset():
        m_ref[...] = jnp.full_like(m_ref, NEG_INF)
        l_ref[...] = jnp.zeros_like(l_ref)

    cmp_valid = _iota((1, n_cmp_pad), 1) * CMP_STRIDE + (CMP_LEN - 1) <= qpos
    p_grp = jnp.zeros((Q, n_cmp_pad), F32)
    s_cmp = _dot_nt_3x(qs, kc_ref[...]) * scale
    vcb = vc_ref[...].astype(BF16)
    for g in range(G):
        p = _masked_softmax(s_cmp[g * Q:(g + 1) * Q], cmp_valid)
        p_grp = p_grp + p
        ob_ref[g] = gate(g, 0) * _dot(p.astype(BF16), vcb)

    ks0 = pl.multiple_of(jnp.maximum(q0 - WINDOW, 0), LANES)
    wlen = WINDOW + Q
    wpos = ks0 + _iota((1, wlen), 1)
    win_bias = jnp.where((wpos <= qpos) & (wpos > qpos - WINDOW), 0.0, NEG_INF)
    reset()
    pv = softmax_tile(kw_ref[pl.ds(ks0, wlen), :].astype(BF16), vw_ref[pl.ds(ks0, wlen), :].astype(BF16),
                      win_bias, wlen)
    for g in range(G):
        rows = pl.ds(g * Q, Q)
        ob_ref[g] = ob_ref[g] + gate(g, 2) * (pv[g * Q:(g + 1) * Q] / jnp.maximum(l_ref[rows, :], 1e-30))

    n_top = min(SEL_TOPN, n_slc)
    j_r = _iota((n_slc, 1), 0)
    n_c = _iota((1, n_cmp_pad), 1) * CMP_STRIDE
    overlap_t = ((n_c < (j_r + 1) * SEL_LEN) & (n_c + CMP_LEN > j_r * SEL_LEN)).astype(BF16)
    p_hi, p_lo = _split_bf16(p_grp)
    imp = _dot_nt(overlap_t, p_hi) + _dot_nt(overlap_t, p_lo)
    cur = (q0 + _iota((1, Q), 1)) >> SEL_SHIFT
    imp = jnp.where((j_r == 0) | (j_r == cur) | (j_r == cur - 1), FORCED_SCORE, imp)
    imp = jnp.where(j_r > cur, -1.0, imp)
    rank = jnp.zeros((n_slc, Q), F32)
    for jp in range(n_slc):
        other = imp[jp:jp + 1, :]
        beats = (other > imp) | ((other == imp) & (j_r > jp))
        rank = rank + jnp.where(beats, 1.0, 0.0)
    sel_t = jnp.where(rank < n_top, 1.0, 0.0).astype(BF16)

    reset()
    acc_ref[...] = jnp.zeros_like(acc_ref)

    def key_tile(kt, carry):
        k0 = pl.multiple_of(kt * TK, TK)
        kpos = k0 + _iota((1, TK), 1)
        expand = (j_r == (kpos >> SEL_SHIFT)).astype(BF16)
        keymask = _dot_tn(sel_t, expand)
        bias = jnp.where((keymask > 0.5) & (kpos <= qpos), 0.0, NEG_INF)
        pv = softmax_tile(ks_ref[pl.ds(k0, TK), :].astype(BF16), vs_ref[pl.ds(k0, TK), :].astype(BF16), bias, TK)
        acc_ref[...] = a_ref[...] * acc_ref[...] + pv
        return carry

    lax.fori_loop(0, (q0 + Q + TK - 1) // TK, key_tile, 0)

    for g in range(G):
        rows = pl.ds(g * Q, Q)
        o_sel = acc_ref[rows, :] / jnp.maximum(l_ref[rows, :], 1e-30)
        o_ref[g] = (ob_ref[g] + gate(g, 1) * o_sel).astype(o_ref.dtype)


def nsa_prompt(tabs, kvc, sel_rows, win_rows, proj, batch, T, Q):
    m = batch * T
    nq = T // Q
    n_cmp_pad = kvc.shape[0] // batch
    G = NSA_GROUP
    gw = G * NSA_DH
    kern = functools.partial(_nsa_prompt_kernel, Q=Q, T=T, TK=512)
    kcol = lambda off: (lambda b, hk, qi: (b, off + hk))
    tab_spec = pl.BlockSpec((Q, LANES), lambda b, hk, qi: (qi, 0))
    return pl.pallas_call(
        kern,
        grid=(batch, NSA_KV_HEADS, nq),
        in_specs=[pl.BlockSpec((Q, gw), lambda b, hk, qi: (b * nq + qi, COL_NQ // gw + hk)),
                  tab_spec, tab_spec, tab_spec,
                  pl.BlockSpec((n_cmp_pad, NSA_DH), kcol(0)),
                  pl.BlockSpec((n_cmp_pad, NSA_DH), kcol(NSA_KV_HEADS)),
                  pl.BlockSpec((T, NSA_DH), kcol(0)),
                  pl.BlockSpec((T, NSA_DH), kcol(NSA_KV_HEADS)),
                  pl.BlockSpec((T, NSA_DH), kcol(0)),
                  pl.BlockSpec((T, NSA_DH), kcol(NSA_KV_HEADS)),
                  pl.BlockSpec((Q, LANES), lambda b, hk, qi: (b * nq + qi, COL_SM // LANES))],
        out_specs=pl.BlockSpec((G, Q, NSA_DH), lambda b, hk, qi: (hk, b * nq + qi, 0)),
        out_shape=jax.ShapeDtypeStruct((NSA_HEADS, m, NSA_DH), proj.dtype),
        scratch_shapes=[pltpu.VMEM((G * Q, NSA_DH), F32), pltpu.VMEM((G * Q, NSA_DH), BF16),
                        pltpu.VMEM((G * Q, WINDOW + Q), F32), pltpu.VMEM((G * Q, WINDOW + Q), BF16),
                        pltpu.VMEM((G * Q, LANES), F32), pltpu.VMEM((G * Q, LANES), F32),
                        pltpu.VMEM((G * Q, LANES), F32), pltpu.VMEM((G * Q, NSA_DH), F32),
                        pltpu.VMEM((G, Q, NSA_DH), F32)],
        compiler_params=_cparams(("parallel", "parallel", "arbitrary")),
        name="nsa_prompt",
    )(proj, *tabs, kvc, kvc, sel_rows, sel_rows, win_rows, win_rows, proj)


SEL_W = 384


def _sample_cmp_select_kernel(q_ref, kvc_ref, ocmp_ref, selm_ref, *, n_slc):
    G, R = NSA_GROUP, SAMPLE_ROWS
    scale = NSA_DH ** -0.5
    n_cmp_pad = kvc_ref.shape[0]
    qpos_s = PAST_LEN + (_iota((G * R, 1), 0) & (R - 1))
    end = _iota((1, n_cmp_pad), 1) * CMP_STRIDE + (CMP_LEN - 1)
    p_grp = []
    for hk in range(NSA_KV_HEADS):
        qs = q_ref[hk * G:(hk + 1) * G].reshape(G * R, NSA_DH)
        kc = kvc_ref[:, hk * NSA_DH:(hk + 1) * NSA_DH]
        vc = kvc_ref[:, (NSA_KV_HEADS + hk) * NSA_DH:(NSA_KV_HEADS + hk + 1) * NSA_DH]
        p = _masked_softmax(_dot_nt_3x(qs, kc) * scale, end <= qpos_s)
        o_cmp = _dot(p.astype(BF16), vc.astype(BF16))
        for g in range(G):
            ocmp_ref[hk * G + g] = o_cmp[g * R:(g + 1) * R]
        p_grp.append(jnp.sum(p.reshape(G, R, n_cmp_pad), axis=0))
    qpos = PAST_LEN + (_iota((NSA_KV_HEADS * R, 1), 0) & (R - 1))
    imp = _importance(jnp.concatenate(p_grp, axis=0), qpos, n_cmp_pad, n_slc, SEL_W)
    sel = _select_topn(imp, min(SEL_TOPN, n_slc))
    for hk in range(NSA_KV_HEADS):
        selm_ref[hk] = sel[hk * R:(hk + 1) * R]


def sample_cmp_select(q_hm, kvc, batch, n_slc):
    R = SAMPLE_ROWS
    n_cmp_pad = kvc.shape[0] // batch
    kern = functools.partial(_sample_cmp_select_kernel, n_slc=n_slc)
    hm_spec = pl.BlockSpec((NSA_HEADS, R, NSA_DH), lambda b: (0, b, 0))
    return pl.pallas_call(
        kern,
        grid=(batch,),
        in_specs=[hm_spec, pl.BlockSpec((n_cmp_pad, KV_ROW), lambda b: (b, 0))],
        out_specs=[hm_spec, pl.BlockSpec((None, NSA_KV_HEADS, R, SEL_W), lambda b: (b, 0, 0, 0))],
        out_shape=[jax.ShapeDtypeStruct((NSA_HEADS, batch * R, NSA_DH), F32),
                   jax.ShapeDtypeStruct((batch, NSA_KV_HEADS, R, SEL_W), F32)],
        compiler_params=_cparams(("parallel",)),
        name="sample_cmp_select",
    )(q_hm, kvc)


def _sample_attn_kernel(pt_ref, q_ref, selm_ref, *refs, n_steps):
    page_refs = refs[:PAGES_PER_STEP]
    nsel_ref, cwin_ref, nwin_ref, sm_ref, ocmp_ref, o_ref, m_ref, l_ref, acc_ref = refs[PAGES_PER_STEP:]
    G, R = NSA_GROUP, SAMPLE_ROWS
    p = pl.program_id(1)
    scale = NSA_DH ** -0.5
    t_s = _iota((G * R, 1), 0) & (R - 1)
    n_keys = PAGES_PER_STEP * PAGE_SIZE

    @pl.when(p == 0)
    def _():
        m_ref[...] = jnp.full_like(m_ref, NEG_INF)
        l_ref[...] = jnp.zeros_like(l_ref)
        acc_ref[...] = jnp.zeros_like(acc_ref)

    def q_stack(hk, mult):
        return (q_ref[hk * G:(hk + 1) * G].reshape(G * R, NSA_DH) * mult).astype(BF16)

    def update(hk, s, pr_of, v):
        m_old = m_ref[hk]
        m_new = jnp.maximum(m_old, jnp.max(s, axis=-1, keepdims=True))
        alpha = jnp.exp2(m_old - m_new)
        pr = pr_of(jnp.exp2(s - jnp.tile(m_new, (1, s.shape[1] // LANES))))
        l_ref[hk] = alpha * l_ref[hk] + jnp.sum(pr, axis=-1, keepdims=True)
        acc_ref[hk] = alpha * acc_ref[hk] + _dot(pr.astype(BF16), v)
        m_ref[hk] = m_new

    key = _iota((1, n_keys), 1)
    blk_of_key = (n_keys // SEL_LEN) * p + (key >> SEL_SHIFT)
    expand = (_iota((SEL_W, 1), 0) == blk_of_key).astype(BF16)
    hks = range(NSA_KV_HEADS)
    kpg = [jnp.concatenate([_page_part(r, hk).astype(BF16) for r in page_refs], axis=0) for hk in hks]
    vpg = [jnp.concatenate([_page_part(r, NSA_KV_HEADS + hk).astype(BF16) for r in page_refs], axis=0)
           for hk in hks]
    keymask = [_dot(selm_ref[hk].astype(BF16), expand) for hk in hks]
    s = [_dot_nt(q_stack(hk, scale * LOG2E), kpg[hk])
         + jnp.where(jnp.concatenate([keymask[hk]] * G, axis=0) > 0.5, 0.0, NEG_INF) for hk in hks]
    m_old = [m_ref[hk] for hk in hks]
    m_new = [jnp.maximum(m_old[hk], jnp.max(s[hk], axis=-1, keepdims=True)) for hk in hks]
    pr = [jnp.exp2(s[hk] - jnp.tile(m_new[hk], (1, n_keys // LANES))) for hk in hks]
    pv = [_dot(pr[hk].astype(BF16), vpg[hk]) for hk in hks]
    for hk in hks:
        alpha = jnp.exp2(m_old[hk] - m_new[hk])
        l_ref[hk] = alpha * l_ref[hk] + jnp.sum(pr[hk], axis=-1, keepdims=True)
        acc_ref[hk] = alpha * acc_ref[hk] + pv[hk]
        m_ref[hk] = m_new[hk]

    def win_part(ref, part):
        return ref[pl.ds(part, WINDOW, stride=KV_PARTS), :]

    @pl.when(p == n_steps - 1)
    def _():
        ng = jax.nn.sigmoid(sm_ref[...])
        new_blk = PAST_LEN // SEL_LEN
        zpad = jnp.zeros((LANES - R, NSA_DH), F32)
        jn = _iota((1, LANES), 1)
        for hk in range(NSA_KV_HEADS):
            klo, vlo = hk * NSA_DH, (NSA_KV_HEADS + hk) * NSA_DH
            kn = jnp.concatenate([nsel_ref[:, klo:klo + NSA_DH], zpad], axis=0)
            vn = jnp.concatenate([nsel_ref[:, vlo:vlo + NSA_DH], zpad], axis=0)
            picked = jnp.concatenate([selm_ref[hk][:, new_blk:new_blk + 1]] * G, axis=0) > 0.5
            valid = picked & (jn <= t_s) & (jn < R)
            s = jnp.where(valid, _dot_nt(q_stack(hk, scale * LOG2E), kn.astype(BF16)), NEG_INF)
            update(hk, s, lambda e: jnp.where(valid, e, 0.0), vn.astype(BF16))
            o_sel = acc_ref[hk] / jnp.maximum(l_ref[hk], 1e-30)
            qs = q_stack(hk, scale)
            kw = jnp.concatenate([win_part(cwin_ref, hk), nwin_ref[:, klo:klo + NSA_DH], zpad], axis=0)
            vw = jnp.concatenate([win_part(cwin_ref, NSA_KV_HEADS + hk), nwin_ref[:, vlo:vlo + NSA_DH], zpad], axis=0)
            iw = _iota((1, WINDOW + LANES), 1)
            wvalid = ((iw < WINDOW) & (iw > t_s)) | ((iw >= WINDOW) & (iw - WINDOW <= t_s) & (iw - WINDOW < R))
            pw = _masked_softmax(_dot_nt(qs, kw.astype(BF16)), wvalid)
            o_win = _dot(pw.astype(BF16), vw.astype(BF16))
            for g in range(G):
                h = hk * G + g
                lane = SM_GATE0 + 3 * h
                rows = slice(g * R, (g + 1) * R)
                o_ref[h] = (ng[:, lane:lane + 1] * ocmp_ref[h] + ng[:, lane + 1:lane + 2] * o_sel[rows]
                            + ng[:, lane + 2:lane + 3] * o_win[rows])


def sample_attn(q_hm, selm, cache_sel, layer, page_table, new_sel, cache_win, new_win, proj, o_cmp_hm):
    batch, n_pages = page_table.shape
    n_steps = n_pages // PAGES_PER_STEP
    G, R = NSA_GROUP, SAMPLE_ROWS
    hm_spec = pl.BlockSpec((NSA_HEADS, R, NSA_DH), lambda b, p, pt: (0, b, 0))
    row_spec = pl.BlockSpec((R, KV_ROW), lambda b, p, pt: (b, 0))
    gs = pltpu.PrefetchScalarGridSpec(
        num_scalar_prefetch=1,
        grid=(batch, n_steps),
        in_specs=[hm_spec,
                  pl.BlockSpec((None, NSA_KV_HEADS, R, SEL_W), lambda b, p, pt: (b, 0, 0, 0))]
                 + _page_specs(layer)
                 + [row_spec,
                    pl.BlockSpec((None, None, WINDOW * KV_PARTS, NSA_DH), lambda b, p, pt: (layer, b, 0, 0)),
                    row_spec,
                    pl.BlockSpec((R, LANES), lambda b, p, pt: (b, COL_SM // LANES)),
                    hm_spec],
        out_specs=hm_spec,
        scratch_shapes=[pltpu.VMEM((NSA_KV_HEADS, G * R, LANES), F32),
                        pltpu.VMEM((NSA_KV_HEADS, G * R, LANES), F32),
                        pltpu.VMEM((NSA_KV_HEADS, G * R, NSA_DH), F32)],
    )
    kern = functools.partial(_sample_attn_kernel, n_steps=n_steps)
    return pl.pallas_call(
        kern,
        grid_spec=gs,
        out_shape=jax.ShapeDtypeStruct((NSA_HEADS, batch * R, NSA_DH), F32),
        compiler_params=_cparams(("parallel", "arbitrary")),
        name="sample_attn",
    )(page_table, q_hm, selm, *([cache_sel] * PAGES_PER_STEP), new_sel, cache_win, new_win, proj, o_cmp_hm)


PACK_W = 512
PACK_SRC = PACK_W // LANES + 1
_MAIN_SEGMENTS = _SRC_SEGMENTS[:7]


def _pack_table():
    first, shift = [], []
    for lo, hi in _MAIN_SEGMENTS:
        assert (hi - lo) % PACK_W == 0
        for c in range(lo, hi, PACK_W):
            first.append(c // LANES)
            shift.append(c % LANES)
    assert len(first) * PACK_W == COL_SM
    n = N_PROJ // PACK_W
    first += [0] * (n - len(first))
    shift += [0] * (n - len(shift))
    return np.array([first, shift], np.int32)


def _pack_kernel(tab_ref, *refs):
    srcs, (ga_ref, ng_ref, o_ref) = refs[:PACK_SRC], refs[PACK_SRC:]
    j = pl.program_id(1)
    n_main = COL_SM // PACK_W
    shift = tab_ref[1, j]
    lane = _iota((1, LANES), 1)

    @pl.when(j >= n_main)
    def _():
        sm = jnp.where(lane < GLA_GATE_RANK, ga_ref[...], jnp.where(lane < SM_USED, ng_ref[...], 0.0))
        o_ref[:, :LANES] = sm.astype(BF16)
        o_ref[:, LANES:] = jnp.zeros((o_ref.shape[0], PACK_W - LANES), BF16)

    for sv in sorted({lo % LANES for lo, _ in _MAIN_SEGMENTS}):
        @pl.when((j < n_main) & (shift == sv))
        def _():
            for i in range(PACK_W // LANES):
                cols = slice(i * LANES, (i + 1) * LANES)
                if sv == 0:
                    o_ref[:, cols] = srcs[i][...].astype(BF16)
                else:
                    a = pltpu.roll(srcs[i][...], LANES - sv, 1)
                    b = pltpu.roll(srcs[i + 1][...], LANES - sv, 1)
                    o_ref[:, cols] = jnp.where(lane < LANES - sv, a, b).astype(BF16)


def _pack_w_in(w_in):
    depth, d, n_in = w_in.shape
    (ga_lo, ga_hi), (ng_lo, ng_hi) = _SRC_SEGMENTS[7:]
    assert ga_lo % LANES == 0 and ga_hi - ga_lo == GLA_GATE_RANK
    assert ng_lo % LANES == GLA_GATE_RANK and ng_hi - ng_lo == SM_USED - GLA_GATE_RANK
    last_blk = (n_in - 1) // LANES

    def src_spec(i):
        return pl.BlockSpec((None, d, LANES), lambda l, j, tab: (l, 0, jnp.minimum(tab[0, j] + i, last_blk)))

    def fixed_spec(col):
        return pl.BlockSpec((None, d, LANES), lambda l, j, tab: (l, 0, col // LANES))

    gs = pltpu.PrefetchScalarGridSpec(
        num_scalar_prefetch=1,
        grid=(depth, N_PROJ // PACK_W),
        in_specs=[src_spec(i) for i in range(PACK_SRC)] + [fixed_spec(ga_lo), fixed_spec(ng_lo)],
        out_specs=pl.BlockSpec((None, d, PACK_W), lambda l, j, tab: (l, 0, j)),
    )
    return pl.pallas_call(
        _pack_kernel,
        grid_spec=gs,
        out_shape=jax.ShapeDtypeStruct((depth, d, N_PROJ), BF16),
        compiler_params=_cparams(("parallel", "arbitrary")),
        name="pack_w_in",
    )(jnp.asarray(_pack_table()), *([w_in] * (PACK_SRC + 2)))


def _layer_prompt(x, lw, tabs, layer, depth, batch, T, stacked_rows, gla_states):
    proj = norm_matmul(x, lw['n0'], lw['w_in'], layer, 1024, 2048, BF16)
    o_a, gla_states = gla(proj, lw['wa2'], lw['ba'], lw['gn'], None, layer, depth, gla_states, batch, T,
                          GLA_CHUNK, 4, GLA_CHUNK)
    (cmp_rows, sel_rows, win_rows), stacked_rows = nsa_prep_prompt(proj, tabs, 512, layer, depth, stacked_rows)
    a, b = chunk_sums(cmp_rows, lw['pe'], lw['mix'], 512)
    kvc = compressed_kv(a, b, batch, lw['w1'], lw['w2'], 128)
    o_b = nsa_prompt(tabs, kvc, sel_rows, win_rows, proj, batch, T, 256)
    x = merge_wo(x, proj, o_a, o_b, lw['n1'], lw['w_o'], layer, 256)
    x = mlp(x, lw['n2'], lw['n3'], lw['w_up'], lw['w_down'], layer, 512, 1024)
    return x, gla_states, stacked_rows


def _layer_sample(x, lw, tabs, layer, depth, batch, cache_cmp, cache_sel, cache_win, s0, page_table, gla_states):
    R = SAMPLE_ROWS
    m = batch * R
    proj = norm_matmul(x, lw['n0'], lw['w_in'], layer, m, 1024, F32)
    o_a, gla_states = gla(proj, lw['wa2'], lw['ba'], lw['gn'], s0, layer, depth, gla_states, batch, R, R, 1, 4)
    q_hm, cmp_rows, sel_rows, win_rows = nsa_prep(proj, tabs, m, with_q=True)
    a, b = chunk_sums_paged(cache_cmp, layer, page_table, lw['pe'], lw['mix'])
    kvc = compressed_kv(a, b, batch, lw['w1'], lw['w2'], 1024)
    n_slc = -(-(PAST_LEN + 4) // SEL_LEN)
    o_cmp, selm = sample_cmp_select(q_hm, kvc, batch, n_slc)
    o_b = sample_attn(q_hm, selm, cache_sel, layer, page_table, sel_rows, cache_win, win_rows, proj, o_cmp)
    x = merge_wo(x, proj, o_a, o_b, lw['n1'], lw['w_o'], layer, m)
    x = mlp(x, lw['n2'], lw['n3'], lw['w_up'], lw['w_down'], layer, m, 1024)
    return x, (cmp_rows, sel_rows, win_rows), gla_states


def kernel(x_prompt, x_sample, cache_cmp_kv, cache_sel_kv, cache_win_kv, state_gla, page_table,
           w_in, gla_wa2, gla_ba, gla_norm, cmp_pe, cmp_mix, cmp_w1, cmp_w2, w_o, norms, w_up, w_down):
    bp, T, d = x_prompt.shape
    bs, ss, _ = x_sample.shape
    R = SAMPLE_ROWS
    depth = w_in.shape[0]
    n_pool = cache_cmp_kv.shape[1]
    wb = cache_win_kv.shape[2]
    assert wb == WINDOW and T % 512 == 0 and ss <= R

    w_in_p = _pack_w_in(w_in)
    w_o_b, w_up_b, w_down_b = w_o.astype(BF16), w_up.astype(BF16), w_down.astype(BF16)
    cache_cmp = cache_cmp_kv.reshape(depth, n_pool, PAGE_SIZE * KV_PARTS, NSA_DH)
    cache_sel = cache_sel_kv.reshape(depth, n_pool, PAGE_SIZE * KV_PARTS, NSA_DH)
    cache_win = cache_win_kv.reshape(depth, bs, wb * KV_PARTS, NSA_DH)

    tabs_p = rope_tables(jnp.arange(T, dtype=jnp.int32))
    tabs_s = tuple(jnp.tile(t, (bs, 1)) for t in rope_tables(PAST_LEN + jnp.arange(R, dtype=jnp.int32)))

    y_p = x_prompt.reshape(bp * T, d)
    y_s = jnp.pad(x_sample, ((0, 0), (0, R - ss), (0, 0))).reshape(bs * R, d)
    st_s, stacked_rows, gla_p, gla_s = [], None, None, None
    for layer in range(depth):
        lw = {'w_in': w_in_p, 'wa2': gla_wa2[layer], 'ba': gla_ba[layer][None, :],
              'gn': gla_norm[layer][None, :], 'pe': cmp_pe[layer], 'mix': cmp_mix[layer],
              'w1': cmp_w1[layer], 'w2': cmp_w2[layer], 'w_o': w_o_b,
              'n0': norms[layer, 0][None, :], 'n1': norms[layer, 1][None, :],
              'n2': norms[layer, 2][None, :], 'n3': norms[layer, 3][None, :],
              'w_up': w_up_b, 'w_down': w_down_b}
        y_p, gla_p, stacked_rows = _layer_prompt(y_p, lw, tabs_p, layer, depth, bp, T, stacked_rows, gla_p)
        y_s, s_s, gla_s = _layer_sample(y_s, lw, tabs_s, layer, depth, bs, cache_cmp, cache_sel, cache_win,
                                        state_gla, page_table, gla_s)
        st_s.append(s_s)

    kv_shape = (2, NSA_KV_HEADS, NSA_DH)

    def rows_p(i):
        return stacked_rows[i].reshape(depth, bp, T, *kv_shape)

    def rows_s(i):
        return jnp.stack([s[i].reshape(bs, R, *kv_shape)[:, :ss] for s in st_s])

    win_p = rows_p(2)[:, :, T - min(WINDOW, T):]
    win_s = jnp.concatenate([cache_win_kv[:, :, ss:], rows_s(2)], axis=2)
    return (y_p.reshape(bp, T, d), y_s.reshape(bs, R, d)[:, :ss],
            rows_p(0), rows_s(0), rows_p(1), rows_s(1), win_p, win_s,
            gla_p, gla_s)
```

```python
import functools

import jax
import jax.numpy as jnp
import numpy as np
from jax import lax
from jax.experimental import pallas as pl
from jax.experimental.pallas import tpu as pltpu

F32 = jnp.float32
BF16 = jnp.bfloat16

D_MODEL = 2048
DEPTH = 4
PAST_LEN = 16384
PAGE_SIZE = 128

GLA_HEADS = 4
GLA_DK = 256
GLA_DV = 512
GLA_GATE_RANK = 16
GLA_TAU = 16.0
GLA_CHUNK = 64
GLA_CHUNK_SHIFT = 6

NSA_HEADS = 16
NSA_KV_HEADS = 2
NSA_DH = 128
NSA_GROUP = NSA_HEADS // NSA_KV_HEADS
CMP_LEN = 32
CMP_STRIDE = 16
CMP_HIDDEN = 256
SEL_LEN = 64
SEL_SHIFT = 6
SEL_TOPN = 16
WINDOW = 512
WIN_QBLK = 128

ROPE_THETA = 500000.0
ROPE_DIM = 32
ROPE_HALF = 16
MLP_HIDDEN = 4 * D_MODEL
NORM_EPS = 1e-6
NEG_INF = -1e30
FORCED_SCORE = 1e6
KV_PARTS = 2 * NSA_KV_HEADS
KV_ROW = KV_PARTS * NSA_DH
PAGES_PER_STEP = 16

VMEM_LIMIT_BYTES = 56 * 1024 * 1024
LANES = 128

COL_MG = 0
COL_GV = 4096
COL_GG = 6144
COL_NQ = 8192
COL_GQ = 10240
COL_GK = 11264
COL_NKV = 12288
COL_SM = 13824
N_PROJ = 14336
SM_GATE0 = GLA_GATE_RANK
SM_USED = GLA_GATE_RANK + 3 * NSA_HEADS

_SRC_SEGMENTS = ((9792, 13888), (2048, 4096), (4096, 6144), (6160, 8208), (0, 1024), (1024, 2048),
                 (8208, 9744), (6144, 6160), (9744, 9792))

SAMPLE_ROWS = 8


def _cparams(sem):
    return pltpu.CompilerParams(dimension_semantics=sem, vmem_limit_bytes=VMEM_LIMIT_BYTES)


def _rms(x, g):
    return x * lax.rsqrt(jnp.mean(x * x, axis=-1, keepdims=True) + NORM_EPS) * g


def _dot(a, b, precision=None):
    return jnp.dot(a, b, preferred_element_type=F32, precision=precision)


def _dot_nt(a, b, precision=None):
    return lax.dot_general(a, b, (((1,), (1,)), ((), ())), preferred_element_type=F32, precision=precision)


def _dot_tn(a, b, precision=None):
    return lax.dot_general(a, b, (((0,), (0,)), ((), ())), preferred_element_type=F32, precision=precision)


def _split_bf16(x):
    hi = x.astype(BF16)
    return hi, (x - hi.astype(F32)).astype(BF16)


def _dot_nt_3x(a, b):
    ah, al = _split_bf16(a)
    bh, bl = _split_bf16(b)
    return (_dot_nt(jnp.concatenate([ah, al], axis=1), jnp.concatenate([bh, bh], axis=1))
            + _dot_nt(ah, bl))


def _dot_3x(a, b):
    ah, al = _split_bf16(a)
    bh, bl = _split_bf16(b)
    return (_dot(jnp.concatenate([ah, al], axis=1), jnp.concatenate([bh, bh], axis=0))
            + _dot(ah, bl))


def _iota(shape, dim):
    return lax.broadcasted_iota(jnp.int32, shape, dim)


def _masked_softmax(s, valid):
    s = jnp.where(valid, s, NEG_INF)
    m = jnp.max(s, axis=-1, keepdims=True)
    p = jnp.where(valid, jnp.exp(s - m), 0.0)
    return p / jnp.maximum(jnp.sum(p, axis=-1, keepdims=True), 1e-30)


def _norm_matmul_kernel(x_ref, g_ref, w_ref, o_ref, h_ref):
    @pl.when(pl.program_id(1) == 0)
    def _():
        h_ref[...] = _rms(x_ref[...], g_ref[...]).astype(BF16)

    o_ref[...] = _dot(h_ref[...], w_ref[...]).astype(o_ref.dtype)


def norm_matmul(x, g, w, layer, tm, tn, out_dtype):
    m, k = x.shape
    n = w.shape[2]
    return pl.pallas_call(
        _norm_matmul_kernel,
        grid=(m // tm, n // tn),
        in_specs=[pl.BlockSpec((tm, k), lambda i, j: (i, 0)),
                  pl.BlockSpec((1, k), lambda i, j: (0, 0)),
                  pl.BlockSpec((None, k, tn), lambda i, j: (layer, 0, j))],
        out_specs=pl.BlockSpec((tm, tn), lambda i, j: (i, j)),
        out_shape=jax.ShapeDtypeStruct((m, n), out_dtype),
        scratch_shapes=[pltpu.VMEM((tm, k), BF16)],
        compiler_params=_cparams(("parallel", "arbitrary")),
        name="norm_matmul",
    )(x, g, w)


def _merge_wo_kernel(x_ref, mg0_ref, mg1_ref, oa_ref, ob_ref, g_ref, wo_ref, o_ref):
    ob = jnp.concatenate([ob_ref[h].astype(F32) for h in range(NSA_HEADS)], axis=1)
    a = (jax.nn.sigmoid(mg0_ref[...].astype(F32)) * oa_ref[...].astype(F32)
         + jax.nn.sigmoid(mg1_ref[...].astype(F32)) * ob)
    m = _dot(a.astype(BF16), wo_ref[...])
    o_ref[...] = x_ref[...] + _rms(m, g_ref[...])


def merge_wo(x, proj, o_a, o_b_hm, g, wo, layer, tm):
    m, d = x.shape
    return pl.pallas_call(
        _merge_wo_kernel,
        grid=(m // tm,),
        in_specs=[pl.BlockSpec((tm, d), lambda i: (i, 0)),
                  pl.BlockSpec((tm, d), lambda i: (i, COL_MG // D_MODEL)),
                  pl.BlockSpec((tm, d), lambda i: (i, COL_MG // D_MODEL + 1)),
                  pl.BlockSpec((tm, d), lambda i: (i, 0)),
                  pl.BlockSpec((NSA_HEADS, tm, NSA_DH), lambda i: (0, i, 0)),
                  pl.BlockSpec((1, d), lambda i: (0, 0)),
                  pl.BlockSpec((None, d, d), lambda i: (layer, 0, 0))],
        out_specs=pl.BlockSpec((tm, d), lambda i: (i, 0)),
        out_shape=jax.ShapeDtypeStruct((m, d), F32),
        compiler_params=_cparams(("parallel",)),
        name="merge_wo",
    )(x, proj, proj, o_a, o_b_hm, g, wo)


def _mlp_kernel(x_ref, g2_ref, g3_ref, wu_ref, wd_ref, o_ref, h_ref):
    j = pl.program_id(1)

    @pl.when(j == 0)
    def _():
        h_ref[...] = _rms(x_ref[...], g2_ref[...]).astype(BF16)
        o_ref[...] = jnp.zeros_like(o_ref)

    u = _dot(h_ref[...], wu_ref[...])
    u = jnp.square(jnp.maximum(u, 0.0)).astype(BF16)
    o_ref[...] += _dot(u, wd_ref[...])

    @pl.when(j == pl.num_programs(1) - 1)
    def _():
        o_ref[...] = x_ref[...] + _rms(o_ref[...], g3_ref[...])


def mlp(x, g2, g3, wu, wd, layer, tm, th):
    m, d = x.shape
    hid = wu.shape[2]
    return pl.pallas_call(
        _mlp_kernel,
        grid=(m // tm, hid // th),
        in_specs=[pl.BlockSpec((tm, d), lambda i, j: (i, 0)),
                  pl.BlockSpec((1, d), lambda i, j: (0, 0)),
                  pl.BlockSpec((1, d), lambda i, j: (0, 0)),
                  pl.BlockSpec((None, d, th), lambda i, j: (layer, 0, j)),
                  pl.BlockSpec((None, th, d), lambda i, j: (layer, j, 0))],
        out_specs=pl.BlockSpec((tm, d), lambda i, j: (i, 0)),
        out_shape=jax.ShapeDtypeStruct((m, d), F32),
        scratch_shapes=[pltpu.VMEM((tm, d), BF16)],
        compiler_params=_cparams(("parallel", "arbitrary")),
        name="mlp",
    )(x, g2, g3, wu, wd)


def _gla_kernel(*refs, rows_in, n_inner, n_valid, has_s0):
    q_ref, k_ref, v_ref, gg_ref, sm_ref, wa2_ref, ba_ref, gn_ref, s0_ref = refs[:9]
    o_ref, sout_ref, st_ref = refs[-3:]
    c = pl.program_id(1)
    C = GLA_CHUNK

    @pl.when(c == 0)
    def _():
        for h in range(GLA_HEADS):
            if has_s0:
                st_ref[h] = s0_ref[h].T
            else:
                st_ref[h] = jnp.zeros((GLA_DV, GLA_DK), F32)

    rp = n_inner * C

    def load(ref):
        x = ref[...].astype(F32)
        if rows_in == C:
            return x
        assert n_inner == 1
        return jnp.concatenate([x, jnp.zeros((C - rows_in, x.shape[1]), x.dtype)], axis=0)

    row = _iota((rp, 1), 0)
    col = _iota((1, rp), 1)
    same_chunk = (row >> GLA_CHUNK_SHIFT) == (col >> GLA_CHUNK_SHIFT)
    causal = same_chunk & (row >= col)
    tril = causal.astype(BF16)
    live = (row & (C - 1)) < n_valid

    ga_h, ga_l = _split_bf16(load(sm_ref)[:, :GLA_GATE_RANK])
    w_h, w_l = _split_bf16(wa2_ref[...])
    z = _dot(ga_h, w_h) + _dot(ga_l, w_h) + _dot(ga_h, w_l) + ba_ref[...]
    log_a = (jnp.minimum(z, 0.0) - jnp.log1p(jnp.exp(-jnp.abs(z)))) / GLA_TAU
    k = load(k_ref)
    if n_valid < C:
        log_a = jnp.where(live, log_a, 0.0)
        k = jnp.where(live, k, 0.0)
    la_h, la_l = _split_bf16(log_a)
    b = _dot(tril, la_h) + _dot(tril, la_l)
    b_last = [b[ci * C + C - 1:ci * C + C, :] for ci in range(n_inner)]
    q_dec = (load(q_ref) * (GLA_DK ** -0.5) * jnp.exp(b)).astype(BF16)
    k_dec = (k * jnp.exp(-b)).astype(BF16)
    k_end = jnp.concatenate([k[ci * C:(ci + 1) * C] * jnp.exp(b_last[ci] - b[ci * C:(ci + 1) * C])
                             for ci in range(n_inner)], axis=0).astype(BF16)
    decay = [jnp.exp(r) for r in b_last]
    vb = load(v_ref).astype(BF16)

    heads = range(GLA_HEADS)
    kq = [slice(h * GLA_DK, (h + 1) * GLA_DK) for h in heads]
    kv = [slice(h * GLA_DV, (h + 1) * GLA_DV) for h in heads]
    att = [jnp.where(causal, _dot_nt(q_dec[:, kq[h]], k_dec[:, kq[h]]), 0.0).astype(BF16) for h in heads]
    intra = [_dot(att[h], vb[:, kv[h]]) for h in heads]
    upd = [[_dot_tn(vb[ci * C:(ci + 1) * C, kv[h]], k_end[ci * C:(ci + 1) * C, kq[h]]) for h in heads]
           for ci in range(n_inner)]
    for ci in range(n_inner):
        cr = slice(ci * C, (ci + 1) * C)
        rs = pl.ds(ci * rows_in, rows_in)
        st = [st_ref[h] for h in heads]
        o = [intra[h][cr] + _dot_nt(q_dec[cr, kq[h]], st[h].astype(BF16)) for h in heads]
        for h in heads:
            st_ref[h] = st[h] * decay[ci][:, kq[h]] + upd[ci][h]
        for h in heads:
            o_n = _rms(o[h], gn_ref[...])
            gg = gg_ref[rs, kv[h]].astype(F32)
            o_ref[rs, kv[h]] = (o_n[:rows_in] * (gg * jax.nn.sigmoid(gg))).astype(o_ref.dtype)

    @pl.when(c == pl.num_programs(1) - 1)
    def _():
        for h in range(GLA_HEADS):
            sout_ref[h] = st_ref[h].T


def gla(proj, wa2, ba, gn, s0, layer, depth, stacked, batch, rows_per_batch, rows_in, n_inner, n_valid):
    m = proj.shape[0]
    r = rows_in * n_inner
    n_steps = rows_per_batch // r
    has_s0 = s0 is not None
    if s0 is None:
        s0 = jnp.zeros((1, 1, GLA_HEADS, GLA_DK, GLA_DV), F32)
        s0_map = lambda b, c: (0, 0, 0, 0, 0)
    else:
        s0_map = lambda b, c: (layer, b, 0, 0, 0)
    wk, wv = GLA_HEADS * GLA_DK, GLA_HEADS * GLA_DV
    rowmap = lambda off: (lambda b, c: (b * n_steps + c, off))
    state_blk = (None, GLA_HEADS, GLA_DK, GLA_DV)
    kern = functools.partial(_gla_kernel, rows_in=rows_in, n_inner=n_inner, n_valid=n_valid, has_s0=has_s0)
    in_specs = [pl.BlockSpec((r, wk), rowmap(COL_GQ // wk)),
                pl.BlockSpec((r, wk), rowmap(COL_GK // wk)),
                pl.BlockSpec((r, wv), rowmap(COL_GV // wv)),
                pl.BlockSpec((r, wv), rowmap(COL_GG // wv)),
                pl.BlockSpec((r, LANES), rowmap(COL_SM // LANES)),
                pl.BlockSpec((GLA_GATE_RANK, wk), lambda b, c: (0, 0)),
                pl.BlockSpec((1, wk), lambda b, c: (0, 0)),
                pl.BlockSpec((1, GLA_DV), lambda b, c: (0, 0)),
                pl.BlockSpec((None,) + state_blk, s0_map)]
    args = (proj, proj, proj, proj, proj, wa2, ba, gn, s0)
    aliases = {}
    if stacked is not None:
        in_specs.append(pl.BlockSpec(memory_space=pl.ANY))
        aliases = {len(args): 1}
        args += (stacked,)
    return pl.pallas_call(
        kern,
        grid=(batch, n_steps),
        in_specs=in_specs,
        out_specs=[pl.BlockSpec((r, wv), rowmap(0)),
                   pl.BlockSpec((None,) + state_blk, lambda b, c: (layer, b, 0, 0, 0))],
        out_shape=[jax.ShapeDtypeStruct((m, D_MODEL), proj.dtype),
                   jax.ShapeDtypeStruct((depth, batch, GLA_HEADS, GLA_DK, GLA_DV), F32)],
        input_output_aliases=aliases,
        scratch_shapes=[pltpu.VMEM((GLA_HEADS, GLA_DV, GLA_DK), F32)],
        compiler_params=_cparams(("parallel", "arbitrary")),
        name="gla",
    )(*args)


def _rope(x, c, sa, sb):
    return x * c + pltpu.roll(x, LANES - ROPE_HALF, 1) * sa + pltpu.roll(x, ROPE_HALF, 1) * sb


def _nsa_prep_kernel(*refs, with_q):
    if with_q:
        nq_ref, nkv_ref, c_ref, sa_ref, sb_ref, q_ref, cmp_ref, sel_ref, win_ref = refs
    else:
        nkv_ref, c_ref, sa_ref, sb_ref, cmp_ref, sel_ref, win_ref = refs
    c, sa, sb = c_ref[...], sa_ref[...], sb_ref[...]
    if with_q:
        for h in range(NSA_HEADS):
            q_ref[h] = _rope(nq_ref[:, h * NSA_DH:(h + 1) * NSA_DH].astype(F32), c, sa, sb)
    for s, out in enumerate((cmp_ref, sel_ref, win_ref)):
        base = s * KV_ROW
        for hh in range(NSA_KV_HEADS):
            lo = hh * NSA_DH
            out[:, lo:lo + NSA_DH] = _rope(nkv_ref[:, base + lo:base + lo + NSA_DH].astype(F32), c, sa, sb)
        half = NSA_KV_HEADS * NSA_DH
        out[:, half:] = nkv_ref[:, base + half:base + KV_ROW].astype(F32)


def nsa_prep(proj, tabs, tr, with_q):
    m = proj.shape[0]
    n_tab = tabs[0].shape[0] // tr
    tab_spec = pl.BlockSpec((tr, LANES), lambda i: (i % n_tab, 0))
    nkv_w = 3 * KV_ROW
    row_spec = pl.BlockSpec((tr, KV_ROW), lambda i: (i, 0))
    row_shape = jax.ShapeDtypeStruct((m, KV_ROW), F32)
    in_specs = [pl.BlockSpec((tr, nkv_w), lambda i: (i, COL_NKV // nkv_w)), tab_spec, tab_spec, tab_spec]
    out_specs, out_shape, args = [row_spec] * 3, [row_shape] * 3, (proj,) + tuple(tabs)
    if with_q:
        in_specs = [pl.BlockSpec((tr, D_MODEL), lambda i: (i, COL_NQ // D_MODEL))] + in_specs
        out_specs = [pl.BlockSpec((NSA_HEADS, tr, NSA_DH), lambda i: (0, i, 0))] + out_specs
        out_shape = [jax.ShapeDtypeStruct((NSA_HEADS, m, NSA_DH), F32)] + out_shape
        args = (proj,) + args
    return pl.pallas_call(
        functools.partial(_nsa_prep_kernel, with_q=with_q),
        grid=(m // tr,),
        in_specs=in_specs,
        out_specs=out_specs,
        out_shape=out_shape,
        compiler_params=_cparams(("parallel",)),
        name="nsa_prep",
    )(*args)


def _nsa_prep_prompt_kernel(*refs):
    nkv_ref, c_ref, sa_ref, sb_ref = refs[:4]
    slabs, stacked = refs[-6:-3], refs[-3:]
    c, sa, sb = c_ref[...], sa_ref[...], sb_ref[...]
    tr = nkv_ref.shape[0]
    for s in range(3):
        for part in range(KV_PARTS):
            lo = part * NSA_DH
            v = nkv_ref[:, s * KV_ROW + lo:s * KV_ROW + lo + NSA_DH].astype(F32)
            if part < NSA_KV_HEADS:
                v = _rope(v, c, sa, sb)
            slabs[s][:, lo:lo + NSA_DH] = v
            stacked[s][pl.ds(part, tr, stride=KV_PARTS), :] = v


def nsa_prep_prompt(proj, tabs, tr, layer, depth, stacked):
    m = proj.shape[0]
    n_tab = tabs[0].shape[0] // tr
    tab_spec = pl.BlockSpec((tr, LANES), lambda i: (i % n_tab, 0))
    nkv_w = 3 * KV_ROW
    row_spec = pl.BlockSpec((tr, KV_ROW), lambda i: (i, 0))
    in_specs = [pl.BlockSpec((tr, nkv_w), lambda i: (i, COL_NKV // nkv_w)), tab_spec, tab_spec, tab_spec]
    args = (proj,) + tuple(tabs)
    aliases = {}
    if stacked is not None:
        in_specs += [pl.BlockSpec(memory_space=pl.ANY)] * 3
        aliases = {len(args) + k: 3 + k for k in range(3)}
        args += tuple(stacked)
    st_spec = pl.BlockSpec((None, tr * KV_PARTS, NSA_DH), lambda i: (layer, i, 0))
    out = pl.pallas_call(
        _nsa_prep_prompt_kernel,
        grid=(m // tr,),
        in_specs=in_specs,
        out_specs=[row_spec] * 3 + [st_spec] * 3,
        out_shape=[jax.ShapeDtypeStruct((m, KV_ROW), F32)] * 3
                  + [jax.ShapeDtypeStruct((depth, m * KV_PARTS, NSA_DH), F32)] * 3,
        input_output_aliases=aliases,
        compiler_params=_cparams(("parallel",)),
        name="nsa_prep_prompt",
    )(*args)
    return out[:3], out[3:]


def rope_tables(pos):
    inv = 1.0 / (ROPE_THETA ** (jnp.arange(ROPE_HALF, dtype=F32) / ROPE_HALF))
    ang = pos.astype(F32)[:, None] * inv[None, :]
    cos, sin = jnp.cos(ang), jnp.sin(ang)
    n = pos.shape[0]
    rest = LANES - ROPE_DIM
    c = jnp.concatenate([cos, cos, jnp.ones((n, rest), F32)], axis=1)
    sa = jnp.concatenate([-sin, jnp.zeros((n, LANES - ROPE_HALF), F32)], axis=1)
    sb = jnp.concatenate([jnp.zeros((n, ROPE_HALF), F32), sin, jnp.zeros((n, rest), F32)], axis=1)
    return c, sa, sb


_HALVES = (slice(0, CMP_STRIDE), slice(CMP_STRIDE, CMP_LEN))


def _pe_mix_sums(pe_ref, mix_ref):
    return [[jnp.sum(pe_ref[kv][h] * mix_ref[kv][h], axis=0, keepdims=True) for h in _HALVES] for kv in range(2)]


SUBLANES = 8


def _sum_rows_8(p):
    sub = _iota((SUBLANES, LANES), 0)
    n = SUBLANES
    stage = list(p)
    for shift, bit in ((4, 4), (2, 2), (1, 1)):
        nxt = []
        half = len(stage) // 2
        for c in range(half):
            lo = stage[c] + pltpu.roll(stage[c], n - shift, 0)
            hi = stage[c + half] + pltpu.roll(stage[c + half], shift, 0)
            nxt.append(jnp.where((sub & bit) == 0, lo, hi))
        stage = nxt
    return stage[0]


def _half_sums(xs, mix, consts):
    nch = xs.shape[0]
    out = []
    for h, c in zip(_HALVES, consts):
        y = xs * mix[h]
        y = y[:, :SUBLANES, :] + y[:, SUBLANES:, :]
        groups = [_sum_rows_8([y[g + i] for i in range(SUBLANES)]) for g in range(0, nch, SUBLANES)]
        out.append(jnp.concatenate(groups, axis=0) + c)
    return out


def _chunk_sums_kernel(x_ref, pe_ref, mix_ref, a_ref, b_ref):
    nch = x_ref.shape[0] // CMP_STRIDE
    consts = _pe_mix_sums(pe_ref, mix_ref)
    for kv in range(2):
        for hh in range(NSA_KV_HEADS):
            lo = (kv * NSA_KV_HEADS + hh) * NSA_DH
            xs = x_ref[:, lo:lo + NSA_DH].reshape(nch, CMP_STRIDE, NSA_DH)
            a_ref[:, lo:lo + NSA_DH], b_ref[:, lo:lo + NSA_DH] = _half_sums(xs, mix_ref[kv], consts[kv])


def _page_specs(layer):
    def spec(i):
        return pl.BlockSpec((None, None, PAGE_SIZE * KV_PARTS, NSA_DH),
                            lambda b, s, pt: (layer, pt[b, s * PAGES_PER_STEP + i], 0, 0))
    return [spec(i) for i in range(PAGES_PER_STEP)]


def _page_part(page_ref, part):
    return page_ref[pl.ds(part, PAGE_SIZE, stride=KV_PARTS), :]


def _chunk_sums_paged_kernel(pt_ref, *refs):
    page_refs, (pe_ref, mix_ref, a_ref, b_ref) = refs[:PAGES_PER_STEP], refs[PAGES_PER_STEP:]
    nch = PAGE_SIZE // CMP_STRIDE
    consts = _pe_mix_sums(pe_ref, mix_ref)
    for i, page_ref in enumerate(page_refs):
        rows = slice(i * nch, (i + 1) * nch)
        for kv in range(2):
            for hh in range(NSA_KV_HEADS):
                part = kv * NSA_KV_HEADS + hh
                lo = part * NSA_DH
                xs = _page_part(page_ref, part).reshape(nch, CMP_STRIDE, NSA_DH)
                a_ref[rows, lo:lo + NSA_DH], b_ref[rows, lo:lo + NSA_DH] = _half_sums(xs, mix_ref[kv], consts[kv])


def chunk_sums(rows, pe, mix, tr):
    m = rows.shape[0]
    nch = tr // CMP_STRIDE
    full = lambda i: (0, 0, 0)
    return pl.pallas_call(
        _chunk_sums_kernel,
        grid=(m // tr,),
        in_specs=[pl.BlockSpec((tr, KV_ROW), lambda i: (i, 0)),
                  pl.BlockSpec((2, CMP_LEN, NSA_DH), full),
                  pl.BlockSpec((2, CMP_LEN, NSA_DH), full)],
        out_specs=[pl.BlockSpec((nch, KV_ROW), lambda i: (i, 0))] * 2,
        out_shape=[jax.ShapeDtypeStruct((m // CMP_STRIDE, KV_ROW), F32)] * 2,
        compiler_params=_cparams(("parallel",)),
        name="chunk_sums",
    )(rows, pe, mix)


def chunk_sums_paged(cache, layer, page_table, pe, mix):
    batch, n_pages = page_table.shape
    nch = PAGES_PER_STEP * PAGE_SIZE // CMP_STRIDE
    n_steps = n_pages // PAGES_PER_STEP
    full = lambda b, s, pt: (0, 0, 0)
    gs = pltpu.PrefetchScalarGridSpec(
        num_scalar_prefetch=1,
        grid=(batch, n_steps),
        in_specs=_page_specs(layer) + [pl.BlockSpec((2, CMP_LEN, NSA_DH), full),
                                       pl.BlockSpec((2, CMP_LEN, NSA_DH), full)],
        out_specs=[pl.BlockSpec((nch, KV_ROW), lambda b, s, pt: (b * n_steps + s, 0))] * 2,
    )
    return pl.pallas_call(
        _chunk_sums_paged_kernel,
        grid_spec=gs,
        out_shape=[jax.ShapeDtypeStruct((batch * n_steps * nch, KV_ROW), F32)] * 2,
        compiler_params=_cparams(("parallel", "arbitrary")),
        name="chunk_sums_paged",
    )(page_table, *([cache] * PAGES_PER_STEP), pe, mix)


def _cmp_mlp_kernel(a_ref, b_ref, w1_ref, w2_ref, o_ref):
    h = a_ref[...] + b_ref[...]
    for kv in range(2):
        for hh in range(NSA_KV_HEADS):
            lo = (kv * NSA_KV_HEADS + hh) * NSA_DH
            y = jax.nn.gelu(_dot_3x(h[:, lo:lo + NSA_DH], w1_ref[kv]))
            o_ref[:, lo:lo + NSA_DH] = _dot_3x(y, w2_ref[kv])


def cmp_mlp(a, b_shift, w1, w2, tr):
    m = a.shape[0]
    return pl.pallas_call(
        _cmp_mlp_kernel,
        grid=(m // tr,),
        in_specs=[pl.BlockSpec((tr, KV_ROW), lambda i: (i, 0)),
                  pl.BlockSpec((tr, KV_ROW), lambda i: (i, 0)),
                  pl.BlockSpec((2, NSA_DH, CMP_HIDDEN), lambda i: (0, 0, 0)),
                  pl.BlockSpec((2, CMP_HIDDEN, NSA_DH), lambda i: (0, 0, 0))],
        out_specs=pl.BlockSpec((tr, KV_ROW), lambda i: (i, 0)),
        out_shape=jax.ShapeDtypeStruct((m, KV_ROW), F32),
        compiler_params=_cparams(("parallel",)),
        name="cmp_mlp",
    )(a, b_shift, w1, w2)


def compressed_kv(a, b, batch, w1, w2, tr):
    nch = a.shape[0] // batch
    b3 = b.reshape(batch, nch, KV_ROW)
    b_shift = jnp.concatenate([b3[:, 1:], jnp.zeros((batch, 1, KV_ROW), F32)], axis=1).reshape(batch * nch, KV_ROW)
    return cmp_mlp(a, b_shift, w1, w2, tr)


def _select_topn(imp, n_top):
    j = _iota(imp.shape, 1)
    big = jnp.int32(imp.shape[1])
    sel = jnp.zeros(imp.shape, F32)
    for _ in range(n_top):
        m = jnp.max(imp, axis=-1, keepdims=True)
        idx = jnp.min(jnp.where(imp == m, j, big), axis=-1, keepdims=True)
        hit = j == idx
        sel = jnp.where(hit, 1.0, sel)
        imp = jnp.where(hit, NEG_INF, imp)
    return sel


def _importance(p_grp, qpos, n_cmp_pad, n_slc, width):
    n_r = _iota((n_cmp_pad, 1), 0) * CMP_STRIDE
    j_c = _iota((1, width), 1)
    overlap = ((n_r < (j_c + 1) * SEL_LEN) & (n_r + CMP_LEN > j_c * SEL_LEN)).astype(BF16)
    p_hi, p_lo = _split_bf16(p_grp)
    imp = _dot(p_hi, overlap) + _dot(p_lo, overlap)
    cur = qpos >> SEL_SHIFT
    forced = (j_c == 0) | (j_c == cur) | (j_c == cur - 1)
    imp = jnp.where(forced, FORCED_SCORE, imp)
    imp = jnp.where(j_c > cur, -1.0, imp)
    return jnp.where(j_c >= n_slc, -2.0, imp)


LOG2E = 1.4426950408889634


def _nsa_prompt_kernel(nq_ref, c_ref, sa_ref, sb_ref, kc_ref, vc_ref, ks_ref, vs_ref, kw_ref, vw_ref, sm_ref, o_ref,
                       qf_ref, qb_ref, s_ref, p_ref, m_ref, l_ref, a_ref, acc_ref, ob_ref, *, Q, T, TK):
    G = NSA_GROUP
    hk = pl.program_id(1)
    qi = pl.program_id(2)
    q0 = qi * Q
    scale = NSA_DH ** -0.5
    n_cmp_pad = kc_ref.shape[0]
    n_slc = T // SEL_LEN
    qpos = q0 + _iota((Q, 1), 0)
    c, sa, sb = c_ref[...], sa_ref[...], sb_ref[...]
    for g in range(G):
        qg = _rope(nq_ref[:, g * NSA_DH:(g + 1) * NSA_DH].astype(F32), c, sa, sb)
        qf_ref[g * Q:(g + 1) * Q, :] = qg
        qb_ref[g * Q:(g + 1) * Q, :] = (qg * (scale * LOG2E)).astype(BF16)
    qs = qf_ref[...]

    ng = jax.nn.sigmoid(sm_ref[...].astype(F32))

    def gate(g, c3):
        lane0 = SM_GATE0 + 3 * g + c3
        lane1 = lane0 + 3 * G
        return jnp.where(hk == 0, ng[:, lane0:lane0 + 1], ng[:, lane1:lane1 + 1])

    def softmax_tile(kb, vb, bias, width):
        s_ref[:, :width] = _dot_nt(qb_ref[...], kb)
        for g in range(G):
            rows = pl.ds(g * Q, Q)
            s = s_ref[rows, :width] + bias
            m_old = m_ref[rows, :]
            m_new = jnp.maximum(m_old, jnp.max(s, axis=-1, keepdims=True))
            alpha = jnp.exp2(m_old - m_new)
            p = jnp.exp2(s - jnp.tile(m_new, (1, width // LANES)))
            l_ref[rows, :] = alpha * l_ref[rows, :] + jnp.sum(p, axis=-1, keepdims=True)
            a_ref[rows, :] = alpha
            m_ref[rows, :] = m_new
            p_ref[rows, :width] = p.astype(BF16)
        return _dot(p_ref[:, :width], vb)

    def reset():
        m_ref[...] = jnp.full_like(m_ref, NEG_INF)
        l_ref[...] = jnp.zeros_like(l_ref)

    cmp_valid = _iota((1, n_cmp_pad), 1) * CMP_STRIDE + (CMP_LEN - 1) <= qpos
    p_grp = jnp.zeros((Q, n_cmp_pad), F32)
    s_cmp = _dot_nt_3x(qs, kc_ref[...]) * scale
    vcb = vc_ref[...].astype(BF16)
    for g in range(G):
        p = _masked_softmax(s_cmp[g * Q:(g + 1) * Q], cmp_valid)
        p_grp = p_grp + p
        ob_ref[g] = gate(g, 0) * _dot(p.astype(BF16), vcb)

    ks0 = pl.multiple_of(jnp.maximum(q0 - WINDOW, 0), LANES)
    wlen = WINDOW + Q
    wpos = ks0 + _iota((1, wlen), 1)
    win_bias = jnp.where((wpos <= qpos) & (wpos > qpos - WINDOW), 0.0, NEG_INF)
    reset()
    pv = softmax_tile(kw_ref[pl.ds(ks0, wlen), :].astype(BF16), vw_ref[pl.ds(ks0, wlen), :].astype(BF16),
                      win_bias, wlen)
    for g in range(G):
        rows = pl.ds(g * Q, Q)
        ob_ref[g] = ob_ref[g] + gate(g, 2) * (pv[g * Q:(g + 1) * Q] / jnp.maximum(l_ref[rows, :], 1e-30))

    n_top = min(SEL_TOPN, n_slc)
    j_r = _iota((n_slc, 1), 0)
    n_c = _iota((1, n_cmp_pad), 1) * CMP_STRIDE
    overlap_t = ((n_c < (j_r + 1) * SEL_LEN) & (n_c + CMP_LEN > j_r * SEL_LEN)).astype(BF16)
    p_hi, p_lo = _split_bf16(p_grp)
    imp = _dot_nt(overlap_t, p_hi) + _dot_nt(overlap_t, p_lo)
    cur = (q0 + _iota((1, Q), 1)) >> SEL_SHIFT
    imp = jnp.where((j_r == 0) | (j_r == cur) | (j_r == cur - 1), FORCED_SCORE, imp)
    imp = jnp.where(j_r > cur, -1.0, imp)
    rank = jnp.zeros((n_slc, Q), F32)
    for jp in range(n_slc):
        other = imp[jp:jp + 1, :]
        beats = (other > imp) | ((other == imp) & (j_r > jp))
        rank = rank + jnp.where(beats, 1.0, 0.0)
    sel_t = jnp.where(rank < n_top, 1.0, 0.0).astype(BF16)

    reset()
    acc_ref[...] = jnp.zeros_like(acc_ref)

    def key_tile(kt, carry):
        k0 = pl.multiple_of(kt * TK, TK)
        kpos = k0 + _iota((1, TK), 1)
        expand = (j_r == (kpos >> SEL_SHIFT)).astype(BF16)
        keymask = _dot_tn(sel_t, expand)
        bias = jnp.where((keymask > 0.5) & (kpos <= qpos), 0.0, NEG_INF)
        pv = softmax_tile(ks_ref[pl.ds(k0, TK), :].astype(BF16), vs_ref[pl.ds(k0, TK), :].astype(BF16), bias, TK)
        acc_ref[...] = a_ref[...] * acc_ref[...] + pv
        return carry

    lax.fori_loop(0, (q0 + Q + TK - 1) // TK, key_tile, 0)

    for g in range(G):
        rows = pl.ds(g * Q, Q)
        o_sel = acc_ref[rows, :] / jnp.maximum(l_ref[rows, :], 1e-30)
        o_ref[g] = (ob_ref[g] + gate(g, 1) * o_sel).astype(o_ref.dtype)


def nsa_prompt(tabs, kvc, sel_rows, win_rows, proj, batch, T, Q):
    m = batch * T
    nq = T // Q
    n_cmp_pad = kvc.shape[0] // batch
    G = NSA_GROUP
    gw = G * NSA_DH
    kern = functools.partial(_nsa_prompt_kernel, Q=Q, T=T, TK=512)
    kcol = lambda off: (lambda b, hk, qi: (b, off + hk))
    tab_spec = pl.BlockSpec((Q, LANES), lambda b, hk, qi: (qi, 0))
    return pl.pallas_call(
        kern,
        grid=(batch, NSA_KV_HEADS, nq),
        in_specs=[pl.BlockSpec((Q, gw), lambda b, hk, qi: (b * nq + qi, COL_NQ // gw + hk)),
                  tab_spec, tab_spec, tab_spec,
                  pl.BlockSpec((n_cmp_pad, NSA_DH), kcol(0)),
                  pl.BlockSpec((n_cmp_pad, NSA_DH), kcol(NSA_KV_HEADS)),
                  pl.BlockSpec((T, NSA_DH), kcol(0)),
                  pl.BlockSpec((T, NSA_DH), kcol(NSA_KV_HEADS)),
                  pl.BlockSpec((T, NSA_DH), kcol(0)),
                  pl.BlockSpec((T, NSA_DH), kcol(NSA_KV_HEADS)),
                  pl.BlockSpec((Q, LANES), lambda b, hk, qi: (b * nq + qi, COL_SM // LANES))],
        out_specs=pl.BlockSpec((G, Q, NSA_DH), lambda b, hk, qi: (hk, b * nq + qi, 0)),
        out_shape=jax.ShapeDtypeStruct((NSA_HEADS, m, NSA_DH), proj.dtype),
        scratch_shapes=[pltpu.VMEM((G * Q, NSA_DH), F32), pltpu.VMEM((G * Q, NSA_DH), BF16),
                        pltpu.VMEM((G * Q, WINDOW + Q), F32), pltpu.VMEM((G * Q, WINDOW + Q), BF16),
                        pltpu.VMEM((G * Q, LANES), F32), pltpu.VMEM((G * Q, LANES), F32),
                        pltpu.VMEM((G * Q, LANES), F32), pltpu.VMEM((G * Q, NSA_DH), F32),
                        pltpu.VMEM((G, Q, NSA_DH), F32)],
        compiler_params=_cparams(("parallel", "parallel", "arbitrary")),
        name="nsa_prompt",
    )(proj, *tabs, kvc, kvc, sel_rows, sel_rows, win_rows, win_rows, proj)


SEL_W = 384


def _sample_cmp_select_kernel(q_ref, kvc_ref, ocmp_ref, selm_ref, *, n_slc):
    G, R = NSA_GROUP, SAMPLE_ROWS
    scale = NSA_DH ** -0.5
    n_cmp_pad = kvc_ref.shape[0]
    qpos_s = PAST_LEN + (_iota((G * R, 1), 0) & (R - 1))
    end = _iota((1, n_cmp_pad), 1) * CMP_STRIDE + (CMP_LEN - 1)
    p_grp = []
    for hk in range(NSA_KV_HEADS):
        qs = q_ref[hk * G:(hk + 1) * G].reshape(G * R, NSA_DH)
        kc = kvc_ref[:, hk * NSA_DH:(hk + 1) * NSA_DH]
        vc = kvc_ref[:, (NSA_KV_HEADS + hk) * NSA_DH:(NSA_KV_HEADS + hk + 1) * NSA_DH]
        p = _masked_softmax(_dot_nt_3x(qs, kc) * scale, end <= qpos_s)
        o_cmp = _dot(p.astype(BF16), vc.astype(BF16))
        for g in range(G):
            ocmp_ref[hk * G + g] = o_cmp[g * R:(g + 1) * R]
        p_grp.append(jnp.sum(p.reshape(G, R, n_cmp_pad), axis=0))
    qpos = PAST_LEN + (_iota((NSA_KV_HEADS * R, 1), 0) & (R - 1))
    imp = _importance(jnp.concatenate(p_grp, axis=0), qpos, n_cmp_pad, n_slc, SEL_W)
    sel = _select_topn(imp, min(SEL_TOPN, n_slc))
    for hk in range(NSA_KV_HEADS):
        selm_ref[hk] = sel[hk * R:(hk + 1) * R]


def sample_cmp_select(q_hm, kvc, batch, n_slc):
    R = SAMPLE_ROWS
    n_cmp_pad = kvc.shape[0] // batch
    kern = functools.partial(_sample_cmp_select_kernel, n_slc=n_slc)
    hm_spec = pl.BlockSpec((NSA_HEADS, R, NSA_DH), lambda b: (0, b, 0))
    return pl.pallas_call(
        kern,
        grid=(batch,),
        in_specs=[hm_spec, pl.BlockSpec((n_cmp_pad, KV_ROW), lambda b: (b, 0))],
        out_specs=[hm_spec, pl.BlockSpec((None, NSA_KV_HEADS, R, SEL_W), lambda b: (b, 0, 0, 0))],
        out_shape=[jax.ShapeDtypeStruct((NSA_HEADS, batch * R, NSA_DH), F32),
                   jax.ShapeDtypeStruct((batch, NSA_KV_HEADS, R, SEL_W), F32)],
        compiler_params=_cparams(("parallel",)),
        name="sample_cmp_select",
    )(q_hm, kvc)


def _sample_attn_kernel(pt_ref, q_ref, selm_ref, *refs, n_steps):
    page_refs = refs[:PAGES_PER_STEP]
    nsel_ref, cwin_ref, nwin_ref, sm_ref, ocmp_ref, o_ref, m_ref, l_ref, acc_ref = refs[PAGES_PER_STEP:]
    G, R = NSA_GROUP, SAMPLE_ROWS
    p = pl.program_id(1)
    scale = NSA_DH ** -0.5
    t_s = _iota((G * R, 1), 0) & (R - 1)
    n_keys = PAGES_PER_STEP * PAGE_SIZE

    @pl.when(p == 0)
    def _():
        m_ref[...] = jnp.full_like(m_ref, NEG_INF)
        l_ref[...] = jnp.zeros_like(l_ref)
        acc_ref[...] = jnp.zeros_like(acc_ref)

    def q_stack(hk, mult):
        return (q_ref[hk * G:(hk + 1) * G].reshape(G * R, NSA_DH) * mult).astype(BF16)

    def update(hk, s, pr_of, v):
        m_old = m_ref[hk]
        m_new = jnp.maximum(m_old, jnp.max(s, axis=-1, keepdims=True))
        alpha = jnp.exp2(m_old - m_new)
        pr = pr_of(jnp.exp2(s - jnp.tile(m_new, (1, s.shape[1] // LANES))))
        l_ref[hk] = alpha * l_ref[hk] + jnp.sum(pr, axis=-1, keepdims=True)
        acc_ref[hk] = alpha * acc_ref[hk] + _dot(pr.astype(BF16), v)
        m_ref[hk] = m_new

    key = _iota((1, n_keys), 1)
    blk_of_key = (n_keys // SEL_LEN) * p + (key >> SEL_SHIFT)
    expand = (_iota((SEL_W, 1), 0) == blk_of_key).astype(BF16)
    hks = range(NSA_KV_HEADS)
    kpg = [jnp.concatenate([_page_part(r, hk).astype(BF16) for r in page_refs], axis=0) for hk in hks]
    vpg = [jnp.concatenate([_page_part(r, NSA_KV_HEADS + hk).astype(BF16) for r in page_refs], axis=0)
           for hk in hks]
    keymask = [_dot(selm_ref[hk].astype(BF16), expand) for hk in hks]
    s = [_dot_nt(q_stack(hk, scale * LOG2E), kpg[hk])
         + jnp.where(jnp.concatenate([keymask[hk]] * G, axis=0) > 0.5, 0.0, NEG_INF) for hk in hks]
    m_old = [m_ref[hk] for hk in hks]
    m_new = [jnp.maximum(m_old[hk], jnp.max(s[hk], axis=-1, keepdims=True)) for hk in hks]
    pr = [jnp.exp2(s[hk] - jnp.tile(m_new[hk], (1, n_keys // LANES))) for hk in hks]
    pv = [_dot(pr[hk].astype(BF16), vpg[hk]) for hk in hks]
    for hk in hks:
        alpha = jnp.exp2(m_old[hk] - m_new[hk])
        l_ref[hk] = alpha * l_ref[hk] + jnp.sum(pr[hk], axis=-1, keepdims=True)
        acc_ref[hk] = alpha * acc_ref[hk] + pv[hk]
        m_ref[hk] = m_new[hk]

    def win_part(ref, part):
        return ref[pl.ds(part, WINDOW, stride=KV_PARTS), :]

    @pl.when(p == n_steps - 1)
    def _():
        ng = jax.nn.sigmoid(sm_ref[...])
        new_blk = PAST_LEN // SEL_LEN
        zpad = jnp.zeros((LANES - R, NSA_DH), F32)
        jn = _iota((1, LANES), 1)
        for hk in range(NSA_KV_HEADS):
            klo, vlo = hk * NSA_DH, (NSA_KV_HEADS + hk) * NSA_DH
            kn = jnp.concatenate([nsel_ref[:, klo:klo + NSA_DH], zpad], axis=0)
            vn = jnp.concatenate([nsel_ref[:, vlo:vlo + NSA_DH], zpad], axis=0)
            picked = jnp.concatenate([selm_ref[hk][:, new_blk:new_blk + 1]] * G, axis=0) > 0.5
            valid = picked & (jn <= t_s) & (jn < R)
            s = jnp.where(valid, _dot_nt(q_stack(hk, scale * LOG2E), kn.astype(BF16)), NEG_INF)
            update(hk, s, lambda e: jnp.where(valid, e, 0.0), vn.astype(BF16))
            o_sel = acc_ref[hk] / jnp.maximum(l_ref[hk], 1e-30)
            qs = q_stack(hk, scale)
            kw = jnp.concatenate([win_part(cwin_ref, hk), nwin_ref[:, klo:klo + NSA_DH], zpad], axis=0)
            vw = jnp.concatenate([win_part(cwin_ref, NSA_KV_HEADS + hk), nwin_ref[:, vlo:vlo + NSA_DH], zpad], axis=0)
            iw = _iota((1, WINDOW + LANES), 1)
            wvalid = ((iw < WINDOW) & (iw > t_s)) | ((iw >= WINDOW) & (iw - WINDOW <= t_s) & (iw - WINDOW < R))
            pw = _masked_softmax(_dot_nt(qs, kw.astype(BF16)), wvalid)
            o_win = _dot(pw.astype(BF16), vw.astype(BF16))
            for g in range(G):
                h = hk * G + g
                lane = SM_GATE0 + 3 * h
                rows = slice(g * R, (g + 1) * R)
                o_ref[h] = (ng[:, lane:lane + 1] * ocmp_ref[h] + ng[:, lane + 1:lane + 2] * o_sel[rows]
                            + ng[:, lane + 2:lane + 3] * o_win[rows])


def sample_attn(q_hm, selm, cache_sel, layer, page_table, new_sel, cache_win, new_win, proj, o_cmp_hm):
    batch, n_pages = page_table.shape
    n_steps = n_pages // PAGES_PER_STEP
    G, R = NSA_GROUP, SAMPLE_ROWS
    hm_spec = pl.BlockSpec((NSA_HEADS, R, NSA_DH), lambda b, p, pt: (0, b, 0))
    row_spec = pl.BlockSpec((R, KV_ROW), lambda b, p, pt: (b, 0))
    gs = pltpu.PrefetchScalarGridSpec(
        num_scalar_prefetch=1,
        grid=(batch, n_steps),
        in_specs=[hm_spec,
                  pl.BlockSpec((None, NSA_KV_HEADS, R, SEL_W), lambda b, p, pt: (b, 0, 0, 0))]
                 + _page_specs(layer)
                 + [row_spec,
                    pl.BlockSpec((None, None, WINDOW * KV_PARTS, NSA_DH), lambda b, p, pt: (layer, b, 0, 0)),
                    row_spec,
                    pl.BlockSpec((R, LANES), lambda b, p, pt: (b, COL_SM // LANES)),
                    hm_spec],
        out_specs=hm_spec,
        scratch_shapes=[pltpu.VMEM((NSA_KV_HEADS, G * R, LANES), F32),
                        pltpu.VMEM((NSA_KV_HEADS, G * R, LANES), F32),
                        pltpu.VMEM((NSA_KV_HEADS, G * R, NSA_DH), F32)],
    )
    kern = functools.partial(_sample_attn_kernel, n_steps=n_steps)
    return pl.pallas_call(
        kern,
        grid_spec=gs,
        out_shape=jax.ShapeDtypeStruct((NSA_HEADS, batch * R, NSA_DH), F32),
        compiler_params=_cparams(("parallel", "arbitrary")),
        name="sample_attn",
    )(page_table, q_hm, selm, *([cache_sel] * PAGES_PER_STEP), new_sel, cache_win, new_win, proj, o_cmp_hm)


PACK_W = 512
PACK_SRC = PACK_W // LANES + 1
_MAIN_SEGMENTS = _SRC_SEGMENTS[:7]


def _pack_table():
    first, shift = [], []
    for lo, hi in _MAIN_SEGMENTS:
        assert (hi - lo) % PACK_W == 0
        for c in range(lo, hi, PACK_W):
            first.append(c // LANES)
            shift.append(c % LANES)
    assert len(first) * PACK_W == COL_SM
    n = N_PROJ // PACK_W
    first += [0] * (n - len(first))
    shift += [0] * (n - len(shift))
    return np.array([first, shift], np.int32)


def _pack_kernel(tab_ref, *refs):
    srcs, (ga_ref, ng_ref, o_ref) = refs[:PACK_SRC], refs[PACK_SRC:]
    j = pl.program_id(1)
    n_main = COL_SM // PACK_W
    shift = tab_ref[1, j]
    lane = _iota((1, LANES), 1)

    @pl.when(j >= n_main)
    def _():
        sm = jnp.where(lane < GLA_GATE_RANK, ga_ref[...], jnp.where(lane < SM_USED, ng_ref[...], 0.0))
        o_ref[:, :LANES] = sm.astype(BF16)
        o_ref[:, LANES:] = jnp.zeros((o_ref.shape[0], PACK_W - LANES), BF16)

    for sv in sorted({lo % LANES for lo, _ in _MAIN_SEGMENTS}):
        @pl.when((j < n_main) & (shift == sv))
        def _():
            for i in range(PACK_W // LANES):
                cols = slice(i * LANES, (i + 1) * LANES)
                if sv == 0:
                    o_ref[:, cols] = srcs[i][...].astype(BF16)
                else:
                    a = pltpu.roll(srcs[i][...], LANES - sv, 1)
                    b = pltpu.roll(srcs[i + 1][...], LANES - sv, 1)
                    o_ref[:, cols] = jnp.where(lane < LANES - sv, a, b).astype(BF16)


def _pack_w_in(w_in):
    depth, d, n_in = w_in.shape
    (ga_lo, ga_hi), (ng_lo, ng_hi) = _SRC_SEGMENTS[7:]
    assert ga_lo % LANES == 0 and ga_hi - ga_lo == GLA_GATE_RANK
    assert ng_lo % LANES == GLA_GATE_RANK and ng_hi - ng_lo == SM_USED - GLA_GATE_RANK
    last_blk = (n_in - 1) // LANES

    def src_spec(i):
        return pl.BlockSpec((None, d, LANES), lambda l, j, tab: (l, 0, jnp.minimum(tab[0, j] + i, last_blk)))

    def fixed_spec(col):
        return pl.BlockSpec((None, d, LANES), lambda l, j, tab: (l, 0, col // LANES))

    gs = pltpu.PrefetchScalarGridSpec(
        num_scalar_prefetch=1,
        grid=(depth, N_PROJ // PACK_W),
        in_specs=[src_spec(i) for i in range(PACK_SRC)] + [fixed_spec(ga_lo), fixed_spec(ng_lo)],
        out_specs=pl.BlockSpec((None, d, PACK_W), lambda l, j, tab: (l, 0, j)),
    )
    return pl.pallas_call(
        _pack_kernel,
        grid_spec=gs,
        out_shape=jax.ShapeDtypeStruct((depth, d, N_PROJ), BF16),
        compiler_params=_cparams(("parallel", "arbitrary")),
        name="pack_w_in",
    )(jnp.asarray(_pack_table()), *([w_in] * (PACK_SRC + 2)))


def _layer_prompt(x, lw, tabs, layer, depth, batch, T, stacked_rows, gla_states):
    proj = norm_matmul(x, lw['n0'], lw['w_in'], layer, 1024, 2048, BF16)
    o_a, gla_states = gla(proj, lw['wa2'], lw['ba'], lw['gn'], None, layer, depth, gla_states, batch, T,
                          GLA_CHUNK, 4, GLA_CHUNK)
    (cmp_rows, sel_rows, win_rows), stacked_rows = nsa_prep_prompt(proj, tabs, 512, layer, depth, stacked_rows)
    a, b = chunk_sums(cmp_rows, lw['pe'], lw['mix'], 512)
    kvc = compressed_kv(a, b, batch, lw['w1'], lw['w2'], 128)
    o_b = nsa_prompt(tabs, kvc, sel_rows, win_rows, proj, batch, T, 256)
    x = merge_wo(x, proj, o_a, o_b, lw['n1'], lw['w_o'], layer, 512)
    x = mlp(x, lw['n2'], lw['n3'], lw['w_up'], lw['w_down'], layer, 1024, 512)
    return x, gla_states, stacked_rows


def _layer_sample(x, lw, tabs, layer, depth, batch, cache_cmp, cache_sel, cache_win, s0, page_table, gla_states):
    R = SAMPLE_ROWS
    m = batch * R
    proj = norm_matmul(x, lw['n0'], lw['w_in'], layer, m, 1024, F32)
    o_a, gla_states = gla(proj, lw['wa2'], lw['ba'], lw['gn'], s0, layer, depth, gla_states, batch, R, R, 1, 4)
    q_hm, cmp_rows, sel_rows, win_rows = nsa_prep(proj, tabs, m, with_q=True)
    a, b = chunk_sums_paged(cache_cmp, layer, page_table, lw['pe'], lw['mix'])
    kvc = compressed_kv(a, b, batch, lw['w1'], lw['w2'], 1024)
    n_slc = -(-(PAST_LEN + 4) // SEL_LEN)
    o_cmp, selm = sample_cmp_select(q_hm, kvc, batch, n_slc)
    o_b = sample_attn(q_hm, selm, cache_sel, layer, page_table, sel_rows, cache_win, win_rows, proj, o_cmp)
    x = merge_wo(x, proj, o_a, o_b, lw['n1'], lw['w_o'], layer, m)
    x = mlp(x, lw['n2'], lw['n3'], lw['w_up'], lw['w_down'], layer, m, 1024)
    return x, (cmp_rows, sel_rows, win_rows), gla_states


def kernel(x_prompt, x_sample, cache_cmp_kv, cache_sel_kv, cache_win_kv, state_gla, page_table,
           w_in, gla_wa2, gla_ba, gla_norm, cmp_pe, cmp_mix, cmp_w1, cmp_w2, w_o, norms, w_up, w_down):
    bp, T, d = x_prompt.shape
    bs, ss, _ = x_sample.shape
    R = SAMPLE_ROWS
    depth = w_in.shape[0]
    n_pool = cache_cmp_kv.shape[1]
    wb = cache_win_kv.shape[2]
    assert wb == WINDOW and T % 512 == 0 and ss <= R

    w_in_p = _pack_w_in(w_in)
    w_o_b, w_up_b, w_down_b = w_o.astype(BF16), w_up.astype(BF16), w_down.astype(BF16)
    cache_cmp = cache_cmp_kv.reshape(depth, n_pool, PAGE_SIZE * KV_PARTS, NSA_DH)
    cache_sel = cache_sel_kv.reshape(depth, n_pool, PAGE_SIZE * KV_PARTS, NSA_DH)
    cache_win = cache_win_kv.reshape(depth, bs, wb * KV_PARTS, NSA_DH)

    tabs_p = rope_tables(jnp.arange(T, dtype=jnp.int32))
    tabs_s = tuple(jnp.tile(t, (bs, 1)) for t in rope_tables(PAST_LEN + jnp.arange(R, dtype=jnp.int32)))

    y_p = x_prompt.reshape(bp * T, d)
    y_s = jnp.pad(x_sample, ((0, 0), (0, R - ss), (0, 0))).reshape(bs * R, d)
    st_s, stacked_rows, gla_p, gla_s = [], None, None, None
    for layer in range(depth):
        lw = {'w_in': w_in_p, 'wa2': gla_wa2[layer], 'ba': gla_ba[layer][None, :],
              'gn': gla_norm[layer][None, :], 'pe': cmp_pe[layer], 'mix': cmp_mix[layer],
              'w1': cmp_w1[layer], 'w2': cmp_w2[layer], 'w_o': w_o_b,
              'n0': norms[layer, 0][None, :], 'n1': norms[layer, 1][None, :],
              'n2': norms[layer, 2][None, :], 'n3': norms[layer, 3][None, :],
              'w_up': w_up_b, 'w_down': w_down_b}
        y_p, gla_p, stacked_rows = _layer_prompt(y_p, lw, tabs_p, layer, depth, bp, T, stacked_rows, gla_p)
        y_s, s_s, gla_s = _layer_sample(y_s, lw, tabs_s, layer, depth, bs, cache_cmp, cache_sel, cache_win,
                                        state_gla, page_table, gla_s)
        st_s.append(s_s)

    kv_shape = (2, NSA_KV_HEADS, NSA_DH)

    def rows_p(i):
        return stacked_rows[i].reshape(depth, bp, T, *kv_shape)

    def rows_s(i):
        return jnp.stack([s[i].reshape(bs, R, *kv_shape)[:, :ss] for s in st_s])

    win_p = rows_p(2)[:, :, T - min(WINDOW, T):]
    win_s = jnp.concatenate([cache_win_kv[:, :, ss:], rows_s(2)], axis=2)
    return (y_p.reshape(bp, T, d), y_s.reshape(bs, R, d)[:, :ss],
            rows_p(0), rows_s(0), rows_p(1), rows_s(1), win_p, win_s,
            gla_p, gla_s)
```

```python
import functools

import jax
import jax.numpy as jnp
import numpy as np
from jax import lax
from jax.experimental import pallas as pl
from jax.experimental.pallas import tpu as pltpu

F32 = jnp.float32
BF16 = jnp.bfloat16

D_MODEL = 2048
DEPTH = 4
PAST_LEN = 16384
PAGE_SIZE = 128

GLA_HEADS = 4
GLA_DK = 256
GLA_DV = 512
GLA_GATE_RANK = 16
GLA_TAU = 16.0
GLA_CHUNK = 64
GLA_CHUNK_SHIFT = 6

NSA_HEADS = 16
NSA_KV_HEADS = 2
NSA_DH = 128
NSA_GROUP = NSA_HEADS // NSA_KV_HEADS
CMP_LEN = 32
CMP_STRIDE = 16
CMP_HIDDEN = 256
SEL_LEN = 64
SEL_SHIFT = 6
SEL_TOPN = 16
WINDOW = 512
WIN_QBLK = 128

ROPE_THETA = 500000.0
ROPE_DIM = 32
ROPE_HALF = 16
MLP_HIDDEN = 4 * D_MODEL
NORM_EPS = 1e-6
NEG_INF = -1e30
FORCED_SCORE = 1e6
KV_PARTS = 2 * NSA_KV_HEADS
KV_ROW = KV_PARTS * NSA_DH
PAGES_PER_STEP = 32

VMEM_LIMIT_BYTES = 56 * 1024 * 1024
LANES = 128

COL_MG = 0
COL_GV = 4096
COL_GG = 6144
COL_NQ = 8192
COL_GQ = 10240
COL_GK = 11264
COL_NKV = 12288
COL_SM = 13824
N_PROJ = 14336
SM_GATE0 = GLA_GATE_RANK
SM_USED = GLA_GATE_RANK + 3 * NSA_HEADS

_SRC_SEGMENTS = ((9792, 13888), (2048, 4096), (4096, 6144), (6160, 8208), (0, 1024), (1024, 2048),
                 (8208, 9744), (6144, 6160), (9744, 9792))

SAMPLE_ROWS = 8


def _cparams(sem):
    return pltpu.CompilerParams(dimension_semantics=sem, vmem_limit_bytes=VMEM_LIMIT_BYTES)


def _rms(x, g):
    return x * lax.rsqrt(jnp.mean(x * x, axis=-1, keepdims=True) + NORM_EPS) * g


def _dot(a, b, precision=None):
    return jnp.dot(a, b, preferred_element_type=F32, precision=precision)


def _dot_nt(a, b, precision=None):
    return lax.dot_general(a, b, (((1,), (1,)), ((), ())), preferred_element_type=F32, precision=precision)


def _dot_tn(a, b, precision=None):
    return lax.dot_general(a, b, (((0,), (0,)), ((), ())), preferred_element_type=F32, precision=precision)


def _split_bf16(x):
    hi = x.astype(BF16)
    return hi, (x - hi.astype(F32)).astype(BF16)


def _dot_nt_3x(a, b):
    ah, al = _split_bf16(a)
    bh, bl = _split_bf16(b)
    return (_dot_nt(jnp.concatenate([ah, al], axis=1), jnp.concatenate([bh, bh], axis=1))
            + _dot_nt(ah, bl))


def _dot_3x(a, b):
    ah, al = _split_bf16(a)
    bh, bl = _split_bf16(b)
    return (_dot(jnp.concatenate([ah, al], axis=1), jnp.concatenate([bh, bh], axis=0))
            + _dot(ah, bl))


def _iota(shape, dim):
    return lax.broadcasted_iota(jnp.int32, shape, dim)


def _masked_softmax(s, valid):
    s = jnp.where(valid, s, NEG_INF)
    m = jnp.max(s, axis=-1, keepdims=True)
    p = jnp.where(valid, jnp.exp(s - m), 0.0)
    return p / jnp.maximum(jnp.sum(p, axis=-1, keepdims=True), 1e-30)


def _norm_matmul_kernel(x_ref, g_ref, w_ref, o_ref, h_ref):
    @pl.when(pl.program_id(1) == 0)
    def _():
        h_ref[...] = _rms(x_ref[...], g_ref[...]).astype(BF16)

    o_ref[...] = _dot(h_ref[...], w_ref[...]).astype(o_ref.dtype)


def norm_matmul(x, g, w, layer, tm, tn, out_dtype):
    m, k = x.shape
    n = w.shape[2]
    return pl.pallas_call(
        _norm_matmul_kernel,
        grid=(m // tm, n // tn),
        in_specs=[pl.BlockSpec((tm, k), lambda i, j: (i, 0)),
                  pl.BlockSpec((1, k), lambda i, j: (0, 0)),
                  pl.BlockSpec((None, k, tn), lambda i, j: (layer, 0, j))],
        out_specs=pl.BlockSpec((tm, tn), lambda i, j: (i, j)),
        out_shape=jax.ShapeDtypeStruct((m, n), out_dtype),
        scratch_shapes=[pltpu.VMEM((tm, k), BF16)],
        compiler_params=_cparams(("parallel", "arbitrary")),
        name="norm_matmul",
    )(x, g, w)


def _merge_wo_kernel(x_ref, mg0_ref, mg1_ref, oa_ref, ob_ref, g_ref, wo_ref, o_ref):
    ob = jnp.concatenate([ob_ref[h].astype(F32) for h in range(NSA_HEADS)], axis=1)
    a = (jax.nn.sigmoid(mg0_ref[...].astype(F32)) * oa_ref[...].astype(F32)
         + jax.nn.sigmoid(mg1_ref[...].astype(F32)) * ob)
    m = _dot(a.astype(BF16), wo_ref[...])
    o_ref[...] = x_ref[...] + _rms(m, g_ref[...])


def merge_wo(x, proj, o_a, o_b_hm, g, wo, layer, tm):
    m, d = x.shape
    return pl.pallas_call(
        _merge_wo_kernel,
        grid=(m // tm,),
        in_specs=[pl.BlockSpec((tm, d), lambda i: (i, 0)),
                  pl.BlockSpec((tm, d), lambda i: (i, COL_MG // D_MODEL)),
                  pl.BlockSpec((tm, d), lambda i: (i, COL_MG // D_MODEL + 1)),
                  pl.BlockSpec((tm, d), lambda i: (i, 0)),
                  pl.BlockSpec((NSA_HEADS, tm, NSA_DH), lambda i: (0, i, 0)),
                  pl.BlockSpec((1, d), lambda i: (0, 0)),
                  pl.BlockSpec((None, d, d), lambda i: (layer, 0, 0))],
        out_specs=pl.BlockSpec((tm, d), lambda i: (i, 0)),
        out_shape=jax.ShapeDtypeStruct((m, d), F32),
        compiler_params=_cparams(("parallel",)),
        name="merge_wo",
    )(x, proj, proj, o_a, o_b_hm, g, wo)


def _mlp_kernel(x_ref, g2_ref, g3_ref, wu_ref, wd_ref, o_ref, h_ref):
    j = pl.program_id(1)

    @pl.when(j == 0)
    def _():
        h_ref[...] = _rms(x_ref[...], g2_ref[...]).astype(BF16)
        o_ref[...] = jnp.zeros_like(o_ref)

    u = _dot(h_ref[...], wu_ref[...])
    u = jnp.square(jnp.maximum(u, 0.0)).astype(BF16)
    o_ref[...] += _dot(u, wd_ref[...])

    @pl.when(j == pl.num_programs(1) - 1)
    def _():
        o_ref[...] = x_ref[...] + _rms(o_ref[...], g3_ref[...])


def mlp(x, g2, g3, wu, wd, layer, tm, th):
    m, d = x.shape
    hid = wu.shape[2]
    return pl.pallas_call(
        _mlp_kernel,
        grid=(m // tm, hid // th),
        in_specs=[pl.BlockSpec((tm, d), lambda i, j: (i, 0)),
                  pl.BlockSpec((1, d), lambda i, j: (0, 0)),
                  pl.BlockSpec((1, d), lambda i, j: (0, 0)),
                  pl.BlockSpec((None, d, th), lambda i, j: (layer, 0, j)),
                  pl.BlockSpec((None, th, d), lambda i, j: (layer, j, 0))],
        out_specs=pl.BlockSpec((tm, d), lambda i, j: (i, 0)),
        out_shape=jax.ShapeDtypeStruct((m, d), F32),
        scratch_shapes=[pltpu.VMEM((tm, d), BF16)],
        compiler_params=_cparams(("parallel", "arbitrary")),
        name="mlp",
    )(x, g2, g3, wu, wd)


def _gla_kernel(*refs, rows_in, n_inner, n_valid, has_s0):
    q_ref, k_ref, v_ref, gg_ref, sm_ref, wa2_ref, ba_ref, gn_ref, s0_ref = refs[:9]
    o_ref, sout_ref, st_ref = refs[-3:]
    c = pl.program_id(1)
    C = GLA_CHUNK

    @pl.when(c == 0)
    def _():
        for h in range(GLA_HEADS):
            if has_s0:
                st_ref[h] = s0_ref[h].T
            else:
                st_ref[h] = jnp.zeros((GLA_DV, GLA_DK), F32)

    rp = n_inner * C

    def load(ref):
        x = ref[...].astype(F32)
        if rows_in == C:
            return x
        assert n_inner == 1
        return jnp.concatenate([x, jnp.zeros((C - rows_in, x.shape[1]), x.dtype)], axis=0)

    row = _iota((rp, 1), 0)
    col = _iota((1, rp), 1)
    same_chunk = (row >> GLA_CHUNK_SHIFT) == (col >> GLA_CHUNK_SHIFT)
    causal = same_chunk & (row >= col)
    tril = causal.astype(BF16)
    live = (row & (C - 1)) < n_valid

    ga_h, ga_l = _split_bf16(load(sm_ref)[:, :GLA_GATE_RANK])
    w_h, w_l = _split_bf16(wa2_ref[...])
    z = _dot(ga_h, w_h) + _dot(ga_l, w_h) + _dot(ga_h, w_l) + ba_ref[...]
    log_a = (jnp.minimum(z, 0.0) - jnp.log1p(jnp.exp(-jnp.abs(z)))) / GLA_TAU
    k = load(k_ref)
    if n_valid < C:
        log_a = jnp.where(live, log_a, 0.0)
        k = jnp.where(live, k, 0.0)
    la_h, la_l = _split_bf16(log_a)
    b = _dot(tril, la_h) + _dot(tril, la_l)
    b_last = [b[ci * C + C - 1:ci * C + C, :] for ci in range(n_inner)]
    q_dec = (load(q_ref) * (GLA_DK ** -0.5) * jnp.exp(b)).astype(BF16)
    decay = [jnp.exp(r) for r in b_last]
    k_dec_f = k * jnp.exp(-b)
    k_dec = k_dec_f.astype(BF16)
    k_end = jnp.concatenate([k_dec_f[ci * C:(ci + 1) * C] * decay[ci] for ci in range(n_inner)],
                            axis=0).astype(BF16)
    vb = load(v_ref).astype(BF16)

    heads = range(GLA_HEADS)
    kq = [slice(h * GLA_DK, (h + 1) * GLA_DK) for h in heads]
    kv = [slice(h * GLA_DV, (h + 1) * GLA_DV) for h in heads]
    att = [jnp.where(causal, _dot_nt(q_dec[:, kq[h]], k_dec[:, kq[h]]), 0.0).astype(BF16) for h in heads]
    intra = [_dot(att[h], vb[:, kv[h]]) for h in heads]
    upd = [[_dot_tn(vb[ci * C:(ci + 1) * C, kv[h]], k_end[ci * C:(ci + 1) * C, kq[h]]) for h in heads]
           for ci in range(n_inner)]
    for ci in range(n_inner):
        cr = slice(ci * C, (ci + 1) * C)
        rs = pl.ds(ci * rows_in, rows_in)
        st = [st_ref[h] for h in heads]
        o = [intra[h][cr] + _dot_nt(q_dec[cr, kq[h]], st[h].astype(BF16)) for h in heads]
        for h in heads:
            st_ref[h] = st[h] * decay[ci][:, kq[h]] + upd[ci][h]
        for h in heads:
            o_n = _rms(o[h], gn_ref[...])
            gg = gg_ref[rs, kv[h]].astype(F32)
            o_ref[rs, kv[h]] = (o_n[:rows_in] * (gg * jax.nn.sigmoid(gg))).astype(o_ref.dtype)

    @pl.when(c == pl.num_programs(1) - 1)
    def _():
        for h in range(GLA_HEADS):
            sout_ref[h] = st_ref[h].T


def gla(proj, wa2, ba, gn, s0, layer, depth, stacked, batch, rows_per_batch, rows_in, n_inner, n_valid):
    m = proj.shape[0]
    r = rows_in * n_inner
    n_steps = rows_per_batch // r
    has_s0 = s0 is not None
    if s0 is None:
        s0 = jnp.zeros((1, 1, GLA_HEADS, GLA_DK, GLA_DV), F32)
        s0_map = lambda b, c: (0, 0, 0, 0, 0)
    else:
        s0_map = lambda b, c: (layer, b, 0, 0, 0)
    wk, wv = GLA_HEADS * GLA_DK, GLA_HEADS * GLA_DV
    rowmap = lambda off: (lambda b, c: (b * n_steps + c, off))
    state_blk = (None, GLA_HEADS, GLA_DK, GLA_DV)
    kern = functools.partial(_gla_kernel, rows_in=rows_in, n_inner=n_inner, n_valid=n_valid, has_s0=has_s0)
    in_specs = [pl.BlockSpec((r, wk), rowmap(COL_GQ // wk)),
                pl.BlockSpec((r, wk), rowmap(COL_GK // wk)),
                pl.BlockSpec((r, wv), rowmap(COL_GV // wv)),
                pl.BlockSpec((r, wv), rowmap(COL_GG // wv)),
                pl.BlockSpec((r, LANES), rowmap(COL_SM // LANES)),
                pl.BlockSpec((GLA_GATE_RANK, wk), lambda b, c: (0, 0)),
                pl.BlockSpec((1, wk), lambda b, c: (0, 0)),
                pl.BlockSpec((1, GLA_DV), lambda b, c: (0, 0)),
                pl.BlockSpec((None,) + state_blk, s0_map)]
    args = (proj, proj, proj, proj, proj, wa2, ba, gn, s0)
    aliases = {}
    if stacked is not None:
        in_specs.append(pl.BlockSpec(memory_space=pl.ANY))
        aliases = {len(args): 1}
        args += (stacked,)
    return pl.pallas_call(
        kern,
        grid=(batch, n_steps),
        in_specs=in_specs,
        out_specs=[pl.BlockSpec((r, wv), rowmap(0)),
                   pl.BlockSpec((None,) + state_blk, lambda b, c: (layer, b, 0, 0, 0))],
        out_shape=[jax.ShapeDtypeStruct((m, D_MODEL), proj.dtype),
                   jax.ShapeDtypeStruct((depth, batch, GLA_HEADS, GLA_DK, GLA_DV), F32)],
        input_output_aliases=aliases,
        scratch_shapes=[pltpu.VMEM((GLA_HEADS, GLA_DV, GLA_DK), F32)],
        compiler_params=_cparams(("parallel", "arbitrary")),
        name="gla",
    )(*args)


def _rope(x, c, sa, sb):
    return x * c + pltpu.roll(x, LANES - ROPE_HALF, 1) * sa + pltpu.roll(x, ROPE_HALF, 1) * sb


def _nsa_prep_kernel(*refs, with_q):
    if with_q:
        nq_ref, nkv_ref, c_ref, sa_ref, sb_ref, q_ref, cmp_ref, sel_ref, win_ref = refs
    else:
        nkv_ref, c_ref, sa_ref, sb_ref, cmp_ref, sel_ref, win_ref = refs
    c, sa, sb = c_ref[...], sa_ref[...], sb_ref[...]
    if with_q:
        for h in range(NSA_HEADS):
            q_ref[h] = _rope(nq_ref[:, h * NSA_DH:(h + 1) * NSA_DH].astype(F32), c, sa, sb)
    for s, out in enumerate((cmp_ref, sel_ref, win_ref)):
        base = s * KV_ROW
        for hh in range(NSA_KV_HEADS):
            lo = hh * NSA_DH
            out[:, lo:lo + NSA_DH] = _rope(nkv_ref[:, base + lo:base + lo + NSA_DH].astype(F32), c, sa, sb)
        half = NSA_KV_HEADS * NSA_DH
        out[:, half:] = nkv_ref[:, base + half:base + KV_ROW].astype(F32)


def nsa_prep(proj, tabs, tr, with_q):
    m = proj.shape[0]
    n_tab = tabs[0].shape[0] // tr
    tab_spec = pl.BlockSpec((tr, LANES), lambda i: (i % n_tab, 0))
    nkv_w = 3 * KV_ROW
    row_spec = pl.BlockSpec((tr, KV_ROW), lambda i: (i, 0))
    row_shape = jax.ShapeDtypeStruct((m, KV_ROW), F32)
    in_specs = [pl.BlockSpec((tr, nkv_w), lambda i: (i, COL_NKV // nkv_w)), tab_spec, tab_spec, tab_spec]
    out_specs, out_shape, args = [row_spec] * 3, [row_shape] * 3, (proj,) + tuple(tabs)
    if with_q:
        in_specs = [pl.BlockSpec((tr, D_MODEL), lambda i: (i, COL_NQ // D_MODEL))] + in_specs
        out_specs = [pl.BlockSpec((NSA_HEADS, tr, NSA_DH), lambda i: (0, i, 0))] + out_specs
        out_shape = [jax.ShapeDtypeStruct((NSA_HEADS, m, NSA_DH), F32)] + out_shape
        args = (proj,) + args
    return pl.pallas_call(
        functools.partial(_nsa_prep_kernel, with_q=with_q),
        grid=(m // tr,),
        in_specs=in_specs,
        out_specs=out_specs,
        out_shape=out_shape,
        compiler_params=_cparams(("parallel",)),
        name="nsa_prep",
    )(*args)


def _nsa_prep_prompt_kernel(*refs):
    nkv_ref, c_ref, sa_ref, sb_ref = refs[:4]
    slabs, stacked = refs[-6:-3], refs[-3:]
    c, sa, sb = c_ref[...], sa_ref[...], sb_ref[...]
    tr = nkv_ref.shape[0]
    for s in range(3):
        for part in range(KV_PARTS):
            lo = part * NSA_DH
            v = nkv_ref[:, s * KV_ROW + lo:s * KV_ROW + lo + NSA_DH].astype(F32)
            if part < NSA_KV_HEADS:
                v = _rope(v, c, sa, sb)
            slabs[s][:, lo:lo + NSA_DH] = v
            stacked[s][pl.ds(part, tr, stride=KV_PARTS), :] = v


def nsa_prep_prompt(proj, tabs, tr, layer, depth, stacked):
    m = proj.shape[0]
    n_tab = tabs[0].shape[0] // tr
    tab_spec = pl.BlockSpec((tr, LANES), lambda i: (i % n_tab, 0))
    nkv_w = 3 * KV_ROW
    row_spec = pl.BlockSpec((tr, KV_ROW), lambda i: (i, 0))
    in_specs = [pl.BlockSpec((tr, nkv_w), lambda i: (i, COL_NKV // nkv_w)), tab_spec, tab_spec, tab_spec]
    args = (proj,) + tuple(tabs)
    aliases = {}
    if stacked is not None:
        in_specs += [pl.BlockSpec(memory_space=pl.ANY)] * 3
        aliases = {len(args) + k: 3 + k for k in range(3)}
        args += tuple(stacked)
    st_spec = pl.BlockSpec((None, tr * KV_PARTS, NSA_DH), lambda i: (layer, i, 0))
    out = pl.pallas_call(
        _nsa_prep_prompt_kernel,
        grid=(m // tr,),
        in_specs=in_specs,
        out_specs=[row_spec] * 3 + [st_spec] * 3,
        out_shape=[jax.ShapeDtypeStruct((m, KV_ROW), F32)] * 3
                  + [jax.ShapeDtypeStruct((depth, m * KV_PARTS, NSA_DH), F32)] * 3,
        input_output_aliases=aliases,
        compiler_params=_cparams(("parallel",)),
        name="nsa_prep_prompt",
    )(*args)
    return out[:3], out[3:]


def rope_tables(pos):
    inv = 1.0 / (ROPE_THETA ** (jnp.arange(ROPE_HALF, dtype=F32) / ROPE_HALF))
    ang = pos.astype(F32)[:, None] * inv[None, :]
    cos, sin = jnp.cos(ang), jnp.sin(ang)
    n = pos.shape[0]
    rest = LANES - ROPE_DIM
    c = jnp.concatenate([cos, cos, jnp.ones((n, rest), F32)], axis=1)
    sa = jnp.concatenate([-sin, jnp.zeros((n, LANES - ROPE_HALF), F32)], axis=1)
    sb = jnp.concatenate([jnp.zeros((n, ROPE_HALF), F32), sin, jnp.zeros((n, rest), F32)], axis=1)
    return c, sa, sb


_HALVES = (slice(0, CMP_STRIDE), slice(CMP_STRIDE, CMP_LEN))


def _pe_mix_sums(pe_ref, mix_ref):
    return [[jnp.sum(pe_ref[kv][h] * mix_ref[kv][h], axis=0, keepdims=True) for h in _HALVES] for kv in range(2)]


SUBLANES = 8


def _sum_rows_8(p):
    sub = _iota((SUBLANES, LANES), 0)
    n = SUBLANES
    stage = list(p)
    for shift, bit in ((4, 4), (2, 2), (1, 1)):
        nxt = []
        half = len(stage) // 2
        for c in range(half):
            lo = stage[c] + pltpu.roll(stage[c], n - shift, 0)
            hi = stage[c + half] + pltpu.roll(stage[c + half], shift, 0)
            nxt.append(jnp.where((sub & bit) == 0, lo, hi))
        stage = nxt
    return stage[0]


def _half_sums(xs, mix, consts):
    nch = xs.shape[0]
    out = []
    for h, c in zip(_HALVES, consts):
        y = xs * mix[h]
        y = y[:, :SUBLANES, :] + y[:, SUBLANES:, :]
        groups = [_sum_rows_8([y[g + i] for i in range(SUBLANES)]) for g in range(0, nch, SUBLANES)]
        out.append(jnp.concatenate(groups, axis=0) + c)
    return out


def _chunk_sums_kernel(x_ref, pe_ref, mix_ref, a_ref, b_ref):
    nch = x_ref.shape[0] // CMP_STRIDE
    consts = _pe_mix_sums(pe_ref, mix_ref)
    for kv in range(2):
        for hh in range(NSA_KV_HEADS):
            lo = (kv * NSA_KV_HEADS + hh) * NSA_DH
            xs = x_ref[:, lo:lo + NSA_DH].reshape(nch, CMP_STRIDE, NSA_DH)
            a_ref[:, lo:lo + NSA_DH], b_ref[:, lo:lo + NSA_DH] = _half_sums(xs, mix_ref[kv], consts[kv])


def _page_specs(layer):
    def spec(i):
        return pl.BlockSpec((None, None, PAGE_SIZE * KV_PARTS, NSA_DH),
                            lambda b, s, pt: (layer, pt[b, s * PAGES_PER_STEP + i], 0, 0))
    return [spec(i) for i in range(PAGES_PER_STEP)]


def _page_part(page_ref, part):
    return page_ref[pl.ds(part, PAGE_SIZE, stride=KV_PARTS), :]


def _chunk_sums_paged_kernel(pt_ref, *refs):
    page_refs, (pe_ref, mix_ref, a_ref, b_ref) = refs[:PAGES_PER_STEP], refs[PAGES_PER_STEP:]
    nch = PAGE_SIZE // CMP_STRIDE
    consts = _pe_mix_sums(pe_ref, mix_ref)
    for i, page_ref in enumerate(page_refs):
        rows = slice(i * nch, (i + 1) * nch)
        for kv in range(2):
            for hh in range(NSA_KV_HEADS):
                part = kv * NSA_KV_HEADS + hh
                lo = part * NSA_DH
                xs = _page_part(page_ref, part).reshape(nch, CMP_STRIDE, NSA_DH)
                a_ref[rows, lo:lo + NSA_DH], b_ref[rows, lo:lo + NSA_DH] = _half_sums(xs, mix_ref[kv], consts[kv])


def chunk_sums(rows, pe, mix, tr):
    m = rows.shape[0]
    nch = tr // CMP_STRIDE
    full = lambda i: (0, 0, 0)
    return pl.pallas_call(
        _chunk_sums_kernel,
        grid=(m // tr,),
        in_specs=[pl.BlockSpec((tr, KV_ROW), lambda i: (i, 0)),
                  pl.BlockSpec((2, CMP_LEN, NSA_DH), full),
                  pl.BlockSpec((2, CMP_LEN, NSA_DH), full)],
        out_specs=[pl.BlockSpec((nch, KV_ROW), lambda i: (i, 0))] * 2,
        out_shape=[jax.ShapeDtypeStruct((m // CMP_STRIDE, KV_ROW), F32)] * 2,
        compiler_params=_cparams(("parallel",)),
        name="chunk_sums",
    )(rows, pe, mix)


def chunk_sums_paged(cache, layer, page_table, pe, mix):
    batch, n_pages = page_table.shape
    nch = PAGES_PER_STEP * PAGE_SIZE // CMP_STRIDE
    n_steps = n_pages // PAGES_PER_STEP
    full = lambda b, s, pt: (0, 0, 0)
    gs = pltpu.PrefetchScalarGridSpec(
        num_scalar_prefetch=1,
        grid=(batch, n_steps),
        in_specs=_page_specs(layer) + [pl.BlockSpec((2, CMP_LEN, NSA_DH), full),
                                       pl.BlockSpec((2, CMP_LEN, NSA_DH), full)],
        out_specs=[pl.BlockSpec((nch, KV_ROW), lambda b, s, pt: (b * n_steps + s, 0))] * 2,
    )
    return pl.pallas_call(
        _chunk_sums_paged_kernel,
        grid_spec=gs,
        out_shape=[jax.ShapeDtypeStruct((batch * n_steps * nch, KV_ROW), F32)] * 2,
        compiler_params=_cparams(("parallel", "arbitrary")),
        name="chunk_sums_paged",
    )(page_table, *([cache] * PAGES_PER_STEP), pe, mix)


def _cmp_mlp_kernel(a_ref, b_ref, w1_ref, w2_ref, o_ref):
    h = a_ref[...] + b_ref[...]
    for kv in range(2):
        for hh in range(NSA_KV_HEADS):
            lo = (kv * NSA_KV_HEADS + hh) * NSA_DH
            y = jax.nn.gelu(_dot_3x(h[:, lo:lo + NSA_DH], w1_ref[kv]))
            o_ref[:, lo:lo + NSA_DH] = _dot_3x(y, w2_ref[kv])


def cmp_mlp(a, b_shift, w1, w2, tr):
    m = a.shape[0]
    return pl.pallas_call(
        _cmp_mlp_kernel,
        grid=(m // tr,),
        in_specs=[pl.BlockSpec((tr, KV_ROW), lambda i: (i, 0)),
                  pl.BlockSpec((tr, KV_ROW), lambda i: (i, 0)),
                  pl.BlockSpec((2, NSA_DH, CMP_HIDDEN), lambda i: (0, 0, 0)),
                  pl.BlockSpec((2, CMP_HIDDEN, NSA_DH), lambda i: (0, 0, 0))],
        out_specs=pl.BlockSpec((tr, KV_ROW), lambda i: (i, 0)),
        out_shape=jax.ShapeDtypeStruct((m, KV_ROW), F32),
        compiler_params=_cparams(("parallel",)),
        name="cmp_mlp",
    )(a, b_shift, w1, w2)


def compressed_kv(a, b, batch, w1, w2, tr):
    nch = a.shape[0] // batch
    b3 = b.reshape(batch, nch, KV_ROW)
    b_shift = jnp.concatenate([b3[:, 1:], jnp.zeros((batch, 1, KV_ROW), F32)], axis=1).reshape(batch * nch, KV_ROW)
    return cmp_mlp(a, b_shift, w1, w2, tr)


def _select_topn(imp, n_top):
    j = _iota(imp.shape, 1)
    big = jnp.int32(imp.shape[1])
    sel = jnp.zeros(imp.shape, F32)
    for _ in range(n_top):
        m = jnp.max(imp, axis=-1, keepdims=True)
        idx = jnp.min(jnp.where(imp == m, j, big), axis=-1, keepdims=True)
        hit = j == idx
        sel = jnp.where(hit, 1.0, sel)
        imp = jnp.where(hit, NEG_INF, imp)
    return sel


def _importance(p_grp, qpos, n_cmp_pad, n_slc, width):
    n_r = _iota((n_cmp_pad, 1), 0) * CMP_STRIDE
    j_c = _iota((1, width), 1)
    overlap = ((n_r < (j_c + 1) * SEL_LEN) & (n_r + CMP_LEN > j_c * SEL_LEN)).astype(BF16)
    p_hi, p_lo = _split_bf16(p_grp)
    imp = _dot(p_hi, overlap) + _dot(p_lo, overlap)
    cur = qpos >> SEL_SHIFT
    forced = (j_c == 0) | (j_c == cur) | (j_c == cur - 1)
    imp = jnp.where(forced, FORCED_SCORE, imp)
    imp = jnp.where(j_c > cur, -1.0, imp)
    return jnp.where(j_c >= n_slc, -2.0, imp)


LOG2E = 1.4426950408889634


def _nsa_prompt_kernel(nq_ref, c_ref, sa_ref, sb_ref, kc_ref, vc_ref, ks_ref, vs_ref, kw_ref, vw_ref, sm_ref, o_ref,
                       qf_ref, qb_ref, s_ref, p_ref, m_ref, l_ref, a_ref, acc_ref, ob_ref, *, Q, T, TK):
    G = NSA_GROUP
    hk = pl.program_id(1)
    qi = pl.program_id(2)
    q0 = qi * Q
    scale = NSA_DH ** -0.5
    n_cmp_pad = kc_ref.shape[0]
    n_slc = T // SEL_LEN
    qpos = q0 + _iota((Q, 1), 0)
    c, sa, sb = c_ref[...], sa_ref[...], sb_ref[...]
    for g in range(G):
        qg = _rope(nq_ref[:, g * NSA_DH:(g + 1) * NSA_DH].astype(F32), c, sa, sb)
        qf_ref[g * Q:(g + 1) * Q, :] = qg
        qb_ref[g * Q:(g + 1) * Q, :] = (qg * (scale * LOG2E)).astype(BF16)
    qs = qf_ref[...]

    ng = jax.nn.sigmoid(sm_ref[...].astype(F32))

    def gate(g, c3):
        lane0 = SM_GATE0 + 3 * g + c3
        lane1 = lane0 + 3 * G
        return jnp.where(hk == 0, ng[:, lane0:lane0 + 1], ng[:, lane1:lane1 + 1])

    def softmax_tile(kb, vb, bias, width):
        s_ref[:, :width] = _dot_nt(qb_ref[...], kb)
        for g in range(G):
            rows = pl.ds(g * Q, Q)
            s = s_ref[rows, :width] + bias
            m_old = m_ref[rows, :]
            m_new = jnp.maximum(m_old, jnp.max(s, axis=-1, keepdims=True))
            alpha = jnp.exp2(m_old - m_new)
            p = jnp.exp2(s - jnp.tile(m_new, (1, width // LANES)))
            l_ref[rows, :] = alpha * l_ref[rows, :] + jnp.sum(p, axis=-1, keepdims=True)
            a_ref[rows, :] = alpha
            m_ref[rows, :] = m_new
            p_ref[rows, :width] = p.astype(BF16)
        return _dot(p_ref[:, :width], vb)

    def reset():
        m_ref[...] = jnp.full_like(m_ref, NEG_INF)
        l_ref[...] = jnp.zeros_like(l_ref)

    cmp_valid = _iota((1, n_cmp_pad), 1) * CMP_STRIDE + (CMP_LEN - 1) <= qpos
    p_grp = jnp.zeros((Q, n_cmp_pad), F32)
    s_cmp = _dot_nt_3x(qs, kc_ref[...]) * scale
    vcb = vc_ref[...].astype(BF16)
    for g in range(G):
        p = _masked_softmax(s_cmp[g * Q:(g + 1) * Q], cmp_valid)
        p_grp = p_grp + p
        ob_ref[g] = gate(g, 0) * _dot(p.astype(BF16), vcb)

    ks0 = pl.multiple_of(jnp.maximum(q0 - WINDOW, 0), LANES)
    wlen = WINDOW + Q
    wpos = ks0 + _iota((1, wlen), 1)
    win_bias = jnp.where((wpos <= qpos) & (wpos > qpos - WINDOW), 0.0, NEG_INF)
    reset()
    pv = softmax_tile(kw_ref[pl.ds(ks0, wlen), :].astype(BF16), vw_ref[pl.ds(ks0, wlen), :].astype(BF16),
                      win_bias, wlen)
    for g in range(G):
        rows = pl.ds(g * Q, Q)
        ob_ref[g] = ob_ref[g] + gate(g, 2) * (pv[g * Q:(g + 1) * Q] / jnp.maximum(l_ref[rows, :], 1e-30))

    n_top = min(SEL_TOPN, n_slc)
    j_r = _iota((n_slc, 1), 0)
    n_c = _iota((1, n_cmp_pad), 1) * CMP_STRIDE
    overlap_t = ((n_c < (j_r + 1) * SEL_LEN) & (n_c + CMP_LEN > j_r * SEL_LEN)).astype(BF16)
    p_hi, p_lo = _split_bf16(p_grp)
    imp = _dot_nt(overlap_t, p_hi) + _dot_nt(overlap_t, p_lo)
    cur = (q0 + _iota((1, Q), 1)) >> SEL_SHIFT
    imp = jnp.where((j_r == 0) | (j_r == cur) | (j_r == cur - 1), FORCED_SCORE, imp)
    imp = jnp.where(j_r > cur, -1.0, imp)
    rank = jnp.zeros((n_slc, Q), F32)
    for jp in range(n_slc):
        other = imp[jp:jp + 1, :]
        beats = (other > imp) | ((other == imp) & (j_r > jp))
        rank = rank + jnp.where(beats, 1.0, 0.0)
    sel_t = jnp.where(rank < n_top, 1.0, 0.0).astype(BF16)

    reset()
    acc_ref[...] = jnp.zeros_like(acc_ref)

    def key_tile(kt, carry):
        k0 = pl.multiple_of(kt * TK, TK)
        kpos = k0 + _iota((1, TK), 1)
        expand = (j_r == (kpos >> SEL_SHIFT)).astype(BF16)
        keymask = _dot_tn(sel_t, expand)
        bias = jnp.where((keymask > 0.5) & (kpos <= qpos), 0.0, NEG_INF)
        pv = softmax_tile(ks_ref[pl.ds(k0, TK), :].astype(BF16), vs_ref[pl.ds(k0, TK), :].astype(BF16), bias, TK)
        acc_ref[...] = a_ref[...] * acc_ref[...] + pv
        return carry

    lax.fori_loop(0, (q0 + Q + TK - 1) // TK, key_tile, 0)

    for g in range(G):
        rows = pl.ds(g * Q, Q)
        o_sel = acc_ref[rows, :] / jnp.maximum(l_ref[rows, :], 1e-30)
        o_ref[g] = (ob_ref[g] + gate(g, 1) * o_sel).astype(o_ref.dtype)


def nsa_prompt(tabs, kvc, sel_rows, win_rows, proj, batch, T, Q):
    m = batch * T
    nq = T // Q
    n_cmp_pad = kvc.shape[0] // batch
    G = NSA_GROUP
    gw = G * NSA_DH
    kern = functools.partial(_nsa_prompt_kernel, Q=Q, T=T, TK=512)
    kcol = lambda off: (lambda b, hk, qi: (b, off + hk))
    tab_spec = pl.BlockSpec((Q, LANES), lambda b, hk, qi: (qi, 0))
    return pl.pallas_call(
        kern,
        grid=(batch, NSA_KV_HEADS, nq),
        in_specs=[pl.BlockSpec((Q, gw), lambda b, hk, qi: (b * nq + qi, COL_NQ // gw + hk)),
                  tab_spec, tab_spec, tab_spec,
                  pl.BlockSpec((n_cmp_pad, NSA_DH), kcol(0)),
                  pl.BlockSpec((n_cmp_pad, NSA_DH), kcol(NSA_KV_HEADS)),
                  pl.BlockSpec((T, NSA_DH), kcol(0)),
                  pl.BlockSpec((T, NSA_DH), kcol(NSA_KV_HEADS)),
                  pl.BlockSpec((T, NSA_DH), kcol(0)),
                  pl.BlockSpec((T, NSA_DH), kcol(NSA_KV_HEADS)),
                  pl.BlockSpec((Q, LANES), lambda b, hk, qi: (b * nq + qi, COL_SM // LANES))],
        out_specs=pl.BlockSpec((G, Q, NSA_DH), lambda b, hk, qi: (hk, b * nq + qi, 0)),
        out_shape=jax.ShapeDtypeStruct((NSA_HEADS, m, NSA_DH), proj.dtype),
        scratch_shapes=[pltpu.VMEM((G * Q, NSA_DH), F32), pltpu.VMEM((G * Q, NSA_DH), BF16),
                        pltpu.VMEM((G * Q, WINDOW + Q), F32), pltpu.VMEM((G * Q, WINDOW + Q), BF16),
                        pltpu.VMEM((G * Q, LANES), F32), pltpu.VMEM((G * Q, LANES), F32),
                        pltpu.VMEM((G * Q, LANES), F32), pltpu.VMEM((G * Q, NSA_DH), F32),
                        pltpu.VMEM((G, Q, NSA_DH), F32)],
        compiler_params=_cparams(("parallel", "parallel", "arbitrary")),
        name="nsa_prompt",
    )(proj, *tabs, kvc, kvc, sel_rows, sel_rows, win_rows, win_rows, proj)


SEL_W = 384


def _sample_cmp_select_kernel(q_ref, kvc_ref, ocmp_ref, selm_ref, *, n_slc):
    G, R = NSA_GROUP, SAMPLE_ROWS
    scale = NSA_DH ** -0.5
    n_cmp_pad = kvc_ref.shape[0]
    qpos_s = PAST_LEN + (_iota((G * R, 1), 0) & (R - 1))
    end = _iota((1, n_cmp_pad), 1) * CMP_STRIDE + (CMP_LEN - 1)
    p_grp = []
    for hk in range(NSA_KV_HEADS):
        qs = q_ref[hk * G:(hk + 1) * G].reshape(G * R, NSA_DH)
        kc = kvc_ref[:, hk * NSA_DH:(hk + 1) * NSA_DH]
        vc = kvc_ref[:, (NSA_KV_HEADS + hk) * NSA_DH:(NSA_KV_HEADS + hk + 1) * NSA_DH]
        p = _masked_softmax(_dot_nt_3x(qs, kc) * scale, end <= qpos_s)
        o_cmp = _dot(p.astype(BF16), vc.astype(BF16))
        for g in range(G):
            ocmp_ref[hk * G + g] = o_cmp[g * R:(g + 1) * R]
        p_grp.append(jnp.sum(p.reshape(G, R, n_cmp_pad), axis=0))
    qpos = PAST_LEN + (_iota((NSA_KV_HEADS * R, 1), 0) & (R - 1))
    imp = _importance(jnp.concatenate(p_grp, axis=0), qpos, n_cmp_pad, n_slc, SEL_W)
    sel = _select_topn(imp, min(SEL_TOPN, n_slc))
    for hk in range(NSA_KV_HEADS):
        selm_ref[hk] = sel[hk * R:(hk + 1) * R]


def sample_cmp_select(q_hm, kvc, batch, n_slc):
    R = SAMPLE_ROWS
    n_cmp_pad = kvc.shape[0] // batch
    kern = functools.partial(_sample_cmp_select_kernel, n_slc=n_slc)
    hm_spec = pl.BlockSpec((NSA_HEADS, R, NSA_DH), lambda b: (0, b, 0))
    return pl.pallas_call(
        kern,
        grid=(batch,),
        in_specs=[hm_spec, pl.BlockSpec((n_cmp_pad, KV_ROW), lambda b: (b, 0))],
        out_specs=[hm_spec, pl.BlockSpec((None, NSA_KV_HEADS, R, SEL_W), lambda b: (b, 0, 0, 0))],
        out_shape=[jax.ShapeDtypeStruct((NSA_HEADS, batch * R, NSA_DH), F32),
                   jax.ShapeDtypeStruct((batch, NSA_KV_HEADS, R, SEL_W), F32)],
        compiler_params=_cparams(("parallel",)),
        name="sample_cmp_select",
    )(q_hm, kvc)


def _sample_attn_kernel(pt_ref, q_ref, selm_ref, *refs, n_steps):
    page_refs = refs[:PAGES_PER_STEP]
    nsel_ref, cwin_ref, nwin_ref, sm_ref, ocmp_ref, o_ref, m_ref, l_ref, acc_ref = refs[PAGES_PER_STEP:]
    G, R = NSA_GROUP, SAMPLE_ROWS
    p = pl.program_id(1)
    scale = NSA_DH ** -0.5
    t_s = _iota((G * R, 1), 0) & (R - 1)
    n_keys = PAGES_PER_STEP * PAGE_SIZE

    @pl.when(p == 0)
    def _():
        m_ref[...] = jnp.full_like(m_ref, NEG_INF)
        l_ref[...] = jnp.zeros_like(l_ref)
        acc_ref[...] = jnp.zeros_like(acc_ref)

    def q_stack(hk, mult):
        return (q_ref[hk * G:(hk + 1) * G].reshape(G * R, NSA_DH) * mult).astype(BF16)

    def update(hk, s, pr_of, v):
        m_old = m_ref[hk]
        m_new = jnp.maximum(m_old, jnp.max(s, axis=-1, keepdims=True))
        alpha = jnp.exp2(m_old - m_new)
        pr = pr_of(jnp.exp2(s - jnp.tile(m_new, (1, s.shape[1] // LANES))))
        l_ref[hk] = alpha * l_ref[hk] + jnp.sum(pr, axis=-1, keepdims=True)
        acc_ref[hk] = alpha * acc_ref[hk] + _dot(pr.astype(BF16), v)
        m_ref[hk] = m_new

    key = _iota((1, n_keys), 1)
    blk_of_key = (n_keys // SEL_LEN) * p + (key >> SEL_SHIFT)
    expand = (_iota((SEL_W, 1), 0) == blk_of_key).astype(BF16)
    hks = range(NSA_KV_HEADS)
    kpg = [jnp.concatenate([_page_part(r, hk).astype(BF16) for r in page_refs], axis=0) for hk in hks]
    vpg = [jnp.concatenate([_page_part(r, NSA_KV_HEADS + hk).astype(BF16) for r in page_refs], axis=0)
           for hk in hks]
    keymask = [_dot(selm_ref[hk].astype(BF16), expand) for hk in hks]
    s = [_dot_nt(q_stack(hk, scale * LOG2E), kpg[hk])
         + jnp.where(jnp.concatenate([keymask[hk]] * G, axis=0) > 0.5, 0.0, NEG_INF) for hk in hks]
    m_old = [m_ref[hk] for hk in hks]
    m_new = [jnp.maximum(m_old[hk], jnp.max(s[hk], axis=-1, keepdims=True)) for hk in hks]
    pr = [jnp.exp2(s[hk] - jnp.tile(m_new[hk], (1, n_keys // LANES))) for hk in hks]
    pv = [_dot(pr[hk].astype(BF16), vpg[hk]) for hk in hks]
    for hk in hks:
        alpha = jnp.exp2(m_old[hk] - m_new[hk])
        l_ref[hk] = alpha * l_ref[hk] + jnp.sum(pr[hk], axis=-1, keepdims=True)
        acc_ref[hk] = alpha * acc_ref[hk] + pv[hk]
        m_ref[hk] = m_new[hk]

    def win_part(ref, part):
        return ref[pl.ds(part, WINDOW, stride=KV_PARTS), :]

    @pl.when(p == n_steps - 1)
    def _():
        ng = jax.nn.sigmoid(sm_ref[...])
        new_blk = PAST_LEN // SEL_LEN
        zpad = jnp.zeros((LANES - R, NSA_DH), F32)
        jn = _iota((1, LANES), 1)
        for hk in range(NSA_KV_HEADS):
            klo, vlo = hk * NSA_DH, (NSA_KV_HEADS + hk) * NSA_DH
            kn = jnp.concatenate([nsel_ref[:, klo:klo + NSA_DH], zpad], axis=0)
            vn = jnp.concatenate([nsel_ref[:, vlo:vlo + NSA_DH], zpad], axis=0)
            picked = jnp.concatenate([selm_ref[hk][:, new_blk:new_blk + 1]] * G, axis=0) > 0.5
            valid = picked & (jn <= t_s) & (jn < R)
            s = jnp.where(valid, _dot_nt(q_stack(hk, scale * LOG2E), kn.astype(BF16)), NEG_INF)
            update(hk, s, lambda e: jnp.where(valid, e, 0.0), vn.astype(BF16))
            o_sel = acc_ref[hk] / jnp.maximum(l_ref[hk], 1e-30)
            qs = q_stack(hk, scale)
            kw = jnp.concatenate([win_part(cwin_ref, hk), nwin_ref[:, klo:klo + NSA_DH], zpad], axis=0)
            vw = jnp.concatenate([win_part(cwin_ref, NSA_KV_HEADS + hk), nwin_ref[:, vlo:vlo + NSA_DH], zpad], axis=0)
            iw = _iota((1, WINDOW + LANES), 1)
            wvalid = ((iw < WINDOW) & (iw > t_s)) | ((iw >= WINDOW) & (iw - WINDOW <= t_s) & (iw - WINDOW < R))
            pw = _masked_softmax(_dot_nt(qs, kw.astype(BF16)), wvalid)
            o_win = _dot(pw.astype(BF16), vw.astype(BF16))
            for g in range(G):
                h = hk * G + g
                lane = SM_GATE0 + 3 * h
                rows = slice(g * R, (g + 1) * R)
                o_ref[h] = (ng[:, lane:lane + 1] * ocmp_ref[h] + ng[:, lane + 1:lane + 2] * o_sel[rows]
                            + ng[:, lane + 2:lane + 3] * o_win[rows])


def sample_attn(q_hm, selm, cache_sel, layer, page_table, new_sel, cache_win, new_win, proj, o_cmp_hm):
    batch, n_pages = page_table.shape
    n_steps = n_pages // PAGES_PER_STEP
    G, R = NSA_GROUP, SAMPLE_ROWS
    hm_spec = pl.BlockSpec((NSA_HEADS, R, NSA_DH), lambda b, p, pt: (0, b, 0))
    row_spec = pl.BlockSpec((R, KV_ROW), lambda b, p, pt: (b, 0))
    gs = pltpu.PrefetchScalarGridSpec(
        num_scalar_prefetch=1,
        grid=(batch, n_steps),
        in_specs=[hm_spec,
                  pl.BlockSpec((None, NSA_KV_HEADS, R, SEL_W), lambda b, p, pt: (b, 0, 0, 0))]
                 + _page_specs(layer)
                 + [row_spec,
                    pl.BlockSpec((None, None, WINDOW * KV_PARTS, NSA_DH), lambda b, p, pt: (layer, b, 0, 0)),
                    row_spec,
                    pl.BlockSpec((R, LANES), lambda b, p, pt: (b, COL_SM // LANES)),
                    hm_spec],
        out_specs=hm_spec,
        scratch_shapes=[pltpu.VMEM((NSA_KV_HEADS, G * R, LANES), F32),
                        pltpu.VMEM((NSA_KV_HEADS, G * R, LANES), F32),
                        pltpu.VMEM((NSA_KV_HEADS, G * R, NSA_DH), F32)],
    )
    kern = functools.partial(_sample_attn_kernel, n_steps=n_steps)
    return pl.pallas_call(
        kern,
        grid_spec=gs,
        out_shape=jax.ShapeDtypeStruct((NSA_HEADS, batch * R, NSA_DH), F32),
        compiler_params=_cparams(("parallel", "arbitrary")),
        name="sample_attn",
    )(page_table, q_hm, selm, *([cache_sel] * PAGES_PER_STEP), new_sel, cache_win, new_win, proj, o_cmp_hm)


PACK_W = 512
PACK_SRC = PACK_W // LANES + 1
_MAIN_SEGMENTS = _SRC_SEGMENTS[:7]


def _pack_table():
    first, shift = [], []
    for lo, hi in _MAIN_SEGMENTS:
        assert (hi - lo) % PACK_W == 0
        for c in range(lo, hi, PACK_W):
            first.append(c // LANES)
            shift.append(c % LANES)
    assert len(first) * PACK_W == COL_SM
    n = N_PROJ // PACK_W
    first += [0] * (n - len(first))
    shift += [0] * (n - len(shift))
    return np.array([first, shift], np.int32)


def _pack_kernel(tab_ref, *refs):
    srcs, (ga_ref, ng_ref, o_ref) = refs[:PACK_SRC], refs[PACK_SRC:]
    j = pl.program_id(1)
    n_main = COL_SM // PACK_W
    shift = tab_ref[1, j]
    lane = _iota((1, LANES), 1)

    @pl.when(j >= n_main)
    def _():
        sm = jnp.where(lane < GLA_GATE_RANK, ga_ref[...], jnp.where(lane < SM_USED, ng_ref[...], 0.0))
        o_ref[:, :LANES] = sm.astype(BF16)
        o_ref[:, LANES:] = jnp.zeros((o_ref.shape[0], PACK_W - LANES), BF16)

    for sv in sorted({lo % LANES for lo, _ in _MAIN_SEGMENTS}):
        @pl.when((j < n_main) & (shift == sv))
        def _():
            for i in range(PACK_W // LANES):
                cols = slice(i * LANES, (i + 1) * LANES)
                if sv == 0:
                    o_ref[:, cols] = srcs[i][...].astype(BF16)
                else:
                    a = pltpu.roll(srcs[i][...], LANES - sv, 1)
                    b = pltpu.roll(srcs[i + 1][...], LANES - sv, 1)
                    o_ref[:, cols] = jnp.where(lane < LANES - sv, a, b).astype(BF16)


def _pack_w_in(w_in):
    depth, d, n_in = w_in.shape
    (ga_lo, ga_hi), (ng_lo, ng_hi) = _SRC_SEGMENTS[7:]
    assert ga_lo % LANES == 0 and ga_hi - ga_lo == GLA_GATE_RANK
    assert ng_lo % LANES == GLA_GATE_RANK and ng_hi - ng_lo == SM_USED - GLA_GATE_RANK
    last_blk = (n_in - 1) // LANES

    def src_spec(i):
        return pl.BlockSpec((None, d, LANES), lambda l, j, tab: (l, 0, jnp.minimum(tab[0, j] + i, last_blk)))

    def fixed_spec(col):
        return pl.BlockSpec((None, d, LANES), lambda l, j, tab: (l, 0, col // LANES))

    gs = pltpu.PrefetchScalarGridSpec(
        num_scalar_prefetch=1,
        grid=(depth, N_PROJ // PACK_W),
        in_specs=[src_spec(i) for i in range(PACK_SRC)] + [fixed_spec(ga_lo), fixed_spec(ng_lo)],
        out_specs=pl.BlockSpec((None, d, PACK_W), lambda l, j, tab: (l, 0, j)),
    )
    return pl.pallas_call(
        _pack_kernel,
        grid_spec=gs,
        out_shape=jax.ShapeDtypeStruct((depth, d, N_PROJ), BF16),
        compiler_params=_cparams(("parallel", "arbitrary")),
        name="pack_w_in",
    )(jnp.asarray(_pack_table()), *([w_in] * (PACK_SRC + 2)))


def _layer_prompt(x, lw, tabs, layer, depth, batch, T, stacked_rows, gla_states):
    proj = norm_matmul(x, lw['n0'], lw['w_in'], layer, 1024, 2048, BF16)
    o_a, gla_states = gla(proj, lw['wa2'], lw['ba'], lw['gn'], None, layer, depth, gla_states, batch, T,
                          GLA_CHUNK, 4, GLA_CHUNK)
    (cmp_rows, sel_rows, win_rows), stacked_rows = nsa_prep_prompt(proj, tabs, 512, layer, depth, stacked_rows)
    a, b = chunk_sums(cmp_rows, lw['pe'], lw['mix'], 512)
    kvc = compressed_kv(a, b, batch, lw['w1'], lw['w2'], 128)
    o_b = nsa_prompt(tabs, kvc, sel_rows, win_rows, proj, batch, T, 256)
    x = merge_wo(x, proj, o_a, o_b, lw['n1'], lw['w_o'], layer, 512)
    x = mlp(x, lw['n2'], lw['n3'], lw['w_up'], lw['w_down'], layer, 512, 1024)
    return x, gla_states, stacked_rows


def _layer_sample(x, lw, tabs, layer, depth, batch, cache_cmp, cache_sel, cache_win, s0, page_table, gla_states):
    R = SAMPLE_ROWS
    m = batch * R
    proj = norm_matmul(x, lw['n0'], lw['w_in'], layer, m, 1024, F32)
    o_a, gla_states = gla(proj, lw['wa2'], lw['ba'], lw['gn'], s0, layer, depth, gla_states, batch, R, R, 1, 4)
    q_hm, cmp_rows, sel_rows, win_rows = nsa_prep(proj, tabs, m, with_q=True)
    a, b = chunk_sums_paged(cache_cmp, layer, page_table, lw['pe'], lw['mix'])
    kvc = compressed_kv(a, b, batch, lw['w1'], lw['w2'], 1024)
    n_slc = -(-(PAST_LEN + 4) // SEL_LEN)
    o_cmp, selm = sample_cmp_select(q_hm, kvc, batch, n_slc)
    o_b = sample_attn(q_hm, selm, cache_sel, layer, page_table, sel_rows, cache_win, win_rows, proj, o_cmp)
    x = merge_wo(x, proj, o_a, o_b, lw['n1'], lw['w_o'], layer, m)
    x = mlp(x, lw['n2'], lw['n3'], lw['w_up'], lw['w_down'], layer, m, 1024)
    return x, (cmp_rows, sel_rows, win_rows), gla_states


def kernel(x_prompt, x_sample, cache_cmp_kv, cache_sel_kv, cache_win_kv, state_gla, page_table,
           w_in, gla_wa2, gla_ba, gla_norm, cmp_pe, cmp_mix, cmp_w1, cmp_w2, w_o, norms, w_up, w_down):
    bp, T, d = x_prompt.shape
    bs, ss, _ = x_sample.shape
    R = SAMPLE_ROWS
    depth = w_in.shape[0]
    n_pool = cache_cmp_kv.shape[1]
    wb = cache_win_kv.shape[2]
    assert wb == WINDOW and T % 512 == 0 and ss <= R

    w_in_p = _pack_w_in(w_in)
    w_o_b, w_up_b, w_down_b = w_o.astype(BF16), w_up.astype(BF16), w_down.astype(BF16)
    cache_cmp = cache_cmp_kv.reshape(depth, n_pool, PAGE_SIZE * KV_PARTS, NSA_DH)
    cache_sel = cache_sel_kv.reshape(depth, n_pool, PAGE_SIZE * KV_PARTS, NSA_DH)
    cache_win = cache_win_kv.reshape(depth, bs, wb * KV_PARTS, NSA_DH)

    tabs_p = rope_tables(jnp.arange(T, dtype=jnp.int32))
    tabs_s = tuple(jnp.tile(t, (bs, 1)) for t in rope_tables(PAST_LEN + jnp.arange(R, dtype=jnp.int32)))

    y_p = x_prompt.reshape(bp * T, d)
    y_s = jnp.pad(x_sample, ((0, 0), (0, R - ss), (0, 0))).reshape(bs * R, d)
    st_s, stacked_rows, gla_p, gla_s = [], None, None, None
    for layer in range(depth):
        lw = {'w_in': w_in_p, 'wa2': gla_wa2[layer], 'ba': gla_ba[layer][None, :],
              'gn': gla_norm[layer][None, :], 'pe': cmp_pe[layer], 'mix': cmp_mix[layer],
              'w1': cmp_w1[layer], 'w2': cmp_w2[layer], 'w_o': w_o_b,
              'n0': norms[layer, 0][None, :], 'n1': norms[layer, 1][None, :],
              'n2': norms[layer, 2][None, :], 'n3': norms[layer, 3][None, :],
              'w_up': w_up_b, 'w_down': w_down_b}
        y_p, gla_p, stacked_rows = _layer_prompt(y_p, lw, tabs_p, layer, depth, bp, T, stacked_rows, gla_p)
        y_s, s_s, gla_s = _layer_sample(y_s, lw, tabs_s, layer, depth, bs, cache_cmp, cache_sel, cache_win,
                                        state_gla, page_table, gla_s)
        st_s.append(s_s)

    kv_shape = (2, NSA_KV_HEADS, NSA_DH)

    def rows_p(i):
        return stacked_rows[i].reshape(depth, bp, T, *kv_shape)

    def rows_s(i):
        return jnp.stack([s[i].reshape(bs, R, *kv_shape)[:, :ss] for s in st_s])

    win_p = rows_p(2)[:, :, T - min(WINDOW, T):]
    win_s = jnp.concatenate([cache_win_kv[:, :, ss:], rows_s(2)], axis=2)
    return (y_p.reshape(bp, T, d), y_s.reshape(bs, R, d)[:, :ss],
            rows_p(0), rows_s(0), rows_p(1), rows_s(1), win_p, win_s,
            gla_p, gla_s)
```

```python
import functools

import jax
import jax.numpy as jnp
import numpy as np
from jax import lax
from jax.experimental import pallas as pl
from jax.experimental.pallas import tpu as pltpu

F32 = jnp.float32
BF16 = jnp.bfloat16

D_MODEL = 2048
DEPTH = 4
PAST_LEN = 16384
PAGE_SIZE = 128

GLA_HEADS = 4
GLA_DK = 256
GLA_DV = 512
GLA_GATE_RANK = 16
GLA_TAU = 16.0
GLA_CHUNK = 64
GLA_CHUNK_SHIFT = 6

NSA_HEADS = 16
NSA_KV_HEADS = 2
NSA_DH = 128
NSA_GROUP = NSA_HEADS // NSA_KV_HEADS
CMP_LEN = 32
CMP_STRIDE = 16
CMP_HIDDEN = 256
SEL_LEN = 64
SEL_SHIFT = 6
SEL_TOPN = 16
WINDOW = 512
WIN_QBLK = 128

ROPE_THETA = 500000.0
ROPE_DIM = 32
ROPE_HALF = 16
MLP_HIDDEN = 4 * D_MODEL
NORM_EPS = 1e-6
NEG_INF = -1e30
FORCED_SCORE = 1e6
KV_PARTS = 2 * NSA_KV_HEADS
KV_ROW = KV_PARTS * NSA_DH
PAGES_PER_STEP = 32

VMEM_LIMIT_BYTES = 56 * 1024 * 1024
LANES = 128

COL_MG = 0
COL_GV = 4096
COL_GG = 6144
COL_NQ = 8192
COL_GQ = 10240
COL_GK = 11264
COL_NKV = 12288
COL_SM = 13824
N_PROJ = 14336
SM_GATE0 = GLA_GATE_RANK
SM_USED = GLA_GATE_RANK + 3 * NSA_HEADS

_SRC_SEGMENTS = ((9792, 13888), (2048, 4096), (4096, 6144), (6160, 8208), (0, 1024), (1024, 2048),
                 (8208, 9744), (6144, 6160), (9744, 9792))

SAMPLE_ROWS = 8


def _cparams(sem):
    return pltpu.CompilerParams(dimension_semantics=sem, vmem_limit_bytes=VMEM_LIMIT_BYTES)


def _rms(x, g):
    return x * lax.rsqrt(jnp.mean(x * x, axis=-1, keepdims=True) + NORM_EPS) * g


def _dot(a, b, precision=None):
    return jnp.dot(a, b, preferred_element_type=F32, precision=precision)


def _dot_nt(a, b, precision=None):
    return lax.dot_general(a, b, (((1,), (1,)), ((), ())), preferred_element_type=F32, precision=precision)


def _dot_tn(a, b, precision=None):
    return lax.dot_general(a, b, (((0,), (0,)), ((), ())), preferred_element_type=F32, precision=precision)


def _split_bf16(x):
    hi = x.astype(BF16)
    return hi, (x - hi.astype(F32)).astype(BF16)


def _dot_nt_3x(a, b):
    ah, al = _split_bf16(a)
    bh, bl = _split_bf16(b)
    return (_dot_nt(jnp.concatenate([ah, al], axis=1), jnp.concatenate([bh, bh], axis=1))
            + _dot_nt(ah, bl))


def _dot_3x(a, b):
    ah, al = _split_bf16(a)
    bh, bl = _split_bf16(b)
    return (_dot(jnp.concatenate([ah, al], axis=1), jnp.concatenate([bh, bh], axis=0))
            + _dot(ah, bl))


def _iota(shape, dim):
    return lax.broadcasted_iota(jnp.int32, shape, dim)


def _masked_softmax(s, valid):
    s = jnp.where(valid, s, NEG_INF)
    m = jnp.max(s, axis=-1, keepdims=True)
    p = jnp.where(valid, jnp.exp(s - m), 0.0)
    return p / jnp.maximum(jnp.sum(p, axis=-1, keepdims=True), 1e-30)


def _norm_matmul_kernel(x_ref, g_ref, w_ref, o_ref, h_ref):
    @pl.when(pl.program_id(1) == 0)
    def _():
        h_ref[...] = _rms(x_ref[...], g_ref[...]).astype(BF16)

    o_ref[...] = _dot(h_ref[...], w_ref[...]).astype(o_ref.dtype)


def norm_matmul(x, g, w, layer, tm, tn, out_dtype):
    m, k = x.shape
    n = w.shape[2]
    return pl.pallas_call(
        _norm_matmul_kernel,
        grid=(m // tm, n // tn),
        in_specs=[pl.BlockSpec((tm, k), lambda i, j: (i, 0)),
                  pl.BlockSpec((1, k), lambda i, j: (0, 0)),
                  pl.BlockSpec((None, k, tn), lambda i, j: (layer, 0, j))],
        out_specs=pl.BlockSpec((tm, tn), lambda i, j: (i, j)),
        out_shape=jax.ShapeDtypeStruct((m, n), out_dtype),
        scratch_shapes=[pltpu.VMEM((tm, k), BF16)],
        compiler_params=_cparams(("parallel", "arbitrary")),
        name="norm_matmul",
    )(x, g, w)


def _merge_wo_kernel(x_ref, mg0_ref, mg1_ref, oa_ref, ob_ref, g_ref, wo_ref, o_ref):
    ob = jnp.concatenate([ob_ref[h].astype(F32) for h in range(NSA_HEADS)], axis=1)
    a = (jax.nn.sigmoid(mg0_ref[...].astype(F32)) * oa_ref[...].astype(F32)
         + jax.nn.sigmoid(mg1_ref[...].astype(F32)) * ob)
    m = _dot(a.astype(BF16), wo_ref[...])
    o_ref[...] = x_ref[...] + _rms(m, g_ref[...])


def merge_wo(x, proj, o_a, o_b_hm, g, wo, layer, tm):
    m, d = x.shape
    return pl.pallas_call(
        _merge_wo_kernel,
        grid=(m // tm,),
        in_specs=[pl.BlockSpec((tm, d), lambda i: (i, 0)),
                  pl.BlockSpec((tm, d), lambda i: (i, COL_MG // D_MODEL)),
                  pl.BlockSpec((tm, d), lambda i: (i, COL_MG // D_MODEL + 1)),
                  pl.BlockSpec((tm, d), lambda i: (i, 0)),
                  pl.BlockSpec((NSA_HEADS, tm, NSA_DH), lambda i: (0, i, 0)),
                  pl.BlockSpec((1, d), lambda i: (0, 0)),
                  pl.BlockSpec((None, d, d), lambda i: (layer, 0, 0))],
        out_specs=pl.BlockSpec((tm, d), lambda i: (i, 0)),
        out_shape=jax.ShapeDtypeStruct((m, d), F32),
        compiler_params=_cparams(("parallel",)),
        name="merge_wo",
    )(x, proj, proj, o_a, o_b_hm, g, wo)


def _mlp_kernel(x_ref, g2_ref, g3_ref, wu_ref, wd_ref, o_ref, h_ref):
    j = pl.program_id(1)

    @pl.when(j == 0)
    def _():
        h_ref[...] = _rms(x_ref[...], g2_ref[...]).astype(BF16)
        o_ref[...] = jnp.zeros_like(o_ref)

    u = _dot(h_ref[...], wu_ref[...])
    u = jnp.square(jnp.maximum(u, 0.0)).astype(BF16)
    o_ref[...] += _dot(u, wd_ref[...])

    @pl.when(j == pl.num_programs(1) - 1)
    def _():
        o_ref[...] = x_ref[...] + _rms(o_ref[...], g3_ref[...])


def mlp(x, g2, g3, wu, wd, layer, tm, th):
    m, d = x.shape
    hid = wu.shape[2]
    return pl.pallas_call(
        _mlp_kernel,
        grid=(m // tm, hid // th),
        in_specs=[pl.BlockSpec((tm, d), lambda i, j: (i, 0)),
                  pl.BlockSpec((1, d), lambda i, j: (0, 0)),
                  pl.BlockSpec((1, d), lambda i, j: (0, 0)),
                  pl.BlockSpec((None, d, th), lambda i, j: (layer, 0, j)),
                  pl.BlockSpec((None, th, d), lambda i, j: (layer, j, 0))],
        out_specs=pl.BlockSpec((tm, d), lambda i, j: (i, 0)),
        out_shape=jax.ShapeDtypeStruct((m, d), F32),
        scratch_shapes=[pltpu.VMEM((tm, d), BF16)],
        compiler_params=_cparams(("parallel", "arbitrary")),
        name="mlp",
    )(x, g2, g3, wu, wd)


def _gla_kernel(*refs, rows_in, n_inner, n_valid, has_s0):
    q_ref, k_ref, v_ref, gg_ref, sm_ref, wa2_ref, ba_ref, gn_ref, s0_ref = refs[:9]
    o_ref, sout_ref, st_ref = refs[-3:]
    c = pl.program_id(1)
    C = GLA_CHUNK

    @pl.when(c == 0)
    def _():
        for h in range(GLA_HEADS):
            if has_s0:
                st_ref[h] = s0_ref[h].T
            else:
                st_ref[h] = jnp.zeros((GLA_DV, GLA_DK), F32)

    rp = n_inner * C

    def load(ref):
        x = ref[...].astype(F32)
        if rows_in == C:
            return x
        assert n_inner == 1
        return jnp.concatenate([x, jnp.zeros((C - rows_in, x.shape[1]), x.dtype)], axis=0)

    row = _iota((rp, 1), 0)
    col = _iota((1, rp), 1)
    same_chunk = (row >> GLA_CHUNK_SHIFT) == (col >> GLA_CHUNK_SHIFT)
    causal = same_chunk & (row >= col)
    tril = causal.astype(BF16)
    live = (row & (C - 1)) < n_valid

    ga_h, ga_l = _split_bf16(load(sm_ref)[:, :GLA_GATE_RANK])
    w_h, w_l = _split_bf16(wa2_ref[...])
    z = _dot(ga_h, w_h) + _dot(ga_l, w_h) + _dot(ga_h, w_l) + ba_ref[...]
    log_a = (jnp.minimum(z, 0.0) - jnp.log1p(jnp.exp(-jnp.abs(z)))) / GLA_TAU
    k = load(k_ref)
    if n_valid < C:
        log_a = jnp.where(live, log_a, 0.0)
        k = jnp.where(live, k, 0.0)
    la_h, la_l = _split_bf16(log_a)
    b = _dot(tril, la_h) + _dot(tril, la_l)
    b_last = [b[ci * C + C - 1:ci * C + C, :] for ci in range(n_inner)]
    q_dec = (load(q_ref) * (GLA_DK ** -0.5) * jnp.exp(b)).astype(BF16)
    decay = [jnp.exp(r) for r in b_last]
    k_dec_f = k * jnp.exp(-b)
    k_dec = k_dec_f.astype(BF16)
    k_end = jnp.concatenate([k_dec_f[ci * C:(ci + 1) * C] * decay[ci] for ci in range(n_inner)],
                            axis=0).astype(BF16)
    vb = load(v_ref).astype(BF16)

    heads = range(GLA_HEADS)
    kq = [slice(h * GLA_DK, (h + 1) * GLA_DK) for h in heads]
    kv = [slice(h * GLA_DV, (h + 1) * GLA_DV) for h in heads]
    att = [jnp.where(causal, _dot_nt(q_dec[:, kq[h]], k_dec[:, kq[h]]), 0.0).astype(BF16) for h in heads]
    intra = [_dot(att[h], vb[:, kv[h]]) for h in heads]
    upd = [[_dot_tn(vb[ci * C:(ci + 1) * C, kv[h]], k_end[ci * C:(ci + 1) * C, kq[h]]) for h in heads]
           for ci in range(n_inner)]
    for ci in range(n_inner):
        cr = slice(ci * C, (ci + 1) * C)
        rs = pl.ds(ci * rows_in, rows_in)
        st = [st_ref[h] for h in heads]
        o = [intra[h][cr] + _dot_nt(q_dec[cr, kq[h]], st[h].astype(BF16)) for h in heads]
        for h in heads:
            st_ref[h] = st[h] * decay[ci][:, kq[h]] + upd[ci][h]
        for h in heads:
            o_n = _rms(o[h], gn_ref[...])
            gg = gg_ref[rs, kv[h]].astype(F32)
            o_ref[rs, kv[h]] = (o_n[:rows_in] * (gg * jax.nn.sigmoid(gg))).astype(o_ref.dtype)

    @pl.when(c == pl.num_programs(1) - 1)
    def _():
        for h in range(GLA_HEADS):
            sout_ref[h] = st_ref[h].T


def gla(proj, wa2, ba, gn, s0, layer, depth, stacked, batch, rows_per_batch, rows_in, n_inner, n_valid):
    m = proj.shape[0]
    r = rows_in * n_inner
    n_steps = rows_per_batch // r
    has_s0 = s0 is not None
    if s0 is None:
        s0 = jnp.zeros((1, 1, GLA_HEADS, GLA_DK, GLA_DV), F32)
        s0_map = lambda b, c: (0, 0, 0, 0, 0)
    else:
        s0_map = lambda b, c: (layer, b, 0, 0, 0)
    wk, wv = GLA_HEADS * GLA_DK, GLA_HEADS * GLA_DV
    rowmap = lambda off: (lambda b, c: (b * n_steps + c, off))
    state_blk = (None, GLA_HEADS, GLA_DK, GLA_DV)
    kern = functools.partial(_gla_kernel, rows_in=rows_in, n_inner=n_inner, n_valid=n_valid, has_s0=has_s0)
    in_specs = [pl.BlockSpec((r, wk), rowmap(COL_GQ // wk)),
                pl.BlockSpec((r, wk), rowmap(COL_GK // wk)),
                pl.BlockSpec((r, wv), rowmap(COL_GV // wv)),
                pl.BlockSpec((r, wv), rowmap(COL_GG // wv)),
                pl.BlockSpec((r, LANES), rowmap(COL_SM // LANES)),
                pl.BlockSpec((GLA_GATE_RANK, wk), lambda b, c: (0, 0)),
                pl.BlockSpec((1, wk), lambda b, c: (0, 0)),
                pl.BlockSpec((1, GLA_DV), lambda b, c: (0, 0)),
                pl.BlockSpec((None,) + state_blk, s0_map)]
    args = (proj, proj, proj, proj, proj, wa2, ba, gn, s0)
    aliases = {}
    if stacked is not None:
        in_specs.append(pl.BlockSpec(memory_space=pl.ANY))
        aliases = {len(args): 1}
        args += (stacked,)
    return pl.pallas_call(
        kern,
        grid=(batch, n_steps),
        in_specs=in_specs,
        out_specs=[pl.BlockSpec((r, wv), rowmap(0)),
                   pl.BlockSpec((None,) + state_blk, lambda b, c: (layer, b, 0, 0, 0))],
        out_shape=[jax.ShapeDtypeStruct((m, D_MODEL), proj.dtype),
                   jax.ShapeDtypeStruct((depth, batch, GLA_HEADS, GLA_DK, GLA_DV), F32)],
        input_output_aliases=aliases,
        scratch_shapes=[pltpu.VMEM((GLA_HEADS, GLA_DV, GLA_DK), F32)],
        compiler_params=_cparams(("parallel", "arbitrary")),
        name="gla",
    )(*args)


def _rope(x, c, sa, sb):
    return x * c + pltpu.roll(x, LANES - ROPE_HALF, 1) * sa + pltpu.roll(x, ROPE_HALF, 1) * sb


def _nsa_prep_kernel(*refs, with_q):
    if with_q:
        nq_ref, nkv_ref, c_ref, sa_ref, sb_ref, q_ref, cmp_ref, sel_ref, win_ref = refs
    else:
        nkv_ref, c_ref, sa_ref, sb_ref, cmp_ref, sel_ref, win_ref = refs
    c, sa, sb = c_ref[...], sa_ref[...], sb_ref[...]
    if with_q:
        for h in range(NSA_HEADS):
            q_ref[h] = _rope(nq_ref[:, h * NSA_DH:(h + 1) * NSA_DH].astype(F32), c, sa, sb)
    for s, out in enumerate((cmp_ref, sel_ref, win_ref)):
        base = s * KV_ROW
        for hh in range(NSA_KV_HEADS):
            lo = hh * NSA_DH
            out[:, lo:lo + NSA_DH] = _rope(nkv_ref[:, base + lo:base + lo + NSA_DH].astype(F32), c, sa, sb)
        half = NSA_KV_HEADS * NSA_DH
        out[:, half:] = nkv_ref[:, base + half:base + KV_ROW].astype(F32)


def nsa_prep(proj, tabs, tr, with_q):
    m = proj.shape[0]
    n_tab = tabs[0].shape[0] // tr
    tab_spec = pl.BlockSpec((tr, LANES), lambda i: (i % n_tab, 0))
    nkv_w = 3 * KV_ROW
    row_spec = pl.BlockSpec((tr, KV_ROW), lambda i: (i, 0))
    row_shape = jax.ShapeDtypeStruct((m, KV_ROW), F32)
    in_specs = [pl.BlockSpec((tr, nkv_w), lambda i: (i, COL_NKV // nkv_w)), tab_spec, tab_spec, tab_spec]
    out_specs, out_shape, args = [row_spec] * 3, [row_shape] * 3, (proj,) + tuple(tabs)
    if with_q:
        in_specs = [pl.BlockSpec((tr, D_MODEL), lambda i: (i, COL_NQ // D_MODEL))] + in_specs
        out_specs = [pl.BlockSpec((NSA_HEADS, tr, NSA_DH), lambda i: (0, i, 0))] + out_specs
        out_shape = [jax.ShapeDtypeStruct((NSA_HEADS, m, NSA_DH), F32)] + out_shape
        args = (proj,) + args
    return pl.pallas_call(
        functools.partial(_nsa_prep_kernel, with_q=with_q),
        grid=(m // tr,),
        in_specs=in_specs,
        out_specs=out_specs,
        out_shape=out_shape,
        compiler_params=_cparams(("parallel",)),
        name="nsa_prep",
    )(*args)


def _nsa_prep_prompt_kernel(*refs):
    nkv_ref, c_ref, sa_ref, sb_ref = refs[:4]
    slabs, stacked = refs[-6:-3], refs[-3:]
    c, sa, sb = c_ref[...], sa_ref[...], sb_ref[...]
    tr = nkv_ref.shape[0]
    for s in range(3):
        for part in range(KV_PARTS):
            lo = part * NSA_DH
            v = nkv_ref[:, s * KV_ROW + lo:s * KV_ROW + lo + NSA_DH].astype(F32)
            if part < NSA_KV_HEADS:
                v = _rope(v, c, sa, sb)
            slabs[s][:, lo:lo + NSA_DH] = v
            stacked[s][pl.ds(part, tr, stride=KV_PARTS), :] = v


def nsa_prep_prompt(proj, tabs, tr, layer, depth, stacked):
    m = proj.shape[0]
    n_tab = tabs[0].shape[0] // tr
    tab_spec = pl.BlockSpec((tr, LANES), lambda i: (i % n_tab, 0))
    nkv_w = 3 * KV_ROW
    row_spec = pl.BlockSpec((tr, KV_ROW), lambda i: (i, 0))
    in_specs = [pl.BlockSpec((tr, nkv_w), lambda i: (i, COL_NKV // nkv_w)), tab_spec, tab_spec, tab_spec]
    args = (proj,) + tuple(tabs)
    aliases = {}
    if stacked is not None:
        in_specs += [pl.BlockSpec(memory_space=pl.ANY)] * 3
        aliases = {len(args) + k: 3 + k for k in range(3)}
        args += tuple(stacked)
    st_spec = pl.BlockSpec((None, tr * KV_PARTS, NSA_DH), lambda i: (layer, i, 0))
    out = pl.pallas_call(
        _nsa_prep_prompt_kernel,
        grid=(m // tr,),
        in_specs=in_specs,
        out_specs=[row_spec] * 3 + [st_spec] * 3,
        out_shape=[jax.ShapeDtypeStruct((m, KV_ROW), F32)] * 3
                  + [jax.ShapeDtypeStruct((depth, m * KV_PARTS, NSA_DH), F32)] * 3,
        input_output_aliases=aliases,
        compiler_params=_cparams(("parallel",)),
        name="nsa_prep_prompt",
    )(*args)
    return out[:3], out[3:]


def rope_tables(pos):
    inv = 1.0 / (ROPE_THETA ** (jnp.arange(ROPE_HALF, dtype=F32) / ROPE_HALF))
    ang = pos.astype(F32)[:, None] * inv[None, :]
    cos, sin = jnp.cos(ang), jnp.sin(ang)
    n = pos.shape[0]
    rest = LANES - ROPE_DIM
    c = jnp.concatenate([cos, cos, jnp.ones((n, rest), F32)], axis=1)
    sa = jnp.concatenate([-sin, jnp.zeros((n, LANES - ROPE_HALF), F32)], axis=1)
    sb = jnp.concatenate([jnp.zeros((n, ROPE_HALF), F32), sin, jnp.zeros((n, rest), F32)], axis=1)
    return c, sa, sb


_HALVES = (slice(0, CMP_STRIDE), slice(CMP_STRIDE, CMP_LEN))


def _pe_mix_sums(pe_ref, mix_ref):
    return [[jnp.sum(pe_ref[kv][h] * mix_ref[kv][h], axis=0, keepdims=True) for h in _HALVES] for kv in range(2)]


SUBLANES = 8


def _sum_rows_8(p):
    sub = _iota((SUBLANES, LANES), 0)
    n = SUBLANES
    stage = list(p)
    for shift, bit in ((4, 4), (2, 2), (1, 1)):
        nxt = []
        half = len(stage) // 2
        for c in range(half):
            lo = stage[c] + pltpu.roll(stage[c], n - shift, 0)
            hi = stage[c + half] + pltpu.roll(stage[c + half], shift, 0)
            nxt.append(jnp.where((sub & bit) == 0, lo, hi))
        stage = nxt
    return stage[0]


def _half_sums(xs, mix, consts):
    nch = xs.shape[0]
    out = []
    for h, c in zip(_HALVES, consts):
        y = xs * mix[h]
        y = y[:, :SUBLANES, :] + y[:, SUBLANES:, :]
        groups = [_sum_rows_8([y[g + i] for i in range(SUBLANES)]) for g in range(0, nch, SUBLANES)]
        out.append(jnp.concatenate(groups, axis=0) + c)
    return out


def _chunk_sums_kernel(x_ref, pe_ref, mix_ref, a_ref, b_ref):
    nch = x_ref.shape[0] // CMP_STRIDE
    consts = _pe_mix_sums(pe_ref, mix_ref)
    for kv in range(2):
        for hh in range(NSA_KV_HEADS):
            lo = (kv * NSA_KV_HEADS + hh) * NSA_DH
            xs = x_ref[:, lo:lo + NSA_DH].reshape(nch, CMP_STRIDE, NSA_DH)
            a_ref[:, lo:lo + NSA_DH], b_ref[:, lo:lo + NSA_DH] = _half_sums(xs, mix_ref[kv], consts[kv])


def _page_specs(layer):
    def spec(i):
        return pl.BlockSpec((None, None, PAGE_SIZE * KV_PARTS, NSA_DH),
                            lambda b, s, pt: (layer, pt[b, s * PAGES_PER_STEP + i], 0, 0))
    return [spec(i) for i in range(PAGES_PER_STEP)]


def _page_part(page_ref, part):
    return page_ref[pl.ds(part, PAGE_SIZE, stride=KV_PARTS), :]


def _chunk_sums_paged_kernel(pt_ref, *refs):
    page_refs, (pe_ref, mix_ref, a_ref, b_ref) = refs[:PAGES_PER_STEP], refs[PAGES_PER_STEP:]
    nch = PAGE_SIZE // CMP_STRIDE
    consts = _pe_mix_sums(pe_ref, mix_ref)
    for i, page_ref in enumerate(page_refs):
        rows = slice(i * nch, (i + 1) * nch)
        for kv in range(2):
            for hh in range(NSA_KV_HEADS):
                part = kv * NSA_KV_HEADS + hh
                lo = part * NSA_DH
                xs = _page_part(page_ref, part).reshape(nch, CMP_STRIDE, NSA_DH)
                a_ref[rows, lo:lo + NSA_DH], b_ref[rows, lo:lo + NSA_DH] = _half_sums(xs, mix_ref[kv], consts[kv])


def chunk_sums(rows, pe, mix, tr):
    m = rows.shape[0]
    nch = tr // CMP_STRIDE
    full = lambda i: (0, 0, 0)
    return pl.pallas_call(
        _chunk_sums_kernel,
        grid=(m // tr,),
        in_specs=[pl.BlockSpec((tr, KV_ROW), lambda i: (i, 0)),
                  pl.BlockSpec((2, CMP_LEN, NSA_DH), full),
                  pl.BlockSpec((2, CMP_LEN, NSA_DH), full)],
        out_specs=[pl.BlockSpec((nch, KV_ROW), lambda i: (i, 0))] * 2,
        out_shape=[jax.ShapeDtypeStruct((m // CMP_STRIDE, KV_ROW), F32)] * 2,
        compiler_params=_cparams(("parallel",)),
        name="chunk_sums",
    )(rows, pe, mix)


def chunk_sums_paged(cache, layer, page_table, pe, mix):
    batch, n_pages = page_table.shape
    nch = PAGES_PER_STEP * PAGE_SIZE // CMP_STRIDE
    n_steps = n_pages // PAGES_PER_STEP
    full = lambda b, s, pt: (0, 0, 0)
    gs = pltpu.PrefetchScalarGridSpec(
        num_scalar_prefetch=1,
        grid=(batch, n_steps),
        in_specs=_page_specs(layer) + [pl.BlockSpec((2, CMP_LEN, NSA_DH), full),
                                       pl.BlockSpec((2, CMP_LEN, NSA_DH), full)],
        out_specs=[pl.BlockSpec((nch, KV_ROW), lambda b, s, pt: (b * n_steps + s, 0))] * 2,
    )
    return pl.pallas_call(
        _chunk_sums_paged_kernel,
        grid_spec=gs,
        out_shape=[jax.ShapeDtypeStruct((batch * n_steps * nch, KV_ROW), F32)] * 2,
        compiler_params=_cparams(("parallel", "arbitrary")),
        name="chunk_sums_paged",
    )(page_table, *([cache] * PAGES_PER_STEP), pe, mix)


def _cmp_mlp_kernel(a_ref, b_ref, w1_ref, w2_ref, o_ref):
    h = a_ref[...] + b_ref[...]
    for kv in range(2):
        for hh in range(NSA_KV_HEADS):
            lo = (kv * NSA_KV_HEADS + hh) * NSA_DH
            y = jax.nn.gelu(_dot_3x(h[:, lo:lo + NSA_DH], w1_ref[kv]))
            o_ref[:, lo:lo + NSA_DH] = _dot_3x(y, w2_ref[kv])


def cmp_mlp(a, b_shift, w1, w2, tr):
    m = a.shape[0]
    return pl.pallas_call(
        _cmp_mlp_kernel,
        grid=(m // tr,),
        in_specs=[pl.BlockSpec((tr, KV_ROW), lambda i: (i, 0)),
                  pl.BlockSpec((tr, KV_ROW), lambda i: (i, 0)),
                  pl.BlockSpec((2, NSA_DH, CMP_HIDDEN), lambda i: (0, 0, 0)),
                  pl.BlockSpec((2, CMP_HIDDEN, NSA_DH), lambda i: (0, 0, 0))],
        out_specs=pl.BlockSpec((tr, KV_ROW), lambda i: (i, 0)),
        out_shape=jax.ShapeDtypeStruct((m, KV_ROW), F32),
        compiler_params=_cparams(("parallel",)),
        name="cmp_mlp",
    )(a, b_shift, w1, w2)


def compressed_kv(a, b, batch, w1, w2, tr):
    nch = a.shape[0] // batch
    b3 = b.reshape(batch, nch, KV_ROW)
    b_shift = jnp.concatenate([b3[:, 1:], jnp.zeros((batch, 1, KV_ROW), F32)], axis=1).reshape(batch * nch, KV_ROW)
    return cmp_mlp(a, b_shift, w1, w2, tr)


def _select_topn(imp, n_top):
    j = _iota(imp.shape, 1)
    big = jnp.int32(imp.shape[1])
    sel = jnp.zeros(imp.shape, F32)
    for _ in range(n_top):
        m = jnp.max(imp, axis=-1, keepdims=True)
        idx = jnp.min(jnp.where(imp == m, j, big), axis=-1, keepdims=True)
        hit = j == idx
        sel = jnp.where(hit, 1.0, sel)
        imp = jnp.where(hit, NEG_INF, imp)
    return sel


def _importance(p_grp, qpos, n_cmp_pad, n_slc, width):
    n_r = _iota((n_cmp_pad, 1), 0) * CMP_STRIDE
    j_c = _iota((1, width), 1)
    overlap = ((n_r < (j_c + 1) * SEL_LEN) & (n_r + CMP_LEN > j_c * SEL_LEN)).astype(BF16)
    p_hi, p_lo = _split_bf16(p_grp)
    imp = _dot(p_hi, overlap) + _dot(p_lo, overlap)
    cur = qpos >> SEL_SHIFT
    forced = (j_c == 0) | (j_c == cur) | (j_c == cur - 1)
    imp = jnp.where(forced, FORCED_SCORE, imp)
    imp = jnp.where(j_c > cur, -1.0, imp)
    return jnp.where(j_c >= n_slc, -2.0, imp)


LOG2E = 1.4426950408889634


def _nsa_prompt_kernel(nq_ref, c_ref, sa_ref, sb_ref, kc_ref, vc_ref, ks_ref, vs_ref, kw_ref, vw_ref, sm_ref, o_ref,
                       qf_ref, qb_ref, s_ref, p_ref, m_ref, l_ref, a_ref, acc_ref, ob_ref, *, Q, T, TK):
    G = NSA_GROUP
    hk = pl.program_id(1)
    qi = pl.program_id(2)
    q0 = qi * Q
    scale = NSA_DH ** -0.5
    n_cmp_pad = kc_ref.shape[0]
    n_slc = T // SEL_LEN
    qpos = q0 + _iota((Q, 1), 0)
    c, sa, sb = c_ref[...], sa_ref[...], sb_ref[...]
    for g in range(G):
        qg = _rope(nq_ref[:, g * NSA_DH:(g + 1) * NSA_DH].astype(F32), c, sa, sb)
        qf_ref[g * Q:(g + 1) * Q, :] = qg
        qb_ref[g * Q:(g + 1) * Q, :] = (qg * (scale * LOG2E)).astype(BF16)
    qs = qf_ref[...]

    ng = jax.nn.sigmoid(sm_ref[...].astype(F32))

    def gate(g, c3):
        lane0 = SM_GATE0 + 3 * g + c3
        lane1 = lane0 + 3 * G
        return jnp.where(hk == 0, ng[:, lane0:lane0 + 1], ng[:, lane1:lane1 + 1])

    def softmax_tile(kb, vb, bias, width):
        s_ref[:, :width] = _dot_nt(qb_ref[...], kb)
        for g in range(G):
            rows = pl.ds(g * Q, Q)
            s = s_ref[rows, :width] + bias
            m_old = m_ref[rows, :]
            m_new = jnp.maximum(m_old, jnp.max(s, axis=-1, keepdims=True))
            alpha = jnp.exp2(m_old - m_new)
            p = jnp.exp2(s - jnp.tile(m_new, (1, width // LANES)))
            l_ref[rows, :] = alpha * l_ref[rows, :] + jnp.sum(p, axis=-1, keepdims=True)
            a_ref[rows, :] = alpha
            m_ref[rows, :] = m_new
            p_ref[rows, :width] = p.astype(BF16)
        return _dot(p_ref[:, :width], vb)

    def reset():
        m_ref[...] = jnp.full_like(m_ref, NEG_INF)
        l_ref[...] = jnp.zeros_like(l_ref)

    cmp_valid = _iota((1, n_cmp_pad), 1) * CMP_STRIDE + (CMP_LEN - 1) <= qpos
    p_grp = jnp.zeros((Q, n_cmp_pad), F32)
    s_cmp = _dot_nt_3x(qs, kc_ref[...]) * scale
    vcb = vc_ref[...].astype(BF16)
    for g in range(G):
        p = _masked_softmax(s_cmp[g * Q:(g + 1) * Q], cmp_valid)
        p_grp = p_grp + p
        ob_ref[g] = gate(g, 0) * _dot(p.astype(BF16), vcb)

    ks0 = pl.multiple_of(jnp.maximum(q0 - WINDOW, 0), LANES)
    wlen = WINDOW + Q
    wpos = ks0 + _iota((1, wlen), 1)
    win_bias = jnp.where((wpos <= qpos) & (wpos > qpos - WINDOW), 0.0, NEG_INF)
    reset()
    pv = softmax_tile(kw_ref[pl.ds(ks0, wlen), :].astype(BF16), vw_ref[pl.ds(ks0, wlen), :].astype(BF16),
                      win_bias, wlen)
    for g in range(G):
        rows = pl.ds(g * Q, Q)
        ob_ref[g] = ob_ref[g] + gate(g, 2) * (pv[g * Q:(g + 1) * Q] / jnp.maximum(l_ref[rows, :], 1e-30))

    n_top = min(SEL_TOPN, n_slc)
    j_r = _iota((n_slc, 1), 0)
    n_c = _iota((1, n_cmp_pad), 1) * CMP_STRIDE
    overlap_t = ((n_c < (j_r + 1) * SEL_LEN) & (n_c + CMP_LEN > j_r * SEL_LEN)).astype(BF16)
    p_hi, p_lo = _split_bf16(p_grp)
    imp = _dot_nt(overlap_t, p_hi) + _dot_nt(overlap_t, p_lo)
    cur = (q0 + _iota((1, Q), 1)) >> SEL_SHIFT
    imp = jnp.where((j_r == 0) | (j_r == cur) | (j_r == cur - 1), FORCED_SCORE, imp)
    imp = jnp.where(j_r > cur, -1.0, imp)
    rank = jnp.zeros((n_slc, Q), F32)
    for jp in range(n_slc):
        other = imp[jp:jp + 1, :]
        beats = (other > imp) | ((other == imp) & (j_r > jp))
        rank = rank + jnp.where(beats, 1.0, 0.0)
    sel_t = jnp.where(rank < n_top, 1.0, 0.0).astype(BF16)

    reset()
    acc_ref[...] = jnp.zeros_like(acc_ref)

    def key_tile(kt, carry):
        k0 = pl.multiple_of(kt * TK, TK)
        kpos = k0 + _iota((1, TK), 1)
        expand = (j_r == (kpos >> SEL_SHIFT)).astype(BF16)
        keymask = _dot_tn(sel_t, expand)
        bias = jnp.where((keymask > 0.5) & (kpos <= qpos), 0.0, NEG_INF)
        pv = softmax_tile(ks_ref[pl.ds(k0, TK), :].astype(BF16), vs_ref[pl.ds(k0, TK), :].astype(BF16), bias, TK)
        acc_ref[...] = a_ref[...] * acc_ref[...] + pv
        return carry

    lax.fori_loop(0, (q0 + Q + TK - 1) // TK, key_tile, 0)

    for g in range(G):
        rows = pl.ds(g * Q, Q)
        o_sel = acc_ref[rows, :] / jnp.maximum(l_ref[rows, :], 1e-30)
        o_ref[g] = (ob_ref[g] + gate(g, 1) * o_sel).astype(o_ref.dtype)


def nsa_prompt(tabs, kvc, sel_rows, win_rows, proj, batch, T, Q):
    m = batch * T
    nq = T // Q
    n_cmp_pad = kvc.shape[0] // batch
    G = NSA_GROUP
    gw = G * NSA_DH
    kern = functools.partial(_nsa_prompt_kernel, Q=Q, T=T, TK=512)
    kcol = lambda off: (lambda b, hk, qi: (b, off + hk))
    tab_spec = pl.BlockSpec((Q, LANES), lambda b, hk, qi: (qi, 0))
    return pl.pallas_call(
        kern,
        grid=(batch, NSA_KV_HEADS, nq),
        in_specs=[pl.BlockSpec((Q, gw), lambda b, hk, qi: (b * nq + qi, COL_NQ // gw + hk)),
                  tab_spec, tab_spec, tab_spec,
                  pl.BlockSpec((n_cmp_pad, NSA_DH), kcol(0)),
                  pl.BlockSpec((n_cmp_pad, NSA_DH), kcol(NSA_KV_HEADS)),
                  pl.BlockSpec((T, NSA_DH), kcol(0)),
                  pl.BlockSpec((T, NSA_DH), kcol(NSA_KV_HEADS)),
                  pl.BlockSpec((T, NSA_DH), kcol(0)),
                  pl.BlockSpec((T, NSA_DH), kcol(NSA_KV_HEADS)),
                  pl.BlockSpec((Q, LANES), lambda b, hk, qi: (b * nq + qi, COL_SM // LANES))],
        out_specs=pl.BlockSpec((G, Q, NSA_DH), lambda b, hk, qi: (hk, b * nq + qi, 0)),
        out_shape=jax.ShapeDtypeStruct((NSA_HEADS, m, NSA_DH), proj.dtype),
        scratch_shapes=[pltpu.VMEM((G * Q, NSA_DH), F32), pltpu.VMEM((G * Q, NSA_DH), BF16),
                        pltpu.VMEM((G * Q, WINDOW + Q), F32), pltpu.VMEM((G * Q, WINDOW + Q), BF16),
                        pltpu.VMEM((G * Q, LANES), F32), pltpu.VMEM((G * Q, LANES), F32),
                        pltpu.VMEM((G * Q, LANES), F32), pltpu.VMEM((G * Q, NSA_DH), F32),
                        pltpu.VMEM((G, Q, NSA_DH), F32)],
        compiler_params=_cparams(("parallel", "parallel", "arbitrary")),
        name="nsa_prompt",
    )(proj, *tabs, kvc, kvc, sel_rows, sel_rows, win_rows, win_rows, proj)


SEL_W = 384


def _sample_cmp_select_kernel(q_ref, kvc_ref, ocmp_ref, selm_ref, *, n_slc):
    G, R = NSA_GROUP, SAMPLE_ROWS
    scale = NSA_DH ** -0.5
    n_cmp_pad = kvc_ref.shape[0]
    qpos_s = PAST_LEN + (_iota((G * R, 1), 0) & (R - 1))
    end = _iota((1, n_cmp_pad), 1) * CMP_STRIDE + (CMP_LEN - 1)
    p_grp = []
    for hk in range(NSA_KV_HEADS):
        qs = q_ref[hk * G:(hk + 1) * G].reshape(G * R, NSA_DH)
        kc = kvc_ref[:, hk * NSA_DH:(hk + 1) * NSA_DH]
        vc = kvc_ref[:, (NSA_KV_HEADS + hk) * NSA_DH:(NSA_KV_HEADS + hk + 1) * NSA_DH]
        p = _masked_softmax(_dot_nt_3x(qs, kc) * scale, end <= qpos_s)
        o_cmp = _dot(p.astype(BF16), vc.astype(BF16))
        for g in range(G):
            ocmp_ref[hk * G + g] = o_cmp[g * R:(g + 1) * R]
        p_grp.append(jnp.sum(p.reshape(G, R, n_cmp_pad), axis=0))
    qpos = PAST_LEN + (_iota((NSA_KV_HEADS * R, 1), 0) & (R - 1))
    imp = _importance(jnp.concatenate(p_grp, axis=0), qpos, n_cmp_pad, n_slc, SEL_W)
    sel = _select_topn(imp, min(SEL_TOPN, n_slc))
    for hk in range(NSA_KV_HEADS):
        selm_ref[hk] = sel[hk * R:(hk + 1) * R]


def sample_cmp_select(q_hm, kvc, batch, n_slc):
    R = SAMPLE_ROWS
    n_cmp_pad = kvc.shape[0] // batch
    kern = functools.partial(_sample_cmp_select_kernel, n_slc=n_slc)
    hm_spec = pl.BlockSpec((NSA_HEADS, R, NSA_DH), lambda b: (0, b, 0))
    return pl.pallas_call(
        kern,
        grid=(batch,),
        in_specs=[hm_spec, pl.BlockSpec((n_cmp_pad, KV_ROW), lambda b: (b, 0))],
        out_specs=[hm_spec, pl.BlockSpec((None, NSA_KV_HEADS, R, SEL_W), lambda b: (b, 0, 0, 0))],
        out_shape=[jax.ShapeDtypeStruct((NSA_HEADS, batch * R, NSA_DH), F32),
                   jax.ShapeDtypeStruct((batch, NSA_KV_HEADS, R, SEL_W), F32)],
        compiler_params=_cparams(("parallel",)),
        name="sample_cmp_select",
    )(q_hm, kvc)


def _sample_attn_kernel(pt_ref, q_ref, selm_ref, *refs, n_steps):
    page_refs = refs[:PAGES_PER_STEP]
    nsel_ref, cwin_ref, nwin_ref, sm_ref, ocmp_ref, o_ref, m_ref, l_ref, acc_ref = refs[PAGES_PER_STEP:]
    G, R = NSA_GROUP, SAMPLE_ROWS
    p = pl.program_id(1)
    scale = NSA_DH ** -0.5
    t_s = _iota((G * R, 1), 0) & (R - 1)
    n_keys = PAGES_PER_STEP * PAGE_SIZE

    @pl.when(p == 0)
    def _():
        m_ref[...] = jnp.full_like(m_ref, NEG_INF)
        l_ref[...] = jnp.zeros_like(l_ref)
        acc_ref[...] = jnp.zeros_like(acc_ref)

    def q_stack(hk, mult):
        return (q_ref[hk * G:(hk + 1) * G].reshape(G * R, NSA_DH) * mult).astype(BF16)

    def update(hk, s, pr_of, v):
        m_old = m_ref[hk]
        m_new = jnp.maximum(m_old, jnp.max(s, axis=-1, keepdims=True))
        alpha = jnp.exp2(m_old - m_new)
        pr = pr_of(jnp.exp2(s - jnp.tile(m_new, (1, s.shape[1] // LANES))))
        l_ref[hk] = alpha * l_ref[hk] + jnp.sum(pr, axis=-1, keepdims=True)
        acc_ref[hk] = alpha * acc_ref[hk] + _dot(pr.astype(BF16), v)
        m_ref[hk] = m_new

    key = _iota((1, n_keys), 1)
    blk_of_key = (n_keys // SEL_LEN) * p + (key >> SEL_SHIFT)
    expand = (_iota((SEL_W, 1), 0) == blk_of_key).astype(BF16)
    hks = range(NSA_KV_HEADS)
    kpg = [jnp.concatenate([_page_part(r, hk).astype(BF16) for r in page_refs], axis=0) for hk in hks]
    vpg = [jnp.concatenate([_page_part(r, NSA_KV_HEADS + hk).astype(BF16) for r in page_refs], axis=0)
           for hk in hks]
    keymask = [_dot(selm_ref[hk].astype(BF16), expand) for hk in hks]
    s = [_dot_nt(q_stack(hk, scale * LOG2E), kpg[hk])
         + jnp.where(jnp.concatenate([keymask[hk]] * G, axis=0) > 0.5, 0.0, NEG_INF) for hk in hks]
    m_old = [m_ref[hk] for hk in hks]
    m_new = [jnp.maximum(m_old[hk], jnp.max(s[hk], axis=-1, keepdims=True)) for hk in hks]
    pr = [jnp.exp2(s[hk] - jnp.tile(m_new[hk], (1, n_keys // LANES))) for hk in hks]
    pv = [_dot(pr[hk].astype(BF16), vpg[hk]) for hk in hks]
    for hk in hks:
        alpha = jnp.exp2(m_old[hk] - m_new[hk])
        l_ref[hk] = alpha * l_ref[hk] + jnp.sum(pr[hk], axis=-1, keepdims=True)
        acc_ref[hk] = alpha * acc_ref[hk] + pv[hk]
        m_ref[hk] = m_new[hk]

    def win_part(ref, part):
        return ref[pl.ds(part, WINDOW, stride=KV_PARTS), :]

    @pl.when(p == n_steps - 1)
    def _():
        ng = jax.nn.sigmoid(sm_ref[...])
        new_blk = PAST_LEN // SEL_LEN
        zpad = jnp.zeros((LANES - R, NSA_DH), F32)
        jn = _iota((1, LANES), 1)
        for hk in range(NSA_KV_HEADS):
            klo, vlo = hk * NSA_DH, (NSA_KV_HEADS + hk) * NSA_DH
            kn = jnp.concatenate([nsel_ref[:, klo:klo + NSA_DH], zpad], axis=0)
            vn = jnp.concatenate([nsel_ref[:, vlo:vlo + NSA_DH], zpad], axis=0)
            picked = jnp.concatenate([selm_ref[hk][:, new_blk:new_blk + 1]] * G, axis=0) > 0.5
            valid = picked & (jn <= t_s) & (jn < R)
            s = jnp.where(valid, _dot_nt(q_stack(hk, scale * LOG2E), kn.astype(BF16)), NEG_INF)
            update(hk, s, lambda e: jnp.where(valid, e, 0.0), vn.astype(BF16))
            o_sel = acc_ref[hk] / jnp.maximum(l_ref[hk], 1e-30)
            qs = q_stack(hk, scale)
            kw = jnp.concatenate([win_part(cwin_ref, hk), nwin_ref[:, klo:klo + NSA_DH], zpad], axis=0)
            vw = jnp.concatenate([win_part(cwin_ref, NSA_KV_HEADS + hk), nwin_ref[:, vlo:vlo + NSA_DH], zpad], axis=0)
            iw = _iota((1, WINDOW + LANES), 1)
            wvalid = ((iw < WINDOW) & (iw > t_s)) | ((iw >= WINDOW) & (iw - WINDOW <= t_s) & (iw - WINDOW < R))
            pw = _masked_softmax(_dot_nt(qs, kw.astype(BF16)), wvalid)
            o_win = _dot(pw.astype(BF16), vw.astype(BF16))
            for g in range(G):
                h = hk * G + g
                lane = SM_GATE0 + 3 * h
                rows = slice(g * R, (g + 1) * R)
                o_ref[h] = (ng[:, lane:lane + 1] * ocmp_ref[h] + ng[:, lane + 1:lane + 2] * o_sel[rows]
                            + ng[:, lane + 2:lane + 3] * o_win[rows])


def sample_attn(q_hm, selm, cache_sel, layer, page_table, new_sel, cache_win, new_win, proj, o_cmp_hm):
    batch, n_pages = page_table.shape
    n_steps = n_pages // PAGES_PER_STEP
    G, R = NSA_GROUP, SAMPLE_ROWS
    hm_spec = pl.BlockSpec((NSA_HEADS, R, NSA_DH), lambda b, p, pt: (0, b, 0))
    row_spec = pl.BlockSpec((R, KV_ROW), lambda b, p, pt: (b, 0))
    gs = pltpu.PrefetchScalarGridSpec(
        num_scalar_prefetch=1,
        grid=(batch, n_steps),
        in_specs=[hm_spec,
                  pl.BlockSpec((None, NSA_KV_HEADS, R, SEL_W), lambda b, p, pt: (b, 0, 0, 0))]
                 + _page_specs(layer)
                 + [row_spec,
                    pl.BlockSpec((None, None, WINDOW * KV_PARTS, NSA_DH), lambda b, p, pt: (layer, b, 0, 0)),
                    row_spec,
                    pl.BlockSpec((R, LANES), lambda b, p, pt: (b, COL_SM // LANES)),
                    hm_spec],
        out_specs=hm_spec,
        scratch_shapes=[pltpu.VMEM((NSA_KV_HEADS, G * R, LANES), F32),
                        pltpu.VMEM((NSA_KV_HEADS, G * R, LANES), F32),
                        pltpu.VMEM((NSA_KV_HEADS, G * R, NSA_DH), F32)],
    )
    kern = functools.partial(_sample_attn_kernel, n_steps=n_steps)
    return pl.pallas_call(
        kern,
        grid_spec=gs,
        out_shape=jax.ShapeDtypeStruct((NSA_HEADS, batch * R, NSA_DH), F32),
        compiler_params=_cparams(("parallel", "arbitrary")),
        name="sample_attn",
    )(page_table, q_hm, selm, *([cache_sel] * PAGES_PER_STEP), new_sel, cache_win, new_win, proj, o_cmp_hm)


PACK_W = 512
PACK_SRC = PACK_W // LANES + 1
_MAIN_SEGMENTS = _SRC_SEGMENTS[:7]


def _pack_table():
    first, shift = [], []
    for lo, hi in _MAIN_SEGMENTS:
        assert (hi - lo) % PACK_W == 0
        for c in range(lo, hi, PACK_W):
            first.append(c // LANES)
            shift.append(c % LANES)
    assert len(first) * PACK_W == COL_SM
    n = N_PROJ // PACK_W
    first += [0] * (n - len(first))
    shift += [0] * (n - len(shift))
    return np.array([first, shift], np.int32)


def _pack_kernel(tab_ref, *refs):
    srcs, (ga_ref, ng_ref, o_ref) = refs[:PACK_SRC], refs[PACK_SRC:]
    j = pl.program_id(1)
    n_main = COL_SM // PACK_W
    shift = tab_ref[1, j]
    lane = _iota((1, LANES), 1)

    @pl.when(j >= n_main)
    def _():
        sm = jnp.where(lane < GLA_GATE_RANK, ga_ref[...].astype(F32),
                       jnp.where(lane < SM_USED, ng_ref[...].astype(F32), 0.0))
        o_ref[:, :LANES] = sm.astype(BF16)
        o_ref[:, LANES:] = jnp.zeros((o_ref.shape[0], PACK_W - LANES), BF16)

    for sv in sorted({lo % LANES for lo, _ in _MAIN_SEGMENTS}):
        @pl.when((j < n_main) & (shift == sv))
        def _():
            for i in range(PACK_W // LANES):
                cols = slice(i * LANES, (i + 1) * LANES)
                if sv == 0:
                    o_ref[:, cols] = srcs[i][...].astype(BF16)
                else:
                    a = pltpu.roll(pltpu.bitcast(srcs[i][...], jnp.uint32), LANES - sv, 1)
                    b = pltpu.roll(pltpu.bitcast(srcs[i + 1][...], jnp.uint32), LANES - sv, 1)
                    o_ref[:, cols] = pltpu.bitcast(jnp.where(lane < LANES - sv, a, b), BF16)


def _pack_w_in(w_in):
    depth, d, n_in = w_in.shape
    (ga_lo, ga_hi), (ng_lo, ng_hi) = _SRC_SEGMENTS[7:]
    assert ga_lo % LANES == 0 and ga_hi - ga_lo == GLA_GATE_RANK
    assert ng_lo % LANES == GLA_GATE_RANK and ng_hi - ng_lo == SM_USED - GLA_GATE_RANK
    last_blk = (n_in - 1) // LANES

    def src_spec(i):
        return pl.BlockSpec((None, d, LANES), lambda l, j, tab: (l, 0, jnp.minimum(tab[0, j] + i, last_blk)))

    def fixed_spec(col):
        return pl.BlockSpec((None, d, LANES), lambda l, j, tab: (l, 0, col // LANES))

    gs = pltpu.PrefetchScalarGridSpec(
        num_scalar_prefetch=1,
        grid=(depth, N_PROJ // PACK_W),
        in_specs=[src_spec(i) for i in range(PACK_SRC)] + [fixed_spec(ga_lo), fixed_spec(ng_lo)],
        out_specs=pl.BlockSpec((None, d, PACK_W), lambda l, j, tab: (l, 0, j)),
    )
    return pl.pallas_call(
        _pack_kernel,
        grid_spec=gs,
        out_shape=jax.ShapeDtypeStruct((depth, d, N_PROJ), BF16),
        compiler_params=_cparams(("parallel", "arbitrary")),
        name="pack_w_in",
    )(jnp.asarray(_pack_table()), *([w_in.astype(BF16)] * (PACK_SRC + 2)))


def _layer_prompt(x, lw, tabs, layer, depth, batch, T, stacked_rows, gla_states):
    proj = norm_matmul(x, lw['n0'], lw['w_in'], layer, 1024, 2048, BF16)
    o_a, gla_states = gla(proj, lw['wa2'], lw['ba'], lw['gn'], None, layer, depth, gla_states, batch, T,
                          GLA_CHUNK, 4, GLA_CHUNK)
    (cmp_rows, sel_rows, win_rows), stacked_rows = nsa_prep_prompt(proj, tabs, 512, layer, depth, stacked_rows)
    a, b = chunk_sums(cmp_rows, lw['pe'], lw['mix'], 512)
    kvc = compressed_kv(a, b, batch, lw['w1'], lw['w2'], 128)
    o_b = nsa_prompt(tabs, kvc, sel_rows, win_rows, proj, batch, T, 256)
    x = merge_wo(x, proj, o_a, o_b, lw['n1'], lw['w_o'], layer, 512)
    x = mlp(x, lw['n2'], lw['n3'], lw['w_up'], lw['w_down'], layer, 512, 1024)
    return x, gla_states, stacked_rows


def _layer_sample(x, lw, tabs, layer, depth, batch, cache_cmp, cache_sel, cache_win, s0, page_table, gla_states):
    R = SAMPLE_ROWS
    m = batch * R
    proj = norm_matmul(x, lw['n0'], lw['w_in'], layer, m, 1024, F32)
    o_a, gla_states = gla(proj, lw['wa2'], lw['ba'], lw['gn'], s0, layer, depth, gla_states, batch, R, R, 1, 4)
    q_hm, cmp_rows, sel_rows, win_rows = nsa_prep(proj, tabs, m, with_q=True)
    a, b = chunk_sums_paged(cache_cmp, layer, page_table, lw['pe'], lw['mix'])
    kvc = compressed_kv(a, b, batch, lw['w1'], lw['w2'], 1024)
    n_slc = -(-(PAST_LEN + 4) // SEL_LEN)
    o_cmp, selm = sample_cmp_select(q_hm, kvc, batch, n_slc)
    o_b = sample_attn(q_hm, selm, cache_sel, layer, page_table, sel_rows, cache_win, win_rows, proj, o_cmp)
    x = merge_wo(x, proj, o_a, o_b, lw['n1'], lw['w_o'], layer, m)
    x = mlp(x, lw['n2'], lw['n3'], lw['w_up'], lw['w_down'], layer, m, 1024)
    return x, (cmp_rows, sel_rows, win_rows), gla_states


def kernel(x_prompt, x_sample, cache_cmp_kv, cache_sel_kv, cache_win_kv, state_gla, page_table,
           w_in, gla_wa2, gla_ba, gla_norm, cmp_pe, cmp_mix, cmp_w1, cmp_w2, w_o, norms, w_up, w_down):
    bp, T, d = x_prompt.shape
    bs, ss, _ = x_sample.shape
    R = SAMPLE_ROWS
    depth = w_in.shape[0]
    n_pool = cache_cmp_kv.shape[1]
    wb = cache_win_kv.shape[2]
    assert wb == WINDOW and T % 512 == 0 and ss <= R

    w_in_p = _pack_w_in(w_in)
    w_o_b, w_up_b, w_down_b = w_o.astype(BF16), w_up.astype(BF16), w_down.astype(BF16)
    cache_cmp = cache_cmp_kv.reshape(depth, n_pool, PAGE_SIZE * KV_PARTS, NSA_DH)
    cache_sel = cache_sel_kv.reshape(depth, n_pool, PAGE_SIZE * KV_PARTS, NSA_DH)
    cache_win = cache_win_kv.reshape(depth, bs, wb * KV_PARTS, NSA_DH)

    tabs_p = rope_tables(jnp.arange(T, dtype=jnp.int32))
    tabs_s = tuple(jnp.tile(t, (bs, 1)) for t in rope_tables(PAST_LEN + jnp.arange(R, dtype=jnp.int32)))

    y_p = x_prompt.reshape(bp * T, d)
    y_s = jnp.pad(x_sample, ((0, 0), (0, R - ss), (0, 0))).reshape(bs * R, d)
    st_s, stacked_rows, gla_p, gla_s = [], None, None, None
    for layer in range(depth):
        lw = {'w_in': w_in_p, 'wa2': gla_wa2[layer], 'ba': gla_ba[layer][None, :],
              'gn': gla_norm[layer][None, :], 'pe': cmp_pe[layer], 'mix': cmp_mix[layer],
              'w1': cmp_w1[layer], 'w2': cmp_w2[layer], 'w_o': w_o_b,
              'n0': norms[layer, 0][None, :], 'n1': norms[layer, 1][None, :],
              'n2': norms[layer, 2][None, :], 'n3': norms[layer, 3][None, :],
              'w_up': w_up_b, 'w_down': w_down_b}
        y_p, gla_p, stacked_rows = _layer_prompt(y_p, lw, tabs_p, layer, depth, bp, T, stacked_rows, gla_p)
        y_s, s_s, gla_s = _layer_sample(y_s, lw, tabs_s, layer, depth, bs, cache_cmp, cache_sel, cache_win,
                                        state_gla, page_table, gla_s)
        st_s.append(s_s)

    kv_shape = (2, NSA_KV_HEADS, NSA_DH)

    def rows_p(i):
        return stacked_rows[i].reshape(depth, bp, T, *kv_shape)

    def rows_s(i):
        return jnp.stack([s[i].reshape(bs, R, *kv_shape)[:, :ss] for s in st_s])

    win_p = rows_p(2)[:, :, T - min(WINDOW, T):]
    win_s = jnp.concatenate([cache_win_kv[:, :, ss:], rows_s(2)], axis=2)
    return (y_p.reshape(bp, T, d), y_s.reshape(bs, R, d)[:, :ss],
            rows_p(0), rows_s(0), rows_p(1), rows_s(1), win_p, win_s,
            gla_p, gla_s)
```

```python
import functools

import jax
import jax.numpy as jnp
import numpy as np
from jax import lax
from jax.experimental import pallas as pl
from jax.experimental.pallas import tpu as pltpu

F32 = jnp.float32
BF16 = jnp.bfloat16

D_MODEL = 2048
DEPTH = 4
PAST_LEN = 16384
PAGE_SIZE = 128

GLA_HEADS = 4
GLA_DK = 256
GLA_DV = 512
GLA_GATE_RANK = 16
GLA_TAU = 16.0
GLA_CHUNK = 64
GLA_CHUNK_SHIFT = 6

NSA_HEADS = 16
NSA_KV_HEADS = 2
NSA_DH = 128
NSA_GROUP = NSA_HEADS // NSA_KV_HEADS
CMP_LEN = 32
CMP_STRIDE = 16
CMP_HIDDEN = 256
SEL_LEN = 64
SEL_SHIFT = 6
SEL_TOPN = 16
WINDOW = 512
WIN_QBLK = 128

ROPE_THETA = 500000.0
ROPE_DIM = 32
ROPE_HALF = 16
MLP_HIDDEN = 4 * D_MODEL
NORM_EPS = 1e-6
NEG_INF = -1e30
FORCED_SCORE = 1e6
KV_PARTS = 2 * NSA_KV_HEADS
KV_ROW = KV_PARTS * NSA_DH
PAGES_PER_STEP = 32

VMEM_LIMIT_BYTES = 56 * 1024 * 1024
LANES = 128

COL_MG = 0
COL_GV = 4096
COL_GG = 6144
COL_NQ = 8192
COL_GQ = 10240
COL_GK = 11264
COL_NKV = 12288
COL_SM = 13824
N_PROJ = 14336
SM_GATE0 = GLA_GATE_RANK
SM_USED = GLA_GATE_RANK + 3 * NSA_HEADS

_SRC_SEGMENTS = ((9792, 13888), (2048, 4096), (4096, 6144), (6160, 8208), (0, 1024), (1024, 2048),
                 (8208, 9744), (6144, 6160), (9744, 9792))

SAMPLE_ROWS = 8


def _cparams(sem):
    return pltpu.CompilerParams(dimension_semantics=sem, vmem_limit_bytes=VMEM_LIMIT_BYTES)


def _rms(x, g):
    return x * lax.rsqrt(jnp.mean(x * x, axis=-1, keepdims=True) + NORM_EPS) * g


def _dot(a, b, precision=None):
    return jnp.dot(a, b, preferred_element_type=F32, precision=precision)


def _dot_nt(a, b, precision=None):
    return lax.dot_general(a, b, (((1,), (1,)), ((), ())), preferred_element_type=F32, precision=precision)


def _dot_tn(a, b, precision=None):
    return lax.dot_general(a, b, (((0,), (0,)), ((), ())), preferred_element_type=F32, precision=precision)


def _split_bf16(x):
    hi = x.astype(BF16)
    return hi, (x - hi.astype(F32)).astype(BF16)


def _dot_nt_3x(a, b):
    ah, al = _split_bf16(a)
    bh, bl = _split_bf16(b)
    return (_dot_nt(jnp.concatenate([ah, al], axis=1), jnp.concatenate([bh, bh], axis=1))
            + _dot_nt(ah, bl))


def _dot_3x(a, b):
    ah, al = _split_bf16(a)
    bh, bl = _split_bf16(b)
    return (_dot(jnp.concatenate([ah, al], axis=1), jnp.concatenate([bh, bh], axis=0))
            + _dot(ah, bl))


def _iota(shape, dim):
    return lax.broadcasted_iota(jnp.int32, shape, dim)


def _masked_softmax(s, valid):
    s = jnp.where(valid, s, NEG_INF)
    m = jnp.max(s, axis=-1, keepdims=True)
    p = jnp.where(valid, jnp.exp(s - m), 0.0)
    return p / jnp.maximum(jnp.sum(p, axis=-1, keepdims=True), 1e-30)


def _norm_matmul_kernel(x_ref, g_ref, w_ref, o_ref, h_ref):
    @pl.when(pl.program_id(1) == 0)
    def _():
        h_ref[...] = _rms(x_ref[...], g_ref[...]).astype(BF16)

    o_ref[...] = _dot(h_ref[...], w_ref[...]).astype(o_ref.dtype)


def norm_matmul(x, g, w, layer, tm, tn, out_dtype):
    m, k = x.shape
    n = w.shape[2]
    return pl.pallas_call(
        _norm_matmul_kernel,
        grid=(m // tm, n // tn),
        in_specs=[pl.BlockSpec((tm, k), lambda i, j: (i, 0)),
                  pl.BlockSpec((1, k), lambda i, j: (0, 0)),
                  pl.BlockSpec((None, k, tn), lambda i, j: (layer, 0, j))],
        out_specs=pl.BlockSpec((tm, tn), lambda i, j: (i, j)),
        out_shape=jax.ShapeDtypeStruct((m, n), out_dtype),
        scratch_shapes=[pltpu.VMEM((tm, k), BF16)],
        compiler_params=_cparams(("parallel", "arbitrary")),
        name="norm_matmul",
    )(x, g, w)


def _merge_wo_kernel(x_ref, mg0_ref, mg1_ref, oa_ref, ob_ref, g_ref, wo_ref, o_ref):
    ob = jnp.concatenate([ob_ref[h].astype(F32) for h in range(NSA_HEADS)], axis=1)
    a = (jax.nn.sigmoid(mg0_ref[...].astype(F32)) * oa_ref[...].astype(F32)
         + jax.nn.sigmoid(mg1_ref[...].astype(F32)) * ob)
    m = _dot(a.astype(BF16), wo_ref[...])
    o_ref[...] = x_ref[...] + _rms(m, g_ref[...])


def merge_wo(x, proj, o_a, o_b_hm, g, wo, layer, tm):
    m, d = x.shape
    return pl.pallas_call(
        _merge_wo_kernel,
        grid=(m // tm,),
        in_specs=[pl.BlockSpec((tm, d), lambda i: (i, 0)),
                  pl.BlockSpec((tm, d), lambda i: (i, COL_MG // D_MODEL)),
                  pl.BlockSpec((tm, d), lambda i: (i, COL_MG // D_MODEL + 1)),
                  pl.BlockSpec((tm, d), lambda i: (i, 0)),
                  pl.BlockSpec((NSA_HEADS, tm, NSA_DH), lambda i: (0, i, 0)),
                  pl.BlockSpec((1, d), lambda i: (0, 0)),
                  pl.BlockSpec((None, d, d), lambda i: (layer, 0, 0))],
        out_specs=pl.BlockSpec((tm, d), lambda i: (i, 0)),
        out_shape=jax.ShapeDtypeStruct((m, d), F32),
        compiler_params=_cparams(("parallel",)),
        name="merge_wo",
    )(x, proj, proj, o_a, o_b_hm, g, wo)


def _mlp_kernel(x_ref, g2_ref, g3_ref, wu_ref, wd_ref, o_ref, h_ref):
    j = pl.program_id(1)

    @pl.when(j == 0)
    def _():
        h_ref[...] = _rms(x_ref[...], g2_ref[...]).astype(BF16)
        o_ref[...] = jnp.zeros_like(o_ref)

    u = _dot(h_ref[...], wu_ref[...])
    u = jnp.square(jnp.maximum(u, 0.0)).astype(BF16)
    o_ref[...] += _dot(u, wd_ref[...])

    @pl.when(j == pl.num_programs(1) - 1)
    def _():
        o_ref[...] = x_ref[...] + _rms(o_ref[...], g3_ref[...])


def mlp(x, g2, g3, wu, wd, layer, tm, th):
    m, d = x.shape
    hid = wu.shape[2]
    return pl.pallas_call(
        _mlp_kernel,
        grid=(m // tm, hid // th),
        in_specs=[pl.BlockSpec((tm, d), lambda i, j: (i, 0)),
                  pl.BlockSpec((1, d), lambda i, j: (0, 0)),
                  pl.BlockSpec((1, d), lambda i, j: (0, 0)),
                  pl.BlockSpec((None, d, th), lambda i, j: (layer, 0, j)),
                  pl.BlockSpec((None, th, d), lambda i, j: (layer, j, 0))],
        out_specs=pl.BlockSpec((tm, d), lambda i, j: (i, 0)),
        out_shape=jax.ShapeDtypeStruct((m, d), F32),
        scratch_shapes=[pltpu.VMEM((tm, d), BF16)],
        compiler_params=_cparams(("parallel", "arbitrary")),
        name="mlp",
    )(x, g2, g3, wu, wd)


def _gla_kernel(*refs, rows_in, n_inner, n_valid, has_s0):
    q_ref, k_ref, v_ref, gg_ref, sm_ref, wa2_ref, ba_ref, gn_ref, s0_ref = refs[:9]
    o_ref, sout_ref, st_ref = refs[-3:]
    c = pl.program_id(1)
    C = GLA_CHUNK

    @pl.when(c == 0)
    def _():
        for h in range(GLA_HEADS):
            if has_s0:
                st_ref[h] = s0_ref[h].T
            else:
                st_ref[h] = jnp.zeros((GLA_DV, GLA_DK), F32)

    rp = n_inner * C

    def load(ref):
        x = ref[...].astype(F32)
        if rows_in == C:
            return x
        assert n_inner == 1
        return jnp.concatenate([x, jnp.zeros((C - rows_in, x.shape[1]), x.dtype)], axis=0)

    row = _iota((rp, 1), 0)
    col = _iota((1, rp), 1)
    same_chunk = (row >> GLA_CHUNK_SHIFT) == (col >> GLA_CHUNK_SHIFT)
    causal = same_chunk & (row >= col)
    tril = causal.astype(BF16)
    live = (row & (C - 1)) < n_valid

    ga_h, ga_l = _split_bf16(load(sm_ref)[:, :GLA_GATE_RANK])
    w_h, w_l = _split_bf16(wa2_ref[...])
    z = _dot(ga_h, w_h) + _dot(ga_l, w_h) + _dot(ga_h, w_l) + ba_ref[...]
    log_a = (jnp.minimum(z, 0.0) - jnp.log1p(jnp.exp(-jnp.abs(z)))) / GLA_TAU
    k = load(k_ref)
    if n_valid < C:
        log_a = jnp.where(live, log_a, 0.0)
        k = jnp.where(live, k, 0.0)
    la_h, la_l = _split_bf16(log_a)
    b = _dot(tril, la_h) + _dot(tril, la_l)
    b_last = [b[ci * C + C - 1:ci * C + C, :] for ci in range(n_inner)]
    q_dec = (load(q_ref) * (GLA_DK ** -0.5) * jnp.exp(b)).astype(BF16)
    decay = [jnp.exp(r) for r in b_last]
    k_dec_f = k * jnp.exp(-b)
    k_dec = k_dec_f.astype(BF16)
    k_end = jnp.concatenate([k_dec_f[ci * C:(ci + 1) * C] * decay[ci] for ci in range(n_inner)],
                            axis=0).astype(BF16)
    vb = load(v_ref).astype(BF16)

    heads = range(GLA_HEADS)
    kq = [slice(h * GLA_DK, (h + 1) * GLA_DK) for h in heads]
    kv = [slice(h * GLA_DV, (h + 1) * GLA_DV) for h in heads]
    att = [jnp.where(causal, _dot_nt(q_dec[:, kq[h]], k_dec[:, kq[h]]), 0.0).astype(BF16) for h in heads]
    intra = [_dot(att[h], vb[:, kv[h]]) for h in heads]
    upd = [[_dot_tn(vb[ci * C:(ci + 1) * C, kv[h]], k_end[ci * C:(ci + 1) * C, kq[h]]) for h in heads]
           for ci in range(n_inner)]
    for ci in range(n_inner):
        cr = slice(ci * C, (ci + 1) * C)
        rs = pl.ds(ci * rows_in, rows_in)
        st = [st_ref[h] for h in heads]
        o = [intra[h][cr] + _dot_nt(q_dec[cr, kq[h]], st[h].astype(BF16)) for h in heads]
        for h in heads:
            st_ref[h] = st[h] * decay[ci][:, kq[h]] + upd[ci][h]
        for h in heads:
            o_n = _rms(o[h], gn_ref[...])
            gg = gg_ref[rs, kv[h]].astype(F32)
            o_ref[rs, kv[h]] = (o_n[:rows_in] * (gg * jax.nn.sigmoid(gg))).astype(o_ref.dtype)

    @pl.when(c == pl.num_programs(1) - 1)
    def _():
        for h in range(GLA_HEADS):
            sout_ref[h] = st_ref[h].T


def gla(proj, wa2, ba, gn, s0, layer, depth, stacked, batch, rows_per_batch, rows_in, n_inner, n_valid):
    m = proj.shape[0]
    r = rows_in * n_inner
    n_steps = rows_per_batch // r
    has_s0 = s0 is not None
    if s0 is None:
        s0 = jnp.zeros((1, 1, GLA_HEADS, GLA_DK, GLA_DV), F32)
        s0_map = lambda b, c: (0, 0, 0, 0, 0)
    else:
        s0_map = lambda b, c: (layer, b, 0, 0, 0)
    wk, wv = GLA_HEADS * GLA_DK, GLA_HEADS * GLA_DV
    rowmap = lambda off: (lambda b, c: (b * n_steps + c, off))
    state_blk = (None, GLA_HEADS, GLA_DK, GLA_DV)
    kern = functools.partial(_gla_kernel, rows_in=rows_in, n_inner=n_inner, n_valid=n_valid, has_s0=has_s0)
    in_specs = [pl.BlockSpec((r, wk), rowmap(COL_GQ // wk)),
                pl.BlockSpec((r, wk), rowmap(COL_GK // wk)),
                pl.BlockSpec((r, wv), rowmap(COL_GV // wv)),
                pl.BlockSpec((r, wv), rowmap(COL_GG // wv)),
                pl.BlockSpec((r, LANES), rowmap(COL_SM // LANES)),
                pl.BlockSpec((GLA_GATE_RANK, wk), lambda b, c: (0, 0)),
                pl.BlockSpec((1, wk), lambda b, c: (0, 0)),
                pl.BlockSpec((1, GLA_DV), lambda b, c: (0, 0)),
                pl.BlockSpec((None,) + state_blk, s0_map)]
    args = (proj, proj, proj, proj, proj, wa2, ba, gn, s0)
    aliases = {}
    if stacked is not None:
        in_specs.append(pl.BlockSpec(memory_space=pl.ANY))
        aliases = {len(args): 1}
        args += (stacked,)
    return pl.pallas_call(
        kern,
        grid=(batch, n_steps),
        in_specs=in_specs,
        out_specs=[pl.BlockSpec((r, wv), rowmap(0)),
                   pl.BlockSpec((None,) + state_blk, lambda b, c: (layer, b, 0, 0, 0))],
        out_shape=[jax.ShapeDtypeStruct((m, D_MODEL), proj.dtype),
                   jax.ShapeDtypeStruct((depth, batch, GLA_HEADS, GLA_DK, GLA_DV), F32)],
        input_output_aliases=aliases,
        scratch_shapes=[pltpu.VMEM((GLA_HEADS, GLA_DV, GLA_DK), F32)],
        compiler_params=_cparams(("parallel", "arbitrary")),
        name="gla",
    )(*args)


def _rope(x, c, sa, sb):
    return x * c + pltpu.roll(x, LANES - ROPE_HALF, 1) * sa + pltpu.roll(x, ROPE_HALF, 1) * sb


def _nsa_prep_kernel(*refs, with_q):
    if with_q:
        nq_ref, nkv_ref, c_ref, sa_ref, sb_ref, q_ref, cmp_ref, sel_ref, win_ref = refs
    else:
        nkv_ref, c_ref, sa_ref, sb_ref, cmp_ref, sel_ref, win_ref = refs
    c, sa, sb = c_ref[...], sa_ref[...], sb_ref[...]
    if with_q:
        for h in range(NSA_HEADS):
            q_ref[h] = _rope(nq_ref[:, h * NSA_DH:(h + 1) * NSA_DH].astype(F32), c, sa, sb)
    for s, out in enumerate((cmp_ref, sel_ref, win_ref)):
        base = s * KV_ROW
        for hh in range(NSA_KV_HEADS):
            lo = hh * NSA_DH
            out[:, lo:lo + NSA_DH] = _rope(nkv_ref[:, base + lo:base + lo + NSA_DH].astype(F32), c, sa, sb)
        half = NSA_KV_HEADS * NSA_DH
        out[:, half:] = nkv_ref[:, base + half:base + KV_ROW].astype(F32)


def nsa_prep(proj, tabs, tr, with_q):
    m = proj.shape[0]
    n_tab = tabs[0].shape[0] // tr
    tab_spec = pl.BlockSpec((tr, LANES), lambda i: (i % n_tab, 0))
    nkv_w = 3 * KV_ROW
    row_spec = pl.BlockSpec((tr, KV_ROW), lambda i: (i, 0))
    row_shape = jax.ShapeDtypeStruct((m, KV_ROW), F32)
    in_specs = [pl.BlockSpec((tr, nkv_w), lambda i: (i, COL_NKV // nkv_w)), tab_spec, tab_spec, tab_spec]
    out_specs, out_shape, args = [row_spec] * 3, [row_shape] * 3, (proj,) + tuple(tabs)
    if with_q:
        in_specs = [pl.BlockSpec((tr, D_MODEL), lambda i: (i, COL_NQ // D_MODEL))] + in_specs
        out_specs = [pl.BlockSpec((NSA_HEADS, tr, NSA_DH), lambda i: (0, i, 0))] + out_specs
        out_shape = [jax.ShapeDtypeStruct((NSA_HEADS, m, NSA_DH), F32)] + out_shape
        args = (proj,) + args
    return pl.pallas_call(
        functools.partial(_nsa_prep_kernel, with_q=with_q),
        grid=(m // tr,),
        in_specs=in_specs,
        out_specs=out_specs,
        out_shape=out_shape,
        compiler_params=_cparams(("parallel",)),
        name="nsa_prep",
    )(*args)


def _nsa_prep_prompt_kernel(*refs):
    nkv_ref, c_ref, sa_ref, sb_ref = refs[:4]
    slabs, stacked = refs[-6:-3], refs[-3:]
    c, sa, sb = c_ref[...], sa_ref[...], sb_ref[...]
    tr = nkv_ref.shape[0]
    for s in range(3):
        for part in range(KV_PARTS):
            lo = part * NSA_DH
            v = nkv_ref[:, s * KV_ROW + lo:s * KV_ROW + lo + NSA_DH].astype(F32)
            if part < NSA_KV_HEADS:
                v = _rope(v, c, sa, sb)
            slabs[s][:, lo:lo + NSA_DH] = v
            stacked[s][pl.ds(part, tr, stride=KV_PARTS), :] = v


def nsa_prep_prompt(proj, tabs, tr, layer, depth, stacked, rows_per_batch):
    m = proj.shape[0]
    tiles_per_batch = rows_per_batch // tr
    n_tab = tabs[0].shape[0] // tr
    tab_spec = pl.BlockSpec((tr, LANES), lambda i: (i % n_tab, 0))
    nkv_w = 3 * KV_ROW
    row_spec = pl.BlockSpec((tr, KV_ROW), lambda i: (i, 0))
    in_specs = [pl.BlockSpec((tr, nkv_w), lambda i: (i, COL_NKV // nkv_w)), tab_spec, tab_spec, tab_spec]
    args = (proj,) + tuple(tabs)
    aliases = {}
    if stacked is not None:
        in_specs += [pl.BlockSpec(memory_space=pl.ANY)] * 3
        aliases = {len(args) + k: 3 + k for k in range(3)}
        args += tuple(stacked)
    st_spec = pl.BlockSpec((None, tr * KV_PARTS, NSA_DH), lambda i: (layer, i, 0))
    tail_spec = pl.BlockSpec((None, tr * KV_PARTS, NSA_DH), lambda i: (layer, i // tiles_per_batch, 0))
    st_shape = jax.ShapeDtypeStruct((depth, m * KV_PARTS, NSA_DH), F32)
    tail_shape = jax.ShapeDtypeStruct((depth, m // tiles_per_batch * KV_PARTS, NSA_DH), F32)
    out = pl.pallas_call(
        _nsa_prep_prompt_kernel,
        grid=(m // tr,),
        in_specs=in_specs,
        out_specs=[row_spec] * 3 + [st_spec, st_spec, tail_spec],
        out_shape=[jax.ShapeDtypeStruct((m, KV_ROW), F32)] * 3 + [st_shape, st_shape, tail_shape],
        input_output_aliases=aliases,
        compiler_params=_cparams(("arbitrary",)),
        name="nsa_prep_prompt",
    )(*args)
    return out[:3], out[3:]


def rope_tables(pos):
    inv = 1.0 / (ROPE_THETA ** (jnp.arange(ROPE_HALF, dtype=F32) / ROPE_HALF))
    ang = pos.astype(F32)[:, None] * inv[None, :]
    cos, sin = jnp.cos(ang), jnp.sin(ang)
    n = pos.shape[0]
    rest = LANES - ROPE_DIM
    c = jnp.concatenate([cos, cos, jnp.ones((n, rest), F32)], axis=1)
    sa = jnp.concatenate([-sin, jnp.zeros((n, LANES - ROPE_HALF), F32)], axis=1)
    sb = jnp.concatenate([jnp.zeros((n, ROPE_HALF), F32), sin, jnp.zeros((n, rest), F32)], axis=1)
    return c, sa, sb


_HALVES = (slice(0, CMP_STRIDE), slice(CMP_STRIDE, CMP_LEN))


def _pe_mix_sums(pe_ref, mix_ref):
    return [[jnp.sum(pe_ref[kv][h] * mix_ref[kv][h], axis=0, keepdims=True) for h in _HALVES] for kv in range(2)]


SUBLANES = 8


def _sum_rows_8(p):
    sub = _iota((SUBLANES, LANES), 0)
    n = SUBLANES
    stage = list(p)
    for shift, bit in ((4, 4), (2, 2), (1, 1)):
        nxt = []
        half = len(stage) // 2
        for c in range(half):
            lo = stage[c] + pltpu.roll(stage[c], n - shift, 0)
            hi = stage[c + half] + pltpu.roll(stage[c + half], shift, 0)
            nxt.append(jnp.where((sub & bit) == 0, lo, hi))
        stage = nxt
    return stage[0]


def _half_sums(xs, mix, consts):
    nch = xs.shape[0]
    out = []
    for h, c in zip(_HALVES, consts):
        y = xs * mix[h]
        y = y[:, :SUBLANES, :] + y[:, SUBLANES:, :]
        groups = [_sum_rows_8([y[g + i] for i in range(SUBLANES)]) for g in range(0, nch, SUBLANES)]
        out.append(jnp.concatenate(groups, axis=0) + c)
    return out


def _chunk_sums_kernel(x_ref, pe_ref, mix_ref, a_ref, b_ref):
    nch = x_ref.shape[0] // CMP_STRIDE
    consts = _pe_mix_sums(pe_ref, mix_ref)
    for kv in range(2):
        for hh in range(NSA_KV_HEADS):
            lo = (kv * NSA_KV_HEADS + hh) * NSA_DH
            xs = x_ref[:, lo:lo + NSA_DH].reshape(nch, CMP_STRIDE, NSA_DH)
            a_ref[:, lo:lo + NSA_DH], b_ref[:, lo:lo + NSA_DH] = _half_sums(xs, mix_ref[kv], consts[kv])


def _page_specs(layer):
    def spec(i):
        return pl.BlockSpec((None, None, PAGE_SIZE * KV_PARTS, NSA_DH),
                            lambda b, s, pt: (layer, pt[b, s * PAGES_PER_STEP + i], 0, 0))
    return [spec(i) for i in range(PAGES_PER_STEP)]


def _page_part(page_ref, part):
    return page_ref[pl.ds(part, PAGE_SIZE, stride=KV_PARTS), :]


def _chunk_sums_paged_kernel(pt_ref, *refs):
    page_refs, (pe_ref, mix_ref, a_ref, b_ref) = refs[:PAGES_PER_STEP], refs[PAGES_PER_STEP:]
    nch = PAGE_SIZE // CMP_STRIDE
    consts = _pe_mix_sums(pe_ref, mix_ref)
    for i, page_ref in enumerate(page_refs):
        rows = slice(i * nch, (i + 1) * nch)
        for kv in range(2):
            for hh in range(NSA_KV_HEADS):
                part = kv * NSA_KV_HEADS + hh
                lo = part * NSA_DH
                xs = _page_part(page_ref, part).reshape(nch, CMP_STRIDE, NSA_DH)
                a_ref[rows, lo:lo + NSA_DH], b_ref[rows, lo:lo + NSA_DH] = _half_sums(xs, mix_ref[kv], consts[kv])


def chunk_sums(rows, pe, mix, tr):
    m = rows.shape[0]
    nch = tr // CMP_STRIDE
    full = lambda i: (0, 0, 0)
    return pl.pallas_call(
        _chunk_sums_kernel,
        grid=(m // tr,),
        in_specs=[pl.BlockSpec((tr, KV_ROW), lambda i: (i, 0)),
                  pl.BlockSpec((2, CMP_LEN, NSA_DH), full),
                  pl.BlockSpec((2, CMP_LEN, NSA_DH), full)],
        out_specs=[pl.BlockSpec((nch, KV_ROW), lambda i: (i, 0))] * 2,
        out_shape=[jax.ShapeDtypeStruct((m // CMP_STRIDE, KV_ROW), F32)] * 2,
        compiler_params=_cparams(("parallel",)),
        name="chunk_sums",
    )(rows, pe, mix)


def chunk_sums_paged(cache, layer, page_table, pe, mix):
    batch, n_pages = page_table.shape
    nch = PAGES_PER_STEP * PAGE_SIZE // CMP_STRIDE
    n_steps = n_pages // PAGES_PER_STEP
    full = lambda b, s, pt: (0, 0, 0)
    gs = pltpu.PrefetchScalarGridSpec(
        num_scalar_prefetch=1,
        grid=(batch, n_steps),
        in_specs=_page_specs(layer) + [pl.BlockSpec((2, CMP_LEN, NSA_DH), full),
                                       pl.BlockSpec((2, CMP_LEN, NSA_DH), full)],
        out_specs=[pl.BlockSpec((nch, KV_ROW), lambda b, s, pt: (b * n_steps + s, 0))] * 2,
    )
    return pl.pallas_call(
        _chunk_sums_paged_kernel,
        grid_spec=gs,
        out_shape=[jax.ShapeDtypeStruct((batch * n_steps * nch, KV_ROW), F32)] * 2,
        compiler_params=_cparams(("parallel", "arbitrary")),
        name="chunk_sums_paged",
    )(page_table, *([cache] * PAGES_PER_STEP), pe, mix)


def _cmp_mlp_kernel(a_ref, b_ref, w1_ref, w2_ref, o_ref):
    h = a_ref[...] + b_ref[...]
    for kv in range(2):
        for hh in range(NSA_KV_HEADS):
            lo = (kv * NSA_KV_HEADS + hh) * NSA_DH
            y = jax.nn.gelu(_dot_3x(h[:, lo:lo + NSA_DH], w1_ref[kv]))
            o_ref[:, lo:lo + NSA_DH] = _dot_3x(y, w2_ref[kv])


def cmp_mlp(a, b_shift, w1, w2, tr):
    m = a.shape[0]
    return pl.pallas_call(
        _cmp_mlp_kernel,
        grid=(m // tr,),
        in_specs=[pl.BlockSpec((tr, KV_ROW), lambda i: (i, 0)),
                  pl.BlockSpec((tr, KV_ROW), lambda i: (i, 0)),
                  pl.BlockSpec((2, NSA_DH, CMP_HIDDEN), lambda i: (0, 0, 0)),
                  pl.BlockSpec((2, CMP_HIDDEN, NSA_DH), lambda i: (0, 0, 0))],
        out_specs=pl.BlockSpec((tr, KV_ROW), lambda i: (i, 0)),
        out_shape=jax.ShapeDtypeStruct((m, KV_ROW), F32),
        compiler_params=_cparams(("parallel",)),
        name="cmp_mlp",
    )(a, b_shift, w1, w2)


def compressed_kv(a, b, batch, w1, w2, tr):
    nch = a.shape[0] // batch
    b3 = b.reshape(batch, nch, KV_ROW)
    b_shift = jnp.concatenate([b3[:, 1:], jnp.zeros((batch, 1, KV_ROW), F32)], axis=1).reshape(batch * nch, KV_ROW)
    return cmp_mlp(a, b_shift, w1, w2, tr)


def _select_topn(imp, n_top):
    j = _iota(imp.shape, 1)
    big = jnp.int32(imp.shape[1])
    sel = jnp.zeros(imp.shape, F32)
    for _ in range(n_top):
        m = jnp.max(imp, axis=-1, keepdims=True)
        idx = jnp.min(jnp.where(imp == m, j, big), axis=-1, keepdims=True)
        hit = j == idx
        sel = jnp.where(hit, 1.0, sel)
        imp = jnp.where(hit, NEG_INF, imp)
    return sel


def _importance(p_grp, qpos, n_cmp_pad, n_slc, width):
    n_r = _iota((n_cmp_pad, 1), 0) * CMP_STRIDE
    j_c = _iota((1, width), 1)
    overlap = ((n_r < (j_c + 1) * SEL_LEN) & (n_r + CMP_LEN > j_c * SEL_LEN)).astype(BF16)
    p_hi, p_lo = _split_bf16(p_grp)
    imp = _dot(p_hi, overlap) + _dot(p_lo, overlap)
    cur = qpos >> SEL_SHIFT
    forced = (j_c == 0) | (j_c == cur) | (j_c == cur - 1)
    imp = jnp.where(forced, FORCED_SCORE, imp)
    imp = jnp.where(j_c > cur, -1.0, imp)
    return jnp.where(j_c >= n_slc, -2.0, imp)


LOG2E = 1.4426950408889634


def _nsa_prompt_kernel(nq_ref, c_ref, sa_ref, sb_ref, kc_ref, vc_ref, ks_ref, vs_ref, kw_ref, vw_ref, sm_ref, o_ref,
                       qf_ref, qb_ref, s_ref, p_ref, m_ref, l_ref, a_ref, acc_ref, ob_ref, *, Q, T, TK):
    G = NSA_GROUP
    hk = pl.program_id(1)
    qi = pl.program_id(2)
    q0 = qi * Q
    scale = NSA_DH ** -0.5
    n_cmp_pad = kc_ref.shape[0]
    n_slc = T // SEL_LEN
    qpos = q0 + _iota((Q, 1), 0)
    c, sa, sb = c_ref[...], sa_ref[...], sb_ref[...]
    for g in range(G):
        qg = _rope(nq_ref[:, g * NSA_DH:(g + 1) * NSA_DH].astype(F32), c, sa, sb)
        qf_ref[g * Q:(g + 1) * Q, :] = qg
        qb_ref[g * Q:(g + 1) * Q, :] = (qg * (scale * LOG2E)).astype(BF16)
    qs = qf_ref[...]

    ng = jax.nn.sigmoid(sm_ref[...].astype(F32))

    def gate(g, c3):
        lane0 = SM_GATE0 + 3 * g + c3
        lane1 = lane0 + 3 * G
        return jnp.where(hk == 0, ng[:, lane0:lane0 + 1], ng[:, lane1:lane1 + 1])

    def softmax_tile(kb, vb, bias, width):
        s_ref[:, :width] = _dot_nt(qb_ref[...], kb)
        for g in range(G):
            rows = pl.ds(g * Q, Q)
            s = s_ref[rows, :width] + bias
            m_old = m_ref[rows, :]
            m_new = jnp.maximum(m_old, jnp.max(s, axis=-1, keepdims=True))
            alpha = jnp.exp2(m_old - m_new)
            p = jnp.exp2(s - jnp.tile(m_new, (1, width // LANES)))
            l_ref[rows, :] = alpha * l_ref[rows, :] + jnp.sum(p, axis=-1, keepdims=True)
            a_ref[rows, :] = alpha
            m_ref[rows, :] = m_new
            p_ref[rows, :width] = p.astype(BF16)
        return _dot(p_ref[:, :width], vb)

    def reset():
        m_ref[...] = jnp.full_like(m_ref, NEG_INF)
        l_ref[...] = jnp.zeros_like(l_ref)

    cmp_valid = _iota((1, n_cmp_pad), 1) * CMP_STRIDE + (CMP_LEN - 1) <= qpos
    p_grp = jnp.zeros((Q, n_cmp_pad), F32)
    s_cmp = _dot_nt_3x(qs, kc_ref[...]) * scale
    vcb = vc_ref[...].astype(BF16)
    for g in range(G):
        p = _masked_softmax(s_cmp[g * Q:(g + 1) * Q], cmp_valid)
        p_grp = p_grp + p
        ob_ref[g] = gate(g, 0) * _dot(p.astype(BF16), vcb)

    ks0 = pl.multiple_of(jnp.maximum(q0 - WINDOW, 0), LANES)
    wlen = WINDOW + Q
    wpos = ks0 + _iota((1, wlen), 1)
    win_bias = jnp.where((wpos <= qpos) & (wpos > qpos - WINDOW), 0.0, NEG_INF)
    reset()
    pv = softmax_tile(kw_ref[pl.ds(ks0, wlen), :].astype(BF16), vw_ref[pl.ds(ks0, wlen), :].astype(BF16),
                      win_bias, wlen)
    for g in range(G):
        rows = pl.ds(g * Q, Q)
        ob_ref[g] = ob_ref[g] + gate(g, 2) * (pv[g * Q:(g + 1) * Q] / jnp.maximum(l_ref[rows, :], 1e-30))

    n_top = min(SEL_TOPN, n_slc)
    j_r = _iota((n_slc, 1), 0)
    n_c = _iota((1, n_cmp_pad), 1) * CMP_STRIDE
    overlap_t = ((n_c < (j_r + 1) * SEL_LEN) & (n_c + CMP_LEN > j_r * SEL_LEN)).astype(BF16)
    p_hi, p_lo = _split_bf16(p_grp)
    imp = _dot_nt(overlap_t, p_hi) + _dot_nt(overlap_t, p_lo)
    cur = (q0 + _iota((1, Q), 1)) >> SEL_SHIFT
    imp = jnp.where((j_r == 0) | (j_r == cur) | (j_r == cur - 1), FORCED_SCORE, imp)
    imp = jnp.where(j_r > cur, -1.0, imp)
    rank = jnp.zeros((n_slc, Q), F32)
    for jp in range(n_slc):
        other = imp[jp:jp + 1, :]
        beats = (other > imp) | ((other == imp) & (j_r > jp))
        rank = rank + jnp.where(beats, 1.0, 0.0)
    sel_t = jnp.where(rank < n_top, 1.0, 0.0).astype(BF16)

    reset()
    acc_ref[...] = jnp.zeros_like(acc_ref)

    def key_tile(kt, carry):
        k0 = pl.multiple_of(kt * TK, TK)
        kpos = k0 + _iota((1, TK), 1)
        expand = (j_r == (kpos >> SEL_SHIFT)).astype(BF16)
        keymask = _dot_tn(sel_t, expand)
        bias = jnp.where((keymask > 0.5) & (kpos <= qpos), 0.0, NEG_INF)
        pv = softmax_tile(ks_ref[pl.ds(k0, TK), :].astype(BF16), vs_ref[pl.ds(k0, TK), :].astype(BF16), bias, TK)
        acc_ref[...] = a_ref[...] * acc_ref[...] + pv
        return carry

    lax.fori_loop(0, (q0 + Q + TK - 1) // TK, key_tile, 0)

    for g in range(G):
        rows = pl.ds(g * Q, Q)
        o_sel = acc_ref[rows, :] / jnp.maximum(l_ref[rows, :], 1e-30)
        o_ref[g] = (ob_ref[g] + gate(g, 1) * o_sel).astype(o_ref.dtype)


def nsa_prompt(tabs, kvc, sel_rows, win_rows, proj, batch, T, Q):
    m = batch * T
    nq = T // Q
    n_cmp_pad = kvc.shape[0] // batch
    G = NSA_GROUP
    gw = G * NSA_DH
    kern = functools.partial(_nsa_prompt_kernel, Q=Q, T=T, TK=512)
    kcol = lambda off: (lambda b, hk, qi: (b, off + hk))
    tab_spec = pl.BlockSpec((Q, LANES), lambda b, hk, qi: (qi, 0))
    return pl.pallas_call(
        kern,
        grid=(batch, NSA_KV_HEADS, nq),
        in_specs=[pl.BlockSpec((Q, gw), lambda b, hk, qi: (b * nq + qi, COL_NQ // gw + hk)),
                  tab_spec, tab_spec, tab_spec,
                  pl.BlockSpec((n_cmp_pad, NSA_DH), kcol(0)),
                  pl.BlockSpec((n_cmp_pad, NSA_DH), kcol(NSA_KV_HEADS)),
                  pl.BlockSpec((T, NSA_DH), kcol(0)),
                  pl.BlockSpec((T, NSA_DH), kcol(NSA_KV_HEADS)),
                  pl.BlockSpec((T, NSA_DH), kcol(0)),
                  pl.BlockSpec((T, NSA_DH), kcol(NSA_KV_HEADS)),
                  pl.BlockSpec((Q, LANES), lambda b, hk, qi: (b * nq + qi, COL_SM // LANES))],
        out_specs=pl.BlockSpec((G, Q, NSA_DH), lambda b, hk, qi: (hk, b * nq + qi, 0)),
        out_shape=jax.ShapeDtypeStruct((NSA_HEADS, m, NSA_DH), proj.dtype),
        scratch_shapes=[pltpu.VMEM((G * Q, NSA_DH), F32), pltpu.VMEM((G * Q, NSA_DH), BF16),
                        pltpu.VMEM((G * Q, WINDOW + Q), F32), pltpu.VMEM((G * Q, WINDOW + Q), BF16),
                        pltpu.VMEM((G * Q, LANES), F32), pltpu.VMEM((G * Q, LANES), F32),
                        pltpu.VMEM((G * Q, LANES), F32), pltpu.VMEM((G * Q, NSA_DH), F32),
                        pltpu.VMEM((G, Q, NSA_DH), F32)],
        compiler_params=_cparams(("parallel", "parallel", "arbitrary")),
        name="nsa_prompt",
    )(proj, *tabs, kvc, kvc, sel_rows, sel_rows, win_rows, win_rows, proj)


SEL_W = 384


def _sample_cmp_select_kernel(q_ref, kvc_ref, ocmp_ref, selm_ref, *, n_slc):
    G, R = NSA_GROUP, SAMPLE_ROWS
    scale = NSA_DH ** -0.5
    n_cmp_pad = kvc_ref.shape[0]
    qpos_s = PAST_LEN + (_iota((G * R, 1), 0) & (R - 1))
    end = _iota((1, n_cmp_pad), 1) * CMP_STRIDE + (CMP_LEN - 1)
    p_grp = []
    for hk in range(NSA_KV_HEADS):
        qs = q_ref[hk * G:(hk + 1) * G].reshape(G * R, NSA_DH)
        kc = kvc_ref[:, hk * NSA_DH:(hk + 1) * NSA_DH]
        vc = kvc_ref[:, (NSA_KV_HEADS + hk) * NSA_DH:(NSA_KV_HEADS + hk + 1) * NSA_DH]
        p = _masked_softmax(_dot_nt_3x(qs, kc) * scale, end <= qpos_s)
        o_cmp = _dot(p.astype(BF16), vc.astype(BF16))
        for g in range(G):
            ocmp_ref[hk * G + g] = o_cmp[g * R:(g + 1) * R]
        p_grp.append(jnp.sum(p.reshape(G, R, n_cmp_pad), axis=0))
    qpos = PAST_LEN + (_iota((NSA_KV_HEADS * R, 1), 0) & (R - 1))
    imp = _importance(jnp.concatenate(p_grp, axis=0), qpos, n_cmp_pad, n_slc, SEL_W)
    forced = imp == FORCED_SCORE
    sel = jnp.where(forced, 1.0, _select_topn(jnp.where(forced, NEG_INF, imp), min(SEL_TOPN, n_slc) - 3))
    for hk in range(NSA_KV_HEADS):
        selm_ref[hk] = sel[hk * R:(hk + 1) * R]


def sample_cmp_select(q_hm, kvc, batch, n_slc):
    R = SAMPLE_ROWS
    n_cmp_pad = kvc.shape[0] // batch
    kern = functools.partial(_sample_cmp_select_kernel, n_slc=n_slc)
    hm_spec = pl.BlockSpec((NSA_HEADS, R, NSA_DH), lambda b: (0, b, 0))
    return pl.pallas_call(
        kern,
        grid=(batch,),
        in_specs=[hm_spec, pl.BlockSpec((n_cmp_pad, KV_ROW), lambda b: (b, 0))],
        out_specs=[hm_spec, pl.BlockSpec((None, NSA_KV_HEADS, R, SEL_W), lambda b: (b, 0, 0, 0))],
        out_shape=[jax.ShapeDtypeStruct((NSA_HEADS, batch * R, NSA_DH), F32),
                   jax.ShapeDtypeStruct((batch, NSA_KV_HEADS, R, SEL_W), F32)],
        compiler_params=_cparams(("parallel",)),
        name="sample_cmp_select",
    )(q_hm, kvc)


def _sample_attn_kernel(pt_ref, q_ref, selm_ref, *refs, n_steps):
    page_refs = refs[:PAGES_PER_STEP]
    nsel_ref, cwin_ref, nwin_ref, sm_ref, ocmp_ref, o_ref, m_ref, l_ref, acc_ref = refs[PAGES_PER_STEP:]
    G, R = NSA_GROUP, SAMPLE_ROWS
    p = pl.program_id(1)
    scale = NSA_DH ** -0.5
    t_s = _iota((G * R, 1), 0) & (R - 1)
    n_keys = PAGES_PER_STEP * PAGE_SIZE

    @pl.when(p == 0)
    def _():
        m_ref[...] = jnp.full_like(m_ref, NEG_INF)
        l_ref[...] = jnp.zeros_like(l_ref)
        acc_ref[...] = jnp.zeros_like(acc_ref)

    def q_stack(hk, mult):
        return (q_ref[hk * G:(hk + 1) * G].reshape(G * R, NSA_DH) * mult).astype(BF16)

    def update(hk, s, pr_of, v):
        m_old = m_ref[hk]
        m_new = jnp.maximum(m_old, jnp.max(s, axis=-1, keepdims=True))
        alpha = jnp.exp2(m_old - m_new)
        pr = pr_of(jnp.exp2(s - jnp.tile(m_new, (1, s.shape[1] // LANES))))
        l_ref[hk] = alpha * l_ref[hk] + jnp.sum(pr, axis=-1, keepdims=True)
        acc_ref[hk] = alpha * acc_ref[hk] + _dot(pr.astype(BF16), v)
        m_ref[hk] = m_new

    key = _iota((1, n_keys), 1)
    blk_of_key = (n_keys // SEL_LEN) * p + (key >> SEL_SHIFT)
    expand = (_iota((SEL_W, 1), 0) == blk_of_key).astype(BF16)
    hks = range(NSA_KV_HEADS)
    kpg = [jnp.concatenate([_page_part(r, hk).astype(BF16) for r in page_refs], axis=0) for hk in hks]
    vpg = [jnp.concatenate([_page_part(r, NSA_KV_HEADS + hk).astype(BF16) for r in page_refs], axis=0)
           for hk in hks]
    keymask = [_dot(selm_ref[hk].astype(BF16), expand) for hk in hks]
    s = [_dot_nt(q_stack(hk, scale * LOG2E), kpg[hk])
         + jnp.where(jnp.concatenate([keymask[hk]] * G, axis=0) > 0.5, 0.0, NEG_INF) for hk in hks]
    m_old = [m_ref[hk] for hk in hks]
    m_new = [jnp.maximum(m_old[hk], jnp.max(s[hk], axis=-1, keepdims=True)) for hk in hks]
    pr = [jnp.exp2(s[hk] - jnp.tile(m_new[hk], (1, n_keys // LANES))) for hk in hks]
    pv = [_dot(pr[hk].astype(BF16), vpg[hk]) for hk in hks]
    for hk in hks:
        alpha = jnp.exp2(m_old[hk] - m_new[hk])
        l_ref[hk] = alpha * l_ref[hk] + jnp.sum(pr[hk], axis=-1, keepdims=True)
        acc_ref[hk] = alpha * acc_ref[hk] + pv[hk]
        m_ref[hk] = m_new[hk]

    def win_part(ref, part):
        return ref[pl.ds(part, WINDOW, stride=KV_PARTS), :]

    @pl.when(p == n_steps - 1)
    def _():
        ng = jax.nn.sigmoid(sm_ref[...])
        new_blk = PAST_LEN // SEL_LEN
        zpad = jnp.zeros((LANES - R, NSA_DH), F32)
        jn = _iota((1, LANES), 1)
        for hk in range(NSA_KV_HEADS):
            klo, vlo = hk * NSA_DH, (NSA_KV_HEADS + hk) * NSA_DH
            kn = jnp.concatenate([nsel_ref[:, klo:klo + NSA_DH], zpad], axis=0)
            vn = jnp.concatenate([nsel_ref[:, vlo:vlo + NSA_DH], zpad], axis=0)
            picked = jnp.concatenate([selm_ref[hk][:, new_blk:new_blk + 1]] * G, axis=0) > 0.5
            valid = picked & (jn <= t_s) & (jn < R)
            s = jnp.where(valid, _dot_nt(q_stack(hk, scale * LOG2E), kn.astype(BF16)), NEG_INF)
            update(hk, s, lambda e: jnp.where(valid, e, 0.0), vn.astype(BF16))
            o_sel = acc_ref[hk] / jnp.maximum(l_ref[hk], 1e-30)
            qs = q_stack(hk, scale)
            kw = jnp.concatenate([win_part(cwin_ref, hk), nwin_ref[:, klo:klo + NSA_DH], zpad], axis=0)
            vw = jnp.concatenate([win_part(cwin_ref, NSA_KV_HEADS + hk), nwin_ref[:, vlo:vlo + NSA_DH], zpad], axis=0)
            iw = _iota((1, WINDOW + LANES), 1)
            wvalid = ((iw < WINDOW) & (iw > t_s)) | ((iw >= WINDOW) & (iw - WINDOW <= t_s) & (iw - WINDOW < R))
            pw = _masked_softmax(_dot_nt(qs, kw.astype(BF16)), wvalid)
            o_win = _dot(pw.astype(BF16), vw.astype(BF16))
            for g in range(G):
                h = hk * G + g
                lane = SM_GATE0 + 3 * h
                rows = slice(g * R, (g + 1) * R)
                o_ref[h] = (ng[:, lane:lane + 1] * ocmp_ref[h] + ng[:, lane + 1:lane + 2] * o_sel[rows]
                            + ng[:, lane + 2:lane + 3] * o_win[rows])


def sample_attn(q_hm, selm, cache_sel, layer, page_table, new_sel, cache_win, new_win, proj, o_cmp_hm):
    batch, n_pages = page_table.shape
    n_steps = n_pages // PAGES_PER_STEP
    G, R = NSA_GROUP, SAMPLE_ROWS
    hm_spec = pl.BlockSpec((NSA_HEADS, R, NSA_DH), lambda b, p, pt: (0, b, 0))
    row_spec = pl.BlockSpec((R, KV_ROW), lambda b, p, pt: (b, 0))
    gs = pltpu.PrefetchScalarGridSpec(
        num_scalar_prefetch=1,
        grid=(batch, n_steps),
        in_specs=[hm_spec,
                  pl.BlockSpec((None, NSA_KV_HEADS, R, SEL_W), lambda b, p, pt: (b, 0, 0, 0))]
                 + _page_specs(layer)
                 + [row_spec,
                    pl.BlockSpec((None, None, WINDOW * KV_PARTS, NSA_DH), lambda b, p, pt: (layer, b, 0, 0)),
                    row_spec,
                    pl.BlockSpec((R, LANES), lambda b, p, pt: (b, COL_SM // LANES)),
                    hm_spec],
        out_specs=hm_spec,
        scratch_shapes=[pltpu.VMEM((NSA_KV_HEADS, G * R, LANES), F32),
                        pltpu.VMEM((NSA_KV_HEADS, G * R, LANES), F32),
                        pltpu.VMEM((NSA_KV_HEADS, G * R, NSA_DH), F32)],
    )
    kern = functools.partial(_sample_attn_kernel, n_steps=n_steps)
    return pl.pallas_call(
        kern,
        grid_spec=gs,
        out_shape=jax.ShapeDtypeStruct((NSA_HEADS, batch * R, NSA_DH), F32),
        compiler_params=_cparams(("parallel", "arbitrary")),
        name="sample_attn",
    )(page_table, q_hm, selm, *([cache_sel] * PAGES_PER_STEP), new_sel, cache_win, new_win, proj, o_cmp_hm)


PACK_W = 512
PACK_SRC = PACK_W // LANES + 1
_MAIN_SEGMENTS = _SRC_SEGMENTS[:7]


def _pack_table():
    first, shift = [], []
    for lo, hi in _MAIN_SEGMENTS:
        assert (hi - lo) % PACK_W == 0
        for c in range(lo, hi, PACK_W):
            first.append(c // LANES)
            shift.append(c % LANES)
    assert len(first) * PACK_W == COL_SM
    n = N_PROJ // PACK_W
    first += [0] * (n - len(first))
    shift += [0] * (n - len(shift))
    return np.array([first, shift], np.int32)


def _pack_kernel(tab_ref, *refs):
    srcs, (ga_ref, ng_ref, o_ref) = refs[:PACK_SRC], refs[PACK_SRC:]
    j = pl.program_id(1)
    n_main = COL_SM // PACK_W
    shift = tab_ref[1, j]
    lane = _iota((1, LANES), 1)

    @pl.when(j >= n_main)
    def _():
        sm = jnp.where(lane < GLA_GATE_RANK, ga_ref[...], jnp.where(lane < SM_USED, ng_ref[...], 0.0))
        o_ref[:, :LANES] = sm.astype(BF16)
        o_ref[:, LANES:] = jnp.zeros((o_ref.shape[0], PACK_W - LANES), BF16)

    for sv in sorted({lo % LANES for lo, _ in _MAIN_SEGMENTS}):
        @pl.when((j < n_main) & (shift == sv))
        def _():
            for i in range(PACK_W // LANES):
                cols = slice(i * LANES, (i + 1) * LANES)
                if sv == 0:
                    o_ref[:, cols] = srcs[i][...].astype(BF16)
                else:
                    a = pltpu.roll(srcs[i][...], LANES - sv, 1)
                    b = pltpu.roll(srcs[i + 1][...], LANES - sv, 1)
                    o_ref[:, cols] = jnp.where(lane < LANES - sv, a, b).astype(BF16)


def _pack_w_in(w_in):
    depth, d, n_in = w_in.shape
    (ga_lo, ga_hi), (ng_lo, ng_hi) = _SRC_SEGMENTS[7:]
    assert ga_lo % LANES == 0 and ga_hi - ga_lo == GLA_GATE_RANK
    assert ng_lo % LANES == GLA_GATE_RANK and ng_hi - ng_lo == SM_USED - GLA_GATE_RANK
    last_blk = (n_in - 1) // LANES

    def src_spec(i):
        return pl.BlockSpec((None, d, LANES), lambda l, j, tab: (l, 0, jnp.minimum(tab[0, j] + i, last_blk)))

    def fixed_spec(col):
        return pl.BlockSpec((None, d, LANES), lambda l, j, tab: (l, 0, col // LANES))

    gs = pltpu.PrefetchScalarGridSpec(
        num_scalar_prefetch=1,
        grid=(depth, N_PROJ // PACK_W),
        in_specs=[src_spec(i) for i in range(PACK_SRC)] + [fixed_spec(ga_lo), fixed_spec(ng_lo)],
        out_specs=pl.BlockSpec((None, d, PACK_W), lambda l, j, tab: (l, 0, j)),
    )
    return pl.pallas_call(
        _pack_kernel,
        grid_spec=gs,
        out_shape=jax.ShapeDtypeStruct((depth, d, N_PROJ), BF16),
        compiler_params=_cparams(("parallel", "arbitrary")),
        name="pack_w_in",
    )(jnp.asarray(_pack_table()), *([w_in] * (PACK_SRC + 2)))


def _layer_prompt(x, lw, tabs, layer, depth, batch, T, stacked_rows, gla_states):
    proj = norm_matmul(x, lw['n0'], lw['w_in'], layer, 1024, 2048, BF16)
    o_a, gla_states = gla(proj, lw['wa2'], lw['ba'], lw['gn'], None, layer, depth, gla_states, batch, T,
                          GLA_CHUNK, 4, GLA_CHUNK)
    (cmp_rows, sel_rows, win_rows), stacked_rows = nsa_prep_prompt(proj, tabs, WINDOW, layer, depth, stacked_rows, T)
    a, b = chunk_sums(cmp_rows, lw['pe'], lw['mix'], 512)
    kvc = compressed_kv(a, b, batch, lw['w1'], lw['w2'], 128)
    o_b = nsa_prompt(tabs, kvc, sel_rows, win_rows, proj, batch, T, 256)
    x = merge_wo(x, proj, o_a, o_b, lw['n1'], lw['w_o'], layer, 512)
    x = mlp(x, lw['n2'], lw['n3'], lw['w_up'], lw['w_down'], layer, 512, 1024)
    return x, gla_states, stacked_rows


def _layer_sample(x, lw, tabs, layer, depth, batch, cache_cmp, cache_sel, cache_win, s0, page_table, gla_states):
    R = SAMPLE_ROWS
    m = batch * R
    proj = norm_matmul(x, lw['n0'], lw['w_in'], layer, m, 1024, F32)
    o_a, gla_states = gla(proj, lw['wa2'], lw['ba'], lw['gn'], s0, layer, depth, gla_states, batch, R, R, 1, 4)
    q_hm, cmp_rows, sel_rows, win_rows = nsa_prep(proj, tabs, m, with_q=True)
    a, b = chunk_sums_paged(cache_cmp, layer, page_table, lw['pe'], lw['mix'])
    kvc = compressed_kv(a, b, batch, lw['w1'], lw['w2'], 1024)
    n_slc = -(-(PAST_LEN + 4) // SEL_LEN)
    o_cmp, selm = sample_cmp_select(q_hm, kvc, batch, n_slc)
    o_b = sample_attn(q_hm, selm, cache_sel, layer, page_table, sel_rows, cache_win, win_rows, proj, o_cmp)
    x = merge_wo(x, proj, o_a, o_b, lw['n1'], lw['w_o'], layer, m)
    x = mlp(x, lw['n2'], lw['n3'], lw['w_up'], lw['w_down'], layer, m, 1024)
    return x, (cmp_rows, sel_rows, win_rows), gla_states


def kernel(x_prompt, x_sample, cache_cmp_kv, cache_sel_kv, cache_win_kv, state_gla, page_table,
           w_in, gla_wa2, gla_ba, gla_norm, cmp_pe, cmp_mix, cmp_w1, cmp_w2, w_o, norms, w_up, w_down):
    bp, T, d = x_prompt.shape
    bs, ss, _ = x_sample.shape
    R = SAMPLE_ROWS
    depth = w_in.shape[0]
    n_pool = cache_cmp_kv.shape[1]
    wb = cache_win_kv.shape[2]
    assert wb == WINDOW and T % 512 == 0 and ss <= R

    w_in_p = _pack_w_in(w_in)
    w_o_b, w_up_b, w_down_b = w_o.astype(BF16), w_up.astype(BF16), w_down.astype(BF16)
    cache_cmp = cache_cmp_kv.reshape(depth, n_pool, PAGE_SIZE * KV_PARTS, NSA_DH)
    cache_sel = cache_sel_kv.reshape(depth, n_pool, PAGE_SIZE * KV_PARTS, NSA_DH)
    cache_win = cache_win_kv.reshape(depth, bs, wb * KV_PARTS, NSA_DH)

    tabs_p = rope_tables(jnp.arange(T, dtype=jnp.int32))
    tabs_s = tuple(jnp.tile(t, (bs, 1)) for t in rope_tables(PAST_LEN + jnp.arange(R, dtype=jnp.int32)))

    y_p = x_prompt.reshape(bp * T, d)
    y_s = jnp.pad(x_sample, ((0, 0), (0, R - ss), (0, 0))).reshape(bs * R, d)
    st_s, stacked_rows, gla_p, gla_s = [], None, None, None
    for layer in range(depth):
        lw = {'w_in': w_in_p, 'wa2': gla_wa2[layer], 'ba': gla_ba[layer][None, :],
              'gn': gla_norm[layer][None, :], 'pe': cmp_pe[layer], 'mix': cmp_mix[layer],
              'w1': cmp_w1[layer], 'w2': cmp_w2[layer], 'w_o': w_o_b,
              'n0': norms[layer, 0][None, :], 'n1': norms[layer, 1][None, :],
              'n2': norms[layer, 2][None, :], 'n3': norms[layer, 3][None, :],
              'w_up': w_up_b, 'w_down': w_down_b}
        y_p, gla_p, stacked_rows = _layer_prompt(y_p, lw, tabs_p, layer, depth, bp, T, stacked_rows, gla_p)
        y_s, s_s, gla_s = _layer_sample(y_s, lw, tabs_s, layer, depth, bs, cache_cmp, cache_sel, cache_win,
                                        state_gla, page_table, gla_s)
        st_s.append(s_s)

    kv_shape = (2, NSA_KV_HEADS, NSA_DH)

    def rows_p(i):
        return stacked_rows[i].reshape(depth, bp, T, *kv_shape)

    def rows_s(i):
        return jnp.stack([s[i].reshape(bs, R, *kv_shape)[:, :ss] for s in st_s])

    win_p = stacked_rows[2].reshape(depth, bp, WINDOW, *kv_shape)
    win_s = jnp.concatenate([cache_win_kv[:, :, ss:], rows_s(2)], axis=2)
    return (y_p.reshape(bp, T, d), y_s.reshape(bs, R, d)[:, :ss],
            rows_p(0), rows_s(0), rows_p(1), rows_s(1), win_p, win_s,
            gla_p, gla_s)
```
